```python
import math
import jax, jax.numpy as jnp
from jax import lax
import numpy as np

D_MODEL = 2048
BATCH = 4
SEQ = 2048
DEPTH = 4
DEC_BATCH = 128
DEC_SEQ = 8
PAST_LEN = 8192
PAGE_SIZE = 128

N_EVEN = (DEPTH + 1) // 2
N_ODD = DEPTH // 2
HEAD_DIM = 128
A_HEADS = 8
A_KV_HEADS = 2
IDX_HEADS = 8
IDX_DIM = 64
TOPK_MAX = 256
REL_BUCKETS = 32
REL_MAX_DIST = 128
B_HEADS = 8
B_DK = 128
B_DV = 128
C_HEADS = 8
C_DK = 64
C_DV = 128
D_HEADS = 8
Q_LORA = 512
KV_LORA = 256
NOPE_DIM = 128
ROPE_DIM = 64
D_VDIM = 128
ROPE_THETA = 10000.0
MEM_TOKENS = 256
MEM_HEADS = 4
D_FF = 5632
CHUNK = 64
Q_BLOCK = 128
EPS = 1e-6
MASK_NEG = -1e30
LB_FLOOR = 1e-20
F32 = jnp.float32

EVEN_SPLITS = [A_HEADS * HEAD_DIM, A_KV_HEADS * HEAD_DIM, A_KV_HEADS * HEAD_DIM, IDX_HEADS * IDX_DIM, IDX_HEADS, IDX_DIM, B_HEADS * B_DK, B_HEADS * B_DK, B_HEADS * B_DV, B_HEADS * B_DV]
ODD_SPLITS = [C_HEADS * C_DK, C_HEADS * C_DK, C_HEADS * C_DV, C_HEADS, C_HEADS, C_HEADS * C_DV, Q_LORA, KV_LORA, ROPE_DIM]
EVEN_IN = sum(EVEN_SPLITS)
ODD_IN = sum(ODD_SPLITS)
MIX_WIDTH = A_HEADS * HEAD_DIM + B_HEADS * B_DV

kernel_name = 'hybrid_dsa_hgrn2_mlstm_mla_decode_step'


def rms_norm(x, g):
    xf = x.astype(F32)
    y = xf * lax.rsqrt(jnp.mean(xf * xf, axis=-1, keepdims=True) + EPS)
    return (y * g.astype(F32)).astype(x.dtype)


def split_cols(h, sizes):
    cuts = [int(c) for c in np.cumsum(sizes)[:-1]]
    return jnp.split(h, cuts, axis=-1)


def swiglu(x, w_gu, w_dn):
    g, u = jnp.split(x @ w_gu, 2, axis=-1)
    return (jax.nn.silu(g) * u) @ w_dn


def to_blocks(x, nb):
    b, t = x.shape[:2]
    return jnp.swapaxes(x.reshape((b, nb, t // nb) + x.shape[2:]), 0, 1)


def from_blocks(x):
    x = jnp.swapaxes(x, 0, 1)
    return x.reshape((x.shape[0], x.shape[1] * x.shape[2]) + x.shape[3:])


def rel_bucket(rel):
    n = jnp.maximum(rel, 0)
    max_exact = REL_BUCKETS // 2
    nf = jnp.maximum(n, 1).astype(F32)
    large = max_exact + (jnp.log(nf / max_exact) / math.log(REL_MAX_DIST / max_exact) * (REL_BUCKETS - max_exact)).astype(jnp.int32)
    return jnp.where(n < max_exact, n, jnp.minimum(large, REL_BUCKETS - 1))


def rope(x, pos):
    half = ROPE_DIM // 2
    inv = ROPE_THETA ** (-jnp.arange(half, dtype=F32) / half)
    ang = pos.astype(F32)[:, None] * inv[None, :]
    shp = (1, pos.shape[0]) + (1,) * (x.ndim - 3) + (half,)
    cos, sin = jnp.cos(ang).reshape(shp), jnp.sin(ang).reshape(shp)
    xf = x.astype(F32)
    x1, x2 = xf[..., :half], xf[..., half:]
    return jnp.concatenate([x1 * cos - x2 * sin, x2 * cos + x1 * sin], axis=-1).astype(x.dtype)


def dsa_select(q_idx, w_idx, k_idx, q_pos, k_pos, n_sel):
    s = jnp.einsum('bthd,bsd->bths', q_idx.astype(F32), k_idx.astype(F32))
    score = jnp.einsum('bths,bth->bts', jax.nn.relu(s), w_idx.astype(F32))
    allowed = k_pos[None, None, :] <= q_pos[None, :, None]
    score = jnp.where(allowed, score, MASK_NEG)
    _, sel = lax.top_k(score, n_sel)
    valid = sel <= q_pos[None, :, None]
    return sel, valid


def dsa_attend(q, kg, vg, q_pos, sel, valid, rel_bias):
    b, t, h, dh = q.shape
    g = h // A_KV_HEADS
    n_sel = sel.shape[-1]
    qg = q.reshape(b, t, A_KV_HEADS, g, dh)
    lg = jnp.einsum('btkgd,btskd->btkgs', qg, kg).astype(F32) * (dh ** -0.5)
    bias = rel_bias.astype(F32)[rel_bucket(q_pos[None, :, None] - sel)]
    lg = lg + jnp.moveaxis(bias, -1, 2).reshape(b, t, A_KV_HEADS, g, n_sel)
    lg = jnp.where(valid[:, :, None, None, :], lg, MASK_NEG)
    pr = jax.nn.softmax(lg, axis=-1).astype(vg.dtype)
    return jnp.einsum('btkgs,btskd->btkgd', pr, vg).reshape(b, t, h, dh)


def dsa_prompt(q, k, v, q_idx, w_idx, k_idx, rel_bias):
    b, t = q.shape[:2]
    n_sel = min(TOPK_MAX, t // 4)
    qb = math.gcd(Q_BLOCK, t)
    nb = t // qb
    k_pos = jnp.arange(t)
    bidx = jnp.arange(b)[:, None, None]

    def block(args):
        q_b, qi_b, wi_b, qp = args
        sel, valid = dsa_select(qi_b, wi_b, k_idx, qp, k_pos, n_sel)
        return dsa_attend(q_b, k[bidx, sel], v[bidx, sel], qp, sel, valid, rel_bias)

    out = lax.map(block, (to_blocks(q, nb), to_blocks(q_idx, nb), to_blocks(w_idx, nb), k_pos.reshape(nb, qb)))
    return from_blocks(out)


def gather_paged_rows(pool, layer, new_rows, page_table, sel, past):
    b = sel.shape[0]
    in_past = sel < past
    ps = jnp.minimum(sel, past - 1)
    phys = jnp.take_along_axis(page_table, (ps // PAGE_SIZE).reshape(b, -1), axis=1).reshape(sel.shape)
    past_rows = pool[layer, phys, ps % PAGE_SIZE]
    new_idx = jnp.clip(sel - past, 0, new_rows.shape[1] - 1)
    cur_rows = new_rows[jnp.arange(b)[:, None, None], new_idx]
    cond = in_past.reshape(in_past.shape + (1,) * (new_rows.ndim - 2))
    return jnp.where(cond, past_rows, cur_rows)


def dsa_sample(q, k_new, v_new, q_idx, w_idx, kidx_new, pool_k, pool_v, pool_kidx, layer, page_table, rel_bias):
    b, t = q.shape[:2]
    past = page_table.shape[1] * PAGE_SIZE
    n_sel = min(TOPK_MAX, (past + t) // 4)
    kidx_past = pool_kidx[layer, page_table].reshape(b, past, IDX_DIM)
    kidx_all = jnp.concatenate([kidx_past, kidx_new.astype(kidx_past.dtype)], axis=1)
    q_pos = past + jnp.arange(t)
    sel, valid = dsa_select(q_idx, w_idx, kidx_all, q_pos, jnp.arange(past + t), n_sel)
    kg = gather_paged_rows(pool_k, layer, k_new, page_table, sel, past)
    vg = gather_paged_rows(pool_v, layer, v_new, page_table, sel, past)
    return dsa_attend(q, kg, vg, q_pos, sel, valid, rel_bias)


def gla_chunked(q, k, v, log_f, s0):
    b, t, h, dk = q.shape
    c = math.gcd(CHUNK, t)
    nc = t // c
    tri = jnp.tril(jnp.ones((c, c), dtype=bool))

    def step(s, inp):
        qc, kc, vc, lf = inp
        cb = jnp.cumsum(lf, axis=1)
        diff = jnp.where(tri[None, :, :, None, None], cb[:, :, None] - cb[:, None, :], MASK_NEG)
        attn = jnp.einsum('bthd,btshd->bhts', qc, jnp.exp(diff) * kc[:, None])
        o = jnp.einsum('bhts,bshv->bthv', attn, vc) + jnp.einsum('bthd,bhdv->bthv', qc * jnp.exp(cb), s)
        c_last = cb[:, -1]
        s = jnp.exp(c_last)[..., None] * s + jnp.einsum('bshd,bshv->bhdv', kc * jnp.exp(c_last[:, None] - cb), vc)
        return s, o

    xs = tuple(to_blocks(a.astype(F32), nc) for a in (q, k, v, log_f))
    s, o = lax.scan(step, s0.astype(F32), xs)
    return from_blocks(o).astype(v.dtype), s


def mlstm_chunked(q, k, v, i_pre, f_pre, c0, n0, m0):
    b, t, h, dk = q.shape
    c = math.gcd(CHUNK, t)
    nc = t // c
    tri = jnp.tril(jnp.ones((c, c), dtype=bool))
    log_f = jax.nn.log_sigmoid(f_pre.astype(F32))

    def step(carry, inp):
        cm, n, m = carry
        qc, kc, vc, ic, lf = inp
        cb = jnp.cumsum(lf, axis=1)
        dmat = jnp.where(tri[None, :, :, None], cb[:, :, None] - cb[:, None] + ic[:, None], MASK_NEG)
        m_state = cb + m[:, None]
        m_t = jnp.maximum(m_state, jnp.max(dmat, axis=2))
        w = jnp.exp(dmat - m_t[:, :, None])
        w0 = jnp.exp(m_state - m_t)
        qk = jnp.einsum('bthd,bshd->btsh', qc, kc) * w
        num = jnp.einsum('btsh,bshv->bthv', qk, vc) + w0[..., None] * jnp.einsum('bthd,bhvd->bthv', qc, cm)
        den = jnp.sum(qk, axis=2) + w0 * jnp.einsum('bthd,bhd->bth', qc, n)
        hc = num / jnp.maximum(jnp.abs(den), jnp.exp(-m_t))[..., None]
        m_last = m_t[:, -1]
        ws = jnp.exp(cb[:, -1:] - cb + ic - m_last[:, None])
        fs = jnp.exp(cb[:, -1] + m - m_last)
        cm = fs[..., None, None] * cm + jnp.einsum('bshv,bshd->bhvd', vc * ws[..., None], kc)
        n = fs[..., None] * n + jnp.einsum('bsh,bshd->bhd', ws, kc)
        return (cm, n, m_last), hc

    xs = tuple(to_blocks(a.astype(F32), nc) for a in (q, k, v, i_pre, log_f))
    (cm, n, m), hs = lax.scan(step, (c0.astype(F32), n0.astype(F32), m0.astype(F32)), xs)
    return from_blocks(hs).astype(v.dtype), (cm, n, m)


def mla_kv(ckv, kpe, w_kvb, kn):
    b, l, _ = ckv.shape
    kv = (ckv @ w_kvb).reshape(b, l, D_HEADS, NOPE_DIM + D_VDIM)
    k_nope, v = kv[..., :NOPE_DIM], kv[..., NOPE_DIM:]
    k_pe = jnp.broadcast_to(kpe[:, :, None, :], (b, l, D_HEADS, ROPE_DIM)).astype(k_nope.dtype)
    return rms_norm(jnp.concatenate([k_nope, k_pe], axis=-1), kn), v


def causal_attn(q, k, v, q_pos, k_pos):
    b, t, h, dq = q.shape
    scale = dq ** -0.5

    def attend(args):
        q_b, qp = args
        lg = jnp.einsum('bthd,bshd->bhts', q_b, k).astype(F32) * scale
        lg = jnp.where(k_pos[None, None, None, :] <= qp[None, None, :, None], lg, MASK_NEG)
        pr = jax.nn.softmax(lg, axis=-1).astype(v.dtype)
        return jnp.einsum('bhts,bshd->bthd', pr, v)

    qb = math.gcd(Q_BLOCK, t)
    nb = t // qb
    if nb == 1:
        return attend((q, q_pos))
    return from_blocks(lax.map(attend, (to_blocks(q, nb), q_pos.reshape(nb, qb))))


def mla_sample(qd, ckv_new, kpe_new, pool_ckv, pool_kpe, layer, page_table, w_kvb, kn):
    b, t = qd.shape[:2]
    past = page_table.shape[1] * PAGE_SIZE
    q_pos = past + jnp.arange(t)
    k_pos = jnp.arange(past + t)

    def one(args):
        q1, c1, r1, pt = args
        c_all = jnp.concatenate([pool_ckv[layer, pt].reshape(past, KV_LORA), c1.astype(pool_ckv.dtype)], axis=0)[None]
        r_all = jnp.concatenate([pool_kpe[layer, pt].reshape(past, ROPE_DIM), r1.astype(pool_kpe.dtype)], axis=0)[None]
        k, v = mla_kv(c_all, r_all, w_kvb, kn)
        return causal_attn(q1[None], k, v, q_pos, k_pos)[0]

    return lax.map(one, (qd, ckv_new, kpe_new, page_table))


def mem_kv(mem, w_kv, kn):
    b, m, _ = mem.shape
    k, v = jnp.split((mem @ w_kv).reshape(b, m, 2 * MEM_HEADS, HEAD_DIM), 2, axis=2)
    return rms_norm(k, kn), v


def mem_attend(h, mk, mv, w_q, w_o, qn):
    b, t, _ = h.shape
    q = rms_norm((h @ w_q).reshape(b, t, MEM_HEADS, HEAD_DIM), qn)
    lg = jnp.einsum('bthd,bshd->bhts', q, mk.astype(q.dtype)).astype(F32) * (HEAD_DIM ** -0.5)
    pr = jax.nn.softmax(lg, axis=-1).astype(q.dtype)
    o = jnp.einsum('bhts,bshd->bthd', pr, mv.astype(q.dtype))
    return o.reshape(b, t, MEM_HEADS * HEAD_DIM) @ w_o


def even_mixer(h, pos, mode, st, p, e, lower_bound):
    b, t, _ = h.shape
    qa, ka, va, qi, wi, ki, qb, fb, ib, gb = split_cols(h @ p['w_in_even'][e], EVEN_SPLITS)
    qa = rms_norm(qa.reshape(b, t, A_HEADS, HEAD_DIM), p['a_qn'][e])
    ka = rms_norm(ka.reshape(b, t, A_KV_HEADS, HEAD_DIM), p['a_kn'][e])
    va = va.reshape(b, t, A_KV_HEADS, HEAD_DIM)
    qi = qi.reshape(b, t, IDX_HEADS, IDX_DIM) * (IDX_DIM ** -0.5)
    wi = wi * (IDX_HEADS ** -0.5)
    if mode == 'prompt':
        oa = dsa_prompt(qa, ka, va, qi, wi, ki, p['rel_bias'])
        s0 = jnp.zeros((b, B_HEADS, B_DK, B_DV), F32)
    else:
        oa = dsa_sample(qa, ka, va, qi, wi, ki, st['cache_a_k'], st['cache_a_v'], st['cache_a_kidx'], e, st['page_table'], p['rel_bias'])
        s0 = st['state_b'][e]
    lb = lower_bound.reshape(B_HEADS, B_DK)
    f_pre = fb.reshape(b, t, B_HEADS, B_DK).astype(F32)
    log_f = jnp.logaddexp(jnp.log(jnp.maximum(lb, LB_FLOOR)), jnp.log1p(-lb) + jax.nn.log_sigmoid(f_pre))
    k_b = (1.0 - lb) * jax.nn.sigmoid(-f_pre)
    q_b = jax.nn.silu(qb.reshape(b, t, B_HEADS, B_DK))
    ob, s_new = gla_chunked(q_b, k_b, ib.reshape(b, t, B_HEADS, B_DV), log_f, s0)
    ob = rms_norm(ob, p['b_on'][e]) * jax.nn.silu(gb.reshape(b, t, B_HEADS, B_DV))
    mixed = jnp.concatenate([oa.reshape(b, t, -1), ob.reshape(b, t, -1).astype(oa.dtype)], axis=-1)
    return mixed, (ka, va, ki, s_new.astype(h.dtype))


def odd_mixer(h, pos, mode, st, p, o):
    b, t, _ = h.shape
    qc, kc, vc, ic, fc, oc, qa, ckv, kpe = split_cols(h @ p['w_in_odd'][o], ODD_SPLITS)
    gate_b = p['c_gate_b'][o].astype(F32)
    i_pre = ic.astype(F32) + gate_b[0]
    f_pre = fc.astype(F32) + gate_b[1]
    q_c = qc.reshape(b, t, C_HEADS, C_DK)
    k_c = kc.reshape(b, t, C_HEADS, C_DK) * (C_DK ** -0.5)
    v_c = vc.reshape(b, t, C_HEADS, C_DV)
    if mode == 'prompt':
        c0 = jnp.zeros((b, C_HEADS, C_DV, C_DK), F32)
        n0 = jnp.zeros((b, C_HEADS, C_DK), F32)
        m0 = jnp.zeros((b, C_HEADS), F32)
    else:
        c0, n0, m0 = st['state_c_C'][o], st['state_c_n'][o], st['state_c_m'][o]
    hc, (c1, n1, m1) = mlstm_chunked(q_c, k_c, v_c, i_pre, f_pre, c0, n0, m0)
    hc = rms_norm(hc, p['c_on'][o]) * jax.nn.sigmoid(oc.reshape(b, t, C_HEADS, C_DV))
    qa = rms_norm(qa, p['d_qa_g'][o])
    qf = (qa @ p['w_d_qb'][o]).reshape(b, t, D_HEADS, NOPE_DIM + ROPE_DIM)
    qd = rms_norm(jnp.concatenate([qf[..., :NOPE_DIM], rope(qf[..., NOPE_DIM:], pos)], axis=-1), p['d_qn'][o])
    ckv = rms_norm(ckv, p['d_kv_g'][o])
    kpe = rope(kpe, pos)
    if mode == 'prompt':
        kd, vd = mla_kv(ckv, kpe, p['w_d_kvb'][o], p['d_kn'][o])
        od = causal_attn(qd, kd, vd, pos, pos)
    else:
        od = mla_sample(qd, ckv, kpe, st['cache_d_ckv'], st['cache_d_kpe'], o, st['page_table'], p['w_d_kvb'][o], p['d_kn'][o])
    mixed = jnp.concatenate([hc.reshape(b, t, -1).astype(od.dtype), od.reshape(b, t, -1)], axis=-1)
    return mixed, (ckv, kpe, c1.astype(h.dtype), n1.astype(h.dtype), m1.astype(h.dtype))


def trunk(x, pos, mode, st, p, lower_bound):
    new = {n: [] for n in ('a_k', 'a_v', 'a_kidx', 'b_S', 'c_C', 'c_n', 'c_m', 'd_ckv', 'd_kpe', 'mem_k', 'mem_v')}
    for l in range(DEPTH):
        x = x + 0.5 * swiglu(rms_norm(x, p['g_ffn1'][l]), p['w_ffn1_gu'][l], p['w_ffn1_dn'][l])
        h = rms_norm(x, p['g_mix'][l])
        if l % 2 == 0:
            mixed, (ka, va, ki, s_b) = even_mixer(h, pos, mode, st, p, l // 2, lower_bound[l // 2])
            new['a_k'].append(ka)
            new['a_v'].append(va)
            new['a_kidx'].append(ki)
            new['b_S'].append(s_b)
        else:
            mixed, (ckv, kpe, c1, n1, m1) = odd_mixer(h, pos, mode, st, p, l // 2)
            new['d_ckv'].append(ckv)
            new['d_kpe'].append(kpe)
            new['c_C'].append(c1)
            new['c_n'].append(n1)
            new['c_m'].append(m1)
        x = x + mixed @ p['w_mix_out'][l]
        if mode == 'prompt':
            mk, mv = mem_kv(st['mem'], p['w_mem_kv'][l], p['mem_kn'][l])
            new['mem_k'].append(mk)
            new['mem_v'].append(mv)
        else:
            mk, mv = st['cache_mem_k'][l], st['cache_mem_v'][l]
        x = x + mem_attend(rms_norm(x, p['g_mem'][l]), mk, mv, p['w_mem_q'][l], p['w_mem_o'][l], p['mem_qn'][l])
        x = x + 0.5 * swiglu(rms_norm(x, p['g_ffn2'][l]), p['w_ffn2_gu'][l], p['w_ffn2_dn'][l])
    return x, {n: jnp.stack(v) for n, v in new.items() if v}


def setup_inputs(seed: int = 0) -> dict:
    key = jax.random.key(seed)
    ks = iter(jax.random.split(key, 64))

    def nrm(shape, scale=1.0):
        return jax.random.normal(next(ks), shape, F32) * scale

    def gain(shape):
        return 1.0 + nrm(shape, 0.02)

    D = D_MODEL
    n_pages = PAST_LEN // PAGE_SIZE
    n_used = DEC_BATCH * n_pages
    n_phys = n_used + max(1, n_used // 4)
    page_table = jax.random.permutation(next(ks), n_phys)[:n_used].reshape(DEC_BATCH, n_pages).astype(jnp.int32)
    return {
        'x_prompt': nrm((BATCH, SEQ, D)),
        'x_sample': nrm((DEC_BATCH, DEC_SEQ, D)),
        'cache_a_k': nrm((N_EVEN, n_phys, PAGE_SIZE, A_KV_HEADS, HEAD_DIM)),
        'cache_a_v': nrm((N_EVEN, n_phys, PAGE_SIZE, A_KV_HEADS, HEAD_DIM)),
        'cache_a_kidx': nrm((N_EVEN, n_phys, PAGE_SIZE, IDX_DIM)),
        'state_b': nrm((N_EVEN, DEC_BATCH, B_HEADS, B_DK, B_DV), 0.3),
        'state_c_C': nrm((N_ODD, DEC_BATCH, C_HEADS, C_DV, C_DK), 0.3),
        'state_c_n': nrm((N_ODD, DEC_BATCH, C_HEADS, C_DK), 0.3),
        'state_c_m': nrm((N_ODD, DEC_BATCH, C_HEADS), 0.5),
        'cache_d_ckv': nrm((N_ODD, n_phys, PAGE_SIZE, KV_LORA)),
        'cache_d_kpe': nrm((N_ODD, n_phys, PAGE_SIZE, ROPE_DIM)),
        'cache_mem_k': nrm((DEPTH, DEC_BATCH, MEM_TOKENS, MEM_HEADS, HEAD_DIM)),
        'cache_mem_v': nrm((DEPTH, DEC_BATCH, MEM_TOKENS, MEM_HEADS, HEAD_DIM)),
        'page_table': page_table,
        'mem_prompt': nrm((BATCH, MEM_TOKENS, D)),
        'g_ffn1': gain((DEPTH, D)),
        'w_ffn1_gu': nrm((DEPTH, D, 2 * D_FF), D ** -0.5),
        'w_ffn1_dn': nrm((DEPTH, D_FF, D), D_FF ** -0.5),
        'g_mix': gain((DEPTH, D)),
        'w_in_even': nrm((N_EVEN, D, EVEN_IN), D ** -0.5),
        'w_in_odd': nrm((N_ODD, D, ODD_IN), D ** -0.5),
        'w_mix_out': nrm((DEPTH, MIX_WIDTH, D), MIX_WIDTH ** -0.5),
        'rel_bias': nrm((REL_BUCKETS, A_HEADS), 0.5),
        'a_qn': gain((N_EVEN, HEAD_DIM)),
        'a_kn': gain((N_EVEN, HEAD_DIM)),
        'b_lb': nrm((N_EVEN, B_HEADS * B_DK), 0.1),
        'b_on': gain((N_EVEN, B_DV)),
        'c_gate_b': jnp.stack([nrm((N_ODD, C_HEADS), 0.1), 3.0 + nrm((N_ODD, C_HEADS), 0.5)], axis=1),
        'c_on': gain((N_ODD, C_DV)),
        'd_qa_g': gain((N_ODD, Q_LORA)),
        'd_kv_g': gain((N_ODD, KV_LORA)),
        'w_d_qb': nrm((N_ODD, Q_LORA, D_HEADS * (NOPE_DIM + ROPE_DIM)), Q_LORA ** -0.5),
        'w_d_kvb': nrm((N_ODD, KV_LORA, D_HEADS * (NOPE_DIM + D_VDIM)), KV_LORA ** -0.5),
        'd_qn': gain((N_ODD, NOPE_DIM + ROPE_DIM)),
        'd_kn': gain((N_ODD, NOPE_DIM + ROPE_DIM)),
        'g_mem': gain((DEPTH, D)),
        'w_mem_q': nrm((DEPTH, D, MEM_HEADS * HEAD_DIM), D ** -0.5),
        'w_mem_kv': nrm((DEPTH, D, 2 * MEM_HEADS * HEAD_DIM), D ** -0.5),
        'w_mem_o': nrm((DEPTH, MEM_HEADS * HEAD_DIM, D), (MEM_HEADS * HEAD_DIM) ** -0.5),
        'mem_qn': gain((DEPTH, HEAD_DIM)),
        'mem_kn': gain((DEPTH, HEAD_DIM)),
        'g_ffn2': gain((DEPTH, D)),
        'w_ffn2_gu': nrm((DEPTH, D, 2 * D_FF), D ** -0.5),
        'w_ffn2_dn': nrm((DEPTH, D_FF, D), D_FF ** -0.5),
    }


def reference(x_prompt, x_sample, cache_a_k, cache_a_v, cache_a_kidx, state_b, state_c_C, state_c_n, state_c_m, cache_d_ckv, cache_d_kpe, cache_mem_k, cache_mem_v, page_table, mem_prompt, g_ffn1, w_ffn1_gu, w_ffn1_dn, g_mix, w_in_even, w_in_odd, w_mix_out, rel_bias, a_qn, a_kn, b_lb, b_on, c_gate_b, c_on, d_qa_g, d_kv_g, w_d_qb, w_d_kvb, d_qn, d_kn, g_mem, w_mem_q, w_mem_kv, w_mem_o, mem_qn, mem_kn, g_ffn2, w_ffn2_gu, w_ffn2_dn):
    p = {'g_ffn1': g_ffn1, 'w_ffn1_gu': w_ffn1_gu, 'w_ffn1_dn': w_ffn1_dn, 'g_mix': g_mix,
         'w_in_even': w_in_even, 'w_in_odd': w_in_odd, 'w_mix_out': w_mix_out, 'rel_bias': rel_bias,
         'a_qn': a_qn, 'a_kn': a_kn, 'b_on': b_on, 'c_gate_b': c_gate_b, 'c_on': c_on,
         'd_qa_g': d_qa_g, 'd_kv_g': d_kv_g, 'w_d_qb': w_d_qb, 'w_d_kvb': w_d_kvb, 'd_qn': d_qn, 'd_kn': d_kn,
         'g_mem': g_mem, 'w_mem_q': w_mem_q, 'w_mem_kv': w_mem_kv, 'w_mem_o': w_mem_o,
         'mem_qn': mem_qn, 'mem_kn': mem_kn, 'g_ffn2': g_ffn2, 'w_ffn2_gu': w_ffn2_gu, 'w_ffn2_dn': w_ffn2_dn}
    lb_soft = jax.nn.softmax(b_lb.astype(F32), axis=0)
    lower_bound = jnp.cumsum(lb_soft, axis=0) - lb_soft[0]

    prompt_pos = jnp.arange(x_prompt.shape[1])
    y_prompt, sp = trunk(x_prompt, prompt_pos, 'prompt', {'mem': mem_prompt}, p, lower_bound)

    past = page_table.shape[1] * PAGE_SIZE
    sample_pos = past + jnp.arange(x_sample.shape[1])
    st = {'cache_a_k': cache_a_k, 'cache_a_v': cache_a_v, 'cache_a_kidx': cache_a_kidx, 'state_b': state_b,
          'state_c_C': state_c_C, 'state_c_n': state_c_n, 'state_c_m': state_c_m,
          'cache_d_ckv': cache_d_ckv, 'cache_d_kpe': cache_d_kpe,
          'cache_mem_k': cache_mem_k, 'cache_mem_v': cache_mem_v, 'page_table': page_table}
    y_sample, ss = trunk(x_sample, sample_pos, 'sample', st, p, lower_bound)

    return (y_prompt, y_sample,
            sp['a_k'], sp['a_v'], sp['a_kidx'], sp['b_S'], sp['c_C'], sp['c_n'], sp['c_m'], sp['d_ckv'], sp['d_kpe'], sp['mem_k'], sp['mem_v'],
            ss['a_k'], ss['a_v'], ss['a_kidx'], ss['b_S'], ss['c_C'], ss['c_n'], ss['c_m'], ss['d_ckv'], ss['d_kpe'])
```

```python
import functools
import math

import jax
import jax.numpy as jnp
import numpy as np
from jax import lax
from jax.experimental import pallas as pl
from jax.experimental.pallas import tpu as pltpu

D_MODEL = 2048
DEPTH = 4
PAGE_SIZE = 128
HEAD_DIM = 128
A_HEADS = 8
A_KV_HEADS = 2
IDX_HEADS = 8
IDX_DIM = 64
TOPK_MAX = 256
REL_BUCKETS = 32
REL_MAX_DIST = 128
B_HEADS = 8
B_DK = 128
B_DV = 128
C_HEADS = 8
C_DK = 64
C_DV = 128
D_HEADS = 8
Q_LORA = 512
KV_LORA = 256
NOPE_DIM = 128
ROPE_DIM = 64
D_VDIM = 128
ROPE_THETA = 10000.0
MEM_HEADS = 4
D_FF = 5632
CHUNK = 64
Q_BLOCK = 128
EPS = 1e-6
MASK_NEG = -1e30
LB_FLOOR = 1e-20
F32 = jnp.float32
BF16 = jnp.bfloat16

EVEN_SPLITS = [A_HEADS * HEAD_DIM, A_KV_HEADS * HEAD_DIM, A_KV_HEADS * HEAD_DIM, IDX_HEADS * IDX_DIM, IDX_HEADS,
               IDX_DIM, B_HEADS * B_DK, B_HEADS * B_DK, B_HEADS * B_DV, B_HEADS * B_DV]
ODD_SPLITS = [C_HEADS * C_DK, C_HEADS * C_DK, C_HEADS * C_DV, C_HEADS, C_HEADS, C_HEADS * C_DV, Q_LORA, KV_LORA,
              ROPE_DIM]
EVEN_IN = sum(EVEN_SPLITS)
ODD_IN = sum(ODD_SPLITS)

LANES = 128
VMEM_LIMIT = 56 * 1024 * 1024


def _round_up(n, m):
    return (n + m - 1) // m * m


def _ffn_body(x_ref, g_ref, wg_ref, wu_ref, wd_ref, o_ref, n_ref):
    f = pl.program_id(1)

    @pl.when(f == 0)
    def _():
        x = x_ref[...]
        ms = jnp.mean(x * x, axis=-1, keepdims=True)
        n_ref[...] = (x * lax.rsqrt(ms + EPS) * g_ref[...]).astype(BF16)
        o_ref[...] = x

    n = n_ref[...]
    a = jnp.dot(n, wg_ref[...], preferred_element_type=F32)
    b = jnp.dot(n, wu_ref[...], preferred_element_type=F32)
    h = (a * jax.nn.sigmoid(a) * b).astype(BF16)
    o_ref[...] += 0.5 * jnp.dot(h, wd_ref[...], preferred_element_type=F32)


def ffn_half_step(x, g, w_gu, w_dn, *, tm=512, tf=512):
    m, d = x.shape
    ff = w_dn.shape[0]
    nf = ff // tf
    return pl.pallas_call(
        _ffn_body,
        grid=(m // tm, nf),
        in_specs=[
            pl.BlockSpec((tm, d), lambda i, f: (i, 0)),
            pl.BlockSpec((1, d), lambda i, f: (0, 0)),
            pl.BlockSpec((d, tf), lambda i, f: (0, f)),
            pl.BlockSpec((d, tf), lambda i, f: (0, f + nf)),
            pl.BlockSpec((tf, d), lambda i, f: (f, 0)),
        ],
        out_specs=pl.BlockSpec((tm, d), lambda i, f: (i, 0)),
        out_shape=jax.ShapeDtypeStruct((m, d), F32),
        scratch_shapes=[pltpu.VMEM((tm, d), BF16)],
        compiler_params=pltpu.CompilerParams(
            dimension_semantics=("parallel", "arbitrary"), vmem_limit_bytes=VMEM_LIMIT),
        name="ffn_half_step",
    )(x, g.reshape(1, d), w_gu, w_gu, w_dn)


def _mm_body(*refs, has_gain, has_res, alpha):
    a_ref, w_ref = refs[0], refs[1]
    k = 2
    g_ref = res_ref = None
    if has_gain:
        g_ref = refs[k]
        k += 1
    if has_res:
        res_ref = refs[k]
        k += 1
    o_ref, n_ref = refs[k], refs[k + 1]

    @pl.when(pl.program_id(1) == 0)
    def _():
        a = a_ref[...]
        if has_gain:
            ms = jnp.mean(a * a, axis=-1, keepdims=True)
            a = a * lax.rsqrt(ms + EPS) * g_ref[...]
        n_ref[...] = a.astype(BF16)

    acc = jnp.dot(n_ref[...], w_ref[...], preferred_element_type=F32)
    if has_res:
        acc = res_ref[...] + alpha * acc
    o_ref[...] = acc


def matmul(a, w, *, gain=None, res=None, alpha=1.0, tm=512, tn=512):
    m, kdim = a.shape
    n = w.shape[1]
    tm = min(tm, m)
    tn = min(tn, n)
    assert m % tm == 0 and n % tn == 0, (m, tm, n, tn)
    in_specs = [pl.BlockSpec((tm, kdim), lambda i, j: (i, 0)),
                pl.BlockSpec((kdim, tn), lambda i, j: (0, j))]
    args = [a, w]
    if gain is not None:
        in_specs.append(pl.BlockSpec((1, kdim), lambda i, j: (0, 0)))
        args.append(gain.reshape(1, kdim))
    if res is not None:
        in_specs.append(pl.BlockSpec((tm, tn), lambda i, j: (i, j)))
        args.append(res)
    return pl.pallas_call(
        functools.partial(_mm_body, has_gain=gain is not None, has_res=res is not None, alpha=alpha),
        grid=(m // tm, n // tn),
        in_specs=in_specs,
        out_specs=pl.BlockSpec((tm, tn), lambda i, j: (i, j)),
        out_shape=jax.ShapeDtypeStruct((m, n), F32),
        scratch_shapes=[pltpu.VMEM((tm, kdim), BF16)],
        compiler_params=pltpu.CompilerParams(
            dimension_semantics=("parallel", "arbitrary"), vmem_limit_bytes=VMEM_LIMIT),
        name="matmul",
    )(*args)


def rms_norm(x, g):
    xf = x.astype(F32)
    y = xf * lax.rsqrt(jnp.mean(xf * xf, axis=-1, keepdims=True) + EPS)
    return (y * g.astype(F32)).astype(x.dtype)


def split_cols(h, sizes):
    cuts = [int(c) for c in np.cumsum(sizes)[:-1]]
    return jnp.split(h, cuts, axis=-1)


def to_blocks(x, nb):
    b, t = x.shape[:2]
    return jnp.swapaxes(x.reshape((b, nb, t // nb) + x.shape[2:]), 0, 1)


def from_blocks(x):
    x = jnp.swapaxes(x, 0, 1)
    return x.reshape((x.shape[0], x.shape[1] * x.shape[2]) + x.shape[3:])


def rel_bucket(rel):
    n = jnp.maximum(rel, 0)
    max_exact = REL_BUCKETS // 2
    nf = jnp.maximum(n, 1).astype(F32)
    large = max_exact + (jnp.log(nf / max_exact) / math.log(REL_MAX_DIST / max_exact)
                         * (REL_BUCKETS - max_exact)).astype(jnp.int32)
    return jnp.where(n < max_exact, n, jnp.minimum(large, REL_BUCKETS - 1))


def rope(x, pos):
    half = ROPE_DIM // 2
    inv = ROPE_THETA ** (-jnp.arange(half, dtype=F32) / half)
    ang = pos.astype(F32)[:, None] * inv[None, :]
    shp = (1, pos.shape[0]) + (1,) * (x.ndim - 3) + (half,)
    cos, sin = jnp.cos(ang).reshape(shp), jnp.sin(ang).reshape(shp)
    xf = x.astype(F32)
    x1, x2 = xf[..., :half], xf[..., half:]
    return jnp.concatenate([x1 * cos - x2 * sin, x2 * cos + x1 * sin], axis=-1).astype(x.dtype)


def dsa_select(q_idx, w_idx, k_idx, q_pos, k_pos, n_sel):
    s = jnp.einsum('bthd,bsd->bths', q_idx.astype(F32), k_idx.astype(F32))
    score = jnp.einsum('bths,bth->bts', jax.nn.relu(s), w_idx.astype(F32))
    allowed = k_pos[None, None, :] <= q_pos[None, :, None]
    score = jnp.where(allowed, score, MASK_NEG)
    _, sel = lax.top_k(score, n_sel)
    valid = sel <= q_pos[None, :, None]
    return sel, valid


def dsa_attend(q, kg, vg, q_pos, sel, valid, rel_bias):
    b, t, h, dh = q.shape
    g = h // A_KV_HEADS
    n_sel = sel.shape[-1]
    qg = q.reshape(b, t, A_KV_HEADS, g, dh)
    lg = jnp.einsum('btkgd,btskd->btkgs', qg, kg).astype(F32) * (dh ** -0.5)
    bias = rel_bias.astype(F32)[rel_bucket(q_pos[None, :, None] - sel)]
    lg = lg + jnp.moveaxis(bias, -1, 2).reshape(b, t, A_KV_HEADS, g, n_sel)
    lg = jnp.where(valid[:, :, None, None, :], lg, MASK_NEG)
    pr = jax.nn.softmax(lg, axis=-1).astype(vg.dtype)
    return jnp.einsum('btkgs,btskd->btkgd', pr, vg).reshape(b, t, h, dh)


def dsa_prompt(q, k, v, q_idx, w_idx, k_idx, rel_bias):
    b, t = q.shape[:2]
    n_sel = min(TOPK_MAX, t // 4)
    qb = math.gcd(Q_BLOCK, t)
    nb = t // qb
    k_pos = jnp.arange(t)
    bidx = jnp.arange(b)[:, None, None]

    def block(args):
        q_b, qi_b, wi_b, qp = args
        sel, valid = dsa_select(qi_b, wi_b, k_idx, qp, k_pos, n_sel)
        return dsa_attend(q_b, k[bidx, sel], v[bidx, sel], qp, sel, valid, rel_bias)

    out = lax.map(block, (to_blocks(q, nb), to_blocks(q_idx, nb), to_blocks(w_idx, nb), k_pos.reshape(nb, qb)))
    return from_blocks(out)


def gather_paged_rows(pool, layer, new_rows, page_table, sel, past):
    b = sel.shape[0]
    in_past = sel < past
    ps = jnp.minimum(sel, past - 1)
    phys = jnp.take_along_axis(page_table, (ps // PAGE_SIZE).reshape(b, -1), axis=1).reshape(sel.shape)
    past_rows = pool[layer, phys, ps % PAGE_SIZE]
    new_idx = jnp.clip(sel - past, 0, new_rows.shape[1] - 1)
    cur_rows = new_rows[jnp.arange(b)[:, None, None], new_idx]
    cond = in_past.reshape(in_past.shape + (1,) * (new_rows.ndim - 2))
    return jnp.where(cond, past_rows, cur_rows)


def dsa_sample(q, k_new, v_new, q_idx, w_idx, kidx_new, pool_k, pool_v, pool_kidx, layer, page_table, rel_bias):
    b, t = q.shape[:2]
    past = page_table.shape[1] * PAGE_SIZE
    n_sel = min(TOPK_MAX, (past + t) // 4)
    kidx_past = pool_kidx[layer, page_table].reshape(b, past, IDX_DIM)
    kidx_all = jnp.concatenate([kidx_past, kidx_new.astype(kidx_past.dtype)], axis=1)
    q_pos = past + jnp.arange(t)
    sel, valid = dsa_select(q_idx, w_idx, kidx_all, q_pos, jnp.arange(past + t), n_sel)
    kg = gather_paged_rows(pool_k, layer, k_new, page_table, sel, past)
    vg = gather_paged_rows(pool_v, layer, v_new, page_table, sel, past)
    return dsa_attend(q, kg, vg, q_pos, sel, valid, rel_bias)


def gla_chunked(q, k, v, log_f, s0):
    b, t, h, dk = q.shape
    c = math.gcd(CHUNK, t)
    nc = t // c
    tri = jnp.tril(jnp.ones((c, c), dtype=bool))

    def step(s, inp):
        qc, kc, vc, lf = inp
        cb = jnp.cumsum(lf, axis=1)
        diff = jnp.where(tri[None, :, :, None, None], cb[:, :, None] - cb[:, None, :], MASK_NEG)
        attn = jnp.einsum('bthd,btshd->bhts', qc, jnp.exp(diff) * kc[:, None])
        o = jnp.einsum('bhts,bshv->bthv', attn, vc) + jnp.einsum('bthd,bhdv->bthv', qc * jnp.exp(cb), s)
        c_last = cb[:, -1]
        s = jnp.exp(c_last)[..., None] * s + jnp.einsum('bshd,bshv->bhdv', kc * jnp.exp(c_last[:, None] - cb), vc)
        return s, o

    xs = tuple(to_blocks(a.astype(F32), nc) for a in (q, k, v, log_f))
    s, o = lax.scan(step, s0.astype(F32), xs)
    return from_blocks(o).astype(v.dtype), s


def mlstm_chunked(q, k, v, i_pre, f_pre, c0, n0, m0):
    b, t, h, dk = q.shape
    c = math.gcd(CHUNK, t)
    nc = t // c
    tri = jnp.tril(jnp.ones((c, c), dtype=bool))
    log_f = jax.nn.log_sigmoid(f_pre.astype(F32))

    def step(carry, inp):
        cm, n, m = carry
        qc, kc, vc, ic, lf = inp
        cb = jnp.cumsum(lf, axis=1)
        dmat = jnp.where(tri[None, :, :, None], cb[:, :, None] - cb[:, None] + ic[:, None], MASK_NEG)
        m_state = cb + m[:, None]
        m_t = jnp.maximum(m_state, jnp.max(dmat, axis=2))
        w = jnp.exp(dmat - m_t[:, :, None])
        w0 = jnp.exp(m_state - m_t)
        qk = jnp.einsum('bthd,bshd->btsh', qc, kc) * w
        num = jnp.einsum('btsh,bshv->bthv', qk, vc) + w0[..., None] * jnp.einsum('bthd,bhvd->bthv', qc, cm)
        den = jnp.sum(qk, axis=2) + w0 * jnp.einsum('bthd,bhd->bth', qc, n)
        hc = num / jnp.maximum(jnp.abs(den), jnp.exp(-m_t))[..., None]
        m_last = m_t[:, -1]
        ws = jnp.exp(cb[:, -1:] - cb + ic - m_last[:, None])
        fs = jnp.exp(cb[:, -1] + m - m_last)
        cm = fs[..., None, None] * cm + jnp.einsum('bshv,bshd->bhvd', vc * ws[..., None], kc)
        n = fs[..., None] * n + jnp.einsum('bsh,bshd->bhd', ws, kc)
        return (cm, n, m_last), hc

    xs = tuple(to_blocks(a.astype(F32), nc) for a in (q, k, v, i_pre, log_f))
    (cm, n, m), hs = lax.scan(step, (c0.astype(F32), n0.astype(F32), m0.astype(F32)), xs)
    return from_blocks(hs).astype(v.dtype), (cm, n, m)


def mla_kv(ckv, kpe, w_kvb, kn):
    b, l, _ = ckv.shape
    kv = (ckv @ w_kvb).reshape(b, l, D_HEADS, NOPE_DIM + D_VDIM)
    k_nope, v = kv[..., :NOPE_DIM], kv[..., NOPE_DIM:]
    k_pe = jnp.broadcast_to(kpe[:, :, None, :], (b, l, D_HEADS, ROPE_DIM)).astype(k_nope.dtype)
    return rms_norm(jnp.concatenate([k_nope, k_pe], axis=-1), kn), v


def causal_attn(q, k, v, q_pos, k_pos):
    b, t, h, dq = q.shape
    scale = dq ** -0.5

    def attend(args):
        q_b, qp = args
        lg = jnp.einsum('bthd,bshd->bhts', q_b, k).astype(F32) * scale
        lg = jnp.where(k_pos[None, None, None, :] <= qp[None, None, :, None], lg, MASK_NEG)
        pr = jax.nn.softmax(lg, axis=-1).astype(v.dtype)
        return jnp.einsum('bhts,bshd->bthd', pr, v)

    qb = math.gcd(Q_BLOCK, t)
    nb = t // qb
    if nb == 1:
        return attend((q, q_pos))
    return from_blocks(lax.map(attend, (to_blocks(q, nb), q_pos.reshape(nb, qb))))


def mla_sample(qd, ckv_new, kpe_new, pool_ckv, pool_kpe, layer, page_table, w_kvb, kn):
    b, t = qd.shape[:2]
    past = page_table.shape[1] * PAGE_SIZE
    q_pos = past + jnp.arange(t)
    k_pos = jnp.arange(past + t)

    def one(args):
        q1, c1, r1, pt = args
        c_all = jnp.concatenate([pool_ckv[layer, pt].reshape(past, KV_LORA), c1.astype(pool_ckv.dtype)], axis=0)[None]
        r_all = jnp.concatenate([pool_kpe[layer, pt].reshape(past, ROPE_DIM), r1.astype(pool_kpe.dtype)], axis=0)[None]
        k, v = mla_kv(c_all, r_all, w_kvb, kn)
        return causal_attn(q1[None], k, v, q_pos, k_pos)[0]

    return lax.map(one, (qd, ckv_new, kpe_new, page_table))


def mem_attend_core(q, mk, mv, qn):
    b, t, _ = q.shape
    q = rms_norm(q.reshape(b, t, MEM_HEADS, HEAD_DIM), qn)
    lg = jnp.einsum('bthd,bshd->bhts', q, mk.astype(q.dtype)).astype(F32) * (HEAD_DIM ** -0.5)
    pr = jax.nn.softmax(lg, axis=-1).astype(q.dtype)
    o = jnp.einsum('bhts,bshd->bthd', pr, mv.astype(q.dtype))
    return o.reshape(b, t, MEM_HEADS * HEAD_DIM)


def even_mixer(proj, mode, st, p, e, lower_bound):
    b, t, _ = proj.shape
    qa, ka, va, qi, wi, ki, qb, fb, ib, gb = split_cols(proj, EVEN_SPLITS)
    qa = rms_norm(qa.reshape(b, t, A_HEADS, HEAD_DIM), p['a_qn'][e])
    ka = rms_norm(ka.reshape(b, t, A_KV_HEADS, HEAD_DIM), p['a_kn'][e])
    va = va.reshape(b, t, A_KV_HEADS, HEAD_DIM)
    qi = qi.reshape(b, t, IDX_HEADS, IDX_DIM) * (IDX_DIM ** -0.5)
    wi = wi * (IDX_HEADS ** -0.5)
    if mode == 'prompt':
        oa = dsa_prompt(qa, ka, va, qi, wi, ki, p['rel_bias'])
        s0 = jnp.zeros((b, B_HEADS, B_DK, B_DV), F32)
    else:
        oa = dsa_sample(qa, ka, va, qi, wi, ki, st['cache_a_k'], st['cache_a_v'], st['cache_a_kidx'], e,
                        st['page_table'], p['rel_bias'])
        s0 = st['state_b'][e]
    lb = lower_bound.reshape(B_HEADS, B_DK)
    f_pre = fb.reshape(b, t, B_HEADS, B_DK).astype(F32)
    log_f = jnp.logaddexp(jnp.log(jnp.maximum(lb, LB_FLOOR)), jnp.log1p(-lb) + jax.nn.log_sigmoid(f_pre))
    k_b = (1.0 - lb) * jax.nn.sigmoid(-f_pre)
    q_b = jax.nn.silu(qb.reshape(b, t, B_HEADS, B_DK))
    ob, s_new = gla_chunked(q_b, k_b, ib.reshape(b, t, B_HEADS, B_DV), log_f, s0)
    ob = rms_norm(ob, p['b_on'][e]) * jax.nn.silu(gb.reshape(b, t, B_HEADS, B_DV))
    mixed = jnp.concatenate([oa.reshape(b, t, -1), ob.reshape(b, t, -1).astype(oa.dtype)], axis=-1)
    return mixed, (ka, va, ki, s_new)


def odd_mixer(proj, pos, mode, st, p, o, wb):
    b, t, _ = proj.shape
    qc, kc, vc, ic, fc, oc, qa, ckv, kpe = split_cols(proj, ODD_SPLITS)
    gate_b = p['c_gate_b'][o].astype(F32)
    i_pre = ic.astype(F32) + gate_b[0]
    f_pre = fc.astype(F32) + gate_b[1]
    q_c = qc.reshape(b, t, C_HEADS, C_DK)
    k_c = kc.reshape(b, t, C_HEADS, C_DK) * (C_DK ** -0.5)
    v_c = vc.reshape(b, t, C_HEADS, C_DV)
    if mode == 'prompt':
        c0 = jnp.zeros((b, C_HEADS, C_DV, C_DK), F32)
        n0 = jnp.zeros((b, C_HEADS, C_DK), F32)
        m0 = jnp.zeros((b, C_HEADS), F32)
    else:
        c0, n0, m0 = st['state_c_C'][o], st['state_c_n'][o], st['state_c_m'][o]
    hc, (c1, n1, m1) = mlstm_chunked(q_c, k_c, v_c, i_pre, f_pre, c0, n0, m0)
    hc = rms_norm(hc, p['c_on'][o]) * jax.nn.sigmoid(oc.reshape(b, t, C_HEADS, C_DV))
    qf = matmul(qa.reshape(b * t, Q_LORA), wb['w_d_qb'][o], gain=p['d_qa_g'][o])
    qf = qf.reshape(b, t, D_HEADS, NOPE_DIM + ROPE_DIM)
    qd = rms_norm(jnp.concatenate([qf[..., :NOPE_DIM], rope(qf[..., NOPE_DIM:], pos)], axis=-1), p['d_qn'][o])
    ckv = rms_norm(ckv, p['d_kv_g'][o])
    kpe = rope(kpe, pos)
    if mode == 'prompt':
        kd, vd = mla_kv(ckv, kpe, p['w_d_kvb'][o], p['d_kn'][o])
        od = causal_attn(qd, kd, vd, pos, pos)
    else:
        od = mla_sample(qd, ckv, kpe, st['cache_d_ckv'], st['cache_d_kpe'], o, st['page_table'], p['w_d_kvb'][o],
                        p['d_kn'][o])
    mixed = jnp.concatenate([hc.reshape(b, t, -1).astype(od.dtype), od.reshape(b, t, -1)], axis=-1)
    return mixed, (ckv, kpe, c1, n1, m1)


def kernel(x_prompt, x_sample, cache_a_k, cache_a_v, cache_a_kidx, state_b, state_c_C, state_c_n, state_c_m, cache_d_ckv, cache_d_kpe, cache_mem_k, cache_mem_v, page_table, mem_prompt, g_ffn1, w_ffn1_gu, w_ffn1_dn, g_mix, w_in_even, w_in_odd, w_mix_out, rel_bias, a_qn, a_kn, b_lb, b_on, c_gate_b, c_on, d_qa_g, d_kv_g, w_d_qb, w_d_kvb, d_qn, d_kn, g_mem, w_mem_q, w_mem_kv, w_mem_o, mem_qn, mem_kn, g_ffn2, w_ffn2_gu, w_ffn2_dn):
    p = {'rel_bias': rel_bias, 'a_qn': a_qn, 'a_kn': a_kn, 'b_on': b_on, 'c_gate_b': c_gate_b, 'c_on': c_on,
         'd_qa_g': d_qa_g, 'd_kv_g': d_kv_g, 'w_d_kvb': w_d_kvb, 'd_qn': d_qn, 'd_kn': d_kn,
         'mem_qn': mem_qn, 'mem_kn': mem_kn}
    bp, tp, d = x_prompt.shape
    bs, ts, _ = x_sample.shape
    n_p, n_s = bp * tp, bs * ts
    n_mem = mem_prompt.shape[1]
    past = page_table.shape[1] * PAGE_SIZE

    even_pad = _round_up(EVEN_IN, 896)
    odd_pad = _round_up(ODD_IN, 1024)
    wb = {
        'ffn1_gu': w_ffn1_gu.astype(BF16), 'ffn1_dn': w_ffn1_dn.astype(BF16),
        'ffn2_gu': w_ffn2_gu.astype(BF16), 'ffn2_dn': w_ffn2_dn.astype(BF16),
        'in_even': jnp.pad(w_in_even.astype(BF16), ((0, 0), (0, 0), (0, even_pad - EVEN_IN))),
        'in_odd': jnp.pad(w_in_odd.astype(BF16), ((0, 0), (0, 0), (0, odd_pad - ODD_IN))),
        'mix_out': w_mix_out.astype(BF16), 'w_d_qb': w_d_qb.astype(BF16),
        'mem_q': w_mem_q.astype(BF16), 'mem_kv': w_mem_kv.astype(BF16), 'mem_o': w_mem_o.astype(BF16),
    }

    lb_soft = jax.nn.softmax(b_lb.astype(F32), axis=0)
    lower_bound = jnp.cumsum(lb_soft, axis=0) - lb_soft[0]

    prompt_pos = jnp.arange(tp)
    sample_pos = past + jnp.arange(ts)
    st = {'cache_a_k': cache_a_k, 'cache_a_v': cache_a_v, 'cache_a_kidx': cache_a_kidx, 'state_b': state_b,
          'state_c_C': state_c_C, 'state_c_n': state_c_n, 'state_c_m': state_c_m,
          'cache_d_ckv': cache_d_ckv, 'cache_d_kpe': cache_d_kpe, 'page_table': page_table}

    x = jnp.concatenate([x_prompt.reshape(n_p, d), x_sample.reshape(n_s, d)], axis=0)
    mem2d = mem_prompt.reshape(bp * n_mem, d)
    new_p = {n: [] for n in ('a_k', 'a_v', 'a_kidx', 'b_S', 'c_C', 'c_n', 'c_m', 'd_ckv', 'd_kpe', 'mem_k', 'mem_v')}
    new_s = {n: [] for n in ('a_k', 'a_v', 'a_kidx', 'b_S', 'c_C', 'c_n', 'c_m', 'd_ckv', 'd_kpe')}

    for l in range(DEPTH):
        x = ffn_half_step(x, g_ffn1[l], wb['ffn1_gu'][l], wb['ffn1_dn'][l])
        if l % 2 == 0:
            e = l // 2
            proj = matmul(x, wb['in_even'][e], gain=g_mix[l], tn=896)[:, :EVEN_IN]
            mixed_p, sp = even_mixer(proj[:n_p].reshape(bp, tp, EVEN_IN), 'prompt', st, p, e, lower_bound[e])
            mixed_s, ss = even_mixer(proj[n_p:].reshape(bs, ts, EVEN_IN), 'sample', st, p, e, lower_bound[e])
            for new, vals in ((new_p, sp), (new_s, ss)):
                for name, v in zip(('a_k', 'a_v', 'a_kidx', 'b_S'), vals):
                    new[name].append(v)
        else:
            o = l // 2
            proj = matmul(x, wb['in_odd'][o], gain=g_mix[l], tn=1024)[:, :ODD_IN]
            mixed_p, sp = odd_mixer(proj[:n_p].reshape(bp, tp, ODD_IN), prompt_pos, 'prompt', st, p, o, wb)
            mixed_s, ss = odd_mixer(proj[n_p:].reshape(bs, ts, ODD_IN), sample_pos, 'sample', st, p, o, wb)
            for new, vals in ((new_p, sp), (new_s, ss)):
                for name, v in zip(('d_ckv', 'd_kpe', 'c_C', 'c_n', 'c_m'), vals):
                    new[name].append(v)
        mixed = jnp.concatenate([mixed_p.reshape(n_p, -1), mixed_s.reshape(n_s, -1)], axis=0)
        x = matmul(mixed, wb['mix_out'][l], res=x)

        kv = matmul(mem2d, wb['mem_kv'][l]).reshape(bp, n_mem, 2 * MEM_HEADS, HEAD_DIM)
        mk_p = rms_norm(kv[:, :, :MEM_HEADS], mem_kn[l])
        mv_p = kv[:, :, MEM_HEADS:]
        new_p['mem_k'].append(mk_p)
        new_p['mem_v'].append(mv_p)
        q = matmul(x, wb['mem_q'][l], gain=g_mem[l])
        o_p = mem_attend_core(q[:n_p].reshape(bp, tp, -1), mk_p, mv_p, mem_qn[l])
        o_s = mem_attend_core(q[n_p:].reshape(bs, ts, -1), cache_mem_k[l], cache_mem_v[l], mem_qn[l])
        o_all = jnp.concatenate([o_p.reshape(n_p, -1), o_s.reshape(n_s, -1)], axis=0)
        x = matmul(o_all, wb['mem_o'][l], res=x)
        x = ffn_half_step(x, g_ffn2[l], wb['ffn2_gu'][l], wb['ffn2_dn'][l])

    sp = {n: jnp.stack(v) for n, v in new_p.items()}
    ss = {n: jnp.stack(v) for n, v in new_s.items()}
    return (x[:n_p].reshape(bp, tp, d), x[n_p:].reshape(bs, ts, d),
            sp['a_k'], sp['a_v'], sp['a_kidx'], sp['b_S'], sp['c_C'], sp['c_n'], sp['c_m'], sp['d_ckv'], sp['d_kpe'],
            sp['mem_k'], sp['mem_v'],
            ss['a_k'], ss['a_v'], ss['a_kidx'], ss['b_S'], ss['c_C'], ss['c_n'], ss['c_m'], ss['d_ckv'], ss['d_kpe'])
```

```python
import functools
import math

import jax
import jax.numpy as jnp
import numpy as np
from jax import lax
from jax.experimental import pallas as pl
from jax.experimental.pallas import tpu as pltpu

D_MODEL = 2048
DEPTH = 4
PAGE_SIZE = 128
HEAD_DIM = 128
A_HEADS = 8
A_KV_HEADS = 2
IDX_HEADS = 8
IDX_DIM = 64
TOPK_MAX = 256
REL_BUCKETS = 32
REL_MAX_DIST = 128
B_HEADS = 8
B_DK = 128
B_DV = 128
C_HEADS = 8
C_DK = 64
C_DV = 128
D_HEADS = 8
Q_LORA = 512
KV_LORA = 256
NOPE_DIM = 128
ROPE_DIM = 64
D_VDIM = 128
ROPE_THETA = 10000.0
MEM_HEADS = 4
D_FF = 5632
CHUNK = 64
Q_BLOCK = 128
EPS = 1e-6
MASK_NEG = -1e30
LB_FLOOR = 1e-20
F32 = jnp.float32
BF16 = jnp.bfloat16

EVEN_SPLITS = [A_HEADS * HEAD_DIM, A_KV_HEADS * HEAD_DIM, A_KV_HEADS * HEAD_DIM, IDX_HEADS * IDX_DIM, IDX_HEADS,
               IDX_DIM, B_HEADS * B_DK, B_HEADS * B_DK, B_HEADS * B_DV, B_HEADS * B_DV]
ODD_SPLITS = [C_HEADS * C_DK, C_HEADS * C_DK, C_HEADS * C_DV, C_HEADS, C_HEADS, C_HEADS * C_DV, Q_LORA, KV_LORA,
              ROPE_DIM]
EVEN_IN = sum(EVEN_SPLITS)
ODD_IN = sum(ODD_SPLITS)

EV_QA, EV_QB, EV_FB, EV_IB, EV_GB, EV_KA, EV_VA, EV_QI, EV_KIKI, EV_WI, EVEN_PAD = (
    0, 1024, 2048, 3072, 4096, 5120, 5376, 5632, 6144, 6272, 6400)
OD_QC, OD_KC, OD_VC, OD_OC, OD_G, OD_QA, OD_CKV, OD_KPE, ODD_PAD = (
    0, 512, 1024, 2048, 3072, 3584, 4096, 4352, 4480)

LANES = 128
VMEM_LIMIT = 56 * 1024 * 1024


def _round_up(n, m):
    return (n + m - 1) // m * m


def _ffn_body(x_ref, g_ref, wg_ref, wu_ref, wd_ref, o_ref, n_ref):
    f = pl.program_id(1)

    @pl.when(f == 0)
    def _():
        x = x_ref[...]
        ms = jnp.mean(x * x, axis=-1, keepdims=True)
        n_ref[...] = (x * lax.rsqrt(ms + EPS) * g_ref[...]).astype(BF16)
        o_ref[...] = x

    n = n_ref[...]
    a = jnp.dot(n, wg_ref[...], preferred_element_type=F32)
    b = jnp.dot(n, wu_ref[...], preferred_element_type=F32)
    h = (a * jax.nn.sigmoid(a) * b).astype(BF16)
    o_ref[...] += 0.5 * jnp.dot(h, wd_ref[...], preferred_element_type=F32)


def ffn_half_step(x, g, w_gu, w_dn, *, tm=512, tf=512):
    m, d = x.shape
    ff = w_dn.shape[0]
    nf = ff // tf
    return pl.pallas_call(
        _ffn_body,
        grid=(m // tm, nf),
        in_specs=[
            pl.BlockSpec((tm, d), lambda i, f: (i, 0)),
            pl.BlockSpec((1, d), lambda i, f: (0, 0)),
            pl.BlockSpec((d, tf), lambda i, f: (0, f)),
            pl.BlockSpec((d, tf), lambda i, f: (0, f + nf)),
            pl.BlockSpec((tf, d), lambda i, f: (f, 0)),
        ],
        out_specs=pl.BlockSpec((tm, d), lambda i, f: (i, 0)),
        out_shape=jax.ShapeDtypeStruct((m, d), F32),
        scratch_shapes=[pltpu.VMEM((tm, d), BF16)],
        compiler_params=pltpu.CompilerParams(
            dimension_semantics=("parallel", "arbitrary"), vmem_limit_bytes=VMEM_LIMIT),
        name="ffn_half_step",
    )(x, g.reshape(1, d), w_gu, w_gu, w_dn)


def _mm_body(*refs, has_gain, has_res, alpha):
    a_ref, w_ref = refs[0], refs[1]
    k = 2
    g_ref = res_ref = None
    if has_gain:
        g_ref = refs[k]
        k += 1
    if has_res:
        res_ref = refs[k]
        k += 1
    o_ref, n_ref = refs[k], refs[k + 1]

    @pl.when(pl.program_id(1) == 0)
    def _():
        a = a_ref[...]
        if has_gain:
            ms = jnp.mean(a * a, axis=-1, keepdims=True)
            a = a * lax.rsqrt(ms + EPS) * g_ref[...]
        n_ref[...] = a.astype(BF16)

    acc = jnp.dot(n_ref[...], w_ref[...], preferred_element_type=F32)
    if has_res:
        acc = res_ref[...] + alpha * acc
    o_ref[...] = acc


def matmul(a, w, *, gain=None, res=None, alpha=1.0, tm=512, tn=512, a_col=0):
    m = a.shape[0]
    kdim, n = w.shape
    tm = min(tm, m)
    tn = min(tn, n)
    assert m % tm == 0 and n % tn == 0, (m, tm, n, tn)
    in_specs = [pl.BlockSpec((tm, kdim), lambda i, j: (i, a_col)),
                pl.BlockSpec((kdim, tn), lambda i, j: (0, j))]
    args = [a, w]
    if gain is not None:
        in_specs.append(pl.BlockSpec((1, kdim), lambda i, j: (0, 0)))
        args.append(gain.reshape(1, kdim))
    if res is not None:
        in_specs.append(pl.BlockSpec((tm, tn), lambda i, j: (i, j)))
        args.append(res)
    return pl.pallas_call(
        functools.partial(_mm_body, has_gain=gain is not None, has_res=res is not None, alpha=alpha),
        grid=(m // tm, n // tn),
        in_specs=in_specs,
        out_specs=pl.BlockSpec((tm, tn), lambda i, j: (i, j)),
        out_shape=jax.ShapeDtypeStruct((m, n), F32),
        scratch_shapes=[pltpu.VMEM((tm, kdim), BF16)],
        compiler_params=pltpu.CompilerParams(
            dimension_semantics=("parallel", "arbitrary"), vmem_limit_bytes=VMEM_LIMIT),
        name="matmul",
    )(*args)


def _head_norm_body(x_ref, g_ref, o32_ref, o16_ref, *, heads):
    outs = []
    for h in range(heads):
        x = x_ref[:, LANES * h:LANES * (h + 1)]
        outs.append(x * lax.rsqrt(jnp.mean(x * x, axis=-1, keepdims=True) + EPS) * g_ref[...])
    y = jnp.concatenate(outs, axis=1)
    o32_ref[...] = y
    o16_ref[...] = y.astype(BF16)


def head_norm(x, col_block, heads, gain, *, tm=1024):
    m = x.shape[0]
    w = heads * LANES
    return pl.pallas_call(
        functools.partial(_head_norm_body, heads=heads),
        grid=(m // tm,),
        in_specs=[pl.BlockSpec((tm, w), lambda i: (i, col_block)),
                  pl.BlockSpec((1, LANES), lambda i: (0, 0))],
        out_specs=[pl.BlockSpec((tm, w), lambda i: (i, 0)), pl.BlockSpec((tm, w), lambda i: (i, 0))],
        out_shape=[jax.ShapeDtypeStruct((m, w), F32), jax.ShapeDtypeStruct((m, w), BF16)],
        compiler_params=pltpu.CompilerParams(dimension_semantics=("parallel",)),
        name="head_norm",
    )(x, gain.reshape(1, LANES))


INT_MIN = -2 ** 31
INT_MAX = 2 ** 31 - 1
_NT = (((1,), (1,)), ((), ()))


def _dot_nt(a, b):
    return lax.dot_general(a, b, _NT, preferred_element_type=F32)


def _sort_key(x):
    b = lax.bitcast_convert_type(x + 0.0, jnp.int32)
    return jnp.where(b >= 0, b, b ^ jnp.int32(INT_MAX))


def _row_count(cond):
    c = jnp.where(cond, 1.0, 0.0)
    if c.ndim == 3:
        c = jnp.sum(c, axis=0)
    return jnp.sum(c, axis=-1, keepdims=True)


def _topk_masks(pieces, k, idx_bits):
    kf = float(k)

    def count(fn):
        tot = None
        for key, idx in pieces:
            c = _row_count(fn(key, idx))
            tot = c if tot is None else tot + c
        return tot

    t0 = jnp.where(count(lambda key, idx: key >= 0) >= kf, jnp.int32(0), jnp.int32(INT_MIN))

    def body(i, t):
        cand = t | jnp.left_shift(jnp.int32(1), 30 - i)
        return jnp.where(count(lambda key, idx: key >= cand) >= kf, cand, t)

    thr = lax.fori_loop(0, 31, body, t0)
    need = kf - count(lambda key, idx: key > thr)
    ties = [(jnp.where(key == thr, idx, INT_MAX), idx) for key, idx in pieces]

    def count_ties(cand):
        tot = None
        for tie, _ in ties:
            c = _row_count(tie < cand)
            tot = c if tot is None else tot + c
        return tot

    def body2(i, cur):
        cand = cur + jnp.left_shift(jnp.int32(1), idx_bits - 1 - i)
        return jnp.where(count_ties(cand) < need, cand, cur)

    cut = lax.fori_loop(0, idx_bits, body2, jnp.zeros_like(thr))
    return [jnp.where(key > thr, 1.0, jnp.where(tie <= cut, 1.0, 0.0))
            for (key, _), (tie, _) in zip(pieces, ties)]


def _softmax_rows(lg):
    mx = jnp.max(lg, axis=-1, keepdims=True)
    p = jnp.exp(lg - mx)
    return p / jnp.sum(p, axis=-1, keepdims=True)


def _dsa_prompt_body(qa_ref, qi_ref, wi_ref, kiki_ref, k_ref, v_ref, bias_ref, qn_ref, o_ref, *, qb, t, n_sel):
    i = pl.program_id(1)
    nsub = qb // LANES
    nk = t // LANES
    kiki = kiki_ref[...].astype(BF16)
    qi = qi_ref[...] * (IDX_DIM ** -0.5)
    w = wi_ref[...] * (IDX_HEADS ** -0.5)
    lane = lax.broadcasted_iota(jnp.int32, (qb, LANES), 1)
    score = jnp.zeros((qb, t), F32)
    for h in range(IDX_HEADS):
        blk = qi[:, LANES * (h // 2):LANES * (h // 2 + 1)]
        keep = (lane < IDX_DIM) if h % 2 == 0 else (lane >= IDX_DIM)
        s = _dot_nt(jnp.where(keep, blk, 0.0).astype(BF16), kiki)
        score = score + jnp.maximum(s, 0.0) * w[:, h:h + 1]
    q_pos = i * qb + lax.broadcasted_iota(jnp.int32, (qb, t), 0)
    k_pos = lax.broadcasted_iota(jnp.int32, (qb, t), 1)
    allowed = k_pos <= q_pos
    score = jnp.where(allowed, score, MASK_NEG)
    (selm,) = _topk_masks([(_sort_key(score), k_pos)], n_sel, max(1, (t - 1).bit_length()))
    sel = jnp.where(allowed, selm, 0.0) > 0.5

    k = k_ref[...]
    v = v_ref[...].astype(BF16)
    outs = []
    for h in range(A_HEADS):
        g = h // (A_HEADS // A_KV_HEADS)
        rows = []
        for a in range(nsub):
            d0 = i * nsub + a
            tiles = []
            for j in range(nk):
                d = d0 - j
                tiles.append(jnp.where(d == 0, bias_ref[h, 0], jnp.where(d == 1, bias_ref[h, 1], bias_ref[h, 2])))
            rows.append(jnp.concatenate(tiles, axis=1))
        bias = jnp.concatenate(rows, axis=0) if nsub > 1 else rows[0]
        q = qa_ref[:, LANES * h:LANES * (h + 1)]
        q = q * lax.rsqrt(jnp.mean(q * q, axis=-1, keepdims=True) + EPS) * qn_ref[...]
        lg = _dot_nt(q.astype(BF16), k[:, LANES * g:LANES * (g + 1)]) * (HEAD_DIM ** -0.5) + bias
        p = _softmax_rows(jnp.where(sel, lg, MASK_NEG))
        outs.append(jnp.dot(p.astype(BF16), v[:, LANES * g:LANES * (g + 1)], preferred_element_type=F32))
    o_ref[...] = jnp.concatenate(outs, axis=1)


def dsa_prompt_attend(proj, ka16, bias_tiles, qn, *, nb, t, qb=256):
    nq = t // qb
    n_sel = min(TOPK_MAX, t // 4)
    wq = A_HEADS * HEAD_DIM
    return pl.pallas_call(
        functools.partial(_dsa_prompt_body, qb=qb, t=t, n_sel=n_sel),
        grid=(nb, nq),
        in_specs=[
            pl.BlockSpec((qb, wq), lambda b, i: (b * nq + i, 0)),
            pl.BlockSpec((qb, IDX_HEADS * IDX_DIM), lambda b, i: (b * nq + i, EV_QI // (IDX_HEADS * IDX_DIM))),
            pl.BlockSpec((qb, LANES), lambda b, i: (b * nq + i, EV_WI // LANES)),
            pl.BlockSpec((t, LANES), lambda b, i: (b, EV_KIKI // LANES)),
            pl.BlockSpec((t, A_KV_HEADS * HEAD_DIM), lambda b, i: (b, 0)),
            pl.BlockSpec((t, A_KV_HEADS * HEAD_DIM), lambda b, i: (b, EV_VA // (A_KV_HEADS * HEAD_DIM))),
            pl.BlockSpec((A_HEADS, 3, LANES, LANES), lambda b, i: (0, 0, 0, 0)),
            pl.BlockSpec((1, HEAD_DIM), lambda b, i: (0, 0)),
        ],
        out_specs=pl.BlockSpec((qb, wq), lambda b, i: (b * nq + i, 0)),
        out_shape=jax.ShapeDtypeStruct((nb * t, wq), F32),
        compiler_params=pltpu.CompilerParams(
            dimension_semantics=("parallel", "arbitrary"), vmem_limit_bytes=VMEM_LIMIT),
        name="dsa_prompt",
    )(proj, proj, proj, proj, ka16, proj, bias_tiles, qn.reshape(1, HEAD_DIM))


def _idx_head_sum(s, w):
    tot = None
    for h in range(IDX_HEADS):
        c = jnp.maximum(s[8 * h:8 * (h + 1)], 0.0) * w[:, h:h + 1]
        tot = c if tot is None else tot + c
    return tot


def _dsa_sel_body(pt_ref, qi_ref, wi_ref, knew_ref, *rest, pp, nj, ts, n_sel):
    pages, mask_ref, sc_ref = rest[:pp], rest[pp], rest[pp + 1]
    j = pl.program_id(1)
    tk = pp * PAGE_SIZE
    qi = qi_ref[...] * (IDX_DIM ** -0.5)
    a = jnp.concatenate([qi[:, IDX_DIM * h:IDX_DIM * (h + 1)] for h in range(IDX_HEADS)], axis=0).astype(BF16)
    w = wi_ref[...] * (IDX_HEADS ** -0.5)
    kt = jnp.concatenate([p[...] for p in pages], axis=0).astype(BF16)
    sc_ref[j] = _idx_head_sum(_dot_nt(a, kt), w)

    @pl.when(j == nj - 1)
    def _():
        knew = knew_ref[...][:, :IDX_DIM]
        kn = jnp.concatenate([knew, jnp.zeros((LANES - ts, IDX_DIM), F32)], axis=0).astype(BF16)
        snew = _idx_head_sum(_dot_nt(a, kn), w)
        lane = lax.broadcasted_iota(jnp.int32, (ts, LANES), 1)
        row = lax.broadcasted_iota(jnp.int32, (ts, LANES), 0)
        causal = lane <= row
        snew = jnp.where(causal, snew, MASK_NEG)
        key_new = jnp.where(lane < ts, _sort_key(snew), INT_MIN)
        key_past = _sort_key(sc_ref[...])
        idx_past = (lax.broadcasted_iota(jnp.int32, (nj, ts, tk), 0) * tk
                    + lax.broadcasted_iota(jnp.int32, (nj, ts, tk), 2))
        idx_new = nj * tk + lane
        m_past, m_new = _topk_masks([(key_past, idx_past), (key_new, idx_new)], n_sel,
                                    (nj * tk + LANES - 1).bit_length())
        for jj in range(nj):
            mask_ref[:, jj * tk:(jj + 1) * tk] = m_past[jj]
        mask_ref[:, nj * tk:] = jnp.where(causal, m_new, 0.0)


def dsa_sample_select(proj, pool_kidx, pt_flat, layer, *, row0, nb, ts, n_pages, pp=8):
    nj = n_pages // pp
    tk = pp * PAGE_SIZE
    past = n_pages * PAGE_SIZE
    n_sel = min(TOPK_MAX, (past + ts) // 4)
    rb0 = row0 // ts

    def page_spec(r):
        return pl.BlockSpec((None, None, PAGE_SIZE, IDX_DIM),
                            lambda b, j, pt: (layer, pt[b * n_pages + j * pp + r], 0, 0))

    grid_spec = pltpu.PrefetchScalarGridSpec(
        num_scalar_prefetch=1,
        grid=(nb, nj),
        in_specs=[
            pl.BlockSpec((ts, IDX_HEADS * IDX_DIM), lambda b, j, pt: (rb0 + b, EV_QI // (IDX_HEADS * IDX_DIM))),
            pl.BlockSpec((ts, LANES), lambda b, j, pt: (rb0 + b, EV_WI // LANES)),
            pl.BlockSpec((ts, LANES), lambda b, j, pt: (rb0 + b, EV_KIKI // LANES)),
        ] + [page_spec(r) for r in range(pp)],
        out_specs=pl.BlockSpec((None, ts, past + LANES), lambda b, j, pt: (b, 0, 0)),
        scratch_shapes=[pltpu.VMEM((nj, ts, tk), F32)],
    )
    return pl.pallas_call(
        functools.partial(_dsa_sel_body, pp=pp, nj=nj, ts=ts, n_sel=n_sel),
        grid_spec=grid_spec,
        out_shape=jax.ShapeDtypeStruct((nb, ts, past + LANES), F32),
        compiler_params=pltpu.CompilerParams(
            dimension_semantics=("parallel", "arbitrary"), vmem_limit_bytes=VMEM_LIMIT),
        name="dsa_sample_select",
    )(pt_flat, proj, proj, proj, *([pool_kidx] * pp))


def _dsa_att_body(pt_ref, q_ref, knew_ref, vnew_ref, qn_ref, mask_ref, mnew_ref, bias_ref, bnew_ref, *rest,
                  pp, nj, ts):
    kpages, vpages = rest[:pp], rest[pp:2 * pp]
    o_ref, qs_ref, m_ref, l_ref, acc_ref = rest[2 * pp:]
    j = pl.program_id(1)
    gsz = A_HEADS // A_KV_HEADS
    rows_g = gsz * ts

    @pl.when(j == 0)
    def _():
        qs = []
        for h in range(A_HEADS):
            q = q_ref[:, LANES * h:LANES * (h + 1)]
            qs.append(q * lax.rsqrt(jnp.mean(q * q, axis=-1, keepdims=True) + EPS) * qn_ref[...])
        qs_ref[...] = jnp.concatenate(qs, axis=0).astype(BF16)
        m_ref[...] = jnp.full(m_ref.shape, MASK_NEG, F32)
        l_ref[...] = jnp.zeros(l_ref.shape, F32)
        acc_ref[...] = jnp.zeros(acc_ref.shape, F32)

    def update(k_tile, v_tile, mask, bias):
        qs = qs_ref[...]
        lg = jnp.concatenate(
            [_dot_nt(qs[rows_g * g:rows_g * (g + 1)], k_tile[:, LANES * g:LANES * (g + 1)])
             for g in range(A_KV_HEADS)], axis=0) * (HEAD_DIM ** -0.5) + bias
        sel = jnp.concatenate([mask] * A_HEADS, axis=0) > 0.5
        lg = jnp.where(sel, lg, MASK_NEG)
        m_old = m_ref[...]
        m_new = jnp.maximum(m_old, jnp.max(lg, axis=-1, keepdims=True))
        alpha = jnp.exp(m_old - m_new)
        p = jnp.where(sel, jnp.exp(lg - m_new), 0.0)
        l_ref[...] = alpha * l_ref[...] + jnp.sum(p, axis=-1, keepdims=True)
        pb = p.astype(BF16)
        pv = jnp.concatenate(
            [jnp.dot(pb[rows_g * g:rows_g * (g + 1)], v_tile[:, LANES * g:LANES * (g + 1)],
                     preferred_element_type=F32) for g in range(A_KV_HEADS)], axis=0)
        acc_ref[...] = alpha * acc_ref[...] + pv
        m_ref[...] = m_new

    k_tile = jnp.concatenate([p[...] for p in kpages], axis=0).astype(BF16)
    v_tile = jnp.concatenate([p[...] for p in vpages], axis=0).astype(BF16)
    update(k_tile, v_tile, mask_ref[...], bias_ref[...].reshape(A_HEADS * ts, pp * PAGE_SIZE))

    @pl.when(j == nj - 1)
    def _():
        pad = jnp.zeros((LANES - ts, A_KV_HEADS * HEAD_DIM), F32)
        kn = jnp.concatenate([knew_ref[...], pad], axis=0).astype(BF16)
        vn = jnp.concatenate([vnew_ref[...], pad], axis=0).astype(BF16)
        update(kn, vn, mnew_ref[...], bnew_ref[...].reshape(A_HEADS * ts, LANES))
        out = acc_ref[...] / l_ref[...]
        o_ref[...] = jnp.concatenate([out[ts * h:ts * (h + 1)] for h in range(A_HEADS)], axis=1)


def dsa_sample_attend(proj, ka32, mask, bias_s, pool_k, pool_v, pt_flat, layer, qn, *, row0, nb, ts, n_pages, pp=8):
    nj = n_pages // pp
    tk = pp * PAGE_SIZE
    past = n_pages * PAGE_SIZE
    rb0 = row0 // ts
    wkv = A_KV_HEADS * HEAD_DIM
    wq = A_HEADS * HEAD_DIM

    def page_spec(r):
        return pl.BlockSpec((None, None, PAGE_SIZE, wkv), lambda b, j, pt: (layer, pt[b * n_pages + j * pp + r], 0, 0))

    grid_spec = pltpu.PrefetchScalarGridSpec(
        num_scalar_prefetch=1,
        grid=(nb, nj),
        in_specs=[
            pl.BlockSpec((ts, wq), lambda b, j, pt: (rb0 + b, 0)),
            pl.BlockSpec((ts, wkv), lambda b, j, pt: (rb0 + b, 0)),
            pl.BlockSpec((ts, wkv), lambda b, j, pt: (rb0 + b, EV_VA // wkv)),
            pl.BlockSpec((1, HEAD_DIM), lambda b, j, pt: (0, 0)),
            pl.BlockSpec((None, ts, tk), lambda b, j, pt: (b, 0, j)),
            pl.BlockSpec((None, ts, LANES), lambda b, j, pt: (b, 0, past // LANES)),
            pl.BlockSpec((A_HEADS, ts, tk), lambda b, j, pt: (0, 0, j)),
            pl.BlockSpec((A_HEADS, ts, LANES), lambda b, j, pt: (0, 0, past // LANES)),
        ] + [page_spec(r) for r in range(pp)] * 2,
        out_specs=pl.BlockSpec((ts, wq), lambda b, j, pt: (b, 0)),
        scratch_shapes=[pltpu.VMEM((A_HEADS * ts, HEAD_DIM), BF16), pltpu.VMEM((A_HEADS * ts, 1), F32),
                        pltpu.VMEM((A_HEADS * ts, 1), F32), pltpu.VMEM((A_HEADS * ts, HEAD_DIM), F32)],
    )
    return pl.pallas_call(
        functools.partial(_dsa_att_body, pp=pp, nj=nj, ts=ts),
        grid_spec=grid_spec,
        out_shape=jax.ShapeDtypeStruct((nb * ts, wq), F32),
        compiler_params=pltpu.CompilerParams(
            dimension_semantics=("parallel", "arbitrary"), vmem_limit_bytes=VMEM_LIMIT),
        name="dsa_sample_attend",
    )(pt_flat, proj, ka32, proj, qn.reshape(1, HEAD_DIM), mask, mask, bias_s, bias_s,
      *([pool_k] * pp), *([pool_v] * pp))


def _mla_sample_body(pt_ref, qlat_ref, qpe_ref, cnew_ref, rnew_ref, wkt_ref, wv_ref, *rest, pp, nj, ts):
    cpages, rpages = rest[:pp], rest[pp:2 * pp]
    o_ref, m_ref, l_ref, acc_ref = rest[2 * pp:]
    j = pl.program_id(1)
    dq = NOPE_DIM + ROPE_DIM
    nrow = D_HEADS * ts

    @pl.when(j == 0)
    def _():
        m_ref[...] = jnp.full(m_ref.shape, MASK_NEG, F32)
        l_ref[...] = jnp.zeros(l_ref.shape, F32)
        acc_ref[...] = jnp.zeros(acc_ref.shape, F32)

    def update(c, r, sel):
        n = c.shape[0]
        cb = c.astype(BF16)
        kt = _dot_nt(wkt_ref[...], cb)
        ss = jnp.sum((kt * kt).reshape(D_HEADS, NOPE_DIM, n), axis=1)
        r2 = r * r
        r2_hi = r2.astype(BF16)
        r2_lo = (r2 - r2_hi.astype(F32)).astype(BF16)
        ones = jnp.ones((8, ROPE_DIM), BF16)
        rss = (_dot_nt(ones, r2_hi) + _dot_nt(ones, r2_lo))[0:1]
        rinv = lax.rsqrt((ss + rss) * (1.0 / dq) + EPS) * (dq ** -0.5)
        lg = _dot_nt(qlat_ref[...], cb) + _dot_nt(qpe_ref[...], r.astype(BF16))
        lg = (lg.reshape(D_HEADS, ts, n) * rinv[:, None, :]).reshape(nrow, n)
        if sel is not None:
            lg = jnp.where(sel, lg, MASK_NEG)
        m_old = m_ref[...]
        m_new = jnp.maximum(m_old, jnp.max(lg, axis=-1, keepdims=True))
        alpha = jnp.exp(m_old - m_new)
        p = jnp.exp(lg - m_new)
        if sel is not None:
            p = jnp.where(sel, p, 0.0)
        l_ref[...] = alpha * l_ref[...] + jnp.sum(p, axis=-1, keepdims=True)
        acc_ref[...] = alpha * acc_ref[...] + jnp.dot(p.astype(BF16), cb, preferred_element_type=F32)
        m_ref[...] = m_new

    c = jnp.concatenate([p[...] for p in cpages], axis=0)
    r = jnp.concatenate([p[...] for p in rpages], axis=0)
    update(c, r, None)

    @pl.when(j == nj - 1)
    def _():
        lane = lax.broadcasted_iota(jnp.int32, (D_HEADS, ts, LANES), 2).reshape(nrow, LANES)
        row_t = lax.broadcasted_iota(jnp.int32, (D_HEADS, ts, LANES), 1).reshape(nrow, LANES)
        update(cnew_ref[...], rnew_ref[...], lane <= row_t)
        lat = (acc_ref[...] / l_ref[...]).astype(BF16)
        o_ref[...] = jnp.concatenate(
            [jnp.dot(lat[ts * h:ts * (h + 1)], wv_ref[h], preferred_element_type=F32) for h in range(D_HEADS)],
            axis=1)


def mla_sample_attend(qlat, qpe, cnew, rnew, wkt, wv, pool_ckv, pool_kpe, pt_flat, layer, *, nb, ts, n_pages, pp=8):
    nj = n_pages // pp
    nrow = D_HEADS * ts

    def cspec(r):
        return pl.BlockSpec((None, None, PAGE_SIZE, KV_LORA), lambda b, j, pt: (layer, pt[b * n_pages + j * pp + r], 0, 0))

    def rspec(r):
        return pl.BlockSpec((None, None, PAGE_SIZE, ROPE_DIM), lambda b, j, pt: (layer, pt[b * n_pages + j * pp + r], 0, 0))

    grid_spec = pltpu.PrefetchScalarGridSpec(
        num_scalar_prefetch=1,
        grid=(nb, nj),
        in_specs=[
            pl.BlockSpec((None, nrow, KV_LORA), lambda b, j, pt: (b, 0, 0)),
            pl.BlockSpec((None, nrow, ROPE_DIM), lambda b, j, pt: (b, 0, 0)),
            pl.BlockSpec((None, LANES, KV_LORA), lambda b, j, pt: (b, 0, 0)),
            pl.BlockSpec((None, LANES, ROPE_DIM), lambda b, j, pt: (b, 0, 0)),
            pl.BlockSpec((D_HEADS * NOPE_DIM, KV_LORA), lambda b, j, pt: (0, 0)),
            pl.BlockSpec((D_HEADS, KV_LORA, D_VDIM), lambda b, j, pt: (0, 0, 0)),
        ] + [cspec(r) for r in range(pp)] + [rspec(r) for r in range(pp)],
        out_specs=pl.BlockSpec((ts, D_HEADS * D_VDIM), lambda b, j, pt: (b, 0)),
        scratch_shapes=[pltpu.VMEM((nrow, 1), F32), pltpu.VMEM((nrow, 1), F32), pltpu.VMEM((nrow, KV_LORA), F32)],
    )
    return pl.pallas_call(
        functools.partial(_mla_sample_body, pp=pp, nj=nj, ts=ts),
        grid_spec=grid_spec,
        out_shape=jax.ShapeDtypeStruct((nb * ts, D_HEADS * D_VDIM), F32),
        compiler_params=pltpu.CompilerParams(
            dimension_semantics=("parallel", "arbitrary"), vmem_limit_bytes=VMEM_LIMIT),
        name="mla_sample",
    )(pt_flat, qlat, qpe, cnew, rnew, wkt, wv, *([pool_ckv] * pp), *([pool_kpe] * pp))


def _mem_attn_body(q_ref, k_ref, v_ref, qn_ref, o_ref, *, nb, tq):
    for b in range(nb):
        k = k_ref[b].astype(BF16)
        v = v_ref[b].astype(BF16)
        outs = []
        for h in range(MEM_HEADS):
            q = q_ref[tq * b:tq * (b + 1), LANES * h:LANES * (h + 1)]
            q = q * lax.rsqrt(jnp.mean(q * q, axis=-1, keepdims=True) + EPS) * qn_ref[...]
            lg = _dot_nt(q.astype(BF16), k[:, LANES * h:LANES * (h + 1)]) * (HEAD_DIM ** -0.5)
            p = _softmax_rows(lg)
            outs.append(jnp.dot(p.astype(BF16), v[:, LANES * h:LANES * (h + 1)], preferred_element_type=F32))
        o_ref[tq * b:tq * (b + 1), :] = jnp.concatenate(outs, axis=1)


def mem_attend(q, mk, mv, qn, *, row0, n_batch, tq, nb, k_index, v_index):
    n_mem = mk.shape[-2]
    w = MEM_HEADS * HEAD_DIM
    rb0 = row0 // (nb * tq)
    kblock = (None,) * (mk.ndim - 3) + (nb, n_mem, w)
    vblock = (None,) * (mv.ndim - 3) + (nb, n_mem, w)
    return pl.pallas_call(
        functools.partial(_mem_attn_body, nb=nb, tq=tq),
        grid=(n_batch // nb,),
        in_specs=[pl.BlockSpec((nb * tq, w), lambda i: (rb0 + i, 0)),
                  pl.BlockSpec(kblock, k_index),
                  pl.BlockSpec(vblock, v_index),
                  pl.BlockSpec((1, HEAD_DIM), lambda i: (0, 0))],
        out_specs=pl.BlockSpec((nb * tq, w), lambda i: (i, 0)),
        out_shape=jax.ShapeDtypeStruct((n_batch * tq, w), F32),
        compiler_params=pltpu.CompilerParams(dimension_semantics=("parallel",), vmem_limit_bytes=VMEM_LIMIT),
        name="mem_attend",
    )(q, mk, mv, qn.reshape(1, HEAD_DIM))


def _causal_attn_body(q_ref, k_ref, v_ref, o_ref, *, qb, t, scale):
    i = pl.program_id(2)
    lg = _dot_nt(q_ref[...], k_ref[...]) * scale
    q_pos = i * qb + lax.broadcasted_iota(jnp.int32, (qb, t), 0)
    k_pos = lax.broadcasted_iota(jnp.int32, (qb, t), 1)
    p = _softmax_rows(jnp.where(k_pos <= q_pos, lg, MASK_NEG))
    o_ref[...] = jnp.dot(p.astype(BF16), v_ref[...], preferred_element_type=F32)


def causal_attend(q, k, v, *, nb, t, heads, dv, scale, qb=512):
    nq = t // qb
    dqk = q.shape[1] // heads
    return pl.pallas_call(
        functools.partial(_causal_attn_body, qb=qb, t=t, scale=scale),
        grid=(nb, heads, nq),
        in_specs=[pl.BlockSpec((qb, dqk), lambda b, h, i: (b * nq + i, h)),
                  pl.BlockSpec((t, dqk), lambda b, h, i: (b, h)),
                  pl.BlockSpec((t, dv), lambda b, h, i: (b, h))],
        out_specs=pl.BlockSpec((qb, dv), lambda b, h, i: (b * nq + i, h)),
        out_shape=jax.ShapeDtypeStruct((nb * t, heads * dv), F32),
        compiler_params=pltpu.CompilerParams(
            dimension_semantics=("parallel", "parallel", "arbitrary"), vmem_limit_bytes=VMEM_LIMIT),
        name="causal_attend",
    )(q, k, v)


def _cumsum_rows(x):
    c = x.shape[0]
    row = lax.broadcasted_iota(jnp.int32, x.shape, 0)
    sh = 1
    while sh < c:
        x = x + jnp.where(row >= sh, pltpu.roll(x, sh, axis=0), 0.0)
        sh *= 2
    return x


def _log_sigmoid(x):
    return jnp.minimum(x, 0.0) - jnp.log1p(jnp.exp(-jnp.abs(x)))


def _sigmoid(x):
    return 1.0 / (1.0 + jnp.exp(-x))


def _hgrn2_seq(q_ref, f_ref, v_ref, g_ref, lb, on, st0, o_ref, *, t, c):
    log_lb = jnp.log(jnp.maximum(lb, LB_FLOOR))
    log_1mlb = jnp.log1p(-lb)
    row = lax.broadcasted_iota(jnp.int32, (c, 1), 0)

    def step(i, st):
        r0 = pl.multiple_of(i * c, c)
        fpre = f_ref[pl.ds(r0, c), :]
        a = log_lb
        b = log_1mlb + _log_sigmoid(fpre)
        hi = jnp.maximum(a, b)
        lf = hi + jnp.log1p(jnp.exp(-jnp.abs(a - b)))
        kk = (1.0 - lb) * _sigmoid(-fpre)
        qpre = q_ref[pl.ds(r0, c), :]
        qv = qpre * _sigmoid(qpre)
        vv = v_ref[pl.ds(r0, c), :]
        cb = _cumsum_rows(lf)
        o = _dot_nt((qv * jnp.exp(cb)).astype(BF16), st.astype(BF16))
        for s in range(c):
            dec = jnp.exp(jnp.minimum(cb - cb[s:s + 1, :], 0.0))
            col = jnp.sum(qv * dec * kk[s:s + 1, :], axis=-1, keepdims=True)
            o = o + jnp.where(row >= s, col, 0.0) * vv[s:s + 1, :]
        c_last = cb[c - 1:c, :]
        kd = (kk * jnp.exp(c_last - cb)).astype(BF16)
        st = st * jnp.exp(c_last) + lax.dot_general(vv.astype(BF16), kd, (((0,), (0,)), ((), ())),
                                                    preferred_element_type=F32)
        gpre = g_ref[pl.ds(r0, c), :]
        o = o * lax.rsqrt(jnp.mean(o * o, axis=-1, keepdims=True) + EPS) * on * (gpre * _sigmoid(gpre))
        o_ref[pl.ds(r0, c), :] = o
        return st

    return lax.fori_loop(0, t // c, step, st0)


def _hgrn2_prompt_body(q_ref, f_ref, v_ref, g_ref, lb_ref, on_ref, o_ref, s_ref, *, t, c):
    st = _hgrn2_seq(q_ref, f_ref, v_ref, g_ref, lb_ref[...], on_ref[...], jnp.zeros((B_DV, B_DK), F32), o_ref,
                    t=t, c=c)
    s_ref[...] = st.T


def hgrn2_prompt(proj, lb, on, *, nb, t):
    c = math.gcd(16, t)
    col = lambda c0: (lambda b, h: (b, c0 // LANES + h))
    return pl.pallas_call(
        functools.partial(_hgrn2_prompt_body, t=t, c=c),
        grid=(nb, B_HEADS),
        in_specs=[pl.BlockSpec((t, LANES), col(EV_QB)), pl.BlockSpec((t, LANES), col(EV_FB)),
                  pl.BlockSpec((t, LANES), col(EV_IB)), pl.BlockSpec((t, LANES), col(EV_GB)),
                  pl.BlockSpec((None, 1, B_DK), lambda b, h: (h, 0, 0)),
                  pl.BlockSpec((1, B_DV), lambda b, h: (0, 0))],
        out_specs=[pl.BlockSpec((t, LANES), lambda b, h: (b, h)),
                   pl.BlockSpec((None, None, B_DK, B_DV), lambda b, h: (b, h, 0, 0))],
        out_shape=[jax.ShapeDtypeStruct((nb * t, B_HEADS * B_DV), F32),
                   jax.ShapeDtypeStruct((nb, B_HEADS, B_DK, B_DV), F32)],
        compiler_params=pltpu.CompilerParams(
            dimension_semantics=("parallel", "parallel"), vmem_limit_bytes=VMEM_LIMIT),
        name="hgrn2_prompt",
    )(proj, proj, proj, proj, lb.reshape(B_HEADS, 1, B_DK), on.reshape(1, B_DV))


def _hgrn2_sample_body(q_ref, f_ref, v_ref, g_ref, lb_ref, on_ref, s0_ref, o_ref, s_ref, *, t):
    for h in range(B_HEADS):
        sl = slice(LANES * h, LANES * (h + 1))
        st = _hgrn2_seq(q_ref.at[:, sl], f_ref.at[:, sl], v_ref.at[:, sl], g_ref.at[:, sl], lb_ref[h], on_ref[...],
                        s0_ref[h].T, o_ref.at[:, sl], t=t, c=t)
        s_ref[h] = st.T


def hgrn2_sample(proj, lb, on, s0, *, row0, nb, t):
    rb0 = row0 // t
    w = B_HEADS * LANES
    col = lambda c0: (lambda b: (rb0 + b, c0 // w))
    return pl.pallas_call(
        functools.partial(_hgrn2_sample_body, t=t),
        grid=(nb,),
        in_specs=[pl.BlockSpec((t, w), col(EV_QB)), pl.BlockSpec((t, w), col(EV_FB)),
                  pl.BlockSpec((t, w), col(EV_IB)), pl.BlockSpec((t, w), col(EV_GB)),
                  pl.BlockSpec((B_HEADS, 1, B_DK), lambda b: (0, 0, 0)),
                  pl.BlockSpec((1, B_DV), lambda b: (0, 0)),
                  pl.BlockSpec((None, B_HEADS, B_DK, B_DV), lambda b: (b, 0, 0, 0))],
        out_specs=[pl.BlockSpec((t, w), lambda b: (b, 0)),
                   pl.BlockSpec((None, B_HEADS, B_DK, B_DV), lambda b: (b, 0, 0, 0))],
        out_shape=[jax.ShapeDtypeStruct((nb * t, w), F32),
                   jax.ShapeDtypeStruct((nb, B_HEADS, B_DK, B_DV), F32)],
        compiler_params=pltpu.CompilerParams(dimension_semantics=("parallel",), vmem_limit_bytes=VMEM_LIMIT),
        name="hgrn2_sample",
    )(proj, proj, proj, proj, lb.reshape(B_HEADS, 1, B_DK), on.reshape(1, B_DV), s0)


def _mlstm_chunk(qm, km_h, v, ic_row, ic_col, cb_col, cb_row, ct, n, m, c):
    r = lax.broadcasted_iota(jnp.int32, (c, c), 0)
    s = lax.broadcasted_iota(jnp.int32, (c, c), 1)
    dmat = jnp.where(r >= s, cb_col - cb_row + ic_row, MASK_NEG)
    m_state = cb_col + m
    m_t = jnp.maximum(m_state, jnp.max(dmat, axis=-1, keepdims=True))
    w = jnp.exp(dmat - m_t)
    w0 = jnp.exp(m_state - m_t)
    qb16 = qm.astype(BF16)
    qk = _dot_nt(qb16, km_h.astype(BF16)) * w
    num = (jnp.dot(qk.astype(BF16), v.astype(BF16), preferred_element_type=F32)
           + w0 * jnp.dot(qb16, ct.astype(BF16), preferred_element_type=F32))
    den = jnp.sum(qk, axis=-1, keepdims=True) + w0 * jnp.sum(qm * n, axis=-1, keepdims=True)
    hc = num / jnp.maximum(jnp.abs(den), jnp.exp(-m_t))
    m_last = m_t[c - 1:c, :]
    cb_last = cb_col[c - 1:c, :]
    ws = jnp.exp(cb_last - cb_col + ic_col - m_last)
    fs = jnp.exp(cb_last + m - m_last)
    ct = fs * ct + lax.dot_general(km_h.astype(BF16), (v * ws).astype(BF16), (((0,), (0,)), ((), ())),
                                   preferred_element_type=F32)
    n = fs * n + jnp.sum(ws * km_h, axis=0, keepdims=True)
    return hc, ct, n, m_last


def _mlstm_seq(q_ref, k_ref, v_ref, og_ref, g_ref, gb_ref, on, npairs, states, o_ref, *, t, c):
    lane = lax.broadcasted_iota(jnp.int32, (c, LANES), 1)

    def step(i, carry):
        r0 = pl.multiple_of(i * c, c)
        new = []
        for p in range(npairs):
            psl = slice(LANES * p, LANES * (p + 1))
            g = g_ref[pl.ds(r0, c), psl] + gb_ref[p]
            cb = _cumsum_rows(_log_sigmoid(g))
            gt = g.T
            cbt = cb.T
            q = q_ref[pl.ds(r0, c), psl]
            k = k_ref[pl.ds(r0, c), psl] * (C_DK ** -0.5)
            for half in range(2):
                ct, n, m = carry[2 * p + half]
                keep = (lane < C_DK) if half == 0 else (lane >= C_DK)
                vsl = slice(C_DV * (2 * p + half), C_DV * (2 * p + half + 1))
                hc, ct, n, m = _mlstm_chunk(
                    jnp.where(keep, q, 0.0), jnp.where(keep, k, 0.0), v_ref[pl.ds(r0, c), vsl],
                    gt[half:half + 1, :], g[:, half:half + 1], cb[:, 2 + half:3 + half], cbt[2 + half:3 + half, :],
                    ct, n, m, c)
                og = og_ref[pl.ds(r0, c), vsl]
                hc = hc * lax.rsqrt(jnp.mean(hc * hc, axis=-1, keepdims=True) + EPS) * on * _sigmoid(og)
                o_ref[pl.ds(r0, c), vsl] = hc
                new.append((ct, n, m))
        return tuple(new)

    return lax.fori_loop(0, t // c, step, tuple(states))


def _mlstm_body(*refs, t, c, npairs, has_state):
    q_ref, k_ref, v_ref, og_ref, g_ref, gb_ref, on_ref = refs[:7]
    refs = refs[7:]
    if has_state:
        c0_ref, n0_ref, m0_ref = refs[:3]
        refs = refs[3:]
    o_ref, c_ref, n_ref, m_ref = refs
    states = []
    for h in range(2 * npairs):
        half = h % 2
        if has_state:
            ct_h = c0_ref[h].T
            z = jnp.zeros((C_DK, C_DV), F32)
            ct = jnp.concatenate([ct_h, z] if half == 0 else [z, ct_h], axis=0)
            n_h = n0_ref[h:h + 1, :]
            zn = jnp.zeros((1, C_DK), F32)
            n = jnp.concatenate([n_h, zn] if half == 0 else [zn, n_h], axis=1)
            m = m0_ref[:, h:h + 1]
        else:
            ct, n, m = jnp.zeros((2 * C_DK, C_DV), F32), jnp.zeros((1, 2 * C_DK), F32), jnp.zeros((1, 1), F32)
        states.append((ct, n, m))
    out = _mlstm_seq(q_ref, k_ref, v_ref, og_ref, g_ref, gb_ref, on_ref[...], npairs, states, o_ref, t=t, c=c)
    for h, (ct, n, m) in enumerate(out):
        rows = slice(C_DK * (h % 2), C_DK * (h % 2 + 1))
        c_ref[h] = ct[rows, :].T
        n_ref[h] = n[:, rows]
        m_ref[h] = jnp.broadcast_to(m, (1, LANES))


def mlstm(proj, gb, on, state, *, row0, nb, t, npairs):
    c = math.gcd(CHUNK, t)
    rb0 = row0 // t
    npg = (C_HEADS // 2) // npairs
    hp = 2 * npairs
    qw, vw = npairs * LANES, npairs * 2 * C_DV
    in_specs = [
        pl.BlockSpec((t, qw), lambda b, j: (rb0 + b, OD_QC // qw + j)),
        pl.BlockSpec((t, qw), lambda b, j: (rb0 + b, OD_KC // qw + j)),
        pl.BlockSpec((t, vw), lambda b, j: (rb0 + b, OD_VC // vw + j)),
        pl.BlockSpec((t, vw), lambda b, j: (rb0 + b, OD_OC // vw + j)),
        pl.BlockSpec((t, qw), lambda b, j: (rb0 + b, OD_G // qw + j)),
        pl.BlockSpec((npairs, 1, LANES), lambda b, j: (j, 0, 0)),
        pl.BlockSpec((1, C_DV), lambda b, j: (0, 0)),
    ]
    args = [proj, proj, proj, proj, proj, gb, on.reshape(1, C_DV)]
    if state is not None:
        in_specs += [pl.BlockSpec((None, hp, C_DV, C_DK), lambda b, j: (b, j, 0, 0)),
                     pl.BlockSpec((None, hp, C_DK), lambda b, j: (b, j, 0)),
                     pl.BlockSpec((None, 1, hp), lambda b, j: (b, 0, j))]
        args += list(state)
    return pl.pallas_call(
        functools.partial(_mlstm_body, t=t, c=c, npairs=npairs, has_state=state is not None),
        grid=(nb, npg),
        in_specs=in_specs,
        out_specs=[pl.BlockSpec((t, vw), lambda b, j: (b, j)),
                   pl.BlockSpec((None, hp, C_DV, C_DK), lambda b, j: (b, j, 0, 0)),
                   pl.BlockSpec((None, hp, 1, C_DK), lambda b, j: (b, j, 0, 0)),
                   pl.BlockSpec((None, hp, 1, LANES), lambda b, j: (b, j, 0, 0))],
        out_shape=[jax.ShapeDtypeStruct((nb * t, C_HEADS * C_DV), F32),
                   jax.ShapeDtypeStruct((nb, C_HEADS, C_DV, C_DK), F32),
                   jax.ShapeDtypeStruct((nb, C_HEADS, 1, C_DK), F32),
                   jax.ShapeDtypeStruct((nb, C_HEADS, 1, LANES), F32)],
        compiler_params=pltpu.CompilerParams(
            dimension_semantics=("parallel", "parallel"), vmem_limit_bytes=VMEM_LIMIT),
        name="mlstm",
    )(*args)


def rms_norm(x, g):
    xf = x.astype(F32)
    y = xf * lax.rsqrt(jnp.mean(xf * xf, axis=-1, keepdims=True) + EPS)
    return (y * g.astype(F32)).astype(x.dtype)


def split_cols(h, sizes):
    cuts = [int(c) for c in np.cumsum(sizes)[:-1]]
    return jnp.split(h, cuts, axis=-1)


def to_blocks(x, nb):
    b, t = x.shape[:2]
    return jnp.swapaxes(x.reshape((b, nb, t // nb) + x.shape[2:]), 0, 1)


def from_blocks(x):
    x = jnp.swapaxes(x, 0, 1)
    return x.reshape((x.shape[0], x.shape[1] * x.shape[2]) + x.shape[3:])


def rel_bucket(rel):
    n = jnp.maximum(rel, 0)
    max_exact = REL_BUCKETS // 2
    nf = jnp.maximum(n, 1).astype(F32)
    large = max_exact + (jnp.log(nf / max_exact) / math.log(REL_MAX_DIST / max_exact)
                         * (REL_BUCKETS - max_exact)).astype(jnp.int32)
    return jnp.where(n < max_exact, n, jnp.minimum(large, REL_BUCKETS - 1))


def rope(x, pos):
    half = ROPE_DIM // 2
    inv = ROPE_THETA ** (-jnp.arange(half, dtype=F32) / half)
    ang = pos.astype(F32)[:, None] * inv[None, :]
    shp = (1, pos.shape[0]) + (1,) * (x.ndim - 3) + (half,)
    cos, sin = jnp.cos(ang).reshape(shp), jnp.sin(ang).reshape(shp)
    xf = x.astype(F32)
    x1, x2 = xf[..., :half], xf[..., half:]
    return jnp.concatenate([x1 * cos - x2 * sin, x2 * cos + x1 * sin], axis=-1).astype(x.dtype)


def dsa_select(q_idx, w_idx, k_idx, q_pos, k_pos, n_sel):
    s = jnp.einsum('bthd,bsd->bths', q_idx.astype(F32), k_idx.astype(F32))
    score = jnp.einsum('bths,bth->bts', jax.nn.relu(s), w_idx.astype(F32))
    allowed = k_pos[None, None, :] <= q_pos[None, :, None]
    score = jnp.where(allowed, score, MASK_NEG)
    _, sel = lax.top_k(score, n_sel)
    valid = sel <= q_pos[None, :, None]
    return sel, valid


def dsa_attend(q, kg, vg, q_pos, sel, valid, rel_bias):
    b, t, h, dh = q.shape
    g = h // A_KV_HEADS
    n_sel = sel.shape[-1]
    qg = q.reshape(b, t, A_KV_HEADS, g, dh)
    lg = jnp.einsum('btkgd,btskd->btkgs', qg, kg).astype(F32) * (dh ** -0.5)
    bias = rel_bias.astype(F32)[rel_bucket(q_pos[None, :, None] - sel)]
    lg = lg + jnp.moveaxis(bias, -1, 2).reshape(b, t, A_KV_HEADS, g, n_sel)
    lg = jnp.where(valid[:, :, None, None, :], lg, MASK_NEG)
    pr = jax.nn.softmax(lg, axis=-1).astype(vg.dtype)
    return jnp.einsum('btkgs,btskd->btkgd', pr, vg).reshape(b, t, h, dh)


def dsa_prompt(q, k, v, q_idx, w_idx, k_idx, rel_bias):
    b, t = q.shape[:2]
    n_sel = min(TOPK_MAX, t // 4)
    qb = math.gcd(Q_BLOCK, t)
    nb = t // qb
    k_pos = jnp.arange(t)
    bidx = jnp.arange(b)[:, None, None]

    def block(args):
        q_b, qi_b, wi_b, qp = args
        sel, valid = dsa_select(qi_b, wi_b, k_idx, qp, k_pos, n_sel)
        return dsa_attend(q_b, k[bidx, sel], v[bidx, sel], qp, sel, valid, rel_bias)

    out = lax.map(block, (to_blocks(q, nb), to_blocks(q_idx, nb), to_blocks(w_idx, nb), k_pos.reshape(nb, qb)))
    return from_blocks(out)


def gather_paged_rows(pool, layer, new_rows, page_table, sel, past):
    b = sel.shape[0]
    in_past = sel < past
    ps = jnp.minimum(sel, past - 1)
    phys = jnp.take_along_axis(page_table, (ps // PAGE_SIZE).reshape(b, -1), axis=1).reshape(sel.shape)
    past_rows = pool[layer, phys, ps % PAGE_SIZE]
    new_idx = jnp.clip(sel - past, 0, new_rows.shape[1] - 1)
    cur_rows = new_rows[jnp.arange(b)[:, None, None], new_idx]
    cond = in_past.reshape(in_past.shape + (1,) * (new_rows.ndim - 2))
    return jnp.where(cond, past_rows, cur_rows)


def dsa_sample(q, k_new, v_new, q_idx, w_idx, kidx_new, pool_k, pool_v, pool_kidx, layer, page_table, rel_bias):
    b, t = q.shape[:2]
    past = page_table.shape[1] * PAGE_SIZE
    n_sel = min(TOPK_MAX, (past + t) // 4)
    kidx_past = pool_kidx[layer, page_table].reshape(b, past, IDX_DIM)
    kidx_all = jnp.concatenate([kidx_past, kidx_new.astype(kidx_past.dtype)], axis=1)
    q_pos = past + jnp.arange(t)
    sel, valid = dsa_select(q_idx, w_idx, kidx_all, q_pos, jnp.arange(past + t), n_sel)
    kg = gather_paged_rows(pool_k, layer, k_new, page_table, sel, past)
    vg = gather_paged_rows(pool_v, layer, v_new, page_table, sel, past)
    return dsa_attend(q, kg, vg, q_pos, sel, valid, rel_bias)


def gla_chunked(q, k, v, log_f, s0):
    b, t, h, dk = q.shape
    c = math.gcd(CHUNK, t)
    nc = t // c
    tri = jnp.tril(jnp.ones((c, c), dtype=bool))

    def step(s, inp):
        qc, kc, vc, lf = inp
        cb = jnp.cumsum(lf, axis=1)
        diff = jnp.where(tri[None, :, :, None, None], cb[:, :, None] - cb[:, None, :], MASK_NEG)
        attn = jnp.einsum('bthd,btshd->bhts', qc, jnp.exp(diff) * kc[:, None])
        o = jnp.einsum('bhts,bshv->bthv', attn, vc) + jnp.einsum('bthd,bhdv->bthv', qc * jnp.exp(cb), s)
        c_last = cb[:, -1]
        s = jnp.exp(c_last)[..., None] * s + jnp.einsum('bshd,bshv->bhdv', kc * jnp.exp(c_last[:, None] - cb), vc)
        return s, o

    xs = tuple(to_blocks(a.astype(F32), nc) for a in (q, k, v, log_f))
    s, o = lax.scan(step, s0.astype(F32), xs)
    return from_blocks(o).astype(v.dtype), s


def mlstm_chunked(q, k, v, i_pre, f_pre, c0, n0, m0):
    b, t, h, dk = q.shape
    c = math.gcd(CHUNK, t)
    nc = t // c
    tri = jnp.tril(jnp.ones((c, c), dtype=bool))
    log_f = jax.nn.log_sigmoid(f_pre.astype(F32))

    def step(carry, inp):
        cm, n, m = carry
        qc, kc, vc, ic, lf = inp
        cb = jnp.cumsum(lf, axis=1)
        dmat = jnp.where(tri[None, :, :, None], cb[:, :, None] - cb[:, None] + ic[:, None], MASK_NEG)
        m_state = cb + m[:, None]
        m_t = jnp.maximum(m_state, jnp.max(dmat, axis=2))
        w = jnp.exp(dmat - m_t[:, :, None])
        w0 = jnp.exp(m_state - m_t)
        qk = jnp.einsum('bthd,bshd->btsh', qc, kc) * w
        num = jnp.einsum('btsh,bshv->bthv', qk, vc) + w0[..., None] * jnp.einsum('bthd,bhvd->bthv', qc, cm)
        den = jnp.sum(qk, axis=2) + w0 * jnp.einsum('bthd,bhd->bth', qc, n)
        hc = num / jnp.maximum(jnp.abs(den), jnp.exp(-m_t))[..., None]
        m_last = m_t[:, -1]
        ws = jnp.exp(cb[:, -1:] - cb + ic - m_last[:, None])
        fs = jnp.exp(cb[:, -1] + m - m_last)
        cm = fs[..., None, None] * cm + jnp.einsum('bshv,bshd->bhvd', vc * ws[..., None], kc)
        n = fs[..., None] * n + jnp.einsum('bsh,bshd->bhd', ws, kc)
        return (cm, n, m_last), hc

    xs = tuple(to_blocks(a.astype(F32), nc) for a in (q, k, v, i_pre, log_f))
    (cm, n, m), hs = lax.scan(step, (c0.astype(F32), n0.astype(F32), m0.astype(F32)), xs)
    return from_blocks(hs).astype(v.dtype), (cm, n, m)


def mla_kv(ckv, kpe, w_kvb, kn):
    b, l, _ = ckv.shape
    kv = (ckv @ w_kvb).reshape(b, l, D_HEADS, NOPE_DIM + D_VDIM)
    k_nope, v = kv[..., :NOPE_DIM], kv[..., NOPE_DIM:]
    k_pe = jnp.broadcast_to(kpe[:, :, None, :], (b, l, D_HEADS, ROPE_DIM)).astype(k_nope.dtype)
    return rms_norm(jnp.concatenate([k_nope, k_pe], axis=-1), kn), v


def causal_attn(q, k, v, q_pos, k_pos):
    b, t, h, dq = q.shape
    scale = dq ** -0.5

    def attend(args):
        q_b, qp = args
        lg = jnp.einsum('bthd,bshd->bhts', q_b, k).astype(F32) * scale
        lg = jnp.where(k_pos[None, None, None, :] <= qp[None, None, :, None], lg, MASK_NEG)
        pr = jax.nn.softmax(lg, axis=-1).astype(v.dtype)
        return jnp.einsum('bhts,bshd->bthd', pr, v)

    qb = math.gcd(Q_BLOCK, t)
    nb = t // qb
    if nb == 1:
        return attend((q, q_pos))
    return from_blocks(lax.map(attend, (to_blocks(q, nb), q_pos.reshape(nb, qb))))


def mla_sample(qd, ckv_new, kpe_new, pool_ckv, pool_kpe, layer, page_table, w_kvb, kn):
    b, t = qd.shape[:2]
    past = page_table.shape[1] * PAGE_SIZE
    q_pos = past + jnp.arange(t)
    k_pos = jnp.arange(past + t)

    def one(args):
        q1, c1, r1, pt = args
        c_all = jnp.concatenate([pool_ckv[layer, pt].reshape(past, KV_LORA), c1.astype(pool_ckv.dtype)], axis=0)[None]
        r_all = jnp.concatenate([pool_kpe[layer, pt].reshape(past, ROPE_DIM), r1.astype(pool_kpe.dtype)], axis=0)[None]
        k, v = mla_kv(c_all, r_all, w_kvb, kn)
        return causal_attn(q1[None], k, v, q_pos, k_pos)[0]

    return lax.map(one, (qd, ckv_new, kpe_new, page_table))


def mem_attend_core(q, mk, mv, qn):
    b, t, _ = q.shape
    q = rms_norm(q.reshape(b, t, MEM_HEADS, HEAD_DIM), qn)
    lg = jnp.einsum('bthd,bshd->bhts', q, mk.astype(q.dtype)).astype(F32) * (HEAD_DIM ** -0.5)
    pr = jax.nn.softmax(lg, axis=-1).astype(q.dtype)
    o = jnp.einsum('bhts,bshd->bthd', pr, mv.astype(q.dtype))
    return o.reshape(b, t, MEM_HEADS * HEAD_DIM)


def hgrn2_mixer(proj, s0, b_on, lower_bound):
    b, t, _ = proj.shape
    qb, fb, ib, gb = (proj[..., c:c + B_HEADS * B_DK] for c in (EV_QB, EV_FB, EV_IB, EV_GB))
    lb = lower_bound.reshape(B_HEADS, B_DK)
    f_pre = fb.reshape(b, t, B_HEADS, B_DK).astype(F32)
    log_f = jnp.logaddexp(jnp.log(jnp.maximum(lb, LB_FLOOR)), jnp.log1p(-lb) + jax.nn.log_sigmoid(f_pre))
    k_b = (1.0 - lb) * jax.nn.sigmoid(-f_pre)
    q_b = jax.nn.silu(qb.reshape(b, t, B_HEADS, B_DK))
    ob, s_new = gla_chunked(q_b, k_b, ib.reshape(b, t, B_HEADS, B_DV), log_f, s0)
    ob = rms_norm(ob, b_on) * jax.nn.silu(gb.reshape(b, t, B_HEADS, B_DV))
    return ob.reshape(b, t, -1), s_new


def dsa_bias_tables(rel_bias, ts, past):
    table = rel_bias.astype(F32)[rel_bucket(jnp.arange(2 * LANES))]
    r = np.arange(LANES)[:, None]
    c = np.arange(LANES)[None, :]
    diag = table[np.clip(r - c, 0, 2 * LANES - 1)]
    prev = table[LANES + r - c]
    far = jnp.broadcast_to(table[2 * LANES - 1], (LANES, LANES, A_HEADS))
    tiles = jnp.moveaxis(jnp.stack([diag, prev, far]), -1, 0)
    rel = past + np.arange(ts)[:, None] - np.arange(past + LANES)[None, :]
    sample = jnp.moveaxis(table[np.clip(rel, 0, 2 * LANES - 1)], -1, 0)
    return tiles, sample


def rope_rows(x, pos):
    half = ROPE_DIM // 2
    inv = ROPE_THETA ** (-jnp.arange(half, dtype=F32) / half)
    ang = pos.astype(F32)[:, None] * inv[None, :]
    shp = (pos.shape[0],) + (1,) * (x.ndim - 2) + (half,)
    cos, sin = jnp.cos(ang).reshape(shp), jnp.sin(ang).reshape(shp)
    x1, x2 = x[..., :half], x[..., half:]
    return jnp.concatenate([x1 * cos - x2 * sin, x2 * cos + x1 * sin], axis=-1)


def mla_sample_pallas(qd, ckv, kpe, st, o, kn, wb):
    b, t = qd.shape[:2]
    qn = (qd[..., :NOPE_DIM] * kn[:NOPE_DIM]).reshape(b * t, D_HEADS * NOPE_DIM)
    qlat = matmul(qn, wb['d_kt_blockdiag'][o])
    qlat = qlat.reshape(b, t, D_HEADS, KV_LORA).transpose(0, 2, 1, 3).reshape(b, D_HEADS * t, KV_LORA)
    qpe = (qd[..., NOPE_DIM:] * kn[NOPE_DIM:]).transpose(0, 2, 1, 3).reshape(b, D_HEADS * t, ROPE_DIM)
    cnew = jnp.pad(ckv, ((0, 0), (0, LANES - t), (0, 0)))
    rnew = jnp.pad(kpe, ((0, 0), (0, LANES - t), (0, 0)))
    od = mla_sample_attend(qlat.astype(BF16), qpe.astype(BF16), cnew, rnew, wb['d_kt'][o], wb['d_v'][o],
                           st['cache_d_ckv'], st['cache_d_kpe'], st['pt_flat'], o,
                           nb=b, ts=t, n_pages=st['page_table'].shape[1])
    return od.reshape(b, t, D_HEADS, D_VDIM)


def odd_mixer(proj, pos, mode, st, p, o, wb):
    b, t, _ = proj.shape
    qc, kc, vc, ic, fc, oc, qa, ckv, kpe = split_cols(proj, ODD_SPLITS)
    gate_b = p['c_gate_b'][o].astype(F32)
    i_pre = ic.astype(F32) + gate_b[0]
    f_pre = fc.astype(F32) + gate_b[1]
    q_c = qc.reshape(b, t, C_HEADS, C_DK)
    k_c = kc.reshape(b, t, C_HEADS, C_DK) * (C_DK ** -0.5)
    v_c = vc.reshape(b, t, C_HEADS, C_DV)
    if mode == 'prompt':
        c0 = jnp.zeros((b, C_HEADS, C_DV, C_DK), F32)
        n0 = jnp.zeros((b, C_HEADS, C_DK), F32)
        m0 = jnp.zeros((b, C_HEADS), F32)
    else:
        c0, n0, m0 = st['state_c_C'][o], st['state_c_n'][o], st['state_c_m'][o]
    hc, (c1, n1, m1) = mlstm_chunked(q_c, k_c, v_c, i_pre, f_pre, c0, n0, m0)
    hc = rms_norm(hc, p['c_on'][o]) * jax.nn.sigmoid(oc.reshape(b, t, C_HEADS, C_DV))
    qf = matmul(qa.reshape(b * t, Q_LORA), wb['w_d_qb'][o], gain=p['d_qa_g'][o])
    qf = qf.reshape(b, t, D_HEADS, NOPE_DIM + ROPE_DIM)
    qd = rms_norm(jnp.concatenate([qf[..., :NOPE_DIM], rope(qf[..., NOPE_DIM:], pos)], axis=-1), p['d_qn'][o])
    ckv = rms_norm(ckv, p['d_kv_g'][o])
    kpe = rope(kpe, pos)
    if mode == 'prompt':
        kd, vd = mla_kv(ckv, kpe, p['w_d_kvb'][o], p['d_kn'][o])
        od = causal_attn(qd, kd, vd, pos, pos)
    else:
        od = mla_sample_pallas(qd, ckv, kpe, st, o, p['d_kn'][o], wb)
    mixed = jnp.concatenate([hc.reshape(b, t, -1).astype(od.dtype), od.reshape(b, t, -1)], axis=-1)
    return mixed, (ckv, kpe, c1, n1, m1)


def kernel(x_prompt, x_sample, cache_a_k, cache_a_v, cache_a_kidx, state_b, state_c_C, state_c_n, state_c_m, cache_d_ckv, cache_d_kpe, cache_mem_k, cache_mem_v, page_table, mem_prompt, g_ffn1, w_ffn1_gu, w_ffn1_dn, g_mix, w_in_even, w_in_odd, w_mix_out, rel_bias, a_qn, a_kn, b_lb, b_on, c_gate_b, c_on, d_qa_g, d_kv_g, w_d_qb, w_d_kvb, d_qn, d_kn, g_mem, w_mem_q, w_mem_kv, w_mem_o, mem_qn, mem_kn, g_ffn2, w_ffn2_gu, w_ffn2_dn):
    p = {'rel_bias': rel_bias, 'a_qn': a_qn, 'a_kn': a_kn, 'b_on': b_on, 'c_gate_b': c_gate_b, 'c_on': c_on,
         'd_qa_g': d_qa_g, 'd_kv_g': d_kv_g, 'w_d_kvb': w_d_kvb, 'd_qn': d_qn, 'd_kn': d_kn,
         'mem_qn': mem_qn, 'mem_kn': mem_kn}
    bp, tp, d = x_prompt.shape
    bs, ts, _ = x_sample.shape
    n_p, n_s = bp * tp, bs * ts
    n_mem = mem_prompt.shape[1]
    past = page_table.shape[1] * PAGE_SIZE

    wo = w_in_odd.astype(BF16)
    n_odd = wo.shape[0]
    oc = np.cumsum([0] + ODD_SPLITS)
    gate_cols = []
    for pr in range(C_HEADS // 2):
        gate_cols += [wo[:, :, oc[3] + 2 * pr:oc[3] + 2 * pr + 2], wo[:, :, oc[4] + 2 * pr:oc[4] + 2 * pr + 2],
                      jnp.zeros((n_odd, d, LANES - 4), BF16)]
    in_odd = jnp.concatenate(
        [wo[:, :, oc[0]:oc[3]], wo[:, :, oc[5]:oc[6]]] + gate_cols
        + [wo[:, :, oc[6]:oc[9]], jnp.zeros((n_odd, d, LANES - ROPE_DIM), BF16)], axis=-1)
    gb = c_gate_b.astype(F32)
    gate_bias = jnp.concatenate(
        [gb[:, 0].reshape(n_odd, C_HEADS // 2, 2), gb[:, 1].reshape(n_odd, C_HEADS // 2, 2),
         jnp.zeros((n_odd, C_HEADS // 2, LANES - 4), F32)], axis=-1)[:, :, None, :]
    we = w_in_even.astype(BF16)
    c_wi = sum(EVEN_SPLITS[:4])
    c_ki = c_wi + IDX_HEADS
    c_qb = c_ki + IDX_DIM
    n_even = we.shape[0]
    c_ka = EVEN_SPLITS[0]
    in_even = jnp.concatenate(
        [we[:, :, :c_ka], we[:, :, c_qb:], we[:, :, c_ka:c_wi], we[:, :, c_ki:c_qb], we[:, :, c_ki:c_qb],
         we[:, :, c_wi:c_ki], jnp.zeros((n_even, d, LANES - IDX_HEADS), BF16)], axis=-1)
    kvb = w_d_kvb.astype(BF16).reshape(-1, KV_LORA, D_HEADS, NOPE_DIM + D_VDIM)
    d_kt3 = kvb[..., :NOPE_DIM].transpose(0, 2, 3, 1)
    eye = jnp.eye(D_HEADS, dtype=BF16)
    d_kt_bd = (d_kt3[:, :, :, None, :] * eye[None, :, None, :, None]).reshape(
        -1, D_HEADS * NOPE_DIM, D_HEADS * KV_LORA)
    wb = {
        'ffn1_gu': w_ffn1_gu.astype(BF16), 'ffn1_dn': w_ffn1_dn.astype(BF16),
        'ffn2_gu': w_ffn2_gu.astype(BF16), 'ffn2_dn': w_ffn2_dn.astype(BF16),
        'in_even': in_even,
        'in_odd': in_odd,
        'mix_out': w_mix_out.astype(BF16), 'w_d_qb': w_d_qb.astype(BF16), 'w_d_kvb': w_d_kvb.astype(BF16),
        'mem_q': w_mem_q.astype(BF16), 'mem_kv': w_mem_kv.astype(BF16), 'mem_o': w_mem_o.astype(BF16),
        'd_kt': d_kt3.reshape(-1, D_HEADS * NOPE_DIM, KV_LORA), 'd_kt_blockdiag': d_kt_bd,
        'd_v': kvb[..., NOPE_DIM:].transpose(0, 2, 1, 3),
    }
    bias_tiles, bias_sample = dsa_bias_tables(rel_bias, ts, past)
    n_pages = page_table.shape[1]
    n_phys = cache_a_k.shape[1]
    pool_k = cache_a_k.reshape(-1, n_phys, PAGE_SIZE, A_KV_HEADS * HEAD_DIM)
    pool_v = cache_a_v.reshape(-1, n_phys, PAGE_SIZE, A_KV_HEADS * HEAD_DIM)
    pt_flat = page_table.reshape(-1)

    lb_soft = jax.nn.softmax(b_lb.astype(F32), axis=0)
    lower_bound = jnp.cumsum(lb_soft, axis=0) - lb_soft[0]

    pos_all = jnp.concatenate([jnp.tile(jnp.arange(tp), bp), jnp.tile(past + jnp.arange(ts), bs)])
    st = {'cache_a_k': cache_a_k, 'cache_a_v': cache_a_v, 'cache_a_kidx': cache_a_kidx, 'state_b': state_b,
          'state_c_C': state_c_C, 'state_c_n': state_c_n, 'state_c_m': state_c_m,
          'cache_d_ckv': cache_d_ckv, 'cache_d_kpe': cache_d_kpe, 'page_table': page_table, 'pt_flat': pt_flat}

    x = jnp.concatenate([x_prompt.reshape(n_p, d), x_sample.reshape(n_s, d)], axis=0)
    mem2d = mem_prompt.reshape(bp * n_mem, d)
    new_p = {n: [] for n in ('a_k', 'a_v', 'a_kidx', 'b_S', 'c_C', 'c_n', 'c_m', 'd_ckv', 'd_kpe', 'mem_k', 'mem_v')}
    new_s = {n: [] for n in ('a_k', 'a_v', 'a_kidx', 'b_S', 'c_C', 'c_n', 'c_m', 'd_ckv', 'd_kpe')}

    for l in range(DEPTH):
        x = ffn_half_step(x, g_ffn1[l], wb['ffn1_gu'][l], wb['ffn1_dn'][l])
        if l % 2 == 0:
            e = l // 2
            proj = matmul(x, wb['in_even'][e], gain=g_mix[l], tn=1280)
            ka32, ka16 = head_norm(proj, EV_KA // (A_KV_HEADS * HEAD_DIM), A_KV_HEADS, a_kn[e])
            oa_p = dsa_prompt_attend(proj, ka16, bias_tiles, a_qn[e], nb=bp, t=tp)
            sel_mask = dsa_sample_select(proj, cache_a_kidx, pt_flat, e, row0=n_p, nb=bs, ts=ts, n_pages=n_pages)
            oa_s = dsa_sample_attend(proj, ka32, sel_mask, bias_sample, pool_k, pool_v, pt_flat, e, a_qn[e],
                                     row0=n_p, nb=bs, ts=ts, n_pages=n_pages)
            lb = lower_bound[e].reshape(B_HEADS, B_DK)
            ob_p, s_p = hgrn2_prompt(proj, lb, b_on[e], nb=bp, t=tp)
            ob_s, s_s = hgrn2_sample(proj, lb, b_on[e], state_b[e], row0=n_p, nb=bs, t=ts)
            mixed_p = jnp.concatenate([oa_p, ob_p], axis=-1)
            mixed_s = jnp.concatenate([oa_s, ob_s], axis=-1)
            va = proj[:, EV_VA:EV_VA + A_KV_HEADS * HEAD_DIM]
            ki = proj[:, EV_KIKI:EV_KIKI + IDX_DIM]
            for new, rows, nb_, t_, s_new in ((new_p, slice(0, n_p), bp, tp, s_p), (new_s, slice(n_p, None), bs, ts, s_s)):
                new['a_k'].append(ka32[rows].reshape(nb_, t_, A_KV_HEADS, HEAD_DIM))
                new['a_v'].append(va[rows].reshape(nb_, t_, A_KV_HEADS, HEAD_DIM))
                new['a_kidx'].append(ki[rows].reshape(nb_, t_, IDX_DIM))
                new['b_S'].append(s_new)
        else:
            o = l // 2
            proj = matmul(x, wb['in_odd'][o], gain=g_mix[l], tn=896)
            hc_p, cc_p, cn_p, cm_p = mlstm(proj, gate_bias[o], c_on[o], None, row0=0, nb=bp, t=tp, npairs=1)
            hc_s, cc_s, cn_s, cm_s = mlstm(
                proj, gate_bias[o], c_on[o], (state_c_C[o], state_c_n[o], state_c_m[o].reshape(bs, 1, C_HEADS)),
                row0=n_p, nb=bs, t=ts, npairs=C_HEADS // 2)
            qf = matmul(proj, wb['w_d_qb'][o], gain=d_qa_g[o], a_col=OD_QA // Q_LORA)
            qf = qf.reshape(n_p + n_s, D_HEADS, NOPE_DIM + ROPE_DIM)
            qd = rms_norm(jnp.concatenate([qf[..., :NOPE_DIM], rope_rows(qf[..., NOPE_DIM:], pos_all)], axis=-1),
                          d_qn[o])
            ckv = rms_norm(proj[:, OD_CKV:OD_CKV + KV_LORA], d_kv_g[o])
            kpe = rope_rows(proj[:, OD_KPE:OD_KPE + ROPE_DIM], pos_all)
            kv = matmul(ckv[:n_p], wb['w_d_kvb'][o]).reshape(n_p, D_HEADS, NOPE_DIM + D_VDIM)
            k_pe = jnp.broadcast_to(kpe[:n_p, None, :], (n_p, D_HEADS, ROPE_DIM))
            kd = rms_norm(jnp.concatenate([kv[..., :NOPE_DIM], k_pe], axis=-1), d_kn[o])
            pad = ((0, 0), (0, 0), (0, 2 * LANES - NOPE_DIM - ROPE_DIM))
            q16 = jnp.pad(qd[:n_p], pad).astype(BF16).reshape(n_p, D_HEADS * 2 * LANES)
            k16 = jnp.pad(kd, pad).astype(BF16).reshape(n_p, D_HEADS * 2 * LANES)
            v16 = kv[..., NOPE_DIM:].astype(BF16).reshape(n_p, D_HEADS * D_VDIM)
            od_p = causal_attend(q16, k16, v16, nb=bp, t=tp, heads=D_HEADS, dv=D_VDIM,
                                 scale=(NOPE_DIM + ROPE_DIM) ** -0.5)
            od_s = mla_sample_pallas(qd[n_p:].reshape(bs, ts, D_HEADS, NOPE_DIM + ROPE_DIM),
                                     ckv[n_p:].reshape(bs, ts, KV_LORA), kpe[n_p:].reshape(bs, ts, ROPE_DIM),
                                     st, o, d_kn[o], wb).reshape(n_s, D_HEADS * D_VDIM)
            mixed_p = jnp.concatenate([hc_p, od_p], axis=-1)
            mixed_s = jnp.concatenate([hc_s, od_s], axis=-1)
            for new, rows, nb_, t_, cc, cn, cm in ((new_p, slice(0, n_p), bp, tp, cc_p, cn_p, cm_p),
                                                   (new_s, slice(n_p, None), bs, ts, cc_s, cn_s, cm_s)):
                new['d_ckv'].append(ckv[rows].reshape(nb_, t_, KV_LORA))
                new['d_kpe'].append(kpe[rows].reshape(nb_, t_, ROPE_DIM))
                new['c_C'].append(cc)
                new['c_n'].append(cn[:, :, 0, :])
                new['c_m'].append(cm[:, :, 0, 0])
        mixed = jnp.concatenate([mixed_p, mixed_s], axis=0)
        x = matmul(mixed, wb['mix_out'][l], res=x)

        kvm = matmul(mem2d, wb['mem_kv'][l])
        mk32, mk16 = head_norm(kvm, 0, MEM_HEADS, mem_kn[l], tm=bp * n_mem)
        wm = MEM_HEADS * HEAD_DIM
        new_p['mem_k'].append(mk32.reshape(bp, n_mem, MEM_HEADS, HEAD_DIM))
        new_p['mem_v'].append(kvm[:, wm:].reshape(bp, n_mem, MEM_HEADS, HEAD_DIM))
        q = matmul(x, wb['mem_q'][l], gain=g_mem[l])
        nqb = 4
        o_p = mem_attend(q, mk16.reshape(bp, n_mem, wm), kvm.reshape(bp, n_mem, 2 * wm), mem_qn[l],
                         row0=0, n_batch=bp * nqb, tq=tp // nqb, nb=1,
                         k_index=lambda i: (i // nqb, 0, 0), v_index=lambda i: (i // nqb, 0, 1))
        o_s = mem_attend(q, cache_mem_k.reshape(DEPTH, bs, n_mem, wm), cache_mem_v.reshape(DEPTH, bs, n_mem, wm),
                         mem_qn[l], row0=n_p, n_batch=bs, tq=ts, nb=8,
                         k_index=lambda i, l=l: (l, i, 0, 0), v_index=lambda i, l=l: (l, i, 0, 0))
        o_all = jnp.concatenate([o_p, o_s], axis=0)
        x = matmul(o_all, wb['mem_o'][l], res=x)
        x = ffn_half_step(x, g_ffn2[l], wb['ffn2_gu'][l], wb['ffn2_dn'][l])

    sp = {n: jnp.stack(v) for n, v in new_p.items()}
    ss = {n: jnp.stack(v) for n, v in new_s.items()}
    return (x[:n_p].reshape(bp, tp, d), x[n_p:].reshape(bs, ts, d),
            sp['a_k'], sp['a_v'], sp['a_kidx'], sp['b_S'], sp['c_C'], sp['c_n'], sp['c_m'], sp['d_ckv'], sp['d_kpe'],
            sp['mem_k'], sp['mem_v'],
            ss['a_k'], ss['a_v'], ss['a_kidx'], ss['b_S'], ss['c_C'], ss['c_n'], ss['c_m'], ss['d_ckv'], ss['d_kpe'])
```

```python
import functools
import math

import jax
import jax.numpy as jnp
import numpy as np
from jax import lax
from jax.experimental import pallas as pl
from jax.experimental.pallas import tpu as pltpu

D_MODEL = 2048
DEPTH = 4
PAGE_SIZE = 128
HEAD_DIM = 128
A_HEADS = 8
A_KV_HEADS = 2
IDX_HEADS = 8
IDX_DIM = 64
TOPK_MAX = 256
REL_BUCKETS = 32
REL_MAX_DIST = 128
B_HEADS = 8
B_DK = 128
B_DV = 128
C_HEADS = 8
C_DK = 64
C_DV = 128
D_HEADS = 8
Q_LORA = 512
KV_LORA = 256
NOPE_DIM = 128
ROPE_DIM = 64
D_VDIM = 128
ROPE_THETA = 10000.0
MEM_HEADS = 4
D_FF = 5632
CHUNK = 64
Q_BLOCK = 128
EPS = 1e-6
MASK_NEG = -1e30
LB_FLOOR = 1e-20
F32 = jnp.float32
BF16 = jnp.bfloat16

EVEN_SPLITS = [A_HEADS * HEAD_DIM, A_KV_HEADS * HEAD_DIM, A_KV_HEADS * HEAD_DIM, IDX_HEADS * IDX_DIM, IDX_HEADS,
               IDX_DIM, B_HEADS * B_DK, B_HEADS * B_DK, B_HEADS * B_DV, B_HEADS * B_DV]
ODD_SPLITS = [C_HEADS * C_DK, C_HEADS * C_DK, C_HEADS * C_DV, C_HEADS, C_HEADS, C_HEADS * C_DV, Q_LORA, KV_LORA,
              ROPE_DIM]
EVEN_IN = sum(EVEN_SPLITS)
ODD_IN = sum(ODD_SPLITS)

EV_QA, EV_QB, EV_FB, EV_IB, EV_GB, EV_KA, EV_VA, EV_QI, EV_KIKI, EV_WI, EVEN_PAD = (
    0, 1024, 2048, 3072, 4096, 5120, 5376, 5632, 6144, 6272, 6400)
OD_QC, OD_KC, OD_VC, OD_OC, OD_G, OD_QA, OD_CKV, OD_KPE, ODD_PAD = (
    0, 512, 1024, 2048, 3072, 3584, 4096, 4352, 4480)

LANES = 128
VMEM_LIMIT = 56 * 1024 * 1024


def _round_up(n, m):
    return (n + m - 1) // m * m


def _ffn_body(x_ref, g_ref, wg_ref, wu_ref, wd_ref, o_ref, n_ref):
    f = pl.program_id(1)

    @pl.when(f == 0)
    def _():
        x = x_ref[...]
        ms = jnp.mean(x * x, axis=-1, keepdims=True)
        n_ref[...] = (x * lax.rsqrt(ms + EPS) * g_ref[...]).astype(BF16)
        o_ref[...] = x

    n = n_ref[...]
    a = jnp.dot(n, wg_ref[...], preferred_element_type=F32)
    b = jnp.dot(n, wu_ref[...], preferred_element_type=F32)
    h = (a * jax.nn.sigmoid(a) * b).astype(BF16)
    o_ref[...] += 0.5 * jnp.dot(h, wd_ref[...], preferred_element_type=F32)


def ffn_half_step(x, g, w_gu, w_dn, *, tm=512, tf=512):
    m, d = x.shape
    ff = w_dn.shape[0]
    nf = ff // tf
    return pl.pallas_call(
        _ffn_body,
        grid=(m // tm, nf),
        in_specs=[
            pl.BlockSpec((tm, d), lambda i, f: (i, 0)),
            pl.BlockSpec((1, d), lambda i, f: (0, 0)),
            pl.BlockSpec((d, tf), lambda i, f: (0, f)),
            pl.BlockSpec((d, tf), lambda i, f: (0, f + nf)),
            pl.BlockSpec((tf, d), lambda i, f: (f, 0)),
        ],
        out_specs=pl.BlockSpec((tm, d), lambda i, f: (i, 0)),
        out_shape=jax.ShapeDtypeStruct((m, d), F32),
        scratch_shapes=[pltpu.VMEM((tm, d), BF16)],
        compiler_params=pltpu.CompilerParams(
            dimension_semantics=("parallel", "arbitrary"), vmem_limit_bytes=VMEM_LIMIT),
        name="ffn_half_step",
    )(x, g.reshape(1, d), w_gu, w_gu, w_dn)


def _mm_body(*refs, has_gain, has_res, alpha):
    a_ref, w_ref = refs[0], refs[1]
    k = 2
    g_ref = res_ref = None
    if has_gain:
        g_ref = refs[k]
        k += 1
    if has_res:
        res_ref = refs[k]
        k += 1
    o_ref, n_ref = refs[k], refs[k + 1]

    @pl.when(pl.program_id(1) == 0)
    def _():
        a = a_ref[...]
        if has_gain:
            ms = jnp.mean(a * a, axis=-1, keepdims=True)
            a = a * lax.rsqrt(ms + EPS) * g_ref[...]
        n_ref[...] = a.astype(BF16)

    acc = jnp.dot(n_ref[...], w_ref[...], preferred_element_type=F32)
    if has_res:
        acc = res_ref[...] + alpha * acc
    o_ref[...] = acc


def matmul(a, w, *, gain=None, res=None, alpha=1.0, tm=512, tn=512, a_col=0):
    m = a.shape[0]
    kdim, n = w.shape
    tm = min(tm, m)
    tn = min(tn, n)
    assert m % tm == 0 and n % tn == 0, (m, tm, n, tn)
    in_specs = [pl.BlockSpec((tm, kdim), lambda i, j: (i, a_col)),
                pl.BlockSpec((kdim, tn), lambda i, j: (0, j))]
    args = [a, w]
    if gain is not None:
        in_specs.append(pl.BlockSpec((1, kdim), lambda i, j: (0, 0)))
        args.append(gain.reshape(1, kdim))
    if res is not None:
        in_specs.append(pl.BlockSpec((tm, tn), lambda i, j: (i, j)))
        args.append(res)
    return pl.pallas_call(
        functools.partial(_mm_body, has_gain=gain is not None, has_res=res is not None, alpha=alpha),
        grid=(m // tm, n // tn),
        in_specs=in_specs,
        out_specs=pl.BlockSpec((tm, tn), lambda i, j: (i, j)),
        out_shape=jax.ShapeDtypeStruct((m, n), F32),
        scratch_shapes=[pltpu.VMEM((tm, kdim), BF16)],
        compiler_params=pltpu.CompilerParams(
            dimension_semantics=("parallel", "arbitrary"), vmem_limit_bytes=VMEM_LIMIT),
        name="matmul",
    )(*args)


def _head_norm_body(x_ref, g_ref, o32_ref, o16_ref, *, heads):
    outs = []
    for h in range(heads):
        x = x_ref[:, LANES * h:LANES * (h + 1)]
        outs.append(x * lax.rsqrt(jnp.mean(x * x, axis=-1, keepdims=True) + EPS) * g_ref[...])
    y = jnp.concatenate(outs, axis=1)
    o32_ref[...] = y
    o16_ref[...] = y.astype(BF16)


def head_norm(x, col_block, heads, gain, *, tm=1024):
    m = x.shape[0]
    w = heads * LANES
    return pl.pallas_call(
        functools.partial(_head_norm_body, heads=heads),
        grid=(m // tm,),
        in_specs=[pl.BlockSpec((tm, w), lambda i: (i, col_block)),
                  pl.BlockSpec((1, LANES), lambda i: (0, 0))],
        out_specs=[pl.BlockSpec((tm, w), lambda i: (i, 0)), pl.BlockSpec((tm, w), lambda i: (i, 0))],
        out_shape=[jax.ShapeDtypeStruct((m, w), F32), jax.ShapeDtypeStruct((m, w), BF16)],
        compiler_params=pltpu.CompilerParams(dimension_semantics=("parallel",)),
        name="head_norm",
    )(x, gain.reshape(1, LANES))


INT_MIN = -2 ** 31
INT_MAX = 2 ** 31 - 1
_NT = (((1,), (1,)), ((), ()))


def _dot_nt(a, b):
    return lax.dot_general(a, b, _NT, preferred_element_type=F32)


def _sort_key(x):
    b = lax.bitcast_convert_type(x + 0.0, jnp.int32)
    return jnp.where(b >= 0, b, b ^ jnp.int32(INT_MAX))


def _row_count(cond):
    c = jnp.where(cond, 1.0, 0.0)
    if c.ndim == 3:
        c = jnp.sum(c, axis=0)
    return jnp.sum(c, axis=-1, keepdims=True)


def _topk_masks(pieces, k, idx_bits):
    kf = float(k)

    def count(fn):
        tot = None
        for key, idx in pieces:
            c = _row_count(fn(key, idx))
            tot = c if tot is None else tot + c
        return tot

    t0 = jnp.where(count(lambda key, idx: key >= 0) >= kf, jnp.int32(0), jnp.int32(INT_MIN))

    def body(i, t):
        cand = t | jnp.left_shift(jnp.int32(1), 30 - i)
        return jnp.where(count(lambda key, idx: key >= cand) >= kf, cand, t)

    thr = lax.fori_loop(0, 31, body, t0)
    need = kf - count(lambda key, idx: key > thr)
    ties = [(jnp.where(key == thr, idx, INT_MAX), idx) for key, idx in pieces]

    def count_ties(cand):
        tot = None
        for tie, _ in ties:
            c = _row_count(tie < cand)
            tot = c if tot is None else tot + c
        return tot

    def body2(i, cur):
        cand = cur + jnp.left_shift(jnp.int32(1), idx_bits - 1 - i)
        return jnp.where(count_ties(cand) < need, cand, cur)

    cut = lax.fori_loop(0, idx_bits, body2, jnp.zeros_like(thr))
    return [jnp.where(key > thr, 1.0, jnp.where(tie <= cut, 1.0, 0.0))
            for (key, _), (tie, _) in zip(pieces, ties)]


def _softmax_rows(lg):
    mx = jnp.max(lg, axis=-1, keepdims=True)
    p = jnp.exp(lg - mx)
    return p / jnp.sum(p, axis=-1, keepdims=True)


def _dsa_prompt_body(qa_ref, qi_ref, wi_ref, kiki_ref, k_ref, v_ref, bias_ref, qn_ref, o_ref, *, qb, t, n_sel):
    i = pl.program_id(1)
    nsub = qb // LANES
    nk = t // LANES
    kiki = kiki_ref[...].astype(BF16)
    qi = qi_ref[...] * (IDX_DIM ** -0.5)
    w = wi_ref[...] * (IDX_HEADS ** -0.5)
    lane = lax.broadcasted_iota(jnp.int32, (qb, LANES), 1)
    score = jnp.zeros((qb, t), F32)
    for h in range(IDX_HEADS):
        blk = qi[:, LANES * (h // 2):LANES * (h // 2 + 1)]
        keep = (lane < IDX_DIM) if h % 2 == 0 else (lane >= IDX_DIM)
        s = _dot_nt(jnp.where(keep, blk, 0.0).astype(BF16), kiki)
        score = score + jnp.maximum(s, 0.0) * w[:, h:h + 1]
    q_pos = i * qb + lax.broadcasted_iota(jnp.int32, (qb, t), 0)
    k_pos = lax.broadcasted_iota(jnp.int32, (qb, t), 1)
    allowed = k_pos <= q_pos
    score = jnp.where(allowed, score, MASK_NEG)
    (selm,) = _topk_masks([(_sort_key(score), k_pos)], n_sel, max(1, (t - 1).bit_length()))
    sel = jnp.where(allowed, selm, 0.0) > 0.5

    k = k_ref[...]
    v = v_ref[...].astype(BF16)
    outs = []
    for h in range(A_HEADS):
        g = h // (A_HEADS // A_KV_HEADS)
        rows = []
        for a in range(nsub):
            d0 = i * nsub + a
            tiles = []
            for j in range(nk):
                d = d0 - j
                tiles.append(jnp.where(d == 0, bias_ref[h, 0], jnp.where(d == 1, bias_ref[h, 1], bias_ref[h, 2])))
            rows.append(jnp.concatenate(tiles, axis=1))
        bias = jnp.concatenate(rows, axis=0) if nsub > 1 else rows[0]
        q = qa_ref[:, LANES * h:LANES * (h + 1)]
        q = q * lax.rsqrt(jnp.mean(q * q, axis=-1, keepdims=True) + EPS) * qn_ref[...]
        lg = _dot_nt(q.astype(BF16), k[:, LANES * g:LANES * (g + 1)]) * (HEAD_DIM ** -0.5) + bias
        p = _softmax_rows(jnp.where(sel, lg, MASK_NEG))
        outs.append(jnp.dot(p.astype(BF16), v[:, LANES * g:LANES * (g + 1)], preferred_element_type=F32))
    o_ref[...] = jnp.concatenate(outs, axis=1)


def dsa_prompt_attend(proj, ka16, bias_tiles, qn, *, nb, t, qb=256):
    nq = t // qb
    n_sel = min(TOPK_MAX, t // 4)
    wq = A_HEADS * HEAD_DIM
    return pl.pallas_call(
        functools.partial(_dsa_prompt_body, qb=qb, t=t, n_sel=n_sel),
        grid=(nb, nq),
        in_specs=[
            pl.BlockSpec((qb, wq), lambda b, i: (b * nq + i, 0)),
            pl.BlockSpec((qb, IDX_HEADS * IDX_DIM), lambda b, i: (b * nq + i, EV_QI // (IDX_HEADS * IDX_DIM))),
            pl.BlockSpec((qb, LANES), lambda b, i: (b * nq + i, EV_WI // LANES)),
            pl.BlockSpec((t, LANES), lambda b, i: (b, EV_KIKI // LANES)),
            pl.BlockSpec((t, A_KV_HEADS * HEAD_DIM), lambda b, i: (b, 0)),
            pl.BlockSpec((t, A_KV_HEADS * HEAD_DIM), lambda b, i: (b, EV_VA // (A_KV_HEADS * HEAD_DIM))),
            pl.BlockSpec((A_HEADS, 3, LANES, LANES), lambda b, i: (0, 0, 0, 0)),
            pl.BlockSpec((1, HEAD_DIM), lambda b, i: (0, 0)),
        ],
        out_specs=pl.BlockSpec((qb, wq), lambda b, i: (b * nq + i, 0)),
        out_shape=jax.ShapeDtypeStruct((nb * t, wq), F32),
        compiler_params=pltpu.CompilerParams(
            dimension_semantics=("parallel", "arbitrary"), vmem_limit_bytes=VMEM_LIMIT),
        name="dsa_prompt",
    )(proj, proj, proj, proj, ka16, proj, bias_tiles, qn.reshape(1, HEAD_DIM))


def _idx_head_sum(s, w):
    tot = None
    for h in range(IDX_HEADS):
        c = jnp.maximum(s[8 * h:8 * (h + 1)], 0.0) * w[:, h:h + 1]
        tot = c if tot is None else tot + c
    return tot


def _dsa_score_body(pt_ref, qi_ref, wi_ref, knew_ref, *rest, pp, ts):
    pages, sp_ref, sn_ref = rest[:pp], rest[pp], rest[pp + 1]
    qi = qi_ref[...] * (IDX_DIM ** -0.5)
    a = jnp.concatenate([qi[:, IDX_DIM * h:IDX_DIM * (h + 1)] for h in range(IDX_HEADS)], axis=0).astype(BF16)
    w = wi_ref[...] * (IDX_HEADS ** -0.5)
    kt = jnp.concatenate([p[...] for p in pages], axis=1).astype(BF16)
    sp_ref[...] = _idx_head_sum(jnp.dot(a, kt, preferred_element_type=F32), w)

    @pl.when(pl.program_id(1) == 0)
    def _():
        knew = knew_ref[...][:, :IDX_DIM]
        kn = jnp.concatenate([knew, jnp.zeros((LANES - ts, IDX_DIM), F32)], axis=0).astype(BF16)
        sn_ref[...] = _idx_head_sum(_dot_nt(a, kn), w)


def _dsa_mask_body(sp_ref, sn_ref, mp_ref, mn_ref, *, ts, n_sel):
    rows, past = sp_ref.shape
    lane = lax.broadcasted_iota(jnp.int32, (rows // ts, ts, LANES), 2).reshape(rows, LANES)
    t_row = lax.broadcasted_iota(jnp.int32, (rows // ts, ts, LANES), 1).reshape(rows, LANES)
    causal = lane <= t_row
    snew = jnp.where(causal, sn_ref[...], MASK_NEG)
    key_new = jnp.where(lane < ts, _sort_key(snew), INT_MIN)
    idx_past = lax.broadcasted_iota(jnp.int32, (rows, past), 1)
    m_past, m_new = _topk_masks([(_sort_key(sp_ref[...]), idx_past), (key_new, past + lane)], n_sel,
                                (past + LANES - 1).bit_length())
    mp_ref[...] = m_past
    mn_ref[...] = jnp.where(causal, m_new, 0.0)


def dsa_sample_select(proj, pool_kidx_t, pt_flat, layer, *, row0, nb, ts, n_pages, pp=32, rows_per_step=128):
    nj = n_pages // pp
    tk = pp * PAGE_SIZE
    past = n_pages * PAGE_SIZE
    n_sel = min(TOPK_MAX, (past + ts) // 4)
    rb0 = row0 // ts
    rows = nb * ts
    rstep = min(rows_per_step, rows)
    sp, sn = _dsa_sample_scores(proj, pool_kidx_t, pt_flat, layer, rb0=rb0, nb=nb, ts=ts, n_pages=n_pages, pp=pp)
    mp, mn = pl.pallas_call(
        functools.partial(_dsa_mask_body, ts=ts, n_sel=n_sel),
        grid=(rows // rstep,),
        in_specs=[pl.BlockSpec((rstep, past), lambda i: (i, 0)), pl.BlockSpec((rstep, LANES), lambda i: (i, 0))],
        out_specs=[pl.BlockSpec((rstep, past), lambda i: (i, 0)), pl.BlockSpec((rstep, LANES), lambda i: (i, 0))],
        out_shape=[jax.ShapeDtypeStruct((rows, past), F32), jax.ShapeDtypeStruct((rows, LANES), F32)],
        compiler_params=pltpu.CompilerParams(dimension_semantics=("parallel",), vmem_limit_bytes=VMEM_LIMIT),
        name="dsa_sample_mask",
    )(sp.reshape(rows, past), sn.reshape(rows, LANES))
    return mp.reshape(nb, ts, past), mn.reshape(nb, ts, LANES)


def _dsa_sample_scores(proj, pool_kidx_t, pt_flat, layer, *, rb0, nb, ts, n_pages, pp):
    nj = n_pages // pp
    tk = pp * PAGE_SIZE
    past = n_pages * PAGE_SIZE

    def page_spec(r):
        return pl.BlockSpec((None, None, IDX_DIM, PAGE_SIZE),
                            lambda b, j, pt: (layer, pt[b * n_pages + j * pp + r], 0, 0))

    grid_spec = pltpu.PrefetchScalarGridSpec(
        num_scalar_prefetch=1,
        grid=(nb, nj),
        in_specs=[
            pl.BlockSpec((ts, IDX_HEADS * IDX_DIM), lambda b, j, pt: (rb0 + b, EV_QI // (IDX_HEADS * IDX_DIM))),
            pl.BlockSpec((ts, LANES), lambda b, j, pt: (rb0 + b, EV_WI // LANES)),
            pl.BlockSpec((ts, LANES), lambda b, j, pt: (rb0 + b, EV_KIKI // LANES)),
        ] + [page_spec(r) for r in range(pp)],
        out_specs=[pl.BlockSpec((None, ts, tk), lambda b, j, pt: (b, 0, j)),
                   pl.BlockSpec((None, ts, LANES), lambda b, j, pt: (b, 0, 0))],
    )
    return pl.pallas_call(
        functools.partial(_dsa_score_body, pp=pp, ts=ts),
        grid_spec=grid_spec,
        out_shape=[jax.ShapeDtypeStruct((nb, ts, past), F32), jax.ShapeDtypeStruct((nb, ts, LANES), F32)],
        compiler_params=pltpu.CompilerParams(
            dimension_semantics=("parallel", "arbitrary"), vmem_limit_bytes=VMEM_LIMIT),
        name="dsa_sample_scores",
    )(pt_flat, proj, proj, proj, *([pool_kidx_t] * pp))


def _dsa_att_body(pt_ref, q_ref, knew_ref, vnew_ref, qn_ref, mask_ref, mnew_ref, bias_ref, bnew_ref, *rest,
                  pp, nj, ts):
    kpages, vpages = rest[:pp], rest[pp:2 * pp]
    o_ref, qs_ref, m_ref, l_ref, acc_ref = rest[2 * pp:]
    j = pl.program_id(1)
    gsz = A_HEADS // A_KV_HEADS
    rows_g = gsz * ts

    @pl.when(j == 0)
    def _():
        qs = []
        for h in range(A_HEADS):
            q = q_ref[:, LANES * h:LANES * (h + 1)]
            qs.append(q * lax.rsqrt(jnp.mean(q * q, axis=-1, keepdims=True) + EPS) * qn_ref[...])
        qs_ref[...] = jnp.concatenate(qs, axis=0).astype(BF16)
        m_ref[...] = jnp.full(m_ref.shape, MASK_NEG, F32)
        l_ref[...] = jnp.zeros(l_ref.shape, F32)
        acc_ref[...] = jnp.zeros(acc_ref.shape, F32)

    def update(k_tiles, v_tiles, mask, bias):
        qs = qs_ref[...]
        lg = jnp.concatenate(
            [_dot_nt(qs[rows_g * g:rows_g * (g + 1)], k_tiles[g])
             for g in range(A_KV_HEADS)], axis=0) * (HEAD_DIM ** -0.5) + bias
        sel = jnp.concatenate([mask] * A_HEADS, axis=0) > 0.5
        lg = jnp.where(sel, lg, MASK_NEG)
        m_old = m_ref[...]
        m_new = jnp.maximum(m_old, jnp.max(lg, axis=-1, keepdims=True))
        alpha = jnp.exp(m_old - m_new)
        p = jnp.where(sel, jnp.exp(lg - m_new), 0.0)
        l_ref[...] = alpha * l_ref[...] + jnp.sum(p, axis=-1, keepdims=True)
        pb = p.astype(BF16)
        pv = jnp.concatenate(
            [jnp.dot(pb[rows_g * g:rows_g * (g + 1)], v_tiles[g], preferred_element_type=F32)
             for g in range(A_KV_HEADS)], axis=0)
        acc_ref[...] = alpha * acc_ref[...] + pv
        m_ref[...] = m_new

    def head_rows(pages, g):
        return jnp.concatenate([p[pl.ds(g, PAGE_SIZE, stride=A_KV_HEADS), :] for p in pages], axis=0).astype(BF16)

    update([head_rows(kpages, g) for g in range(A_KV_HEADS)], [head_rows(vpages, g) for g in range(A_KV_HEADS)],
           mask_ref[...], bias_ref[...].reshape(A_HEADS * ts, pp * PAGE_SIZE))

    @pl.when(j == nj - 1)
    def _():
        pad = jnp.zeros((LANES - ts, A_KV_HEADS * HEAD_DIM), F32)
        kn = jnp.concatenate([knew_ref[...], pad], axis=0).astype(BF16)
        vn = jnp.concatenate([vnew_ref[...], pad], axis=0).astype(BF16)
        update([kn[:, LANES * g:LANES * (g + 1)] for g in range(A_KV_HEADS)],
               [vn[:, LANES * g:LANES * (g + 1)] for g in range(A_KV_HEADS)],
               mnew_ref[...], bnew_ref[...].reshape(A_HEADS * ts, LANES))
        out = acc_ref[...] / l_ref[...]
        o_ref[...] = jnp.concatenate([out[ts * h:ts * (h + 1)] for h in range(A_HEADS)], axis=1)


def dsa_sample_attend(proj, ka32, mask_past, mask_new, bias_s, pool_k, pool_v, pt_flat, layer, qn, *, row0, nb, ts,
                      n_pages, pp=16):
    nj = n_pages // pp
    tk = pp * PAGE_SIZE
    past = n_pages * PAGE_SIZE
    rb0 = row0 // ts
    wkv = A_KV_HEADS * HEAD_DIM
    wq = A_HEADS * HEAD_DIM

    def page_spec(r):
        return pl.BlockSpec((None, None, PAGE_SIZE * A_KV_HEADS, HEAD_DIM),
                            lambda b, j, pt: (layer, pt[b * n_pages + j * pp + r], 0, 0))

    grid_spec = pltpu.PrefetchScalarGridSpec(
        num_scalar_prefetch=1,
        grid=(nb, nj),
        in_specs=[
            pl.BlockSpec((ts, wq), lambda b, j, pt: (rb0 + b, 0)),
            pl.BlockSpec((ts, wkv), lambda b, j, pt: (rb0 + b, 0)),
            pl.BlockSpec((ts, wkv), lambda b, j, pt: (rb0 + b, EV_VA // wkv)),
            pl.BlockSpec((1, HEAD_DIM), lambda b, j, pt: (0, 0)),
            pl.BlockSpec((None, ts, tk), lambda b, j, pt: (b, 0, j)),
            pl.BlockSpec((None, ts, LANES), lambda b, j, pt: (b, 0, 0)),
            pl.BlockSpec((A_HEADS, ts, tk), lambda b, j, pt: (0, 0, j)),
            pl.BlockSpec((A_HEADS, ts, LANES), lambda b, j, pt: (0, 0, past // LANES)),
        ] + [page_spec(r) for r in range(pp)] * 2,
        out_specs=pl.BlockSpec((ts, wq), lambda b, j, pt: (b, 0)),
        scratch_shapes=[pltpu.VMEM((A_HEADS * ts, HEAD_DIM), BF16), pltpu.VMEM((A_HEADS * ts, 1), F32),
                        pltpu.VMEM((A_HEADS * ts, 1), F32), pltpu.VMEM((A_HEADS * ts, HEAD_DIM), F32)],
    )
    return pl.pallas_call(
        functools.partial(_dsa_att_body, pp=pp, nj=nj, ts=ts),
        grid_spec=grid_spec,
        out_shape=jax.ShapeDtypeStruct((nb * ts, wq), F32),
        compiler_params=pltpu.CompilerParams(
            dimension_semantics=("parallel", "arbitrary"), vmem_limit_bytes=VMEM_LIMIT),
        name="dsa_sample_attend",
    )(pt_flat, proj, ka32, proj, qn.reshape(1, HEAD_DIM), mask_past, mask_new, bias_s, bias_s,
      *([pool_k] * pp), *([pool_v] * pp))


def _mla_sample_body(pt_ref, qlat_ref, qpe_ref, cnew_ref, rnew_ref, wkt_ref, wv_ref, *rest, pp, nj, ts):
    cpages, rpages = rest[:pp], rest[pp:2 * pp]
    o_ref, m_ref, l_ref, acc_ref = rest[2 * pp:]
    j = pl.program_id(1)
    dq = NOPE_DIM + ROPE_DIM
    nrow = D_HEADS * ts

    @pl.when(j == 0)
    def _():
        m_ref[...] = jnp.full(m_ref.shape, MASK_NEG, F32)
        l_ref[...] = jnp.zeros(l_ref.shape, F32)
        acc_ref[...] = jnp.zeros(acc_ref.shape, F32)

    def update(c, rt, sel):
        n = c.shape[0]
        cb = c.astype(BF16)
        kt = _dot_nt(wkt_ref[...], cb)
        ss = jnp.sum((kt * kt).reshape(D_HEADS, NOPE_DIM, n), axis=1)
        rss = jnp.sum(rt * rt, axis=0, keepdims=True)
        rinv = lax.rsqrt((ss + rss) * (1.0 / dq) + EPS) * (dq ** -0.5)
        lg = _dot_nt(qlat_ref[...], cb) + jnp.dot(qpe_ref[...], rt.astype(BF16),
                                                  preferred_element_type=F32)
        lg = (lg.reshape(D_HEADS, ts, n) * rinv[:, None, :]).reshape(nrow, n)
        if sel is not None:
            lg = jnp.where(sel, lg, MASK_NEG)
        m_old = m_ref[...]
        m_new = jnp.maximum(m_old, jnp.max(lg, axis=-1, keepdims=True))
        alpha = jnp.exp(m_old - m_new)
        p = jnp.exp(lg - m_new)
        if sel is not None:
            p = jnp.where(sel, p, 0.0)
        l_ref[...] = alpha * l_ref[...] + jnp.sum(p, axis=-1, keepdims=True)
        acc_ref[...] = alpha * acc_ref[...] + jnp.dot(p.astype(BF16), cb, preferred_element_type=F32)
        m_ref[...] = m_new

    c = jnp.concatenate([p[...] for p in cpages], axis=0)
    rt = jnp.concatenate([p[...] for p in rpages], axis=1)
    update(c, rt, None)

    @pl.when(j == nj - 1)
    def _():
        lane = lax.broadcasted_iota(jnp.int32, (D_HEADS, ts, LANES), 2).reshape(nrow, LANES)
        row_t = lax.broadcasted_iota(jnp.int32, (D_HEADS, ts, LANES), 1).reshape(nrow, LANES)
        update(cnew_ref[...], rnew_ref[...], lane <= row_t)
        lat = (acc_ref[...] / l_ref[...]).astype(BF16)
        o_ref[...] = jnp.concatenate(
            [jnp.dot(lat[ts * h:ts * (h + 1)], wv_ref[h], preferred_element_type=F32) for h in range(D_HEADS)],
            axis=1)


def mla_sample_attend(qlat, qpe, cnew, rnew_t, wkt, wv, pool_ckv, pool_kpe_t, pt_flat, layer, *, nb, ts, n_pages,
                      pp=16):
    nj = n_pages // pp
    nrow = D_HEADS * ts

    def cspec(r):
        return pl.BlockSpec((None, None, PAGE_SIZE, KV_LORA), lambda b, j, pt: (layer, pt[b * n_pages + j * pp + r], 0, 0))

    def rspec(r):
        return pl.BlockSpec((None, None, ROPE_DIM, PAGE_SIZE), lambda b, j, pt: (layer, pt[b * n_pages + j * pp + r], 0, 0))

    grid_spec = pltpu.PrefetchScalarGridSpec(
        num_scalar_prefetch=1,
        grid=(nb, nj),
        in_specs=[
            pl.BlockSpec((None, nrow, KV_LORA), lambda b, j, pt: (b, 0, 0)),
            pl.BlockSpec((None, nrow, ROPE_DIM), lambda b, j, pt: (b, 0, 0)),
            pl.BlockSpec((None, LANES, KV_LORA), lambda b, j, pt: (b, 0, 0)),
            pl.BlockSpec((None, ROPE_DIM, LANES), lambda b, j, pt: (b, 0, 0)),
            pl.BlockSpec((D_HEADS * NOPE_DIM, KV_LORA), lambda b, j, pt: (0, 0)),
            pl.BlockSpec((D_HEADS, KV_LORA, D_VDIM), lambda b, j, pt: (0, 0, 0)),
        ] + [cspec(r) for r in range(pp)] + [rspec(r) for r in range(pp)],
        out_specs=pl.BlockSpec((ts, D_HEADS * D_VDIM), lambda b, j, pt: (b, 0)),
        scratch_shapes=[pltpu.VMEM((nrow, 1), F32), pltpu.VMEM((nrow, 1), F32), pltpu.VMEM((nrow, KV_LORA), F32)],
    )
    return pl.pallas_call(
        functools.partial(_mla_sample_body, pp=pp, nj=nj, ts=ts),
        grid_spec=grid_spec,
        out_shape=jax.ShapeDtypeStruct((nb * ts, D_HEADS * D_VDIM), F32),
        compiler_params=pltpu.CompilerParams(
            dimension_semantics=("parallel", "arbitrary"), vmem_limit_bytes=VMEM_LIMIT),
        name="mla_sample",
    )(pt_flat, qlat, qpe, cnew, rnew_t, wkt, wv, *([pool_ckv] * pp), *([pool_kpe_t] * pp))


def _mem_attn_body(q_ref, k_ref, v_ref, qn_ref, o_ref, *, nb, tq, n_mem, head_rows):
    def head(ref, b, h):
        if head_rows:
            return ref[b, pl.ds(h, n_mem, stride=MEM_HEADS), :].astype(BF16)
        return ref[b, :, LANES * h:LANES * (h + 1)].astype(BF16)

    for b in range(nb):
        outs = []
        for h in range(MEM_HEADS):
            q = q_ref[tq * b:tq * (b + 1), LANES * h:LANES * (h + 1)]
            q = q * lax.rsqrt(jnp.mean(q * q, axis=-1, keepdims=True) + EPS) * qn_ref[...]
            lg = _dot_nt(q.astype(BF16), head(k_ref, b, h)) * (HEAD_DIM ** -0.5)
            p = _softmax_rows(lg)
            outs.append(jnp.dot(p.astype(BF16), head(v_ref, b, h), preferred_element_type=F32))
        o_ref[tq * b:tq * (b + 1), :] = jnp.concatenate(outs, axis=1)


def mem_attend(q, mk, mv, qn, *, row0, n_batch, tq, nb, n_mem, head_rows, k_index, v_index):
    w = MEM_HEADS * HEAD_DIM
    rb0 = row0 // (nb * tq)
    slab = (n_mem * MEM_HEADS, HEAD_DIM) if head_rows else (n_mem, w)
    kblock = (None,) * (mk.ndim - 3) + (nb,) + slab
    vblock = (None,) * (mv.ndim - 3) + (nb,) + slab
    return pl.pallas_call(
        functools.partial(_mem_attn_body, nb=nb, tq=tq, n_mem=n_mem, head_rows=head_rows),
        grid=(n_batch // nb,),
        in_specs=[pl.BlockSpec((nb * tq, w), lambda i: (rb0 + i, 0)),
                  pl.BlockSpec(kblock, k_index),
                  pl.BlockSpec(vblock, v_index),
                  pl.BlockSpec((1, HEAD_DIM), lambda i: (0, 0))],
        out_specs=pl.BlockSpec((nb * tq, w), lambda i: (i, 0)),
        out_shape=jax.ShapeDtypeStruct((n_batch * tq, w), F32),
        compiler_params=pltpu.CompilerParams(dimension_semantics=("parallel",), vmem_limit_bytes=VMEM_LIMIT),
        name="mem_attend",
    )(q, mk, mv, qn.reshape(1, HEAD_DIM))


def _causal_attn_body(q_ref, k_ref, v_ref, o_ref, *, qb, t, scale):
    i = pl.program_id(2)
    lg = _dot_nt(q_ref[...], k_ref[...]) * scale
    q_pos = i * qb + lax.broadcasted_iota(jnp.int32, (qb, t), 0)
    k_pos = lax.broadcasted_iota(jnp.int32, (qb, t), 1)
    p = _softmax_rows(jnp.where(k_pos <= q_pos, lg, MASK_NEG))
    o_ref[...] = jnp.dot(p.astype(BF16), v_ref[...], preferred_element_type=F32)


def causal_attend(q, k, v, *, nb, t, heads, dv, scale, qb=512):
    nq = t // qb
    dqk = q.shape[1] // heads
    return pl.pallas_call(
        functools.partial(_causal_attn_body, qb=qb, t=t, scale=scale),
        grid=(nb, heads, nq),
        in_specs=[pl.BlockSpec((qb, dqk), lambda b, h, i: (b * nq + i, h)),
                  pl.BlockSpec((t, dqk), lambda b, h, i: (b, h)),
                  pl.BlockSpec((t, dv), lambda b, h, i: (b, h))],
        out_specs=pl.BlockSpec((qb, dv), lambda b, h, i: (b * nq + i, h)),
        out_shape=jax.ShapeDtypeStruct((nb * t, heads * dv), F32),
        compiler_params=pltpu.CompilerParams(
            dimension_semantics=("parallel", "parallel", "arbitrary"), vmem_limit_bytes=VMEM_LIMIT),
        name="causal_attend",
    )(q, k, v)


def _cumsum_rows(x):
    c = x.shape[0]
    row = lax.broadcasted_iota(jnp.int32, x.shape, 0)
    sh = 1
    while sh < c:
        x = x + jnp.where(row >= sh, pltpu.roll(x, sh, axis=0), 0.0)
        sh *= 2
    return x


def _log_sigmoid(x):
    return jnp.minimum(x, 0.0) - jnp.log1p(jnp.exp(-jnp.abs(x)))


def _sigmoid(x):
    return 1.0 / (1.0 + jnp.exp(-x))


def _hgrn2_run(q_ref, f_ref, v_ref, g_ref, lb_ref, on, st_ref, o_ref, *, nh, t, c):
    row = lax.broadcasted_iota(jnp.int32, (c, 1), 0)

    def step(i, carry):
        r0 = pl.multiple_of(i * c, c)
        for h in range(nh):
            sl = slice(LANES * h, LANES * (h + 1))
            lb = lb_ref[h]
            fpre = f_ref[pl.ds(r0, c), sl]
            a = jnp.log(jnp.maximum(lb, LB_FLOOR))
            b = jnp.log1p(-lb) + _log_sigmoid(fpre)
            lf = jnp.maximum(a, b) + jnp.log1p(jnp.exp(-jnp.abs(a - b)))
            kk = (1.0 - lb) * _sigmoid(-fpre)
            qpre = q_ref[pl.ds(r0, c), sl]
            qv = qpre * _sigmoid(qpre)
            vv = v_ref[pl.ds(r0, c), sl]
            cb = _cumsum_rows(lf)
            st = st_ref[h]
            o = _dot_nt((qv * jnp.exp(cb)).astype(BF16), st.astype(BF16))
            for s in range(c):
                dec = jnp.exp(jnp.minimum(cb - cb[s:s + 1, :], 0.0))
                col = jnp.sum(qv * dec * kk[s:s + 1, :], axis=-1, keepdims=True)
                o = o + jnp.where(row >= s, col, 0.0) * vv[s:s + 1, :]
            c_last = cb[c - 1:c, :]
            kd = (kk * jnp.exp(c_last - cb)).astype(BF16)
            st_ref[h] = st * jnp.exp(c_last) + lax.dot_general(vv.astype(BF16), kd, (((0,), (0,)), ((), ())),
                                                               preferred_element_type=F32)
            gpre = g_ref[pl.ds(r0, c), sl]
            o = o * lax.rsqrt(jnp.mean(o * o, axis=-1, keepdims=True) + EPS) * on * (gpre * _sigmoid(gpre))
            o_ref[pl.ds(r0, c), sl] = o
        return carry

    lax.fori_loop(0, t // c, step, 0)


def _hgrn2_prompt_body(q_ref, f_ref, v_ref, g_ref, lb_ref, on_ref, o_ref, s_ref, st_ref, *, nh, t, c):
    st_ref[...] = jnp.zeros(st_ref.shape, F32)
    _hgrn2_run(q_ref, f_ref, v_ref, g_ref, lb_ref, on_ref[...], st_ref, o_ref, nh=nh, t=t, c=c)
    for h in range(nh):
        s_ref[h] = st_ref[h].T


def hgrn2_prompt(proj, lb, on, *, nb, t, nh=4):
    c = math.gcd(16, t)
    w = nh * LANES
    col = lambda c0: (lambda b, j: (b, c0 // w + j))
    return pl.pallas_call(
        functools.partial(_hgrn2_prompt_body, nh=nh, t=t, c=c),
        grid=(nb, B_HEADS // nh),
        in_specs=[pl.BlockSpec((t, w), col(EV_QB)), pl.BlockSpec((t, w), col(EV_FB)),
                  pl.BlockSpec((t, w), col(EV_IB)), pl.BlockSpec((t, w), col(EV_GB)),
                  pl.BlockSpec((nh, 1, B_DK), lambda b, j: (j, 0, 0)),
                  pl.BlockSpec((1, B_DV), lambda b, j: (0, 0))],
        out_specs=[pl.BlockSpec((t, w), lambda b, j: (b, j)),
                   pl.BlockSpec((None, nh, B_DK, B_DV), lambda b, j: (b, j, 0, 0))],
        out_shape=[jax.ShapeDtypeStruct((nb * t, B_HEADS * B_DV), F32),
                   jax.ShapeDtypeStruct((nb, B_HEADS, B_DK, B_DV), F32)],
        scratch_shapes=[pltpu.VMEM((nh, B_DV, B_DK), F32)],
        compiler_params=pltpu.CompilerParams(
            dimension_semantics=("parallel", "parallel"), vmem_limit_bytes=VMEM_LIMIT),
        name="hgrn2_prompt",
    )(proj, proj, proj, proj, lb.reshape(B_HEADS, 1, B_DK), on.reshape(1, B_DV))


def _hgrn2_sample_body(q_ref, f_ref, v_ref, g_ref, lb_ref, on_ref, s0_ref, o_ref, s_ref, st_ref, *, t):
    for h in range(B_HEADS):
        st_ref[h] = s0_ref[h].T
    _hgrn2_run(q_ref, f_ref, v_ref, g_ref, lb_ref, on_ref[...], st_ref, o_ref, nh=B_HEADS, t=t, c=t)
    for h in range(B_HEADS):
        s_ref[h] = st_ref[h].T


def hgrn2_sample(proj, lb, on, s0, *, row0, nb, t):
    rb0 = row0 // t
    w = B_HEADS * LANES
    col = lambda c0: (lambda b: (rb0 + b, c0 // w))
    return pl.pallas_call(
        functools.partial(_hgrn2_sample_body, t=t),
        grid=(nb,),
        in_specs=[pl.BlockSpec((t, w), col(EV_QB)), pl.BlockSpec((t, w), col(EV_FB)),
                  pl.BlockSpec((t, w), col(EV_IB)), pl.BlockSpec((t, w), col(EV_GB)),
                  pl.BlockSpec((B_HEADS, 1, B_DK), lambda b: (0, 0, 0)),
                  pl.BlockSpec((1, B_DV), lambda b: (0, 0)),
                  pl.BlockSpec((None, B_HEADS, B_DK, B_DV), lambda b: (b, 0, 0, 0))],
        out_specs=[pl.BlockSpec((t, w), lambda b: (b, 0)),
                   pl.BlockSpec((None, B_HEADS, B_DK, B_DV), lambda b: (b, 0, 0, 0))],
        out_shape=[jax.ShapeDtypeStruct((nb * t, w), F32),
                   jax.ShapeDtypeStruct((nb, B_HEADS, B_DK, B_DV), F32)],
        scratch_shapes=[pltpu.VMEM((B_HEADS, B_DV, B_DK), F32)],
        compiler_params=pltpu.CompilerParams(dimension_semantics=("parallel",), vmem_limit_bytes=VMEM_LIMIT),
        name="hgrn2_sample",
    )(proj, proj, proj, proj, lb.reshape(B_HEADS, 1, B_DK), on.reshape(1, B_DV), s0)


def _mlstm_chunk(qm, km_h, v, ic_row, ic_col, cb_col, cb_row, ct, n, m, c):
    r = lax.broadcasted_iota(jnp.int32, (c, c), 0)
    s = lax.broadcasted_iota(jnp.int32, (c, c), 1)
    dmat = jnp.where(r >= s, cb_col - cb_row + ic_row, MASK_NEG)
    m_state = cb_col + m
    m_t = jnp.maximum(m_state, jnp.max(dmat, axis=-1, keepdims=True))
    w = jnp.exp(dmat - m_t)
    w0 = jnp.exp(m_state - m_t)
    qb16 = qm.astype(BF16)
    qk = _dot_nt(qb16, km_h.astype(BF16)) * w
    num = (jnp.dot(qk.astype(BF16), v.astype(BF16), preferred_element_type=F32)
           + w0 * jnp.dot(qb16, ct.astype(BF16), preferred_element_type=F32))
    den = jnp.sum(qk, axis=-1, keepdims=True) + w0 * jnp.sum(qm * n, axis=-1, keepdims=True)
    hc = num / jnp.maximum(jnp.abs(den), jnp.exp(-m_t))
    m_last = m_t[c - 1:c, :]
    cb_last = cb_col[c - 1:c, :]
    ws = jnp.exp(cb_last - cb_col + ic_col - m_last)
    fs = jnp.exp(cb_last + m - m_last)
    ct = fs * ct + lax.dot_general(km_h.astype(BF16), (v * ws).astype(BF16), (((0,), (0,)), ((), ())),
                                   preferred_element_type=F32)
    n = fs * n + jnp.sum(ws * km_h, axis=0, keepdims=True)
    return hc, ct, n, m_last


def _mlstm_seq(q_ref, k_ref, v_ref, og_ref, g_ref, gb_ref, on, npairs, states, o_ref, *, t, c):
    lane = lax.broadcasted_iota(jnp.int32, (c, LANES), 1)

    def step(i, carry):
        r0 = pl.multiple_of(i * c, c)
        new = []
        for p in range(npairs):
            psl = slice(LANES * p, LANES * (p + 1))
            g = g_ref[pl.ds(r0, c), psl] + gb_ref[p]
            cb = _cumsum_rows(_log_sigmoid(g))
            gt = g.T
            cbt = cb.T
            q = q_ref[pl.ds(r0, c), psl]
            k = k_ref[pl.ds(r0, c), psl] * (C_DK ** -0.5)
            for half in range(2):
                ct, n, m = carry[2 * p + half]
                keep = (lane < C_DK) if half == 0 else (lane >= C_DK)
                vsl = slice(C_DV * (2 * p + half), C_DV * (2 * p + half + 1))
                hc, ct, n, m = _mlstm_chunk(
                    jnp.where(keep, q, 0.0), jnp.where(keep, k, 0.0), v_ref[pl.ds(r0, c), vsl],
                    gt[half:half + 1, :], g[:, half:half + 1], cb[:, 2 + half:3 + half], cbt[2 + half:3 + half, :],
                    ct, n, m, c)
                og = og_ref[pl.ds(r0, c), vsl]
                hc = hc * lax.rsqrt(jnp.mean(hc * hc, axis=-1, keepdims=True) + EPS) * on * _sigmoid(og)
                o_ref[pl.ds(r0, c), vsl] = hc
                new.append((ct, n, m))
        return tuple(new)

    return lax.fori_loop(0, t // c, step, tuple(states))


def _mlstm_body(*refs, t, c, npairs, has_state):
    q_ref, k_ref, v_ref, og_ref, g_ref, gb_ref, on_ref = refs[:7]
    refs = refs[7:]
    if has_state:
        c0_ref, n0_ref, m0_ref = refs[:3]
        refs = refs[3:]
    o_ref, c_ref, n_ref, m_ref = refs
    states = []
    for h in range(2 * npairs):
        half = h % 2
        if has_state:
            ct_h = c0_ref[h]
            z = jnp.zeros((C_DK, C_DV), F32)
            ct = jnp.concatenate([ct_h, z] if half == 0 else [z, ct_h], axis=0)
            n_h = n0_ref[h:h + 1, :]
            zn = jnp.zeros((1, C_DK), F32)
            n = jnp.concatenate([n_h, zn] if half == 0 else [zn, n_h], axis=1)
            m = m0_ref[:, h:h + 1]
        else:
            ct, n, m = jnp.zeros((2 * C_DK, C_DV), F32), jnp.zeros((1, 2 * C_DK), F32), jnp.zeros((1, 1), F32)
        states.append((ct, n, m))
    out = _mlstm_seq(q_ref, k_ref, v_ref, og_ref, g_ref, gb_ref, on_ref[...], npairs, states, o_ref, t=t, c=c)
    for h, (ct, n, m) in enumerate(out):
        rows = slice(C_DK * (h % 2), C_DK * (h % 2 + 1))
        c_ref[h] = ct[rows, :]
        n_ref[h] = n[:, rows]
        m_ref[h] = jnp.broadcast_to(m, (1, LANES))


def mlstm(proj, gb, on, state, *, row0, nb, t, npairs):
    c = math.gcd(CHUNK, t)
    rb0 = row0 // t
    npg = (C_HEADS // 2) // npairs
    hp = 2 * npairs
    qw, vw = npairs * LANES, npairs * 2 * C_DV
    in_specs = [
        pl.BlockSpec((t, qw), lambda b, j: (rb0 + b, OD_QC // qw + j)),
        pl.BlockSpec((t, qw), lambda b, j: (rb0 + b, OD_KC // qw + j)),
        pl.BlockSpec((t, vw), lambda b, j: (rb0 + b, OD_VC // vw + j)),
        pl.BlockSpec((t, vw), lambda b, j: (rb0 + b, OD_OC // vw + j)),
        pl.BlockSpec((t, qw), lambda b, j: (rb0 + b, OD_G // qw + j)),
        pl.BlockSpec((npairs, 1, LANES), lambda b, j: (j, 0, 0)),
        pl.BlockSpec((1, C_DV), lambda b, j: (0, 0)),
    ]
    args = [proj, proj, proj, proj, proj, gb, on.reshape(1, C_DV)]
    if state is not None:
        in_specs += [pl.BlockSpec((None, hp, C_DK, C_DV), lambda b, j: (b, j, 0, 0)),
                     pl.BlockSpec((None, hp, C_DK), lambda b, j: (b, j, 0)),
                     pl.BlockSpec((None, 1, hp), lambda b, j: (b, 0, j))]
        args += list(state)
    return pl.pallas_call(
        functools.partial(_mlstm_body, t=t, c=c, npairs=npairs, has_state=state is not None),
        grid=(nb, npg),
        in_specs=in_specs,
        out_specs=[pl.BlockSpec((t, vw), lambda b, j: (b, j)),
                   pl.BlockSpec((None, hp, C_DK, C_DV), lambda b, j: (b, j, 0, 0)),
                   pl.BlockSpec((None, hp, 1, C_DK), lambda b, j: (b, j, 0, 0)),
                   pl.BlockSpec((None, hp, 1, LANES), lambda b, j: (b, j, 0, 0))],
        out_shape=[jax.ShapeDtypeStruct((nb * t, C_HEADS * C_DV), F32),
                   jax.ShapeDtypeStruct((nb, C_HEADS, C_DK, C_DV), F32),
                   jax.ShapeDtypeStruct((nb, C_HEADS, 1, C_DK), F32),
                   jax.ShapeDtypeStruct((nb, C_HEADS, 1, LANES), F32)],
        compiler_params=pltpu.CompilerParams(
            dimension_semantics=("parallel", "parallel"), vmem_limit_bytes=VMEM_LIMIT),
        name="mlstm",
    )(*args)


def rms_norm(x, g):
    xf = x.astype(F32)
    y = xf * lax.rsqrt(jnp.mean(xf * xf, axis=-1, keepdims=True) + EPS)
    return (y * g.astype(F32)).astype(x.dtype)


def split_cols(h, sizes):
    cuts = [int(c) for c in np.cumsum(sizes)[:-1]]
    return jnp.split(h, cuts, axis=-1)


def to_blocks(x, nb):
    b, t = x.shape[:2]
    return jnp.swapaxes(x.reshape((b, nb, t // nb) + x.shape[2:]), 0, 1)


def from_blocks(x):
    x = jnp.swapaxes(x, 0, 1)
    return x.reshape((x.shape[0], x.shape[1] * x.shape[2]) + x.shape[3:])


def rel_bucket(rel):
    n = jnp.maximum(rel, 0)
    max_exact = REL_BUCKETS // 2
    nf = jnp.maximum(n, 1).astype(F32)
    large = max_exact + (jnp.log(nf / max_exact) / math.log(REL_MAX_DIST / max_exact)
                         * (REL_BUCKETS - max_exact)).astype(jnp.int32)
    return jnp.where(n < max_exact, n, jnp.minimum(large, REL_BUCKETS - 1))


def rope(x, pos):
    half = ROPE_DIM // 2
    inv = ROPE_THETA ** (-jnp.arange(half, dtype=F32) / half)
    ang = pos.astype(F32)[:, None] * inv[None, :]
    shp = (1, pos.shape[0]) + (1,) * (x.ndim - 3) + (half,)
    cos, sin = jnp.cos(ang).reshape(shp), jnp.sin(ang).reshape(shp)
    xf = x.astype(F32)
    x1, x2 = xf[..., :half], xf[..., half:]
    return jnp.concatenate([x1 * cos - x2 * sin, x2 * cos + x1 * sin], axis=-1).astype(x.dtype)


def dsa_select(q_idx, w_idx, k_idx, q_pos, k_pos, n_sel):
    s = jnp.einsum('bthd,bsd->bths', q_idx.astype(F32), k_idx.astype(F32))
    score = jnp.einsum('bths,bth->bts', jax.nn.relu(s), w_idx.astype(F32))
    allowed = k_pos[None, None, :] <= q_pos[None, :, None]
    score = jnp.where(allowed, score, MASK_NEG)
    _, sel = lax.top_k(score, n_sel)
    valid = sel <= q_pos[None, :, None]
    return sel, valid


def dsa_attend(q, kg, vg, q_pos, sel, valid, rel_bias):
    b, t, h, dh = q.shape
    g = h // A_KV_HEADS
    n_sel = sel.shape[-1]
    qg = q.reshape(b, t, A_KV_HEADS, g, dh)
    lg = jnp.einsum('btkgd,btskd->btkgs', qg, kg).astype(F32) * (dh ** -0.5)
    bias = rel_bias.astype(F32)[rel_bucket(q_pos[None, :, None] - sel)]
    lg = lg + jnp.moveaxis(bias, -1, 2).reshape(b, t, A_KV_HEADS, g, n_sel)
    lg = jnp.where(valid[:, :, None, None, :], lg, MASK_NEG)
    pr = jax.nn.softmax(lg, axis=-1).astype(vg.dtype)
    return jnp.einsum('btkgs,btskd->btkgd', pr, vg).reshape(b, t, h, dh)


def dsa_prompt(q, k, v, q_idx, w_idx, k_idx, rel_bias):
    b, t = q.shape[:2]
    n_sel = min(TOPK_MAX, t // 4)
    qb = math.gcd(Q_BLOCK, t)
    nb = t // qb
    k_pos = jnp.arange(t)
    bidx = jnp.arange(b)[:, None, None]

    def block(args):
        q_b, qi_b, wi_b, qp = args
        sel, valid = dsa_select(qi_b, wi_b, k_idx, qp, k_pos, n_sel)
        return dsa_attend(q_b, k[bidx, sel], v[bidx, sel], qp, sel, valid, rel_bias)

    out = lax.map(block, (to_blocks(q, nb), to_blocks(q_idx, nb), to_blocks(w_idx, nb), k_pos.reshape(nb, qb)))
    return from_blocks(out)


def gather_paged_rows(pool, layer, new_rows, page_table, sel, past):
    b = sel.shape[0]
    in_past = sel < past
    ps = jnp.minimum(sel, past - 1)
    phys = jnp.take_along_axis(page_table, (ps // PAGE_SIZE).reshape(b, -1), axis=1).reshape(sel.shape)
    past_rows = pool[layer, phys, ps % PAGE_SIZE]
    new_idx = jnp.clip(sel - past, 0, new_rows.shape[1] - 1)
    cur_rows = new_rows[jnp.arange(b)[:, None, None], new_idx]
    cond = in_past.reshape(in_past.shape + (1,) * (new_rows.ndim - 2))
    return jnp.where(cond, past_rows, cur_rows)


def dsa_sample(q, k_new, v_new, q_idx, w_idx, kidx_new, pool_k, pool_v, pool_kidx, layer, page_table, rel_bias):
    b, t = q.shape[:2]
    past = page_table.shape[1] * PAGE_SIZE
    n_sel = min(TOPK_MAX, (past + t) // 4)
    kidx_past = pool_kidx[layer, page_table].reshape(b, past, IDX_DIM)
    kidx_all = jnp.concatenate([kidx_past, kidx_new.astype(kidx_past.dtype)], axis=1)
    q_pos = past + jnp.arange(t)
    sel, valid = dsa_select(q_idx, w_idx, kidx_all, q_pos, jnp.arange(past + t), n_sel)
    kg = gather_paged_rows(pool_k, layer, k_new, page_table, sel, past)
    vg = gather_paged_rows(pool_v, layer, v_new, page_table, sel, past)
    return dsa_attend(q, kg, vg, q_pos, sel, valid, rel_bias)


def gla_chunked(q, k, v, log_f, s0):
    b, t, h, dk = q.shape
    c = math.gcd(CHUNK, t)
    nc = t // c
    tri = jnp.tril(jnp.ones((c, c), dtype=bool))

    def step(s, inp):
        qc, kc, vc, lf = inp
        cb = jnp.cumsum(lf, axis=1)
        diff = jnp.where(tri[None, :, :, None, None], cb[:, :, None] - cb[:, None, :], MASK_NEG)
        attn = jnp.einsum('bthd,btshd->bhts', qc, jnp.exp(diff) * kc[:, None])
        o = jnp.einsum('bhts,bshv->bthv', attn, vc) + jnp.einsum('bthd,bhdv->bthv', qc * jnp.exp(cb), s)
        c_last = cb[:, -1]
        s = jnp.exp(c_last)[..., None] * s + jnp.einsum('bshd,bshv->bhdv', kc * jnp.exp(c_last[:, None] - cb), vc)
        return s, o

    xs = tuple(to_blocks(a.astype(F32), nc) for a in (q, k, v, log_f))
    s, o = lax.scan(step, s0.astype(F32), xs)
    return from_blocks(o).astype(v.dtype), s


def mlstm_chunked(q, k, v, i_pre, f_pre, c0, n0, m0):
    b, t, h, dk = q.shape
    c = math.gcd(CHUNK, t)
    nc = t // c
    tri = jnp.tril(jnp.ones((c, c), dtype=bool))
    log_f = jax.nn.log_sigmoid(f_pre.astype(F32))

    def step(carry, inp):
        cm, n, m = carry
        qc, kc, vc, ic, lf = inp
        cb = jnp.cumsum(lf, axis=1)
        dmat = jnp.where(tri[None, :, :, None], cb[:, :, None] - cb[:, None] + ic[:, None], MASK_NEG)
        m_state = cb + m[:, None]
        m_t = jnp.maximum(m_state, jnp.max(dmat, axis=2))
        w = jnp.exp(dmat - m_t[:, :, None])
        w0 = jnp.exp(m_state - m_t)
        qk = jnp.einsum('bthd,bshd->btsh', qc, kc) * w
        num = jnp.einsum('btsh,bshv->bthv', qk, vc) + w0[..., None] * jnp.einsum('bthd,bhvd->bthv', qc, cm)
        den = jnp.sum(qk, axis=2) + w0 * jnp.einsum('bthd,bhd->bth', qc, n)
        hc = num / jnp.maximum(jnp.abs(den), jnp.exp(-m_t))[..., None]
        m_last = m_t[:, -1]
        ws = jnp.exp(cb[:, -1:] - cb + ic - m_last[:, None])
        fs = jnp.exp(cb[:, -1] + m - m_last)
        cm = fs[..., None, None] * cm + jnp.einsum('bshv,bshd->bhvd', vc * ws[..., None], kc)
        n = fs[..., None] * n + jnp.einsum('bsh,bshd->bhd', ws, kc)
        return (cm, n, m_last), hc

    xs = tuple(to_blocks(a.astype(F32), nc) for a in (q, k, v, i_pre, log_f))
    (cm, n, m), hs = lax.scan(step, (c0.astype(F32), n0.astype(F32), m0.astype(F32)), xs)
    return from_blocks(hs).astype(v.dtype), (cm, n, m)


def mla_kv(ckv, kpe, w_kvb, kn):
    b, l, _ = ckv.shape
    kv = (ckv @ w_kvb).reshape(b, l, D_HEADS, NOPE_DIM + D_VDIM)
    k_nope, v = kv[..., :NOPE_DIM], kv[..., NOPE_DIM:]
    k_pe = jnp.broadcast_to(kpe[:, :, None, :], (b, l, D_HEADS, ROPE_DIM)).astype(k_nope.dtype)
    return rms_norm(jnp.concatenate([k_nope, k_pe], axis=-1), kn), v


def causal_attn(q, k, v, q_pos, k_pos):
    b, t, h, dq = q.shape
    scale = dq ** -0.5

    def attend(args):
        q_b, qp = args
        lg = jnp.einsum('bthd,bshd->bhts', q_b, k).astype(F32) * scale
        lg = jnp.where(k_pos[None, None, None, :] <= qp[None, None, :, None], lg, MASK_NEG)
        pr = jax.nn.softmax(lg, axis=-1).astype(v.dtype)
        return jnp.einsum('bhts,bshd->bthd', pr, v)

    qb = math.gcd(Q_BLOCK, t)
    nb = t // qb
    if nb == 1:
        return attend((q, q_pos))
    return from_blocks(lax.map(attend, (to_blocks(q, nb), q_pos.reshape(nb, qb))))


def mla_sample(qd, ckv_new, kpe_new, pool_ckv, pool_kpe, layer, page_table, w_kvb, kn):
    b, t = qd.shape[:2]
    past = page_table.shape[1] * PAGE_SIZE
    q_pos = past + jnp.arange(t)
    k_pos = jnp.arange(past + t)

    def one(args):
        q1, c1, r1, pt = args
        c_all = jnp.concatenate([pool_ckv[layer, pt].reshape(past, KV_LORA), c1.astype(pool_ckv.dtype)], axis=0)[None]
        r_all = jnp.concatenate([pool_kpe[layer, pt].reshape(past, ROPE_DIM), r1.astype(pool_kpe.dtype)], axis=0)[None]
        k, v = mla_kv(c_all, r_all, w_kvb, kn)
        return causal_attn(q1[None], k, v, q_pos, k_pos)[0]

    return lax.map(one, (qd, ckv_new, kpe_new, page_table))


def mem_attend_core(q, mk, mv, qn):
    b, t, _ = q.shape
    q = rms_norm(q.reshape(b, t, MEM_HEADS, HEAD_DIM), qn)
    lg = jnp.einsum('bthd,bshd->bhts', q, mk.astype(q.dtype)).astype(F32) * (HEAD_DIM ** -0.5)
    pr = jax.nn.softmax(lg, axis=-1).astype(q.dtype)
    o = jnp.einsum('bhts,bshd->bthd', pr, mv.astype(q.dtype))
    return o.reshape(b, t, MEM_HEADS * HEAD_DIM)


def hgrn2_mixer(proj, s0, b_on, lower_bound):
    b, t, _ = proj.shape
    qb, fb, ib, gb = (proj[..., c:c + B_HEADS * B_DK] for c in (EV_QB, EV_FB, EV_IB, EV_GB))
    lb = lower_bound.reshape(B_HEADS, B_DK)
    f_pre = fb.reshape(b, t, B_HEADS, B_DK).astype(F32)
    log_f = jnp.logaddexp(jnp.log(jnp.maximum(lb, LB_FLOOR)), jnp.log1p(-lb) + jax.nn.log_sigmoid(f_pre))
    k_b = (1.0 - lb) * jax.nn.sigmoid(-f_pre)
    q_b = jax.nn.silu(qb.reshape(b, t, B_HEADS, B_DK))
    ob, s_new = gla_chunked(q_b, k_b, ib.reshape(b, t, B_HEADS, B_DV), log_f, s0)
    ob = rms_norm(ob, b_on) * jax.nn.silu(gb.reshape(b, t, B_HEADS, B_DV))
    return ob.reshape(b, t, -1), s_new


def dsa_bias_tables(rel_bias, ts, past):
    table = rel_bias.astype(F32)[rel_bucket(jnp.arange(2 * LANES))]
    r = np.arange(LANES)[:, None]
    c = np.arange(LANES)[None, :]
    diag = table[np.clip(r - c, 0, 2 * LANES - 1)]
    prev = table[LANES + r - c]
    far = jnp.broadcast_to(table[2 * LANES - 1], (LANES, LANES, A_HEADS))
    tiles = jnp.moveaxis(jnp.stack([diag, prev, far]), -1, 0)
    rel = past + np.arange(ts)[:, None] - np.arange(past + LANES)[None, :]
    sample = jnp.moveaxis(table[np.clip(rel, 0, 2 * LANES - 1)], -1, 0)
    return tiles, sample


def rope_rows(x, pos):
    half = ROPE_DIM // 2
    inv = ROPE_THETA ** (-jnp.arange(half, dtype=F32) / half)
    ang = pos.astype(F32)[:, None] * inv[None, :]
    shp = (pos.shape[0],) + (1,) * (x.ndim - 2) + (half,)
    cos, sin = jnp.cos(ang).reshape(shp), jnp.sin(ang).reshape(shp)
    x1, x2 = x[..., :half], x[..., half:]
    return jnp.concatenate([x1 * cos - x2 * sin, x2 * cos + x1 * sin], axis=-1)


def mla_sample_pallas(qd, ckv, kpe, st, o, kn, wb):
    b, t = qd.shape[:2]
    qn = (qd[..., :NOPE_DIM] * kn[:NOPE_DIM]).reshape(b * t, D_HEADS * NOPE_DIM)
    qlat = matmul(qn, wb['d_kt_blockdiag'][o])
    qlat = qlat.reshape(b, t, D_HEADS, KV_LORA).transpose(0, 2, 1, 3).reshape(b, D_HEADS * t, KV_LORA)
    qpe = (qd[..., NOPE_DIM:] * kn[NOPE_DIM:]).transpose(0, 2, 1, 3).reshape(b, D_HEADS * t, ROPE_DIM)
    cnew = jnp.pad(ckv, ((0, 0), (0, LANES - t), (0, 0)))
    rnew_t = jnp.swapaxes(jnp.pad(kpe, ((0, 0), (0, LANES - t), (0, 0))), 1, 2)
    od = mla_sample_attend(qlat.astype(BF16), qpe.astype(BF16), cnew, rnew_t, wb['d_kt'][o], wb['d_v'][o],
                           st['cache_d_ckv'], st['pool_kpe_t'], st['pt_flat'], o,
                           nb=b, ts=t, n_pages=st['page_table'].shape[1])
    return od.reshape(b, t, D_HEADS, D_VDIM)


def odd_mixer(proj, pos, mode, st, p, o, wb):
    b, t, _ = proj.shape
    qc, kc, vc, ic, fc, oc, qa, ckv, kpe = split_cols(proj, ODD_SPLITS)
    gate_b = p['c_gate_b'][o].astype(F32)
    i_pre = ic.astype(F32) + gate_b[0]
    f_pre = fc.astype(F32) + gate_b[1]
    q_c = qc.reshape(b, t, C_HEADS, C_DK)
    k_c = kc.reshape(b, t, C_HEADS, C_DK) * (C_DK ** -0.5)
    v_c = vc.reshape(b, t, C_HEADS, C_DV)
    if mode == 'prompt':
        c0 = jnp.zeros((b, C_HEADS, C_DV, C_DK), F32)
        n0 = jnp.zeros((b, C_HEADS, C_DK), F32)
        m0 = jnp.zeros((b, C_HEADS), F32)
    else:
        c0, n0, m0 = st['state_c_C'][o], st['state_c_n'][o], st['state_c_m'][o]
    hc, (c1, n1, m1) = mlstm_chunked(q_c, k_c, v_c, i_pre, f_pre, c0, n0, m0)
    hc = rms_norm(hc, p['c_on'][o]) * jax.nn.sigmoid(oc.reshape(b, t, C_HEADS, C_DV))
    qf = matmul(qa.reshape(b * t, Q_LORA), wb['w_d_qb'][o], gain=p['d_qa_g'][o])
    qf = qf.reshape(b, t, D_HEADS, NOPE_DIM + ROPE_DIM)
    qd = rms_norm(jnp.concatenate([qf[..., :NOPE_DIM], rope(qf[..., NOPE_DIM:], pos)], axis=-1), p['d_qn'][o])
    ckv = rms_norm(ckv, p['d_kv_g'][o])
    kpe = rope(kpe, pos)
    if mode == 'prompt':
        kd, vd = mla_kv(ckv, kpe, p['w_d_kvb'][o], p['d_kn'][o])
        od = causal_attn(qd, kd, vd, pos, pos)
    else:
        od = mla_sample_pallas(qd, ckv, kpe, st, o, p['d_kn'][o], wb)
    mixed = jnp.concatenate([hc.reshape(b, t, -1).astype(od.dtype), od.reshape(b, t, -1)], axis=-1)
    return mixed, (ckv, kpe, c1, n1, m1)


def kernel(x_prompt, x_sample, cache_a_k, cache_a_v, cache_a_kidx, state_b, state_c_C, state_c_n, state_c_m, cache_d_ckv, cache_d_kpe, cache_mem_k, cache_mem_v, page_table, mem_prompt, g_ffn1, w_ffn1_gu, w_ffn1_dn, g_mix, w_in_even, w_in_odd, w_mix_out, rel_bias, a_qn, a_kn, b_lb, b_on, c_gate_b, c_on, d_qa_g, d_kv_g, w_d_qb, w_d_kvb, d_qn, d_kn, g_mem, w_mem_q, w_mem_kv, w_mem_o, mem_qn, mem_kn, g_ffn2, w_ffn2_gu, w_ffn2_dn):
    p = {'rel_bias': rel_bias, 'a_qn': a_qn, 'a_kn': a_kn, 'b_on': b_on, 'c_gate_b': c_gate_b, 'c_on': c_on,
         'd_qa_g': d_qa_g, 'd_kv_g': d_kv_g, 'w_d_kvb': w_d_kvb, 'd_qn': d_qn, 'd_kn': d_kn,
         'mem_qn': mem_qn, 'mem_kn': mem_kn}
    bp, tp, d = x_prompt.shape
    bs, ts, _ = x_sample.shape
    n_p, n_s = bp * tp, bs * ts
    n_mem = mem_prompt.shape[1]
    past = page_table.shape[1] * PAGE_SIZE

    wo = w_in_odd.astype(BF16)
    n_odd = wo.shape[0]
    oc = np.cumsum([0] + ODD_SPLITS)
    gate_cols = []
    for pr in range(C_HEADS // 2):
        gate_cols += [wo[:, :, oc[3] + 2 * pr:oc[3] + 2 * pr + 2], wo[:, :, oc[4] + 2 * pr:oc[4] + 2 * pr + 2],
                      jnp.zeros((n_odd, d, LANES - 4), BF16)]
    in_odd = jnp.concatenate(
        [wo[:, :, oc[0]:oc[3]], wo[:, :, oc[5]:oc[6]]] + gate_cols
        + [wo[:, :, oc[6]:oc[9]], jnp.zeros((n_odd, d, LANES - ROPE_DIM), BF16)], axis=-1)
    gb = c_gate_b.astype(F32)
    gate_bias = jnp.concatenate(
        [gb[:, 0].reshape(n_odd, C_HEADS // 2, 2), gb[:, 1].reshape(n_odd, C_HEADS // 2, 2),
         jnp.zeros((n_odd, C_HEADS // 2, LANES - 4), F32)], axis=-1)[:, :, None, :]
    we = w_in_even.astype(BF16)
    c_wi = sum(EVEN_SPLITS[:4])
    c_ki = c_wi + IDX_HEADS
    c_qb = c_ki + IDX_DIM
    n_even = we.shape[0]
    c_ka = EVEN_SPLITS[0]
    in_even = jnp.concatenate(
        [we[:, :, :c_ka], we[:, :, c_qb:], we[:, :, c_ka:c_wi], we[:, :, c_ki:c_qb], we[:, :, c_ki:c_qb],
         we[:, :, c_wi:c_ki], jnp.zeros((n_even, d, LANES - IDX_HEADS), BF16)], axis=-1)
    kvb = w_d_kvb.astype(BF16).reshape(-1, KV_LORA, D_HEADS, NOPE_DIM + D_VDIM)
    d_kt3 = kvb[..., :NOPE_DIM].transpose(0, 2, 3, 1)
    eye = jnp.eye(D_HEADS, dtype=BF16)
    d_kt_bd = (d_kt3[:, :, :, None, :] * eye[None, :, None, :, None]).reshape(
        -1, D_HEADS * NOPE_DIM, D_HEADS * KV_LORA)
    wb = {
        'ffn1_gu': w_ffn1_gu.astype(BF16), 'ffn1_dn': w_ffn1_dn.astype(BF16),
        'ffn2_gu': w_ffn2_gu.astype(BF16), 'ffn2_dn': w_ffn2_dn.astype(BF16),
        'in_even': in_even,
        'in_odd': in_odd,
        'mix_out': w_mix_out.astype(BF16), 'w_d_qb': w_d_qb.astype(BF16), 'w_d_kvb': w_d_kvb.astype(BF16),
        'mem_q': w_mem_q.astype(BF16), 'mem_kv': w_mem_kv.astype(BF16), 'mem_o': w_mem_o.astype(BF16),
        'd_kt': d_kt3.reshape(-1, D_HEADS * NOPE_DIM, KV_LORA), 'd_kt_blockdiag': d_kt_bd,
        'd_v': kvb[..., NOPE_DIM:].transpose(0, 2, 1, 3),
    }
    bias_tiles, bias_sample = dsa_bias_tables(rel_bias, ts, past)
    n_pages = page_table.shape[1]
    n_phys = cache_a_k.shape[1]
    pool_k = cache_a_k.reshape(-1, n_phys, PAGE_SIZE * A_KV_HEADS, HEAD_DIM)
    pool_v = cache_a_v.reshape(-1, n_phys, PAGE_SIZE * A_KV_HEADS, HEAD_DIM)
    pool_kidx_t = jnp.swapaxes(cache_a_kidx, 2, 3)
    pool_kpe_t = jnp.swapaxes(cache_d_kpe, 2, 3)
    mem_k_rows = cache_mem_k.reshape(DEPTH, bs, n_mem * MEM_HEADS, HEAD_DIM)
    mem_v_rows = cache_mem_v.reshape(DEPTH, bs, n_mem * MEM_HEADS, HEAD_DIM)
    pt_flat = page_table.reshape(-1)

    lb_soft = jax.nn.softmax(b_lb.astype(F32), axis=0)
    lower_bound = jnp.cumsum(lb_soft, axis=0) - lb_soft[0]

    pos_all = jnp.concatenate([jnp.tile(jnp.arange(tp), bp), jnp.tile(past + jnp.arange(ts), bs)])
    st = {'cache_a_k': cache_a_k, 'cache_a_v': cache_a_v, 'cache_a_kidx': cache_a_kidx, 'state_b': state_b,
          'state_c_C': state_c_C, 'state_c_n': state_c_n, 'state_c_m': state_c_m,
          'cache_d_ckv': cache_d_ckv, 'pool_kpe_t': pool_kpe_t, 'page_table': page_table, 'pt_flat': pt_flat}

    x = jnp.concatenate([x_prompt.reshape(n_p, d), x_sample.reshape(n_s, d)], axis=0)
    mem2d = mem_prompt.reshape(bp * n_mem, d)
    new_p = {n: [] for n in ('a_k', 'a_v', 'a_kidx', 'b_S', 'c_C', 'c_n', 'c_m', 'd_ckv', 'd_kpe', 'mem_k', 'mem_v')}
    new_s = {n: [] for n in ('a_k', 'a_v', 'a_kidx', 'b_S', 'c_C', 'c_n', 'c_m', 'd_ckv', 'd_kpe')}

    for l in range(DEPTH):
        x = ffn_half_step(x, g_ffn1[l], wb['ffn1_gu'][l], wb['ffn1_dn'][l])
        if l % 2 == 0:
            e = l // 2
            proj = matmul(x, wb['in_even'][e], gain=g_mix[l], tn=1280)
            ka32, ka16 = head_norm(proj, EV_KA // (A_KV_HEADS * HEAD_DIM), A_KV_HEADS, a_kn[e])
            oa_p = dsa_prompt_attend(proj, ka16, bias_tiles, a_qn[e], nb=bp, t=tp)
            sel_past, sel_new = dsa_sample_select(proj, pool_kidx_t, pt_flat, e, row0=n_p, nb=bs, ts=ts,
                                                  n_pages=n_pages)
            oa_s = dsa_sample_attend(proj, ka32, sel_past, sel_new, bias_sample, pool_k, pool_v, pt_flat, e, a_qn[e],
                                     row0=n_p, nb=bs, ts=ts, n_pages=n_pages)
            lb = lower_bound[e].reshape(B_HEADS, B_DK)
            ob_p, s_p = hgrn2_prompt(proj, lb, b_on[e], nb=bp, t=tp)
            ob_s, s_s = hgrn2_sample(proj, lb, b_on[e], state_b[e], row0=n_p, nb=bs, t=ts)
            mixed_p = jnp.concatenate([oa_p, ob_p], axis=-1)
            mixed_s = jnp.concatenate([oa_s, ob_s], axis=-1)
            va = proj[:, EV_VA:EV_VA + A_KV_HEADS * HEAD_DIM]
            ki = proj[:, EV_KIKI:EV_KIKI + IDX_DIM]
            for new, rows, nb_, t_, s_new in ((new_p, slice(0, n_p), bp, tp, s_p), (new_s, slice(n_p, None), bs, ts, s_s)):
                new['a_k'].append(ka32[rows].reshape(nb_, t_, A_KV_HEADS, HEAD_DIM))
                new['a_v'].append(va[rows].reshape(nb_, t_, A_KV_HEADS, HEAD_DIM))
                new['a_kidx'].append(ki[rows].reshape(nb_, t_, IDX_DIM))
                new['b_S'].append(s_new)
        else:
            o = l // 2
            proj = matmul(x, wb['in_odd'][o], gain=g_mix[l], tn=896)
            hc_p, cc_p, cn_p, cm_p = mlstm(proj, gate_bias[o], c_on[o], None, row0=0, nb=bp, t=tp, npairs=1)
            hc_s, cc_s, cn_s, cm_s = mlstm(
                proj, gate_bias[o], c_on[o],
                (jnp.swapaxes(state_c_C[o], -1, -2), state_c_n[o], state_c_m[o].reshape(bs, 1, C_HEADS)),
                row0=n_p, nb=bs, t=ts, npairs=C_HEADS // 2)
            qf = matmul(proj, wb['w_d_qb'][o], gain=d_qa_g[o], a_col=OD_QA // Q_LORA)
            qf = qf.reshape(n_p + n_s, D_HEADS, NOPE_DIM + ROPE_DIM)
            qd = rms_norm(jnp.concatenate([qf[..., :NOPE_DIM], rope_rows(qf[..., NOPE_DIM:], pos_all)], axis=-1),
                          d_qn[o])
            ckv = rms_norm(proj[:, OD_CKV:OD_CKV + KV_LORA], d_kv_g[o])
            kpe = rope_rows(proj[:, OD_KPE:OD_KPE + ROPE_DIM], pos_all)
            kv = matmul(ckv[:n_p], wb['w_d_kvb'][o]).reshape(n_p, D_HEADS, NOPE_DIM + D_VDIM)
            k_pe = jnp.broadcast_to(kpe[:n_p, None, :], (n_p, D_HEADS, ROPE_DIM))
            kd = rms_norm(jnp.concatenate([kv[..., :NOPE_DIM], k_pe], axis=-1), d_kn[o])
            pad = ((0, 0), (0, 0), (0, 2 * LANES - NOPE_DIM - ROPE_DIM))
            q16 = jnp.pad(qd[:n_p], pad).astype(BF16).reshape(n_p, D_HEADS * 2 * LANES)
            k16 = jnp.pad(kd, pad).astype(BF16).reshape(n_p, D_HEADS * 2 * LANES)
            v16 = kv[..., NOPE_DIM:].astype(BF16).reshape(n_p, D_HEADS * D_VDIM)
            od_p = causal_attend(q16, k16, v16, nb=bp, t=tp, heads=D_HEADS, dv=D_VDIM,
                                 scale=(NOPE_DIM + ROPE_DIM) ** -0.5)
            od_s = mla_sample_pallas(qd[n_p:].reshape(bs, ts, D_HEADS, NOPE_DIM + ROPE_DIM),
                                     ckv[n_p:].reshape(bs, ts, KV_LORA), kpe[n_p:].reshape(bs, ts, ROPE_DIM),
                                     st, o, d_kn[o], wb).reshape(n_s, D_HEADS * D_VDIM)
            mixed_p = jnp.concatenate([hc_p, od_p], axis=-1)
            mixed_s = jnp.concatenate([hc_s, od_s], axis=-1)
            for new, rows, nb_, t_, cc, cn, cm in ((new_p, slice(0, n_p), bp, tp, cc_p, cn_p, cm_p),
                                                   (new_s, slice(n_p, None), bs, ts, cc_s, cn_s, cm_s)):
                new['d_ckv'].append(ckv[rows].reshape(nb_, t_, KV_LORA))
                new['d_kpe'].append(kpe[rows].reshape(nb_, t_, ROPE_DIM))
                new['c_C'].append(jnp.swapaxes(cc, -1, -2))
                new['c_n'].append(cn[:, :, 0, :])
                new['c_m'].append(cm[:, :, 0, 0])
        mixed = jnp.concatenate([mixed_p, mixed_s], axis=0)
        x = matmul(mixed, wb['mix_out'][l], res=x)

        kvm = matmul(mem2d, wb['mem_kv'][l])
        mk32, mk16 = head_norm(kvm, 0, MEM_HEADS, mem_kn[l], tm=bp * n_mem)
        wm = MEM_HEADS * HEAD_DIM
        new_p['mem_k'].append(mk32.reshape(bp, n_mem, MEM_HEADS, HEAD_DIM))
        new_p['mem_v'].append(kvm[:, wm:].reshape(bp, n_mem, MEM_HEADS, HEAD_DIM))
        q = matmul(x, wb['mem_q'][l], gain=g_mem[l])
        nqb = 4
        o_p = mem_attend(q, mk16.reshape(bp, n_mem, wm), kvm.reshape(bp, n_mem, 2 * wm), mem_qn[l],
                         row0=0, n_batch=bp * nqb, tq=tp // nqb, nb=1, n_mem=n_mem, head_rows=False,
                         k_index=lambda i: (i // nqb, 0, 0), v_index=lambda i: (i // nqb, 0, 1))
        o_s = mem_attend(q, mem_k_rows, mem_v_rows, mem_qn[l], row0=n_p, n_batch=bs, tq=ts, nb=8,
                         n_mem=n_mem, head_rows=True,
                         k_index=lambda i, l=l: (l, i, 0, 0), v_index=lambda i, l=l: (l, i, 0, 0))
        o_all = jnp.concatenate([o_p, o_s], axis=0)
        x = matmul(o_all, wb['mem_o'][l], res=x)
        x = ffn_half_step(x, g_ffn2[l], wb['ffn2_gu'][l], wb['ffn2_dn'][l])

    sp = {n: jnp.stack(v) for n, v in new_p.items()}
    ss = {n: jnp.stack(v) for n, v in new_s.items()}
    return (x[:n_p].reshape(bp, tp, d), x[n_p:].reshape(bs, ts, d),
            sp['a_k'], sp['a_v'], sp['a_kidx'], sp['b_S'], sp['c_C'], sp['c_n'], sp['c_m'], sp['d_ckv'], sp['d_kpe'],
            sp['mem_k'], sp['mem_v'],
            ss['a_k'], ss['a_v'], ss['a_kidx'], ss['b_S'], ss['c_C'], ss['c_n'], ss['c_m'], ss['d_ckv'], ss['d_kpe'])
```

```python
import functools
import math

import jax
import jax.numpy as jnp
import numpy as np
from jax import lax
from jax.experimental import pallas as pl
from jax.experimental.pallas import tpu as pltpu

D_MODEL = 2048
DEPTH = 4
PAGE_SIZE = 128
HEAD_DIM = 128
A_HEADS = 8
A_KV_HEADS = 2
IDX_HEADS = 8
IDX_DIM = 64
TOPK_MAX = 256
REL_BUCKETS = 32
REL_MAX_DIST = 128
B_HEADS = 8
B_DK = 128
B_DV = 128
C_HEADS = 8
C_DK = 64
C_DV = 128
D_HEADS = 8
Q_LORA = 512
KV_LORA = 256
NOPE_DIM = 128
ROPE_DIM = 64
D_VDIM = 128
ROPE_THETA = 10000.0
MEM_HEADS = 4
D_FF = 5632
CHUNK = 64
Q_BLOCK = 128
EPS = 1e-6
MASK_NEG = -1e30
LB_FLOOR = 1e-20
F32 = jnp.float32
BF16 = jnp.bfloat16

EVEN_SPLITS = [A_HEADS * HEAD_DIM, A_KV_HEADS * HEAD_DIM, A_KV_HEADS * HEAD_DIM, IDX_HEADS * IDX_DIM, IDX_HEADS,
               IDX_DIM, B_HEADS * B_DK, B_HEADS * B_DK, B_HEADS * B_DV, B_HEADS * B_DV]
ODD_SPLITS = [C_HEADS * C_DK, C_HEADS * C_DK, C_HEADS * C_DV, C_HEADS, C_HEADS, C_HEADS * C_DV, Q_LORA, KV_LORA,
              ROPE_DIM]
EVEN_IN = sum(EVEN_SPLITS)
ODD_IN = sum(ODD_SPLITS)

EV_QA, EV_QB, EV_FB, EV_IB, EV_GB, EV_KA, EV_VA, EV_QI, EV_KIKI, EV_WI, EVEN_PAD = (
    0, 1024, 2048, 3072, 4096, 5120, 5376, 5632, 6144, 6272, 6400)
OD_QC, OD_KC, OD_VC, OD_OC, OD_G, OD_QA, OD_CKV, OD_KPE, ODD_PAD = (
    0, 512, 1024, 2048, 3072, 3584, 4096, 4352, 4480)

LANES = 128
VMEM_LIMIT = 56 * 1024 * 1024


def _round_up(n, m):
    return (n + m - 1) // m * m


def _ffn_body(x_ref, g_ref, wg_ref, wu_ref, wd_ref, o_ref, n_ref):
    f = pl.program_id(1)

    @pl.when(f == 0)
    def _():
        x = x_ref[...]
        ms = jnp.mean(x * x, axis=-1, keepdims=True)
        n_ref[...] = (x * lax.rsqrt(ms + EPS) * g_ref[...]).astype(BF16)
        o_ref[...] = x

    n = n_ref[...]
    a = jnp.dot(n, wg_ref[...], preferred_element_type=F32)
    b = jnp.dot(n, wu_ref[...], preferred_element_type=F32)
    h = (a * jax.nn.sigmoid(a) * b).astype(BF16)
    o_ref[...] += 0.5 * jnp.dot(h, wd_ref[...], preferred_element_type=F32)


def ffn_half_step(x, g, w_gu, w_dn, *, tm=512, tf=512):
    m, d = x.shape
    ff = w_dn.shape[0]
    nf = ff // tf
    return pl.pallas_call(
        _ffn_body,
        grid=(m // tm, nf),
        in_specs=[
            pl.BlockSpec((tm, d), lambda i, f: (i, 0)),
            pl.BlockSpec((1, d), lambda i, f: (0, 0)),
            pl.BlockSpec((d, tf), lambda i, f: (0, f)),
            pl.BlockSpec((d, tf), lambda i, f: (0, f + nf)),
            pl.BlockSpec((tf, d), lambda i, f: (f, 0)),
        ],
        out_specs=pl.BlockSpec((tm, d), lambda i, f: (i, 0)),
        out_shape=jax.ShapeDtypeStruct((m, d), F32),
        scratch_shapes=[pltpu.VMEM((tm, d), BF16)],
        compiler_params=pltpu.CompilerParams(
            dimension_semantics=("parallel", "arbitrary"), vmem_limit_bytes=VMEM_LIMIT),
        name="ffn_half_step",
    )(x, g.reshape(1, d), w_gu, w_gu, w_dn)


def _mm_body(*refs, has_gain, has_res, alpha):
    a_ref, w_ref = refs[0], refs[1]
    k = 2
    g_ref = res_ref = None
    if has_gain:
        g_ref = refs[k]
        k += 1
    if has_res:
        res_ref = refs[k]
        k += 1
    o_ref, n_ref = refs[k], refs[k + 1]

    @pl.when(pl.program_id(1) == 0)
    def _():
        a = a_ref[...]
        if has_gain:
            ms = jnp.mean(a * a, axis=-1, keepdims=True)
            a = a * lax.rsqrt(ms + EPS) * g_ref[...]
        n_ref[...] = a.astype(BF16)

    acc = jnp.dot(n_ref[...], w_ref[...], preferred_element_type=F32)
    if has_res:
        acc = res_ref[...] + alpha * acc
    o_ref[...] = acc


def matmul(a, w, *, gain=None, res=None, alpha=1.0, tm=512, tn=512, a_col=0):
    m = a.shape[0]
    kdim, n = w.shape
    tm = min(tm, m)
    tn = min(tn, n)
    assert m % tm == 0 and n % tn == 0, (m, tm, n, tn)
    in_specs = [pl.BlockSpec((tm, kdim), lambda i, j: (i, a_col)),
                pl.BlockSpec((kdim, tn), lambda i, j: (0, j))]
    args = [a, w]
    if gain is not None:
        in_specs.append(pl.BlockSpec((1, kdim), lambda i, j: (0, 0)))
        args.append(gain.reshape(1, kdim))
    if res is not None:
        in_specs.append(pl.BlockSpec((tm, tn), lambda i, j: (i, j)))
        args.append(res)
    return pl.pallas_call(
        functools.partial(_mm_body, has_gain=gain is not None, has_res=res is not None, alpha=alpha),
        grid=(m // tm, n // tn),
        in_specs=in_specs,
        out_specs=pl.BlockSpec((tm, tn), lambda i, j: (i, j)),
        out_shape=jax.ShapeDtypeStruct((m, n), F32),
        scratch_shapes=[pltpu.VMEM((tm, kdim), BF16)],
        compiler_params=pltpu.CompilerParams(
            dimension_semantics=("parallel", "arbitrary"), vmem_limit_bytes=VMEM_LIMIT),
        name="matmul",
    )(*args)


def _head_norm_body(x_ref, g_ref, o32_ref, o16_ref, *, heads):
    outs = []
    for h in range(heads):
        x = x_ref[:, LANES * h:LANES * (h + 1)]
        outs.append(x * lax.rsqrt(jnp.mean(x * x, axis=-1, keepdims=True) + EPS) * g_ref[...])
    y = jnp.concatenate(outs, axis=1)
    o32_ref[...] = y
    o16_ref[...] = y.astype(BF16)


def head_norm(x, col_block, heads, gain, *, tm=1024):
    m = x.shape[0]
    w = heads * LANES
    return pl.pallas_call(
        functools.partial(_head_norm_body, heads=heads),
        grid=(m // tm,),
        in_specs=[pl.BlockSpec((tm, w), lambda i: (i, col_block)),
                  pl.BlockSpec((1, LANES), lambda i: (0, 0))],
        out_specs=[pl.BlockSpec((tm, w), lambda i: (i, 0)), pl.BlockSpec((tm, w), lambda i: (i, 0))],
        out_shape=[jax.ShapeDtypeStruct((m, w), F32), jax.ShapeDtypeStruct((m, w), BF16)],
        compiler_params=pltpu.CompilerParams(dimension_semantics=("parallel",)),
        name="head_norm",
    )(x, gain.reshape(1, LANES))


INT_MIN = -2 ** 31
INT_MAX = 2 ** 31 - 1
_NT = (((1,), (1,)), ((), ()))


def _dot_nt(a, b):
    return lax.dot_general(a, b, _NT, preferred_element_type=F32)


def _sort_key(x):
    b = lax.bitcast_convert_type(x + 0.0, jnp.int32)
    return jnp.where(b >= 0, b, b ^ jnp.int32(INT_MAX))


def _row_count(cond):
    c = jnp.where(cond, 1.0, 0.0)
    if c.ndim == 3:
        c = jnp.sum(c, axis=0)
    return jnp.sum(c, axis=-1, keepdims=True)


def _topk_masks(pieces, k, idx_bits):
    kf = float(k)

    def count(fn):
        tot = None
        for key, idx in pieces:
            c = _row_count(fn(key, idx))
            tot = c if tot is None else tot + c
        return tot

    t0 = jnp.where(count(lambda key, idx: key >= 0) >= kf, jnp.int32(0), jnp.int32(INT_MIN))

    def body(i, t):
        cand = t | jnp.left_shift(jnp.int32(1), 30 - i)
        return jnp.where(count(lambda key, idx: key >= cand) >= kf, cand, t)

    thr = lax.fori_loop(0, 31, body, t0)
    need = kf - count(lambda key, idx: key > thr)
    ties = [(jnp.where(key == thr, idx, INT_MAX), idx) for key, idx in pieces]

    def count_ties(cand):
        tot = None
        for tie, _ in ties:
            c = _row_count(tie < cand)
            tot = c if tot is None else tot + c
        return tot

    def body2(i, cur):
        cand = cur + jnp.left_shift(jnp.int32(1), idx_bits - 1 - i)
        return jnp.where(count_ties(cand) < need, cand, cur)

    cut = lax.fori_loop(0, idx_bits, body2, jnp.zeros_like(thr))
    return [jnp.where(key > thr, 1.0, jnp.where(tie <= cut, 1.0, 0.0))
            for (key, _), (tie, _) in zip(pieces, ties)]


def _softmax_rows(lg):
    mx = jnp.max(lg, axis=-1, keepdims=True)
    p = jnp.exp(lg - mx)
    return p / jnp.sum(p, axis=-1, keepdims=True)


def _dsa_prompt_body(qa_ref, qi_ref, wi_ref, kiki_ref, k_ref, v_ref, bias_ref, qn_ref, o_ref, *, qb, t, n_sel, grp):
    i = pl.program_id(1)
    nsub = qb // LANES
    nq = t // qb

    def run(kw):
        nk = kw // LANES
        kiki = kiki_ref[0:kw, :].astype(BF16)
        qi = qi_ref[...] * (IDX_DIM ** -0.5)
        w = wi_ref[...] * (IDX_HEADS ** -0.5)
        lane = lax.broadcasted_iota(jnp.int32, (qb, LANES), 1)
        score = jnp.zeros((qb, kw), F32)
        for h in range(IDX_HEADS):
            blk = qi[:, LANES * (h // 2):LANES * (h // 2 + 1)]
            keep = (lane < IDX_DIM) if h % 2 == 0 else (lane >= IDX_DIM)
            s = _dot_nt(jnp.where(keep, blk, 0.0).astype(BF16), kiki)
            score = score + jnp.maximum(s, 0.0) * w[:, h:h + 1]
        q_pos = i * qb + lax.broadcasted_iota(jnp.int32, (qb, kw), 0)
        k_pos = lax.broadcasted_iota(jnp.int32, (qb, kw), 1)
        allowed = k_pos <= q_pos
        score = jnp.where(allowed, score, MASK_NEG)
        pieces = [(_sort_key(score), k_pos)]
        nv = _round_up(n_sel, LANES) if kw < t else 0
        if nv:
            pieces.append((_sort_key(jnp.full((qb, nv), MASK_NEG, F32)),
                           kw + lax.broadcasted_iota(jnp.int32, (qb, nv), 1)))
        selm = _topk_masks(pieces, n_sel, max(1, (kw + nv - 1).bit_length()))[0]
        sel = jnp.where(allowed, selm, 0.0) > 0.5

        k = k_ref[0:kw, :]
        v = v_ref[0:kw, :].astype(BF16)
        outs = []
        for h in range(A_HEADS):
            g = h // (A_HEADS // A_KV_HEADS)
            rows = []
            for a in range(nsub):
                d0 = i * nsub + a
                tiles = []
                for j in range(nk):
                    d = d0 - j
                    tiles.append(jnp.where(d == 0, bias_ref[h, 0],
                                           jnp.where(d == 1, bias_ref[h, 1], bias_ref[h, 2])))
                rows.append(jnp.concatenate(tiles, axis=1))
            bias = jnp.concatenate(rows, axis=0) if nsub > 1 else rows[0]
            q = qa_ref[:, LANES * h:LANES * (h + 1)]
            q = q * lax.rsqrt(jnp.mean(q * q, axis=-1, keepdims=True) + EPS) * qn_ref[...]
            lg = _dot_nt(q.astype(BF16), k[:, LANES * g:LANES * (g + 1)]) * (HEAD_DIM ** -0.5) + bias
            p = _softmax_rows(jnp.where(sel, lg, MASK_NEG))
            outs.append(jnp.dot(p.astype(BF16), v[:, LANES * g:LANES * (g + 1)], preferred_element_type=F32))
        o_ref[...] = jnp.concatenate(outs, axis=1)

    for v0 in range(0, nq, grp):
        kw = min(t, (v0 + grp) * qb)
        pl.when(i // grp == v0 // grp)(functools.partial(run, kw))


def dsa_prompt_attend(proj, ka16, bias_tiles, qn, *, nb, t, qb=256, grp=2):
    nq = t // qb
    n_sel = min(TOPK_MAX, t // 4)
    wq = A_HEADS * HEAD_DIM
    return pl.pallas_call(
        functools.partial(_dsa_prompt_body, qb=qb, t=t, n_sel=n_sel, grp=grp),
        grid=(nb, nq),
        in_specs=[
            pl.BlockSpec((qb, wq), lambda b, i: (b * nq + i, 0)),
            pl.BlockSpec((qb, IDX_HEADS * IDX_DIM), lambda b, i: (b * nq + i, EV_QI // (IDX_HEADS * IDX_DIM))),
            pl.BlockSpec((qb, LANES), lambda b, i: (b * nq + i, EV_WI // LANES)),
            pl.BlockSpec((t, LANES), lambda b, i: (b, EV_KIKI // LANES)),
            pl.BlockSpec((t, A_KV_HEADS * HEAD_DIM), lambda b, i: (b, 0)),
            pl.BlockSpec((t, A_KV_HEADS * HEAD_DIM), lambda b, i: (b, EV_VA // (A_KV_HEADS * HEAD_DIM))),
            pl.BlockSpec((A_HEADS, 3, LANES, LANES), lambda b, i: (0, 0, 0, 0)),
            pl.BlockSpec((1, HEAD_DIM), lambda b, i: (0, 0)),
        ],
        out_specs=pl.BlockSpec((qb, wq), lambda b, i: (b * nq + i, 0)),
        out_shape=jax.ShapeDtypeStruct((nb * t, wq), F32),
        compiler_params=pltpu.CompilerParams(
            dimension_semantics=("parallel", "arbitrary"), vmem_limit_bytes=VMEM_LIMIT),
        name="dsa_prompt",
    )(proj, proj, proj, proj, ka16, proj, bias_tiles, qn.reshape(1, HEAD_DIM))


def _idx_head_sum(s, w):
    tot = None
    for h in range(IDX_HEADS):
        c = jnp.maximum(s[8 * h:8 * (h + 1)], 0.0) * w[:, h:h + 1]
        tot = c if tot is None else tot + c
    return tot


def _dsa_score_body(pt_ref, qi_ref, wi_ref, knew_ref, *rest, pp, ts):
    pages, sp_ref, sn_ref = rest[:pp], rest[pp], rest[pp + 1]
    qi = qi_ref[...] * (IDX_DIM ** -0.5)
    a = jnp.concatenate([qi[:, IDX_DIM * h:IDX_DIM * (h + 1)] for h in range(IDX_HEADS)], axis=0).astype(BF16)
    w = wi_ref[...] * (IDX_HEADS ** -0.5)
    kt = jnp.concatenate([p[...] for p in pages], axis=1).astype(BF16)
    sp_ref[...] = _idx_head_sum(jnp.dot(a, kt, preferred_element_type=F32), w)

    @pl.when(pl.program_id(1) == 0)
    def _():
        knew = knew_ref[...][:, :IDX_DIM]
        kn = jnp.concatenate([knew, jnp.zeros((LANES - ts, IDX_DIM), F32)], axis=0).astype(BF16)
        sn_ref[...] = _idx_head_sum(_dot_nt(a, kn), w)


def _dsa_mask_body(sp_ref, sn_ref, mp_ref, mn_ref, *, ts, n_sel):
    rows, past = sp_ref.shape
    lane = lax.broadcasted_iota(jnp.int32, (rows // ts, ts, LANES), 2).reshape(rows, LANES)
    t_row = lax.broadcasted_iota(jnp.int32, (rows // ts, ts, LANES), 1).reshape(rows, LANES)
    causal = lane <= t_row
    snew = jnp.where(causal, sn_ref[...], MASK_NEG)
    key_new = jnp.where(lane < ts, _sort_key(snew), INT_MIN)
    idx_past = lax.broadcasted_iota(jnp.int32, (rows, past), 1)
    m_past, m_new = _topk_masks([(_sort_key(sp_ref[...]), idx_past), (key_new, past + lane)], n_sel,
                                (past + LANES - 1).bit_length())
    mp_ref[...] = m_past
    mn_ref[...] = jnp.where(causal, m_new, 0.0)


def dsa_sample_select(proj, pool_kidx_t, pt_flat, layer, *, row0, nb, ts, n_pages, pp=32, rows_per_step=128):
    nj = n_pages // pp
    tk = pp * PAGE_SIZE
    past = n_pages * PAGE_SIZE
    n_sel = min(TOPK_MAX, (past + ts) // 4)
    rb0 = row0 // ts
    rows = nb * ts
    rstep = min(rows_per_step, rows)
    sp, sn = _dsa_sample_scores(proj, pool_kidx_t, pt_flat, layer, rb0=rb0, nb=nb, ts=ts, n_pages=n_pages, pp=pp)
    mp, mn = pl.pallas_call(
        functools.partial(_dsa_mask_body, ts=ts, n_sel=n_sel),
        grid=(rows // rstep,),
        in_specs=[pl.BlockSpec((rstep, past), lambda i: (i, 0)), pl.BlockSpec((rstep, LANES), lambda i: (i, 0))],
        out_specs=[pl.BlockSpec((rstep, past), lambda i: (i, 0)), pl.BlockSpec((rstep, LANES), lambda i: (i, 0))],
        out_shape=[jax.ShapeDtypeStruct((rows, past), F32), jax.ShapeDtypeStruct((rows, LANES), F32)],
        compiler_params=pltpu.CompilerParams(dimension_semantics=("parallel",), vmem_limit_bytes=VMEM_LIMIT),
        name="dsa_sample_mask",
    )(sp.reshape(rows, past), sn.reshape(rows, LANES))
    return mp.reshape(nb, ts, past), mn.reshape(nb, ts, LANES)


def _dsa_sample_scores(proj, pool_kidx_t, pt_flat, layer, *, rb0, nb, ts, n_pages, pp):
    nj = n_pages // pp
    tk = pp * PAGE_SIZE
    past = n_pages * PAGE_SIZE

    def page_spec(r):
        return pl.BlockSpec((None, None, IDX_DIM, PAGE_SIZE),
                            lambda b, j, pt: (layer, pt[b * n_pages + j * pp + r], 0, 0))

    grid_spec = pltpu.PrefetchScalarGridSpec(
        num_scalar_prefetch=1,
        grid=(nb, nj),
        in_specs=[
            pl.BlockSpec((ts, IDX_HEADS * IDX_DIM), lambda b, j, pt: (rb0 + b, EV_QI // (IDX_HEADS * IDX_DIM))),
            pl.BlockSpec((ts, LANES), lambda b, j, pt: (rb0 + b, EV_WI // LANES)),
            pl.BlockSpec((ts, LANES), lambda b, j, pt: (rb0 + b, EV_KIKI // LANES)),
        ] + [page_spec(r) for r in range(pp)],
        out_specs=[pl.BlockSpec((None, ts, tk), lambda b, j, pt: (b, 0, j)),
                   pl.BlockSpec((None, ts, LANES), lambda b, j, pt: (b, 0, 0))],
    )
    return pl.pallas_call(
        functools.partial(_dsa_score_body, pp=pp, ts=ts),
        grid_spec=grid_spec,
        out_shape=[jax.ShapeDtypeStruct((nb, ts, past), F32), jax.ShapeDtypeStruct((nb, ts, LANES), F32)],
        compiler_params=pltpu.CompilerParams(
            dimension_semantics=("parallel", "arbitrary"), vmem_limit_bytes=VMEM_LIMIT),
        name="dsa_sample_scores",
    )(pt_flat, proj, proj, proj, *([pool_kidx_t] * pp))


def _dsa_att_body(pt_ref, q_ref, knew_ref, vnew_ref, qn_ref, mask_ref, mnew_ref, bias_ref, bnew_ref, *rest,
                  pp, sub, nj, ts):
    kpages, vpages = rest[:pp], rest[pp:2 * pp]
    o_ref, qs_ref, m_ref, l_ref, acc_ref = rest[2 * pp:]
    j = pl.program_id(1)
    gsz = A_HEADS // A_KV_HEADS
    rows_g = gsz * ts

    @pl.when(j == 0)
    def _():
        qs = []
        for h in range(A_HEADS):
            q = q_ref[:, LANES * h:LANES * (h + 1)]
            qs.append(q * lax.rsqrt(jnp.mean(q * q, axis=-1, keepdims=True) + EPS) * qn_ref[...])
        qs_ref[...] = jnp.concatenate(qs, axis=0).astype(BF16)
        m_ref[...] = jnp.full(m_ref.shape, MASK_NEG, F32)
        l_ref[...] = jnp.zeros(l_ref.shape, F32)
        acc_ref[...] = jnp.zeros(acc_ref.shape, F32)

    def partial(k_tiles, v_tiles, mask, bias):
        qs = qs_ref[...]
        lg = jnp.concatenate(
            [_dot_nt(qs[rows_g * g:rows_g * (g + 1)], k_tiles[g])
             for g in range(A_KV_HEADS)], axis=0) * (HEAD_DIM ** -0.5) + bias
        sel = jnp.concatenate([mask] * A_HEADS, axis=0) > 0.5
        lg = jnp.where(sel, lg, MASK_NEG)
        m = jnp.max(lg, axis=-1, keepdims=True)
        p = jnp.where(sel, jnp.exp(lg - m), 0.0)
        pb = p.astype(BF16)
        pv = jnp.concatenate(
            [jnp.dot(pb[rows_g * g:rows_g * (g + 1)], v_tiles[g], preferred_element_type=F32)
             for g in range(A_KV_HEADS)], axis=0)
        return m, jnp.sum(p, axis=-1, keepdims=True), pv

    def merge(parts):
        m_old = m_ref[...]
        m_new = m_old
        for m, _, _ in parts:
            m_new = jnp.maximum(m_new, m)
        alpha = jnp.exp(m_old - m_new)
        l_new = alpha * l_ref[...]
        acc = alpha * acc_ref[...]
        for m, l, pv in parts:
            a = jnp.exp(m - m_new)
            l_new = l_new + a * l
            acc = acc + a * pv
        m_ref[...] = m_new
        l_ref[...] = l_new
        acc_ref[...] = acc

    def head_rows(pages, g):
        return jnp.concatenate([p[pl.ds(g, PAGE_SIZE, stride=A_KV_HEADS), :] for p in pages], axis=0).astype(BF16)

    bias_all = bias_ref[...].reshape(A_HEADS * ts, pp * PAGE_SIZE)
    mask_all = mask_ref[...]
    parts = []
    for s0 in range(0, pp, sub):
        kp, vp = kpages[s0:s0 + sub], vpages[s0:s0 + sub]
        cols = slice(s0 * PAGE_SIZE, (s0 + sub) * PAGE_SIZE)
        parts.append(partial([head_rows(kp, g) for g in range(A_KV_HEADS)],
                             [head_rows(vp, g) for g in range(A_KV_HEADS)], mask_all[:, cols], bias_all[:, cols]))
    merge(parts)

    @pl.when(j == nj - 1)
    def _():
        pad = jnp.zeros((LANES - ts, A_KV_HEADS * HEAD_DIM), F32)
        kn = jnp.concatenate([knew_ref[...], pad], axis=0).astype(BF16)
        vn = jnp.concatenate([vnew_ref[...], pad], axis=0).astype(BF16)
        merge([partial([kn[:, LANES * g:LANES * (g + 1)] for g in range(A_KV_HEADS)],
                       [vn[:, LANES * g:LANES * (g + 1)] for g in range(A_KV_HEADS)],
                       mnew_ref[...], bnew_ref[...].reshape(A_HEADS * ts, LANES))])
        out = acc_ref[...] / l_ref[...]
        o_ref[...] = jnp.concatenate([out[ts * h:ts * (h + 1)] for h in range(A_HEADS)], axis=1)


def dsa_sample_attend(proj, ka32, mask_past, mask_new, bias_s, pool_k, pool_v, pt_flat, layer, qn, *, row0, nb, ts,
                      n_pages, pp=32, sub=8):
    nj = n_pages // pp
    tk = pp * PAGE_SIZE
    past = n_pages * PAGE_SIZE
    rb0 = row0 // ts
    wkv = A_KV_HEADS * HEAD_DIM
    wq = A_HEADS * HEAD_DIM

    def page_spec(r):
        return pl.BlockSpec((None, None, PAGE_SIZE * A_KV_HEADS, HEAD_DIM),
                            lambda b, j, pt: (layer, pt[b * n_pages + j * pp + r], 0, 0))

    grid_spec = pltpu.PrefetchScalarGridSpec(
        num_scalar_prefetch=1,
        grid=(nb, nj),
        in_specs=[
            pl.BlockSpec((ts, wq), lambda b, j, pt: (rb0 + b, 0)),
            pl.BlockSpec((ts, wkv), lambda b, j, pt: (rb0 + b, 0)),
            pl.BlockSpec((ts, wkv), lambda b, j, pt: (rb0 + b, EV_VA // wkv)),
            pl.BlockSpec((1, HEAD_DIM), lambda b, j, pt: (0, 0)),
            pl.BlockSpec((None, ts, tk), lambda b, j, pt: (b, 0, j)),
            pl.BlockSpec((None, ts, LANES), lambda b, j, pt: (b, 0, 0)),
            pl.BlockSpec((A_HEADS, ts, tk), lambda b, j, pt: (0, 0, j)),
            pl.BlockSpec((A_HEADS, ts, LANES), lambda b, j, pt: (0, 0, past // LANES)),
        ] + [page_spec(r) for r in range(pp)] * 2,
        out_specs=pl.BlockSpec((ts, wq), lambda b, j, pt: (b, 0)),
        scratch_shapes=[pltpu.VMEM((A_HEADS * ts, HEAD_DIM), BF16), pltpu.VMEM((A_HEADS * ts, 1), F32),
                        pltpu.VMEM((A_HEADS * ts, 1), F32), pltpu.VMEM((A_HEADS * ts, HEAD_DIM), F32)],
    )
    return pl.pallas_call(
        functools.partial(_dsa_att_body, pp=pp, sub=min(sub, pp), nj=nj, ts=ts),
        grid_spec=grid_spec,
        out_shape=jax.ShapeDtypeStruct((nb * ts, wq), F32),
        compiler_params=pltpu.CompilerParams(
            dimension_semantics=("parallel", "arbitrary"), vmem_limit_bytes=VMEM_LIMIT),
        name="dsa_sample_attend",
    )(pt_flat, proj, ka32, proj, qn.reshape(1, HEAD_DIM), mask_past, mask_new, bias_s, bias_s,
      *([pool_k] * pp), *([pool_v] * pp))


def _mla_sample_body(pt_ref, qlat_ref, qpe_ref, cnew_ref, rnew_ref, wkt_ref, wv_ref, *rest, pp, sub, nj, ts):
    cpages, rpages = rest[:pp], rest[pp:2 * pp]
    o_ref, m_ref, l_ref, acc_ref = rest[2 * pp:]
    j = pl.program_id(1)
    dq = NOPE_DIM + ROPE_DIM
    nrow = D_HEADS * ts

    @pl.when(j == 0)
    def _():
        m_ref[...] = jnp.full(m_ref.shape, MASK_NEG, F32)
        l_ref[...] = jnp.zeros(l_ref.shape, F32)
        acc_ref[...] = jnp.zeros(acc_ref.shape, F32)

    def update(c, rt, sel):
        n = c.shape[0]
        cb = c.astype(BF16)
        kt = _dot_nt(wkt_ref[...], cb)
        ss = jnp.sum((kt * kt).reshape(D_HEADS, NOPE_DIM, n), axis=1)
        rss = jnp.sum(rt * rt, axis=0, keepdims=True)
        rinv = lax.rsqrt((ss + rss) * (1.0 / dq) + EPS) * (dq ** -0.5)
        lg = _dot_nt(qlat_ref[...], cb) + jnp.dot(qpe_ref[...], rt.astype(BF16),
                                                  preferred_element_type=F32)
        lg = (lg.reshape(D_HEADS, ts, n) * rinv[:, None, :]).reshape(nrow, n)
        if sel is not None:
            lg = jnp.where(sel, lg, MASK_NEG)
        m = jnp.max(lg, axis=-1, keepdims=True)
        p = jnp.exp(lg - m)
        if sel is not None:
            p = jnp.where(sel, p, 0.0)
        return m, jnp.sum(p, axis=-1, keepdims=True), jnp.dot(p.astype(BF16), cb, preferred_element_type=F32)

    def merge(parts):
        m_old = m_ref[...]
        m_new = m_old
        for m, _, _ in parts:
            m_new = jnp.maximum(m_new, m)
        alpha = jnp.exp(m_old - m_new)
        l_new = alpha * l_ref[...]
        acc = alpha * acc_ref[...]
        for m, l, pv in parts:
            a = jnp.exp(m - m_new)
            l_new = l_new + a * l
            acc = acc + a * pv
        m_ref[...] = m_new
        l_ref[...] = l_new
        acc_ref[...] = acc

    parts = []
    for s0 in range(0, pp, sub):
        c = jnp.concatenate([p[...] for p in cpages[s0:s0 + sub]], axis=0)
        rt = jnp.concatenate([p[...] for p in rpages[s0:s0 + sub]], axis=1)
        parts.append(update(c, rt, None))
    merge(parts)

    @pl.when(j == nj - 1)
    def _():
        lane = lax.broadcasted_iota(jnp.int32, (D_HEADS, ts, LANES), 2).reshape(nrow, LANES)
        row_t = lax.broadcasted_iota(jnp.int32, (D_HEADS, ts, LANES), 1).reshape(nrow, LANES)
        merge([update(cnew_ref[...], rnew_ref[...], lane <= row_t)])
        lat = (acc_ref[...] / l_ref[...]).astype(BF16)
        o_ref[...] = jnp.concatenate(
            [jnp.dot(lat[ts * h:ts * (h + 1)], wv_ref[h], preferred_element_type=F32) for h in range(D_HEADS)],
            axis=1)


def mla_sample_attend(qlat, qpe, cnew, rnew_t, wkt, wv, pool_ckv, pool_kpe_t, pt_flat, layer, *, nb, ts, n_pages,
                      pp=32, sub=8):
    nj = n_pages // pp
    nrow = D_HEADS * ts

    def cspec(r):
        return pl.BlockSpec((None, None, PAGE_SIZE, KV_LORA), lambda b, j, pt: (layer, pt[b * n_pages + j * pp + r], 0, 0))

    def rspec(r):
        return pl.BlockSpec((None, None, ROPE_DIM, PAGE_SIZE), lambda b, j, pt: (layer, pt[b * n_pages + j * pp + r], 0, 0))

    grid_spec = pltpu.PrefetchScalarGridSpec(
        num_scalar_prefetch=1,
        grid=(nb, nj),
        in_specs=[
            pl.BlockSpec((None, nrow, KV_LORA), lambda b, j, pt: (b, 0, 0)),
            pl.BlockSpec((None, nrow, ROPE_DIM), lambda b, j, pt: (b, 0, 0)),
            pl.BlockSpec((None, LANES, KV_LORA), lambda b, j, pt: (b, 0, 0)),
            pl.BlockSpec((None, ROPE_DIM, LANES), lambda b, j, pt: (b, 0, 0)),
            pl.BlockSpec((D_HEADS * NOPE_DIM, KV_LORA), lambda b, j, pt: (0, 0)),
            pl.BlockSpec((D_HEADS, KV_LORA, D_VDIM), lambda b, j, pt: (0, 0, 0)),
        ] + [cspec(r) for r in range(pp)] + [rspec(r) for r in range(pp)],
        out_specs=pl.BlockSpec((ts, D_HEADS * D_VDIM), lambda b, j, pt: (b, 0)),
        scratch_shapes=[pltpu.VMEM((nrow, 1), F32), pltpu.VMEM((nrow, 1), F32), pltpu.VMEM((nrow, KV_LORA), F32)],
    )
    return pl.pallas_call(
        functools.partial(_mla_sample_body, pp=pp, sub=min(sub, pp), nj=nj, ts=ts),
        grid_spec=grid_spec,
        out_shape=jax.ShapeDtypeStruct((nb * ts, D_HEADS * D_VDIM), F32),
        compiler_params=pltpu.CompilerParams(
            dimension_semantics=("parallel", "arbitrary"), vmem_limit_bytes=VMEM_LIMIT),
        name="mla_sample",
    )(pt_flat, qlat, qpe, cnew, rnew_t, wkt, wv, *([pool_ckv] * pp), *([pool_kpe_t] * pp))


def _mem_attn_body(q_ref, k_ref, v_ref, qn_ref, o_ref, *, nb, tq, n_mem, head_rows):
    def head(ref, b, h):
        if head_rows:
            return ref[b, pl.ds(h, n_mem, stride=MEM_HEADS), :].astype(BF16)
        return ref[b, :, LANES * h:LANES * (h + 1)].astype(BF16)

    for b in range(nb):
        outs = []
        for h in range(MEM_HEADS):
            q = q_ref[tq * b:tq * (b + 1), LANES * h:LANES * (h + 1)]
            q = q * lax.rsqrt(jnp.mean(q * q, axis=-1, keepdims=True) + EPS) * qn_ref[...]
            lg = _dot_nt(q.astype(BF16), head(k_ref, b, h)) * (HEAD_DIM ** -0.5)
            p = _softmax_rows(lg)
            outs.append(jnp.dot(p.astype(BF16), head(v_ref, b, h), preferred_element_type=F32))
        o_ref[tq * b:tq * (b + 1), :] = jnp.concatenate(outs, axis=1)


def mem_attend(q, mk, mv, qn, *, row0, n_batch, tq, nb, n_mem, head_rows, k_index, v_index):
    w = MEM_HEADS * HEAD_DIM
    rb0 = row0 // (nb * tq)
    slab = (n_mem * MEM_HEADS, HEAD_DIM) if head_rows else (n_mem, w)
    kblock = (None,) * (mk.ndim - 3) + (nb,) + slab
    vblock = (None,) * (mv.ndim - 3) + (nb,) + slab
    return pl.pallas_call(
        functools.partial(_mem_attn_body, nb=nb, tq=tq, n_mem=n_mem, head_rows=head_rows),
        grid=(n_batch // nb,),
        in_specs=[pl.BlockSpec((nb * tq, w), lambda i: (rb0 + i, 0)),
                  pl.BlockSpec(kblock, k_index),
                  pl.BlockSpec(vblock, v_index),
                  pl.BlockSpec((1, HEAD_DIM), lambda i: (0, 0))],
        out_specs=pl.BlockSpec((nb * tq, w), lambda i: (i, 0)),
        out_shape=jax.ShapeDtypeStruct((n_batch * tq, w), F32),
        compiler_params=pltpu.CompilerParams(dimension_semantics=("parallel",), vmem_limit_bytes=VMEM_LIMIT),
        name="mem_attend",
    )(q, mk, mv, qn.reshape(1, HEAD_DIM))


def _causal_attn_body(q_ref, k_ref, v_ref, o_ref, *, qb, t, scale):
    i = pl.program_id(2)
    lg = _dot_nt(q_ref[...], k_ref[...]) * scale
    q_pos = i * qb + lax.broadcasted_iota(jnp.int32, (qb, t), 0)
    k_pos = lax.broadcasted_iota(jnp.int32, (qb, t), 1)
    p = _softmax_rows(jnp.where(k_pos <= q_pos, lg, MASK_NEG))
    o_ref[...] = jnp.dot(p.astype(BF16), v_ref[...], preferred_element_type=F32)


def causal_attend(q, k, v, *, nb, t, heads, dv, scale, qb=512):
    nq = t // qb
    dqk = q.shape[1] // heads
    return pl.pallas_call(
        functools.partial(_causal_attn_body, qb=qb, t=t, scale=scale),
        grid=(nb, heads, nq),
        in_specs=[pl.BlockSpec((qb, dqk), lambda b, h, i: (b * nq + i, h)),
                  pl.BlockSpec((t, dqk), lambda b, h, i: (b, h)),
                  pl.BlockSpec((t, dv), lambda b, h, i: (b, h))],
        out_specs=pl.BlockSpec((qb, dv), lambda b, h, i: (b * nq + i, h)),
        out_shape=jax.ShapeDtypeStruct((nb * t, heads * dv), F32),
        compiler_params=pltpu.CompilerParams(
            dimension_semantics=("parallel", "parallel", "arbitrary"), vmem_limit_bytes=VMEM_LIMIT),
        name="causal_attend",
    )(q, k, v)


def _cumsum_rows(x):
    c = x.shape[0]
    row = lax.broadcasted_iota(jnp.int32, x.shape, 0)
    sh = 1
    while sh < c:
        x = x + jnp.where(row >= sh, pltpu.roll(x, sh, axis=0), 0.0)
        sh *= 2
    return x


def _log_sigmoid(x):
    return jnp.minimum(x, 0.0) - jnp.log1p(jnp.exp(-jnp.abs(x)))


def _sigmoid(x):
    return 1.0 / (1.0 + jnp.exp(-x))


def _hgrn2_run(q_ref, f_ref, v_ref, g_ref, lb_ref, on, st_ref, o_ref, *, nh, t, c):
    row = lax.broadcasted_iota(jnp.int32, (c, 1), 0)

    def step(i, carry):
        r0 = pl.multiple_of(i * c, c)
        for h in range(nh):
            sl = slice(LANES * h, LANES * (h + 1))
            lb = lb_ref[h]
            fpre = f_ref[pl.ds(r0, c), sl]
            a = jnp.log(jnp.maximum(lb, LB_FLOOR))
            b = jnp.log1p(-lb) + _log_sigmoid(fpre)
            lf = jnp.maximum(a, b) + jnp.log1p(jnp.exp(-jnp.abs(a - b)))
            kk = (1.0 - lb) * _sigmoid(-fpre)
            qpre = q_ref[pl.ds(r0, c), sl]
            qv = qpre * _sigmoid(qpre)
            vv = v_ref[pl.ds(r0, c), sl]
            cb = _cumsum_rows(lf)
            st = st_ref[h]
            o = _dot_nt((qv * jnp.exp(cb)).astype(BF16), st.astype(BF16))
            for s in range(c):
                dec = jnp.exp(jnp.minimum(cb - cb[s:s + 1, :], 0.0))
                col = jnp.sum(qv * dec * kk[s:s + 1, :], axis=-1, keepdims=True)
                o = o + jnp.where(row >= s, col, 0.0) * vv[s:s + 1, :]
            c_last = cb[c - 1:c, :]
            kd = (kk * jnp.exp(c_last - cb)).astype(BF16)
            st_ref[h] = st * jnp.exp(c_last) + lax.dot_general(vv.astype(BF16), kd, (((0,), (0,)), ((), ())),
                                                               preferred_element_type=F32)
            gpre = g_ref[pl.ds(r0, c), sl]
            o = o * lax.rsqrt(jnp.mean(o * o, axis=-1, keepdims=True) + EPS) * on * (gpre * _sigmoid(gpre))
            o_ref[pl.ds(r0, c), sl] = o
        return carry

    lax.fori_loop(0, t // c, step, 0)


def _hgrn2_prompt_body(q_ref, f_ref, v_ref, g_ref, lb_ref, on_ref, o_ref, s_ref, st_ref, *, nh, t, c):
    st_ref[...] = jnp.zeros(st_ref.shape, F32)
    _hgrn2_run(q_ref, f_ref, v_ref, g_ref, lb_ref, on_ref[...], st_ref, o_ref, nh=nh, t=t, c=c)
    for h in range(nh):
        s_ref[h] = st_ref[h].T


def hgrn2_prompt(proj, lb, on, *, nb, t, nh=4):
    c = math.gcd(16, t)
    w = nh * LANES
    col = lambda c0: (lambda b, j: (b, c0 // w + j))
    return pl.pallas_call(
        functools.partial(_hgrn2_prompt_body, nh=nh, t=t, c=c),
        grid=(nb, B_HEADS // nh),
        in_specs=[pl.BlockSpec((t, w), col(EV_QB)), pl.BlockSpec((t, w), col(EV_FB)),
                  pl.BlockSpec((t, w), col(EV_IB)), pl.BlockSpec((t, w), col(EV_GB)),
                  pl.BlockSpec((nh, 1, B_DK), lambda b, j: (j, 0, 0)),
                  pl.BlockSpec((1, B_DV), lambda b, j: (0, 0))],
        out_specs=[pl.BlockSpec((t, w), lambda b, j: (b, j)),
                   pl.BlockSpec((None, nh, B_DK, B_DV), lambda b, j: (b, j, 0, 0))],
        out_shape=[jax.ShapeDtypeStruct((nb * t, B_HEADS * B_DV), F32),
                   jax.ShapeDtypeStruct((nb, B_HEADS, B_DK, B_DV), F32)],
        scratch_shapes=[pltpu.VMEM((nh, B_DV, B_DK), F32)],
        compiler_params=pltpu.CompilerParams(
            dimension_semantics=("parallel", "parallel"), vmem_limit_bytes=VMEM_LIMIT),
        name="hgrn2_prompt",
    )(proj, proj, proj, proj, lb.reshape(B_HEADS, 1, B_DK), on.reshape(1, B_DV))


def _hgrn2_sample_body(q_ref, f_ref, v_ref, g_ref, lb_ref, on_ref, s0_ref, o_ref, s_ref, st_ref, *, t):
    for h in range(B_HEADS):
        st_ref[h] = s0_ref[h].T
    _hgrn2_run(q_ref, f_ref, v_ref, g_ref, lb_ref, on_ref[...], st_ref, o_ref, nh=B_HEADS, t=t, c=t)
    for h in range(B_HEADS):
        s_ref[h] = st_ref[h].T


def hgrn2_sample(proj, lb, on, s0, *, row0, nb, t):
    rb0 = row0 // t
    w = B_HEADS * LANES
    col = lambda c0: (lambda b: (rb0 + b, c0 // w))
    return pl.pallas_call(
        functools.partial(_hgrn2_sample_body, t=t),
        grid=(nb,),
        in_specs=[pl.BlockSpec((t, w), col(EV_QB)), pl.BlockSpec((t, w), col(EV_FB)),
                  pl.BlockSpec((t, w), col(EV_IB)), pl.BlockSpec((t, w), col(EV_GB)),
                  pl.BlockSpec((B_HEADS, 1, B_DK), lambda b: (0, 0, 0)),
                  pl.BlockSpec((1, B_DV), lambda b: (0, 0)),
                  pl.BlockSpec((None, B_HEADS, B_DK, B_DV), lambda b: (b, 0, 0, 0))],
        out_specs=[pl.BlockSpec((t, w), lambda b: (b, 0)),
                   pl.BlockSpec((None, B_HEADS, B_DK, B_DV), lambda b: (b, 0, 0, 0))],
        out_shape=[jax.ShapeDtypeStruct((nb * t, w), F32),
                   jax.ShapeDtypeStruct((nb, B_HEADS, B_DK, B_DV), F32)],
        scratch_shapes=[pltpu.VMEM((B_HEADS, B_DV, B_DK), F32)],
        compiler_params=pltpu.CompilerParams(dimension_semantics=("parallel",), vmem_limit_bytes=VMEM_LIMIT),
        name="hgrn2_sample",
    )(proj, proj, proj, proj, lb.reshape(B_HEADS, 1, B_DK), on.reshape(1, B_DV), s0)


def _mlstm_chunk(qm, km_h, v, ic_row, ic_col, cb_col, cb_row, ct, n, m, c):
    r = lax.broadcasted_iota(jnp.int32, (c, c), 0)
    s = lax.broadcasted_iota(jnp.int32, (c, c), 1)
    dmat = jnp.where(r >= s, cb_col - cb_row + ic_row, MASK_NEG)
    m_state = cb_col + m
    m_t = jnp.maximum(m_state, jnp.max(dmat, axis=-1, keepdims=True))
    w = jnp.exp(dmat - m_t)
    w0 = jnp.exp(m_state - m_t)
    qb16 = qm.astype(BF16)
    qk = _dot_nt(qb16, km_h.astype(BF16)) * w
    num = (jnp.dot(qk.astype(BF16), v.astype(BF16), preferred_element_type=F32)
           + w0 * jnp.dot(qb16, ct.astype(BF16), preferred_element_type=F32))
    den = jnp.sum(qk, axis=-1, keepdims=True) + w0 * jnp.sum(qm * n, axis=-1, keepdims=True)
    hc = num / jnp.maximum(jnp.abs(den), jnp.exp(-m_t))
    m_last = m_t[c - 1:c, :]
    cb_last = cb_col[c - 1:c, :]
    ws = jnp.exp(cb_last - cb_col + ic_col - m_last)
    fs = jnp.exp(cb_last + m - m_last)
    ct = fs * ct + lax.dot_general(km_h.astype(BF16), (v * ws).astype(BF16), (((0,), (0,)), ((), ())),
                                   preferred_element_type=F32)
    n = fs * n + jnp.sum(ws * km_h, axis=0, keepdims=True)
    return hc, ct, n, m_last


def _mlstm_seq(q_ref, k_ref, v_ref, og_ref, g_ref, gb_ref, on, npairs, states, o_ref, *, t, c):
    lane = lax.broadcasted_iota(jnp.int32, (c, LANES), 1)

    def step(i, carry):
        r0 = pl.multiple_of(i * c, c)
        new = []
        for p in range(npairs):
            psl = slice(LANES * p, LANES * (p + 1))
            g = g_ref[pl.ds(r0, c), psl] + gb_ref[p]
            cb = _cumsum_rows(_log_sigmoid(g))
            gt = g.T
            cbt = cb.T
            q = q_ref[pl.ds(r0, c), psl]
            k = k_ref[pl.ds(r0, c), psl] * (C_DK ** -0.5)
            for half in range(2):
                ct, n, m = carry[2 * p + half]
                keep = (lane < C_DK) if half == 0 else (lane >= C_DK)
                vsl = slice(C_DV * (2 * p + half), C_DV * (2 * p + half + 1))
                hc, ct, n, m = _mlstm_chunk(
                    jnp.where(keep, q, 0.0), jnp.where(keep, k, 0.0), v_ref[pl.ds(r0, c), vsl],
                    gt[half:half + 1, :], g[:, half:half + 1], cb[:, 2 + half:3 + half], cbt[2 + half:3 + half, :],
                    ct, n, m, c)
                og = og_ref[pl.ds(r0, c), vsl]
                hc = hc * lax.rsqrt(jnp.mean(hc * hc, axis=-1, keepdims=True) + EPS) * on * _sigmoid(og)
                o_ref[pl.ds(r0, c), vsl] = hc
                new.append((ct, n, m))
        return tuple(new)

    return lax.fori_loop(0, t // c, step, tuple(states))


def _mlstm_body(*refs, t, c, npairs, has_state):
    q_ref, k_ref, v_ref, og_ref, g_ref, gb_ref, on_ref = refs[:7]
    refs = refs[7:]
    if has_state:
        c0_ref, n0_ref, m0_ref = refs[:3]
        refs = refs[3:]
    o_ref, c_ref, n_ref, m_ref = refs
    states = []
    for h in range(2 * npairs):
        half = h % 2
        if has_state:
            ct_h = c0_ref[h]
            z = jnp.zeros((C_DK, C_DV), F32)
            ct = jnp.concatenate([ct_h, z] if half == 0 else [z, ct_h], axis=0)
            n_h = n0_ref[h:h + 1, :]
            zn = jnp.zeros((1, C_DK), F32)
            n = jnp.concatenate([n_h, zn] if half == 0 else [zn, n_h], axis=1)
            m = m0_ref[:, h:h + 1]
        else:
            ct, n, m = jnp.zeros((2 * C_DK, C_DV), F32), jnp.zeros((1, 2 * C_DK), F32), jnp.zeros((1, 1), F32)
        states.append((ct, n, m))
    out = _mlstm_seq(q_ref, k_ref, v_ref, og_ref, g_ref, gb_ref, on_ref[...], npairs, states, o_ref, t=t, c=c)
    for h, (ct, n, m) in enumerate(out):
        rows = slice(C_DK * (h % 2), C_DK * (h % 2 + 1))
        c_ref[h] = ct[rows, :]
        n_ref[h] = n[:, rows]
        m_ref[h] = jnp.broadcast_to(m, (1, LANES))


def mlstm(proj, gb, on, state, *, row0, nb, t, npairs):
    c = math.gcd(CHUNK, t)
    rb0 = row0 // t
    npg = (C_HEADS // 2) // npairs
    hp = 2 * npairs
    qw, vw = npairs * LANES, npairs * 2 * C_DV
    in_specs = [
        pl.BlockSpec((t, qw), lambda b, j: (rb0 + b, OD_QC // qw + j)),
        pl.BlockSpec((t, qw), lambda b, j: (rb0 + b, OD_KC // qw + j)),
        pl.BlockSpec((t, vw), lambda b, j: (rb0 + b, OD_VC // vw + j)),
        pl.BlockSpec((t, vw), lambda b, j: (rb0 + b, OD_OC // vw + j)),
        pl.BlockSpec((t, qw), lambda b, j: (rb0 + b, OD_G // qw + j)),
        pl.BlockSpec((npairs, 1, LANES), lambda b, j: (j, 0, 0)),
        pl.BlockSpec((1, C_DV), lambda b, j: (0, 0)),
    ]
    args = [proj, proj, proj, proj, proj, gb, on.reshape(1, C_DV)]
    if state is not None:
        in_specs += [pl.BlockSpec((None, hp, C_DK, C_DV), lambda b, j: (b, j, 0, 0)),
                     pl.BlockSpec((None, hp, C_DK), lambda b, j: (b, j, 0)),
                     pl.BlockSpec((None, 1, hp), lambda b, j: (b, 0, j))]
        args += list(state)
    return pl.pallas_call(
        functools.partial(_mlstm_body, t=t, c=c, npairs=npairs, has_state=state is not None),
        grid=(nb, npg),
        in_specs=in_specs,
        out_specs=[pl.BlockSpec((t, vw), lambda b, j: (b, j)),
                   pl.BlockSpec((None, hp, C_DK, C_DV), lambda b, j: (b, j, 0, 0)),
                   pl.BlockSpec((None, hp, 1, C_DK), lambda b, j: (b, j, 0, 0)),
                   pl.BlockSpec((None, hp, 1, LANES), lambda b, j: (b, j, 0, 0))],
        out_shape=[jax.ShapeDtypeStruct((nb * t, C_HEADS * C_DV), F32),
                   jax.ShapeDtypeStruct((nb, C_HEADS, C_DK, C_DV), F32),
                   jax.ShapeDtypeStruct((nb, C_HEADS, 1, C_DK), F32),
                   jax.ShapeDtypeStruct((nb, C_HEADS, 1, LANES), F32)],
        compiler_params=pltpu.CompilerParams(
            dimension_semantics=("parallel", "parallel"), vmem_limit_bytes=VMEM_LIMIT),
        name="mlstm",
    )(*args)


def rms_norm(x, g):
    xf = x.astype(F32)
    y = xf * lax.rsqrt(jnp.mean(xf * xf, axis=-1, keepdims=True) + EPS)
    return (y * g.astype(F32)).astype(x.dtype)


def split_cols(h, sizes):
    cuts = [int(c) for c in np.cumsum(sizes)[:-1]]
    return jnp.split(h, cuts, axis=-1)


def to_blocks(x, nb):
    b, t = x.shape[:2]
    return jnp.swapaxes(x.reshape((b, nb, t // nb) + x.shape[2:]), 0, 1)


def from_blocks(x):
    x = jnp.swapaxes(x, 0, 1)
    return x.reshape((x.shape[0], x.shape[1] * x.shape[2]) + x.shape[3:])


def rel_bucket(rel):
    n = jnp.maximum(rel, 0)
    max_exact = REL_BUCKETS // 2
    nf = jnp.maximum(n, 1).astype(F32)
    large = max_exact + (jnp.log(nf / max_exact) / math.log(REL_MAX_DIST / max_exact)
                         * (REL_BUCKETS - max_exact)).astype(jnp.int32)
    return jnp.where(n < max_exact, n, jnp.minimum(large, REL_BUCKETS - 1))


def rope(x, pos):
    half = ROPE_DIM // 2
    inv = ROPE_THETA ** (-jnp.arange(half, dtype=F32) / half)
    ang = pos.astype(F32)[:, None] * inv[None, :]
    shp = (1, pos.shape[0]) + (1,) * (x.ndim - 3) + (half,)
    cos, sin = jnp.cos(ang).reshape(shp), jnp.sin(ang).reshape(shp)
    xf = x.astype(F32)
    x1, x2 = xf[..., :half], xf[..., half:]
    return jnp.concatenate([x1 * cos - x2 * sin, x2 * cos + x1 * sin], axis=-1).astype(x.dtype)


def dsa_select(q_idx, w_idx, k_idx, q_pos, k_pos, n_sel):
    s = jnp.einsum('bthd,bsd->bths', q_idx.astype(F32), k_idx.astype(F32))
    score = jnp.einsum('bths,bth->bts', jax.nn.relu(s), w_idx.astype(F32))
    allowed = k_pos[None, None, :] <= q_pos[None, :, None]
    score = jnp.where(allowed, score, MASK_NEG)
    _, sel = lax.top_k(score, n_sel)
    valid = sel <= q_pos[None, :, None]
    return sel, valid


def dsa_attend(q, kg, vg, q_pos, sel, valid, rel_bias):
    b, t, h, dh = q.shape
    g = h // A_KV_HEADS
    n_sel = sel.shape[-1]
    qg = q.reshape(b, t, A_KV_HEADS, g, dh)
    lg = jnp.einsum('btkgd,btskd->btkgs', qg, kg).astype(F32) * (dh ** -0.5)
    bias = rel_bias.astype(F32)[rel_bucket(q_pos[None, :, None] - sel)]
    lg = lg + jnp.moveaxis(bias, -1, 2).reshape(b, t, A_KV_HEADS, g, n_sel)
    lg = jnp.where(valid[:, :, None, None, :], lg, MASK_NEG)
    pr = jax.nn.softmax(lg, axis=-1).astype(vg.dtype)
    return jnp.einsum('btkgs,btskd->btkgd', pr, vg).reshape(b, t, h, dh)


def dsa_prompt(q, k, v, q_idx, w_idx, k_idx, rel_bias):
    b, t = q.shape[:2]
    n_sel = min(TOPK_MAX, t // 4)
    qb = math.gcd(Q_BLOCK, t)
    nb = t // qb
    k_pos = jnp.arange(t)
    bidx = jnp.arange(b)[:, None, None]

    def block(args):
        q_b, qi_b, wi_b, qp = args
        sel, valid = dsa_select(qi_b, wi_b, k_idx, qp, k_pos, n_sel)
        return dsa_attend(q_b, k[bidx, sel], v[bidx, sel], qp, sel, valid, rel_bias)

    out = lax.map(block, (to_blocks(q, nb), to_blocks(q_idx, nb), to_blocks(w_idx, nb), k_pos.reshape(nb, qb)))
    return from_blocks(out)


def gather_paged_rows(pool, layer, new_rows, page_table, sel, past):
    b = sel.shape[0]
    in_past = sel < past
    ps = jnp.minimum(sel, past - 1)
    phys = jnp.take_along_axis(page_table, (ps // PAGE_SIZE).reshape(b, -1), axis=1).reshape(sel.shape)
    past_rows = pool[layer, phys, ps % PAGE_SIZE]
    new_idx = jnp.clip(sel - past, 0, new_rows.shape[1] - 1)
    cur_rows = new_rows[jnp.arange(b)[:, None, None], new_idx]
    cond = in_past.reshape(in_past.shape + (1,) * (new_rows.ndim - 2))
    return jnp.where(cond, past_rows, cur_rows)


def dsa_sample(q, k_new, v_new, q_idx, w_idx, kidx_new, pool_k, pool_v, pool_kidx, layer, page_table, rel_bias):
    b, t = q.shape[:2]
    past = page_table.shape[1] * PAGE_SIZE
    n_sel = min(TOPK_MAX, (past + t) // 4)
    kidx_past = pool_kidx[layer, page_table].reshape(b, past, IDX_DIM)
    kidx_all = jnp.concatenate([kidx_past, kidx_new.astype(kidx_past.dtype)], axis=1)
    q_pos = past + jnp.arange(t)
    sel, valid = dsa_select(q_idx, w_idx, kidx_all, q_pos, jnp.arange(past + t), n_sel)
    kg = gather_paged_rows(pool_k, layer, k_new, page_table, sel, past)
    vg = gather_paged_rows(pool_v, layer, v_new, page_table, sel, past)
    return dsa_attend(q, kg, vg, q_pos, sel, valid, rel_bias)


def gla_chunked(q, k, v, log_f, s0):
    b, t, h, dk = q.shape
    c = math.gcd(CHUNK, t)
    nc = t // c
    tri = jnp.tril(jnp.ones((c, c), dtype=bool))

    def step(s, inp):
        qc, kc, vc, lf = inp
        cb = jnp.cumsum(lf, axis=1)
        diff = jnp.where(tri[None, :, :, None, None], cb[:, :, None] - cb[:, None, :], MASK_NEG)
        attn = jnp.einsum('bthd,btshd->bhts', qc, jnp.exp(diff) * kc[:, None])
        o = jnp.einsum('bhts,bshv->bthv', attn, vc) + jnp.einsum('bthd,bhdv->bthv', qc * jnp.exp(cb), s)
        c_last = cb[:, -1]
        s = jnp.exp(c_last)[..., None] * s + jnp.einsum('bshd,bshv->bhdv', kc * jnp.exp(c_last[:, None] - cb), vc)
        return s, o

    xs = tuple(to_blocks(a.astype(F32), nc) for a in (q, k, v, log_f))
    s, o = lax.scan(step, s0.astype(F32), xs)
    return from_blocks(o).astype(v.dtype), s


def mlstm_chunked(q, k, v, i_pre, f_pre, c0, n0, m0):
    b, t, h, dk = q.shape
    c = math.gcd(CHUNK, t)
    nc = t // c
    tri = jnp.tril(jnp.ones((c, c), dtype=bool))
    log_f = jax.nn.log_sigmoid(f_pre.astype(F32))

    def step(carry, inp):
        cm, n, m = carry
        qc, kc, vc, ic, lf = inp
        cb = jnp.cumsum(lf, axis=1)
        dmat = jnp.where(tri[None, :, :, None], cb[:, :, None] - cb[:, None] + ic[:, None], MASK_NEG)
        m_state = cb + m[:, None]
        m_t = jnp.maximum(m_state, jnp.max(dmat, axis=2))
        w = jnp.exp(dmat - m_t[:, :, None])
        w0 = jnp.exp(m_state - m_t)
        qk = jnp.einsum('bthd,bshd->btsh', qc, kc) * w
        num = jnp.einsum('btsh,bshv->bthv', qk, vc) + w0[..., None] * jnp.einsum('bthd,bhvd->bthv', qc, cm)
        den = jnp.sum(qk, axis=2) + w0 * jnp.einsum('bthd,bhd->bth', qc, n)
        hc = num / jnp.maximum(jnp.abs(den), jnp.exp(-m_t))[..., None]
        m_last = m_t[:, -1]
        ws = jnp.exp(cb[:, -1:] - cb + ic - m_last[:, None])
        fs = jnp.exp(cb[:, -1] + m - m_last)
        cm = fs[..., None, None] * cm + jnp.einsum('bshv,bshd->bhvd', vc * ws[..., None], kc)
        n = fs[..., None] * n + jnp.einsum('bsh,bshd->bhd', ws, kc)
        return (cm, n, m_last), hc

    xs = tuple(to_blocks(a.astype(F32), nc) for a in (q, k, v, i_pre, log_f))
    (cm, n, m), hs = lax.scan(step, (c0.astype(F32), n0.astype(F32), m0.astype(F32)), xs)
    return from_blocks(hs).astype(v.dtype), (cm, n, m)


def mla_kv(ckv, kpe, w_kvb, kn):
    b, l, _ = ckv.shape
    kv = (ckv @ w_kvb).reshape(b, l, D_HEADS, NOPE_DIM + D_VDIM)
    k_nope, v = kv[..., :NOPE_DIM], kv[..., NOPE_DIM:]
    k_pe = jnp.broadcast_to(kpe[:, :, None, :], (b, l, D_HEADS, ROPE_DIM)).astype(k_nope.dtype)
    return rms_norm(jnp.concatenate([k_nope, k_pe], axis=-1), kn), v


def causal_attn(q, k, v, q_pos, k_pos):
    b, t, h, dq = q.shape
    scale = dq ** -0.5

    def attend(args):
        q_b, qp = args
        lg = jnp.einsum('bthd,bshd->bhts', q_b, k).astype(F32) * scale
        lg = jnp.where(k_pos[None, None, None, :] <= qp[None, None, :, None], lg, MASK_NEG)
        pr = jax.nn.softmax(lg, axis=-1).astype(v.dtype)
        return jnp.einsum('bhts,bshd->bthd', pr, v)

    qb = math.gcd(Q_BLOCK, t)
    nb = t // qb
    if nb == 1:
        return attend((q, q_pos))
    return from_blocks(lax.map(attend, (to_blocks(q, nb), q_pos.reshape(nb, qb))))


def mla_sample(qd, ckv_new, kpe_new, pool_ckv, pool_kpe, layer, page_table, w_kvb, kn):
    b, t = qd.shape[:2]
    past = page_table.shape[1] * PAGE_SIZE
    q_pos = past + jnp.arange(t)
    k_pos = jnp.arange(past + t)

    def one(args):
        q1, c1, r1, pt = args
        c_all = jnp.concatenate([pool_ckv[layer, pt].reshape(past, KV_LORA), c1.astype(pool_ckv.dtype)], axis=0)[None]
        r_all = jnp.concatenate([pool_kpe[layer, pt].reshape(past, ROPE_DIM), r1.astype(pool_kpe.dtype)], axis=0)[None]
        k, v = mla_kv(c_all, r_all, w_kvb, kn)
        return causal_attn(q1[None], k, v, q_pos, k_pos)[0]

    return lax.map(one, (qd, ckv_new, kpe_new, page_table))


def mem_attend_core(q, mk, mv, qn):
    b, t, _ = q.shape
    q = rms_norm(q.reshape(b, t, MEM_HEADS, HEAD_DIM), qn)
    lg = jnp.einsum('bthd,bshd->bhts', q, mk.astype(q.dtype)).astype(F32) * (HEAD_DIM ** -0.5)
    pr = jax.nn.softmax(lg, axis=-1).astype(q.dtype)
    o = jnp.einsum('bhts,bshd->bthd', pr, mv.astype(q.dtype))
    return o.reshape(b, t, MEM_HEADS * HEAD_DIM)


def hgrn2_mixer(proj, s0, b_on, lower_bound):
    b, t, _ = proj.shape
    qb, fb, ib, gb = (proj[..., c:c + B_HEADS * B_DK] for c in (EV_QB, EV_FB, EV_IB, EV_GB))
    lb = lower_bound.reshape(B_HEADS, B_DK)
    f_pre = fb.reshape(b, t, B_HEADS, B_DK).astype(F32)
    log_f = jnp.logaddexp(jnp.log(jnp.maximum(lb, LB_FLOOR)), jnp.log1p(-lb) + jax.nn.log_sigmoid(f_pre))
    k_b = (1.0 - lb) * jax.nn.sigmoid(-f_pre)
    q_b = jax.nn.silu(qb.reshape(b, t, B_HEADS, B_DK))
    ob, s_new = gla_chunked(q_b, k_b, ib.reshape(b, t, B_HEADS, B_DV), log_f, s0)
    ob = rms_norm(ob, b_on) * jax.nn.silu(gb.reshape(b, t, B_HEADS, B_DV))
    return ob.reshape(b, t, -1), s_new


def dsa_bias_tables(rel_bias, ts, past):
    table = rel_bias.astype(F32)[rel_bucket(jnp.arange(2 * LANES))]
    r = np.arange(LANES)[:, None]
    c = np.arange(LANES)[None, :]
    diag = table[np.clip(r - c, 0, 2 * LANES - 1)]
    prev = table[LANES + r - c]
    far = jnp.broadcast_to(table[2 * LANES - 1], (LANES, LANES, A_HEADS))
    tiles = jnp.moveaxis(jnp.stack([diag, prev, far]), -1, 0)
    n_far = max(past - 2 * LANES, 0)
    rel = past + np.arange(ts)[:, None] - np.arange(n_far, past + LANES)[None, :]
    sample = jnp.concatenate([jnp.broadcast_to(table[2 * LANES - 1], (ts, n_far, A_HEADS)),
                              table[np.clip(rel, 0, 2 * LANES - 1)]], axis=1)
    sample = jnp.moveaxis(sample, -1, 0)
    return tiles, sample


def rope_rows(x, pos):
    half = ROPE_DIM // 2
    inv = ROPE_THETA ** (-jnp.arange(half, dtype=F32) / half)
    ang = pos.astype(F32)[:, None] * inv[None, :]
    shp = (pos.shape[0],) + (1,) * (x.ndim - 2) + (half,)
    cos, sin = jnp.cos(ang).reshape(shp), jnp.sin(ang).reshape(shp)
    x1, x2 = x[..., :half], x[..., half:]
    return jnp.concatenate([x1 * cos - x2 * sin, x2 * cos + x1 * sin], axis=-1)


def mla_sample_pallas(qd, ckv, kpe, st, o, kn, wb):
    b, t = qd.shape[:2]
    qn = (qd[..., :NOPE_DIM] * kn[:NOPE_DIM]).reshape(b * t, D_HEADS * NOPE_DIM)
    qlat = matmul(qn, wb['d_kt_blockdiag'][o])
    qlat = qlat.reshape(b, t, D_HEADS, KV_LORA).transpose(0, 2, 1, 3).reshape(b, D_HEADS * t, KV_LORA)
    qpe = (qd[..., NOPE_DIM:] * kn[NOPE_DIM:]).transpose(0, 2, 1, 3).reshape(b, D_HEADS * t, ROPE_DIM)
    cnew = jnp.pad(ckv, ((0, 0), (0, LANES - t), (0, 0)))
    rnew_t = jnp.swapaxes(jnp.pad(kpe, ((0, 0), (0, LANES - t), (0, 0))), 1, 2)
    od = mla_sample_attend(qlat.astype(BF16), qpe.astype(BF16), cnew, rnew_t, wb['d_kt'][o], wb['d_v'][o],
                           st['cache_d_ckv'], st['pool_kpe_t'], st['pt_flat'], o,
                           nb=b, ts=t, n_pages=st['page_table'].shape[1])
    return od.reshape(b, t, D_HEADS, D_VDIM)


def odd_mixer(proj, pos, mode, st, p, o, wb):
    b, t, _ = proj.shape
    qc, kc, vc, ic, fc, oc, qa, ckv, kpe = split_cols(proj, ODD_SPLITS)
    gate_b = p['c_gate_b'][o].astype(F32)
    i_pre = ic.astype(F32) + gate_b[0]
    f_pre = fc.astype(F32) + gate_b[1]
    q_c = qc.reshape(b, t, C_HEADS, C_DK)
    k_c = kc.reshape(b, t, C_HEADS, C_DK) * (C_DK ** -0.5)
    v_c = vc.reshape(b, t, C_HEADS, C_DV)
    if mode == 'prompt':
        c0 = jnp.zeros((b, C_HEADS, C_DV, C_DK), F32)
        n0 = jnp.zeros((b, C_HEADS, C_DK), F32)
        m0 = jnp.zeros((b, C_HEADS), F32)
    else:
        c0, n0, m0 = st['state_c_C'][o], st['state_c_n'][o], st['state_c_m'][o]
    hc, (c1, n1, m1) = mlstm_chunked(q_c, k_c, v_c, i_pre, f_pre, c0, n0, m0)
    hc = rms_norm(hc, p['c_on'][o]) * jax.nn.sigmoid(oc.reshape(b, t, C_HEADS, C_DV))
    qf = matmul(qa.reshape(b * t, Q_LORA), wb['w_d_qb'][o], gain=p['d_qa_g'][o])
    qf = qf.reshape(b, t, D_HEADS, NOPE_DIM + ROPE_DIM)
    qd = rms_norm(jnp.concatenate([qf[..., :NOPE_DIM], rope(qf[..., NOPE_DIM:], pos)], axis=-1), p['d_qn'][o])
    ckv = rms_norm(ckv, p['d_kv_g'][o])
    kpe = rope(kpe, pos)
    if mode == 'prompt':
        kd, vd = mla_kv(ckv, kpe, p['w_d_kvb'][o], p['d_kn'][o])
        od = causal_attn(qd, kd, vd, pos, pos)
    else:
        od = mla_sample_pallas(qd, ckv, kpe, st, o, p['d_kn'][o], wb)
    mixed = jnp.concatenate([hc.reshape(b, t, -1).astype(od.dtype), od.reshape(b, t, -1)], axis=-1)
    return mixed, (ckv, kpe, c1, n1, m1)


def kernel(x_prompt, x_sample, cache_a_k, cache_a_v, cache_a_kidx, state_b, state_c_C, state_c_n, state_c_m, cache_d_ckv, cache_d_kpe, cache_mem_k, cache_mem_v, page_table, mem_prompt, g_ffn1, w_ffn1_gu, w_ffn1_dn, g_mix, w_in_even, w_in_odd, w_mix_out, rel_bias, a_qn, a_kn, b_lb, b_on, c_gate_b, c_on, d_qa_g, d_kv_g, w_d_qb, w_d_kvb, d_qn, d_kn, g_mem, w_mem_q, w_mem_kv, w_mem_o, mem_qn, mem_kn, g_ffn2, w_ffn2_gu, w_ffn2_dn):
    p = {'rel_bias': rel_bias, 'a_qn': a_qn, 'a_kn': a_kn, 'b_on': b_on, 'c_gate_b': c_gate_b, 'c_on': c_on,
         'd_qa_g': d_qa_g, 'd_kv_g': d_kv_g, 'w_d_kvb': w_d_kvb, 'd_qn': d_qn, 'd_kn': d_kn,
         'mem_qn': mem_qn, 'mem_kn': mem_kn}
    bp, tp, d = x_prompt.shape
    bs, ts, _ = x_sample.shape
    n_p, n_s = bp * tp, bs * ts
    n_mem = mem_prompt.shape[1]
    past = page_table.shape[1] * PAGE_SIZE

    wo = w_in_odd.astype(BF16)
    n_odd = wo.shape[0]
    oc = np.cumsum([0] + ODD_SPLITS)
    gate_cols = []
    for pr in range(C_HEADS // 2):
        gate_cols += [wo[:, :, oc[3] + 2 * pr:oc[3] + 2 * pr + 2], wo[:, :, oc[4] + 2 * pr:oc[4] + 2 * pr + 2],
                      jnp.zeros((n_odd, d, LANES - 4), BF16)]
    in_odd = jnp.concatenate(
        [wo[:, :, oc[0]:oc[3]], wo[:, :, oc[5]:oc[6]]] + gate_cols
        + [wo[:, :, oc[6]:oc[9]], jnp.zeros((n_odd, d, LANES - ROPE_DIM), BF16)], axis=-1)
    gb = c_gate_b.astype(F32)
    gate_bias = jnp.concatenate(
        [gb[:, 0].reshape(n_odd, C_HEADS // 2, 2), gb[:, 1].reshape(n_odd, C_HEADS // 2, 2),
         jnp.zeros((n_odd, C_HEADS // 2, LANES - 4), F32)], axis=-1)[:, :, None, :]
    we = w_in_even.astype(BF16)
    c_wi = sum(EVEN_SPLITS[:4])
    c_ki = c_wi + IDX_HEADS
    c_qb = c_ki + IDX_DIM
    n_even = we.shape[0]
    c_ka = EVEN_SPLITS[0]
    in_even = jnp.concatenate(
        [we[:, :, :c_ka], we[:, :, c_qb:], we[:, :, c_ka:c_wi], we[:, :, c_ki:c_qb], we[:, :, c_ki:c_qb],
         we[:, :, c_wi:c_ki], jnp.zeros((n_even, d, LANES - IDX_HEADS), BF16)], axis=-1)
    kvb = w_d_kvb.astype(BF16).reshape(-1, KV_LORA, D_HEADS, NOPE_DIM + D_VDIM)
    d_kt3 = kvb[..., :NOPE_DIM].transpose(0, 2, 3, 1)
    eye = jnp.eye(D_HEADS, dtype=BF16)
    d_kt_bd = (d_kt3[:, :, :, None, :] * eye[None, :, None, :, None]).reshape(
        -1, D_HEADS * NOPE_DIM, D_HEADS * KV_LORA)
    wb = {
        'ffn1_gu': w_ffn1_gu.astype(BF16), 'ffn1_dn': w_ffn1_dn.astype(BF16),
        'ffn2_gu': w_ffn2_gu.astype(BF16), 'ffn2_dn': w_ffn2_dn.astype(BF16),
        'in_even': in_even,
        'in_odd': in_odd,
        'mix_out': w_mix_out.astype(BF16), 'w_d_qb': w_d_qb.astype(BF16), 'w_d_kvb': w_d_kvb.astype(BF16),
        'mem_q': w_mem_q.astype(BF16), 'mem_kv': w_mem_kv.astype(BF16), 'mem_o': w_mem_o.astype(BF16),
        'd_kt': d_kt3.reshape(-1, D_HEADS * NOPE_DIM, KV_LORA), 'd_kt_blockdiag': d_kt_bd,
        'd_v': kvb[..., NOPE_DIM:].transpose(0, 2, 1, 3),
    }
    bias_tiles, bias_sample = dsa_bias_tables(rel_bias, ts, past)
    n_pages = page_table.shape[1]
    n_phys = cache_a_k.shape[1]
    pool_k = cache_a_k.reshape(-1, n_phys, PAGE_SIZE * A_KV_HEADS, HEAD_DIM)
    pool_v = cache_a_v.reshape(-1, n_phys, PAGE_SIZE * A_KV_HEADS, HEAD_DIM)
    pool_kidx_t = jnp.swapaxes(cache_a_kidx, 2, 3)
    pool_kpe_t = jnp.swapaxes(cache_d_kpe, 2, 3)
    mem_k_rows = cache_mem_k.reshape(DEPTH, bs, n_mem * MEM_HEADS, HEAD_DIM)
    mem_v_rows = cache_mem_v.reshape(DEPTH, bs, n_mem * MEM_HEADS, HEAD_DIM)
    pt_flat = page_table.reshape(-1)

    lb_soft = jax.nn.softmax(b_lb.astype(F32), axis=0)
    lower_bound = jnp.cumsum(lb_soft, axis=0) - lb_soft[0]

    pos_all = jnp.concatenate([jnp.tile(jnp.arange(tp), bp), jnp.tile(past + jnp.arange(ts), bs)])
    st = {'cache_a_k': cache_a_k, 'cache_a_v': cache_a_v, 'cache_a_kidx': cache_a_kidx, 'state_b': state_b,
          'state_c_C': state_c_C, 'state_c_n': state_c_n, 'state_c_m': state_c_m,
          'cache_d_ckv': cache_d_ckv, 'pool_kpe_t': pool_kpe_t, 'page_table': page_table, 'pt_flat': pt_flat}

    x = jnp.concatenate([x_prompt.reshape(n_p, d), x_sample.reshape(n_s, d)], axis=0)
    mem2d = mem_prompt.reshape(bp * n_mem, d)
    new_p = {n: [] for n in ('a_k', 'a_v', 'a_kidx', 'b_S', 'c_C', 'c_n', 'c_m', 'd_ckv', 'd_kpe', 'mem_k', 'mem_v')}
    new_s = {n: [] for n in ('a_k', 'a_v', 'a_kidx', 'b_S', 'c_C', 'c_n', 'c_m', 'd_ckv', 'd_kpe')}

    for l in range(DEPTH):
        x = ffn_half_step(x, g_ffn1[l], wb['ffn1_gu'][l], wb['ffn1_dn'][l])
        if l % 2 == 0:
            e = l // 2
            proj = matmul(x, wb['in_even'][e], gain=g_mix[l], tn=1280)
            ka32, ka16 = head_norm(proj, EV_KA // (A_KV_HEADS * HEAD_DIM), A_KV_HEADS, a_kn[e])
            oa_p = dsa_prompt_attend(proj, ka16, bias_tiles, a_qn[e], nb=bp, t=tp)
            sel_past, sel_new = dsa_sample_select(proj, pool_kidx_t, pt_flat, e, row0=n_p, nb=bs, ts=ts,
                                                  n_pages=n_pages)
            oa_s = dsa_sample_attend(proj, ka32, sel_past, sel_new, bias_sample, pool_k, pool_v, pt_flat, e, a_qn[e],
                                     row0=n_p, nb=bs, ts=ts, n_pages=n_pages)
            lb = lower_bound[e].reshape(B_HEADS, B_DK)
            ob_p, s_p = hgrn2_prompt(proj, lb, b_on[e], nb=bp, t=tp)
            ob_s, s_s = hgrn2_sample(proj, lb, b_on[e], state_b[e], row0=n_p, nb=bs, t=ts)
            mixed_p = jnp.concatenate([oa_p, ob_p], axis=-1)
            mixed_s = jnp.concatenate([oa_s, ob_s], axis=-1)
            va = proj[:, EV_VA:EV_VA + A_KV_HEADS * HEAD_DIM]
            ki = proj[:, EV_KIKI:EV_KIKI + IDX_DIM]
            for new, rows, nb_, t_, s_new in ((new_p, slice(0, n_p), bp, tp, s_p), (new_s, slice(n_p, None), bs, ts, s_s)):
                new['a_k'].append(ka32[rows].reshape(nb_, t_, A_KV_HEADS, HEAD_DIM))
                new['a_v'].append(va[rows].reshape(nb_, t_, A_KV_HEADS, HEAD_DIM))
                new['a_kidx'].append(ki[rows].reshape(nb_, t_, IDX_DIM))
                new['b_S'].append(s_new)
        else:
            o = l // 2
            proj = matmul(x, wb['in_odd'][o], gain=g_mix[l], tn=896)
            hc_p, cc_p, cn_p, cm_p = mlstm(proj, gate_bias[o], c_on[o], None, row0=0, nb=bp, t=tp, npairs=2)
            hc_s, cc_s, cn_s, cm_s = mlstm(
                proj, gate_bias[o], c_on[o],
                (jnp.swapaxes(state_c_C[o], -1, -2), state_c_n[o], state_c_m[o].reshape(bs, 1, C_HEADS)),
                row0=n_p, nb=bs, t=ts, npairs=C_HEADS // 2)
            qf = matmul(proj, wb['w_d_qb'][o], gain=d_qa_g[o], a_col=OD_QA // Q_LORA)
            qf = qf.reshape(n_p + n_s, D_HEADS, NOPE_DIM + ROPE_DIM)
            qd = rms_norm(jnp.concatenate([qf[..., :NOPE_DIM], rope_rows(qf[..., NOPE_DIM:], pos_all)], axis=-1),
                          d_qn[o])
            ckv = rms_norm(proj[:, OD_CKV:OD_CKV + KV_LORA], d_kv_g[o])
            kpe = rope_rows(proj[:, OD_KPE:OD_KPE + ROPE_DIM], pos_all)
            kv = matmul(ckv[:n_p], wb['w_d_kvb'][o]).reshape(n_p, D_HEADS, NOPE_DIM + D_VDIM)
            k_pe = jnp.broadcast_to(kpe[:n_p, None, :], (n_p, D_HEADS, ROPE_DIM))
            kd = rms_norm(jnp.concatenate([kv[..., :NOPE_DIM], k_pe], axis=-1), d_kn[o])
            pad = ((0, 0), (0, 0), (0, 2 * LANES - NOPE_DIM - ROPE_DIM))
            q16 = jnp.pad(qd[:n_p], pad).astype(BF16).reshape(n_p, D_HEADS * 2 * LANES)
            k16 = jnp.pad(kd, pad).astype(BF16).reshape(n_p, D_HEADS * 2 * LANES)
            v16 = kv[..., NOPE_DIM:].astype(BF16).reshape(n_p, D_HEADS * D_VDIM)
            od_p = causal_attend(q16, k16, v16, nb=bp, t=tp, heads=D_HEADS, dv=D_VDIM,
                                 scale=(NOPE_DIM + ROPE_DIM) ** -0.5)
            od_s = mla_sample_pallas(qd[n_p:].reshape(bs, ts, D_HEADS, NOPE_DIM + ROPE_DIM),
                                     ckv[n_p:].reshape(bs, ts, KV_LORA), kpe[n_p:].reshape(bs, ts, ROPE_DIM),
                                     st, o, d_kn[o], wb).reshape(n_s, D_HEADS * D_VDIM)
            mixed_p = jnp.concatenate([hc_p, od_p], axis=-1)
            mixed_s = jnp.concatenate([hc_s, od_s], axis=-1)
            for new, rows, nb_, t_, cc, cn, cm in ((new_p, slice(0, n_p), bp, tp, cc_p, cn_p, cm_p),
                                                   (new_s, slice(n_p, None), bs, ts, cc_s, cn_s, cm_s)):
                new['d_ckv'].append(ckv[rows].reshape(nb_, t_, KV_LORA))
                new['d_kpe'].append(kpe[rows].reshape(nb_, t_, ROPE_DIM))
                new['c_C'].append(jnp.swapaxes(cc, -1, -2))
                new['c_n'].append(cn[:, :, 0, :])
                new['c_m'].append(cm[:, :, 0, 0])
        mixed = jnp.concatenate([mixed_p, mixed_s], axis=0)
        x = matmul(mixed, wb['mix_out'][l], res=x)

        kvm = matmul(mem2d, wb['mem_kv'][l])
        mk32, mk16 = head_norm(kvm, 0, MEM_HEADS, mem_kn[l], tm=bp * n_mem)
        wm = MEM_HEADS * HEAD_DIM
        new_p['mem_k'].append(mk32.reshape(bp, n_mem, MEM_HEADS, HEAD_DIM))
        new_p['mem_v'].append(kvm[:, wm:].reshape(bp, n_mem, MEM_HEADS, HEAD_DIM))
        q = matmul(x, wb['mem_q'][l], gain=g_mem[l])
        nqb = 4
        o_p = mem_attend(q, mk16.reshape(bp, n_mem, wm), kvm.reshape(bp, n_mem, 2 * wm), mem_qn[l],
                         row0=0, n_batch=bp * nqb, tq=tp // nqb, nb=1, n_mem=n_mem, head_rows=False,
                         k_index=lambda i: (i // nqb, 0, 0), v_index=lambda i: (i // nqb, 0, 1))
        o_s = mem_attend(q, mem_k_rows, mem_v_rows, mem_qn[l], row0=n_p, n_batch=bs, tq=ts, nb=8,
                         n_mem=n_mem, head_rows=True,
                         k_index=lambda i, l=l: (l, i, 0, 0), v_index=lambda i, l=l: (l, i, 0, 0))
        o_all = jnp.concatenate([o_p, o_s], axis=0)
        x = matmul(o_all, wb['mem_o'][l], res=x)
        x = ffn_half_step(x, g_ffn2[l], wb['ffn2_gu'][l], wb['ffn2_dn'][l])

    sp = {n: jnp.stack(v) for n, v in new_p.items()}
    ss = {n: jnp.stack(v) for n, v in new_s.items()}
    return (x[:n_p].reshape(bp, tp, d), x[n_p:].reshape(bs, ts, d),
            sp['a_k'], sp['a_v'], sp['a_kidx'], sp['b_S'], sp['c_C'], sp['c_n'], sp['c_m'], sp['d_ckv'], sp['d_kpe'],
            sp['mem_k'], sp['mem_v'],
            ss['a_k'], ss['a_v'], ss['a_kidx'], ss['b_S'], ss['c_C'], ss['c_n'], ss['c_m'], ss['d_ckv'], ss['d_kpe'])
```

```python
import functools
import math

import jax
import jax.numpy as jnp
import numpy as np
from jax import lax
from jax.experimental import pallas as pl
from jax.experimental.pallas import tpu as pltpu

D_MODEL = 2048
DEPTH = 4
PAGE_SIZE = 128
HEAD_DIM = 128
A_HEADS = 8
A_KV_HEADS = 2
IDX_HEADS = 8
IDX_DIM = 64
TOPK_MAX = 256
REL_BUCKETS = 32
REL_MAX_DIST = 128
B_HEADS = 8
B_DK = 128
B_DV = 128
C_HEADS = 8
C_DK = 64
C_DV = 128
D_HEADS = 8
Q_LORA = 512
KV_LORA = 256
NOPE_DIM = 128
ROPE_DIM = 64
D_VDIM = 128
ROPE_THETA = 10000.0
MEM_HEADS = 4
D_FF = 5632
CHUNK = 64
Q_BLOCK = 128
EPS = 1e-6
MASK_NEG = -1e30
LB_FLOOR = 1e-20
F32 = jnp.float32
BF16 = jnp.bfloat16

EVEN_SPLITS = [A_HEADS * HEAD_DIM, A_KV_HEADS * HEAD_DIM, A_KV_HEADS * HEAD_DIM, IDX_HEADS * IDX_DIM, IDX_HEADS,
               IDX_DIM, B_HEADS * B_DK, B_HEADS * B_DK, B_HEADS * B_DV, B_HEADS * B_DV]
ODD_SPLITS = [C_HEADS * C_DK, C_HEADS * C_DK, C_HEADS * C_DV, C_HEADS, C_HEADS, C_HEADS * C_DV, Q_LORA, KV_LORA,
              ROPE_DIM]
EVEN_IN = sum(EVEN_SPLITS)
ODD_IN = sum(ODD_SPLITS)

EV_QA, EV_QB, EV_FB, EV_IB, EV_GB, EV_KA, EV_VA, EV_QI, EV_KIKI, EV_WI, EVEN_PAD = (
    0, 1024, 2048, 3072, 4096, 5120, 5376, 5632, 6144, 6272, 6400)
OD_QC, OD_KC, OD_VC, OD_OC, OD_G, OD_QA, OD_CKV, OD_KPE, ODD_PAD = (
    0, 512, 1024, 2048, 3072, 3584, 4096, 4352, 4480)

LANES = 128
VMEM_LIMIT = 56 * 1024 * 1024


def _round_up(n, m):
    return (n + m - 1) // m * m


def _ffn_body(x_ref, g_ref, wg_ref, wu_ref, wd_ref, o_ref, n_ref):
    f = pl.program_id(1)

    @pl.when(f == 0)
    def _():
        x = x_ref[...]
        ms = jnp.mean(x * x, axis=-1, keepdims=True)
        n_ref[...] = (x * lax.rsqrt(ms + EPS) * g_ref[...]).astype(BF16)
        o_ref[...] = x

    n = n_ref[...]
    a = jnp.dot(n, wg_ref[...], preferred_element_type=F32)
    b = jnp.dot(n, wu_ref[...], preferred_element_type=F32)
    h = (a * jax.nn.sigmoid(a) * b).astype(BF16)
    o_ref[...] += 0.5 * jnp.dot(h, wd_ref[...], preferred_element_type=F32)


def ffn_half_step(x, g, w_gu, w_dn, layer, *, tm=512, tf=512):
    m, d = x.shape
    ff = w_dn.shape[1]
    nf = ff // tf
    return pl.pallas_call(
        _ffn_body,
        grid=(m // tm, nf),
        in_specs=[
            pl.BlockSpec((tm, d), lambda i, f: (i, 0)),
            pl.BlockSpec((1, d), lambda i, f: (0, 0)),
            pl.BlockSpec((None, d, tf), lambda i, f: (layer, 0, f)),
            pl.BlockSpec((None, d, tf), lambda i, f: (layer, 0, f + nf)),
            pl.BlockSpec((None, tf, d), lambda i, f: (layer, f, 0)),
        ],
        out_specs=pl.BlockSpec((tm, d), lambda i, f: (i, 0)),
        out_shape=jax.ShapeDtypeStruct((m, d), F32),
        scratch_shapes=[pltpu.VMEM((tm, d), BF16)],
        compiler_params=pltpu.CompilerParams(
            dimension_semantics=("parallel", "arbitrary"), vmem_limit_bytes=VMEM_LIMIT),
        name="ffn_half_step",
    )(x, g.reshape(1, d), w_gu, w_gu, w_dn)


def _mm_body(*refs, has_gain, has_res, alpha):
    a_ref, w_ref = refs[0], refs[1]
    k = 2
    g_ref = res_ref = None
    if has_gain:
        g_ref = refs[k]
        k += 1
    if has_res:
        res_ref = refs[k]
        k += 1
    o_ref, n_ref = refs[k], refs[k + 1]

    @pl.when(pl.program_id(1) == 0)
    def _():
        a = a_ref[...]
        if has_gain:
            ms = jnp.mean(a * a, axis=-1, keepdims=True)
            a = a * lax.rsqrt(ms + EPS) * g_ref[...]
        n_ref[...] = a.astype(BF16)

    acc = jnp.dot(n_ref[...], w_ref[...], preferred_element_type=F32)
    if has_res:
        acc = res_ref[...] + alpha * acc
    o_ref[...] = acc


def matmul(a, w, *, gain=None, res=None, alpha=1.0, tm=512, tn=512, a_col=0, layer=None):
    m = a.shape[0]
    kdim, n = w.shape[-2:]
    tm = min(tm, m)
    tn = min(tn, n)
    assert m % tm == 0 and n % tn == 0, (m, tm, n, tn)
    w_spec = (pl.BlockSpec((kdim, tn), lambda i, j: (0, j)) if layer is None
              else pl.BlockSpec((None, kdim, tn), lambda i, j: (layer, 0, j)))
    in_specs = [pl.BlockSpec((tm, kdim), lambda i, j: (i, a_col)), w_spec]
    args = [a, w]
    if gain is not None:
        in_specs.append(pl.BlockSpec((1, kdim), lambda i, j: (0, 0)))
        args.append(gain.reshape(1, kdim))
    if res is not None:
        in_specs.append(pl.BlockSpec((tm, tn), lambda i, j: (i, j)))
        args.append(res)
    return pl.pallas_call(
        functools.partial(_mm_body, has_gain=gain is not None, has_res=res is not None, alpha=alpha),
        grid=(m // tm, n // tn),
        in_specs=in_specs,
        out_specs=pl.BlockSpec((tm, tn), lambda i, j: (i, j)),
        out_shape=jax.ShapeDtypeStruct((m, n), F32),
        scratch_shapes=[pltpu.VMEM((tm, kdim), BF16)],
        compiler_params=pltpu.CompilerParams(
            dimension_semantics=("parallel", "arbitrary"), vmem_limit_bytes=VMEM_LIMIT),
        name="matmul",
    )(*args)


def _head_norm_body(x_ref, g_ref, o32_ref, o16_ref, *, heads):
    outs = []
    for h in range(heads):
        x = x_ref[:, LANES * h:LANES * (h + 1)]
        outs.append(x * lax.rsqrt(jnp.mean(x * x, axis=-1, keepdims=True) + EPS) * g_ref[...])
    y = jnp.concatenate(outs, axis=1)
    o32_ref[...] = y
    o16_ref[...] = y.astype(BF16)


def head_norm(x, col_block, heads, gain, *, tm=1024):
    m = x.shape[0]
    w = heads * LANES
    return pl.pallas_call(
        functools.partial(_head_norm_body, heads=heads),
        grid=(m // tm,),
        in_specs=[pl.BlockSpec((tm, w), lambda i: (i, col_block)),
                  pl.BlockSpec((1, LANES), lambda i: (0, 0))],
        out_specs=[pl.BlockSpec((tm, w), lambda i: (i, 0)), pl.BlockSpec((tm, w), lambda i: (i, 0))],
        out_shape=[jax.ShapeDtypeStruct((m, w), F32), jax.ShapeDtypeStruct((m, w), BF16)],
        compiler_params=pltpu.CompilerParams(dimension_semantics=("parallel",)),
        name="head_norm",
    )(x, gain.reshape(1, LANES))


INT_MIN = -2 ** 31
INT_MAX = 2 ** 31 - 1
_NT = (((1,), (1,)), ((), ()))


def _dot_nt(a, b):
    return lax.dot_general(a, b, _NT, preferred_element_type=F32)


def _sort_key(x):
    b = lax.bitcast_convert_type(x + 0.0, jnp.int32)
    return jnp.where(b >= 0, b, b ^ jnp.int32(INT_MAX))


def _row_count(cond):
    c = jnp.where(cond, 1.0, 0.0)
    if c.ndim == 3:
        c = jnp.sum(c, axis=0)
    return jnp.sum(c, axis=-1, keepdims=True)


def _topk_masks(pieces, k, idx_bits):
    kf = float(k)

    def count(fn):
        tot = None
        for key, idx in pieces:
            c = _row_count(fn(key, idx))
            tot = c if tot is None else tot + c
        return tot

    t0 = jnp.where(count(lambda key, idx: key >= 0) >= kf, jnp.int32(0), jnp.int32(INT_MIN))

    def body(i, t):
        cand = t | jnp.left_shift(jnp.int32(1), 30 - i)
        return jnp.where(count(lambda key, idx: key >= cand) >= kf, cand, t)

    thr = lax.fori_loop(0, 31, body, t0)
    need = kf - count(lambda key, idx: key > thr)
    ties = [(jnp.where(key == thr, idx, INT_MAX), idx) for key, idx in pieces]

    def count_ties(cand):
        tot = None
        for tie, _ in ties:
            c = _row_count(tie < cand)
            tot = c if tot is None else tot + c
        return tot

    def body2(i, cur):
        cand = cur + jnp.left_shift(jnp.int32(1), idx_bits - 1 - i)
        return jnp.where(count_ties(cand) < need, cand, cur)

    cut = lax.fori_loop(0, idx_bits, body2, jnp.zeros_like(thr))
    return [jnp.where(key > thr, 1.0, jnp.where(tie <= cut, 1.0, 0.0))
            for (key, _), (tie, _) in zip(pieces, ties)]


def _softmax_rows(lg):
    mx = jnp.max(lg, axis=-1, keepdims=True)
    p = jnp.exp(lg - mx)
    return p / jnp.sum(p, axis=-1, keepdims=True)


def _dsa_prompt_body(qa_ref, qi_ref, wi_ref, kiki_ref, k_ref, v_ref, bias_ref, qn_ref, o_ref, *, qb, t, n_sel, i0, kw):
    i = i0 + pl.program_id(1)
    nsub = qb // LANES

    def run(kw):
        nk = kw // LANES
        kiki = kiki_ref[0:kw, :].astype(BF16)
        qi = qi_ref[...] * (IDX_DIM ** -0.5)
        w = wi_ref[...] * (IDX_HEADS ** -0.5)
        lane = lax.broadcasted_iota(jnp.int32, (qb, LANES), 1)
        score = jnp.zeros((qb, kw), F32)
        for h in range(IDX_HEADS):
            blk = qi[:, LANES * (h // 2):LANES * (h // 2 + 1)]
            keep = (lane < IDX_DIM) if h % 2 == 0 else (lane >= IDX_DIM)
            s = _dot_nt(jnp.where(keep, blk, 0.0).astype(BF16), kiki)
            score = score + jnp.maximum(s, 0.0) * w[:, h:h + 1]
        q_pos = i * qb + lax.broadcasted_iota(jnp.int32, (qb, kw), 0)
        k_pos = lax.broadcasted_iota(jnp.int32, (qb, kw), 1)
        allowed = k_pos <= q_pos
        score = jnp.where(allowed, score, MASK_NEG)
        pieces = [(_sort_key(score), k_pos)]
        nv = _round_up(n_sel, LANES) if kw < t else 0
        if nv:
            pieces.append((_sort_key(jnp.full((qb, nv), MASK_NEG, F32)),
                           kw + lax.broadcasted_iota(jnp.int32, (qb, nv), 1)))
        selm = _topk_masks(pieces, n_sel, max(1, (kw + nv - 1).bit_length()))[0]
        sel = jnp.where(allowed, selm, 0.0) > 0.5

        k = k_ref[0:kw, :]
        v = v_ref[0:kw, :].astype(BF16)
        outs = []
        for h in range(A_HEADS):
            g = h // (A_HEADS // A_KV_HEADS)
            rows = []
            for a in range(nsub):
                d0 = i * nsub + a
                tiles = []
                for j in range(nk):
                    d = d0 - j
                    tiles.append(jnp.where(d == 0, bias_ref[h, 0],
                                           jnp.where(d == 1, bias_ref[h, 1], bias_ref[h, 2])))
                rows.append(jnp.concatenate(tiles, axis=1))
            bias = jnp.concatenate(rows, axis=0) if nsub > 1 else rows[0]
            q = qa_ref[:, LANES * h:LANES * (h + 1)]
            q = q * lax.rsqrt(jnp.mean(q * q, axis=-1, keepdims=True) + EPS) * qn_ref[...]
            lg = _dot_nt(q.astype(BF16), k[:, LANES * g:LANES * (g + 1)]) * (HEAD_DIM ** -0.5) + bias
            p = _softmax_rows(jnp.where(sel, lg, MASK_NEG))
            outs.append(jnp.dot(p.astype(BF16), v[:, LANES * g:LANES * (g + 1)], preferred_element_type=F32))
        o_ref[...] = jnp.concatenate(outs, axis=1)

    run(kw)


def dsa_prompt_attend(proj, ka16, bias_tiles, qn, *, nb, t, qb=256, splits=2):
    nq = t // qb
    wq = A_HEADS * HEAD_DIM
    per = -(-nq // splits)
    outs = []
    for i0 in range(0, nq, per):
        ni = min(per, nq - i0)
        o = _dsa_prompt_call(proj, ka16, bias_tiles, qn, nb=nb, t=t, qb=qb, i0=i0, ni=ni, kw=min(t, (i0 + ni) * qb))
        outs.append(o.reshape(nb, ni * qb, wq))
    return jnp.concatenate(outs, axis=1).reshape(nb * t, wq)


def _dsa_prompt_call(proj, ka16, bias_tiles, qn, *, nb, t, qb, i0, ni, kw):
    nq = t // qb
    n_sel = min(TOPK_MAX, t // 4)
    wq = A_HEADS * HEAD_DIM
    assert t % kw == 0, (t, kw)
    return pl.pallas_call(
        functools.partial(_dsa_prompt_body, qb=qb, t=t, n_sel=n_sel, i0=i0, kw=kw),
        grid=(nb, ni),
        in_specs=[
            pl.BlockSpec((qb, wq), lambda b, i: (b * nq + i0 + i, 0)),
            pl.BlockSpec((qb, IDX_HEADS * IDX_DIM), lambda b, i: (b * nq + i0 + i, EV_QI // (IDX_HEADS * IDX_DIM))),
            pl.BlockSpec((qb, LANES), lambda b, i: (b * nq + i0 + i, EV_WI // LANES)),
            pl.BlockSpec((kw, LANES), lambda b, i: (b * (t // kw), EV_KIKI // LANES)),
            pl.BlockSpec((kw, A_KV_HEADS * HEAD_DIM), lambda b, i: (b * (t // kw), 0)),
            pl.BlockSpec((kw, A_KV_HEADS * HEAD_DIM), lambda b, i: (b * (t // kw), EV_VA // (A_KV_HEADS * HEAD_DIM))),
            pl.BlockSpec((A_HEADS, 3, LANES, LANES), lambda b, i: (0, 0, 0, 0)),
            pl.BlockSpec((1, HEAD_DIM), lambda b, i: (0, 0)),
        ],
        out_specs=pl.BlockSpec((qb, wq), lambda b, i: (b * ni + i, 0)),
        out_shape=jax.ShapeDtypeStruct((nb * ni * qb, wq), F32),
        compiler_params=pltpu.CompilerParams(
            dimension_semantics=("parallel", "arbitrary"), vmem_limit_bytes=VMEM_LIMIT),
        name="dsa_prompt",
    )(proj, proj, proj, proj, ka16, proj, bias_tiles, qn.reshape(1, HEAD_DIM))


def _idx_head_sum(s, w):
    tot = None
    for h in range(IDX_HEADS):
        c = jnp.maximum(s[8 * h:8 * (h + 1)], 0.0) * w[:, h:h + 1]
        tot = c if tot is None else tot + c
    return tot


def _dsa_score_body(pt_ref, qi_ref, wi_ref, knew_ref, *rest, pp, ts):
    pages, sp_ref, sn_ref = rest[:pp], rest[pp], rest[pp + 1]
    qi = qi_ref[...] * (IDX_DIM ** -0.5)
    a = jnp.concatenate([qi[:, IDX_DIM * h:IDX_DIM * (h + 1)] for h in range(IDX_HEADS)], axis=0).astype(BF16)
    w = wi_ref[...] * (IDX_HEADS ** -0.5)
    kt = jnp.concatenate([p[...] for p in pages], axis=1).astype(BF16)
    sp_ref[...] = _idx_head_sum(jnp.dot(a, kt, preferred_element_type=F32), w)

    @pl.when(pl.program_id(1) == 0)
    def _():
        knew = knew_ref[...][:, :IDX_DIM]
        kn = jnp.concatenate([knew, jnp.zeros((LANES - ts, IDX_DIM), F32)], axis=0).astype(BF16)
        sn_ref[...] = _idx_head_sum(_dot_nt(a, kn), w)


def _dsa_mask_body(sp_ref, sn_ref, mp_ref, mn_ref, *, ts, n_sel):
    rows, past = sp_ref.shape
    lane = lax.broadcasted_iota(jnp.int32, (rows // ts, ts, LANES), 2).reshape(rows, LANES)
    t_row = lax.broadcasted_iota(jnp.int32, (rows // ts, ts, LANES), 1).reshape(rows, LANES)
    causal = lane <= t_row
    snew = jnp.where(causal, sn_ref[...], MASK_NEG)
    key_new = jnp.where(lane < ts, _sort_key(snew), INT_MIN)
    idx_past = lax.broadcasted_iota(jnp.int32, (rows, past), 1)
    m_past, m_new = _topk_masks([(_sort_key(sp_ref[...]), idx_past), (key_new, past + lane)], n_sel,
                                (past + LANES - 1).bit_length())
    mp_ref[...] = m_past
    mn_ref[...] = jnp.where(causal, m_new, 0.0)


def dsa_sample_select(proj, pool_kidx_t, pt_flat, layer, *, row0, nb, ts, n_pages, pp=32, rows_per_step=128):
    nj = n_pages // pp
    tk = pp * PAGE_SIZE
    past = n_pages * PAGE_SIZE
    n_sel = min(TOPK_MAX, (past + ts) // 4)
    rb0 = row0 // ts
    rows = nb * ts
    rstep = min(rows_per_step, rows)
    sp, sn = _dsa_sample_scores(proj, pool_kidx_t, pt_flat, layer, rb0=rb0, nb=nb, ts=ts, n_pages=n_pages, pp=pp)
    mp, mn = pl.pallas_call(
        functools.partial(_dsa_mask_body, ts=ts, n_sel=n_sel),
        grid=(rows // rstep,),
        in_specs=[pl.BlockSpec((rstep, past), lambda i: (i, 0)), pl.BlockSpec((rstep, LANES), lambda i: (i, 0))],
        out_specs=[pl.BlockSpec((rstep, past), lambda i: (i, 0)), pl.BlockSpec((rstep, LANES), lambda i: (i, 0))],
        out_shape=[jax.ShapeDtypeStruct((rows, past), F32), jax.ShapeDtypeStruct((rows, LANES), F32)],
        compiler_params=pltpu.CompilerParams(dimension_semantics=("parallel",), vmem_limit_bytes=VMEM_LIMIT),
        name="dsa_sample_mask",
    )(sp.reshape(rows, past), sn.reshape(rows, LANES))
    return mp.reshape(nb, ts, past), mn.reshape(nb, ts, LANES)


def _dsa_sample_scores(proj, pool_kidx_t, pt_flat, layer, *, rb0, nb, ts, n_pages, pp):
    nj = n_pages // pp
    tk = pp * PAGE_SIZE
    past = n_pages * PAGE_SIZE

    def page_spec(r):
        return pl.BlockSpec((None, None, IDX_DIM, PAGE_SIZE),
                            lambda b, j, pt: (layer, pt[b * n_pages + j * pp + r], 0, 0))

    grid_spec = pltpu.PrefetchScalarGridSpec(
        num_scalar_prefetch=1,
        grid=(nb, nj),
        in_specs=[
            pl.BlockSpec((ts, IDX_HEADS * IDX_DIM), lambda b, j, pt: (rb0 + b, EV_QI // (IDX_HEADS * IDX_DIM))),
            pl.BlockSpec((ts, LANES), lambda b, j, pt: (rb0 + b, EV_WI // LANES)),
            pl.BlockSpec((ts, LANES), lambda b, j, pt: (rb0 + b, EV_KIKI // LANES)),
        ] + [page_spec(r) for r in range(pp)],
        out_specs=[pl.BlockSpec((None, ts, tk), lambda b, j, pt: (b, 0, j)),
                   pl.BlockSpec((None, ts, LANES), lambda b, j, pt: (b, 0, 0))],
    )
    return pl.pallas_call(
        functools.partial(_dsa_score_body, pp=pp, ts=ts),
        grid_spec=grid_spec,
        out_shape=[jax.ShapeDtypeStruct((nb, ts, past), F32), jax.ShapeDtypeStruct((nb, ts, LANES), F32)],
        compiler_params=pltpu.CompilerParams(
            dimension_semantics=("parallel", "arbitrary"), vmem_limit_bytes=VMEM_LIMIT),
        name="dsa_sample_scores",
    )(pt_flat, proj, proj, proj, *([pool_kidx_t] * pp))


def _dsa_att_body(pt_ref, q_ref, knew_ref, vnew_ref, qn_ref, mask_ref, mnew_ref, bias_ref, bnew_ref, *rest,
                  pp, sub, nj, ts):
    kpages, vpages = rest[:pp], rest[pp:2 * pp]
    o_ref, qs_ref, m_ref, l_ref, acc_ref = rest[2 * pp:]
    j = pl.program_id(1)
    gsz = A_HEADS // A_KV_HEADS
    rows_g = gsz * ts

    @pl.when(j == 0)
    def _():
        qs = []
        for h in range(A_HEADS):
            q = q_ref[:, LANES * h:LANES * (h + 1)]
            qs.append(q * lax.rsqrt(jnp.mean(q * q, axis=-1, keepdims=True) + EPS) * qn_ref[...])
        qs_ref[...] = jnp.concatenate(qs, axis=0).astype(BF16)
        m_ref[...] = jnp.full(m_ref.shape, MASK_NEG, F32)
        l_ref[...] = jnp.zeros(l_ref.shape, F32)
        acc_ref[...] = jnp.zeros(acc_ref.shape, F32)

    def partial(k_tiles, v_tiles, mask, bias):
        qs = qs_ref[...]
        lg = jnp.concatenate(
            [_dot_nt(qs[rows_g * g:rows_g * (g + 1)], k_tiles[g])
             for g in range(A_KV_HEADS)], axis=0) * (HEAD_DIM ** -0.5) + bias
        sel = jnp.concatenate([mask] * A_HEADS, axis=0) > 0.5
        lg = jnp.where(sel, lg, MASK_NEG)
        m = jnp.max(lg, axis=-1, keepdims=True)
        p = jnp.where(sel, jnp.exp(lg - m), 0.0)
        pb = p.astype(BF16)
        pv = jnp.concatenate(
            [jnp.dot(pb[rows_g * g:rows_g * (g + 1)], v_tiles[g], preferred_element_type=F32)
             for g in range(A_KV_HEADS)], axis=0)
        return m, jnp.sum(p, axis=-1, keepdims=True), pv

    def merge(parts):
        m_old = m_ref[...]
        m_new = m_old
        for m, _, _ in parts:
            m_new = jnp.maximum(m_new, m)
        alpha = jnp.exp(m_old - m_new)
        l_new = alpha * l_ref[...]
        acc = alpha * acc_ref[...]
        for m, l, pv in parts:
            a = jnp.exp(m - m_new)
            l_new = l_new + a * l
            acc = acc + a * pv
        m_ref[...] = m_new
        l_ref[...] = l_new
        acc_ref[...] = acc

    def head_rows(pages, g):
        return jnp.concatenate([p[pl.ds(g, PAGE_SIZE, stride=A_KV_HEADS), :] for p in pages], axis=0).astype(BF16)

    bias_all = bias_ref[...].reshape(A_HEADS * ts, pp * PAGE_SIZE)
    mask_all = mask_ref[...]
    parts = []
    for s0 in range(0, pp, sub):
        kp, vp = kpages[s0:s0 + sub], vpages[s0:s0 + sub]
        cols = slice(s0 * PAGE_SIZE, (s0 + sub) * PAGE_SIZE)
        parts.append(partial([head_rows(kp, g) for g in range(A_KV_HEADS)],
                             [head_rows(vp, g) for g in range(A_KV_HEADS)], mask_all[:, cols], bias_all[:, cols]))
    merge(parts)

    @pl.when(j == nj - 1)
    def _():
        pad = jnp.zeros((LANES - ts, A_KV_HEADS * HEAD_DIM), F32)
        kn = jnp.concatenate([knew_ref[...], pad], axis=0).astype(BF16)
        vn = jnp.concatenate([vnew_ref[...], pad], axis=0).astype(BF16)
        merge([partial([kn[:, LANES * g:LANES * (g + 1)] for g in range(A_KV_HEADS)],
                       [vn[:, LANES * g:LANES * (g + 1)] for g in range(A_KV_HEADS)],
                       mnew_ref[...], bnew_ref[...].reshape(A_HEADS * ts, LANES))])
        out = acc_ref[...] / l_ref[...]
        o_ref[...] = jnp.concatenate([out[ts * h:ts * (h + 1)] for h in range(A_HEADS)], axis=1)


def dsa_sample_attend(proj, ka32, mask_past, mask_new, bias_s, pool_k, pool_v, pt_flat, layer, qn, *, row0, nb, ts,
                      n_pages, pp=32, sub=8):
    nj = n_pages // pp
    tk = pp * PAGE_SIZE
    past = n_pages * PAGE_SIZE
    rb0 = row0 // ts
    wkv = A_KV_HEADS * HEAD_DIM
    wq = A_HEADS * HEAD_DIM

    def page_spec(r):
        return pl.BlockSpec((None, None, PAGE_SIZE * A_KV_HEADS, HEAD_DIM),
                            lambda b, j, pt: (layer, pt[b * n_pages + j * pp + r], 0, 0))

    grid_spec = pltpu.PrefetchScalarGridSpec(
        num_scalar_prefetch=1,
        grid=(nb, nj),
        in_specs=[
            pl.BlockSpec((ts, wq), lambda b, j, pt: (rb0 + b, 0)),
            pl.BlockSpec((ts, wkv), lambda b, j, pt: (rb0 + b, 0)),
            pl.BlockSpec((ts, wkv), lambda b, j, pt: (rb0 + b, EV_VA // wkv)),
            pl.BlockSpec((1, HEAD_DIM), lambda b, j, pt: (0, 0)),
            pl.BlockSpec((None, ts, tk), lambda b, j, pt: (b, 0, j)),
            pl.BlockSpec((None, ts, LANES), lambda b, j, pt: (b, 0, 0)),
            pl.BlockSpec((A_HEADS, ts, tk), lambda b, j, pt: (0, 0, j)),
            pl.BlockSpec((A_HEADS, ts, LANES), lambda b, j, pt: (0, 0, past // LANES)),
        ] + [page_spec(r) for r in range(pp)] * 2,
        out_specs=pl.BlockSpec((ts, wq), lambda b, j, pt: (b, 0)),
        scratch_shapes=[pltpu.VMEM((A_HEADS * ts, HEAD_DIM), BF16), pltpu.VMEM((A_HEADS * ts, 1), F32),
                        pltpu.VMEM((A_HEADS * ts, 1), F32), pltpu.VMEM((A_HEADS * ts, HEAD_DIM), F32)],
    )
    return pl.pallas_call(
        functools.partial(_dsa_att_body, pp=pp, sub=min(sub, pp), nj=nj, ts=ts),
        grid_spec=grid_spec,
        out_shape=jax.ShapeDtypeStruct((nb * ts, wq), F32),
        compiler_params=pltpu.CompilerParams(
            dimension_semantics=("parallel", "arbitrary"), vmem_limit_bytes=VMEM_LIMIT),
        name="dsa_sample_attend",
    )(pt_flat, proj, ka32, proj, qn.reshape(1, HEAD_DIM), mask_past, mask_new, bias_s, bias_s,
      *([pool_k] * pp), *([pool_v] * pp))


def _mla_sample_body(pt_ref, qlat_ref, qpe_ref, cnew_ref, rnew_ref, wkt_ref, wv_ref, *rest, pp, sub, nj, ts):
    cpages, rpages = rest[:pp], rest[pp:2 * pp]
    o_ref, m_ref, l_ref, acc_ref = rest[2 * pp:]
    j = pl.program_id(1)
    dq = NOPE_DIM + ROPE_DIM
    nrow = D_HEADS * ts

    @pl.when(j == 0)
    def _():
        m_ref[...] = jnp.full(m_ref.shape, MASK_NEG, F32)
        l_ref[...] = jnp.zeros(l_ref.shape, F32)
        acc_ref[...] = jnp.zeros(acc_ref.shape, F32)

    def update(c, rt, sel):
        n = c.shape[0]
        cb = c.astype(BF16)
        both = _dot_nt(jnp.concatenate([wkt_ref[...], qlat_ref[...]], axis=0), cb)
        kt = both[:D_HEADS * NOPE_DIM]
        ss = jnp.sum((kt * kt).reshape(D_HEADS, NOPE_DIM, n), axis=1)
        rss = jnp.sum(rt * rt, axis=0, keepdims=True)
        rinv = lax.rsqrt((ss + rss) * (1.0 / dq) + EPS) * (dq ** -0.5)
        lg = both[D_HEADS * NOPE_DIM:] + jnp.dot(qpe_ref[...], rt.astype(BF16),
                                                 preferred_element_type=F32)
        lg = (lg.reshape(D_HEADS, ts, n) * rinv[:, None, :]).reshape(nrow, n)
        if sel is not None:
            lg = jnp.where(sel, lg, MASK_NEG)
        m = jnp.max(lg, axis=-1, keepdims=True)
        p = jnp.exp(lg - m)
        if sel is not None:
            p = jnp.where(sel, p, 0.0)
        return m, jnp.sum(p, axis=-1, keepdims=True), jnp.dot(p.astype(BF16), cb, preferred_element_type=F32)

    def merge(parts):
        m_old = m_ref[...]
        m_new = m_old
        for m, _, _ in parts:
            m_new = jnp.maximum(m_new, m)
        alpha = jnp.exp(m_old - m_new)
        l_new = alpha * l_ref[...]
        acc = alpha * acc_ref[...]
        for m, l, pv in parts:
            a = jnp.exp(m - m_new)
            l_new = l_new + a * l
            acc = acc + a * pv
        m_ref[...] = m_new
        l_ref[...] = l_new
        acc_ref[...] = acc

    parts = []
    for s0 in range(0, pp, sub):
        c = jnp.concatenate([p[...] for p in cpages[s0:s0 + sub]], axis=0)
        rt = jnp.concatenate([p[...] for p in rpages[s0:s0 + sub]], axis=1)
        parts.append(update(c, rt, None))
    merge(parts)

    @pl.when(j == nj - 1)
    def _():
        lane = lax.broadcasted_iota(jnp.int32, (D_HEADS, ts, LANES), 2).reshape(nrow, LANES)
        row_t = lax.broadcasted_iota(jnp.int32, (D_HEADS, ts, LANES), 1).reshape(nrow, LANES)
        merge([update(cnew_ref[...], rnew_ref[...], lane <= row_t)])
        lat = (acc_ref[...] / l_ref[...]).astype(BF16)
        o_ref[...] = jnp.concatenate(
            [jnp.dot(lat[ts * h:ts * (h + 1)], wv_ref[h], preferred_element_type=F32) for h in range(D_HEADS)],
            axis=1)


def mla_sample_attend(qlat, qpe, cnew, rnew_t, wkt, wv, pool_ckv, pool_kpe_t, pt_flat, layer, *, nb, ts, n_pages,
                      pp=32, sub=8):
    nj = n_pages // pp
    nrow = D_HEADS * ts

    def cspec(r):
        return pl.BlockSpec((None, None, PAGE_SIZE, KV_LORA), lambda b, j, pt: (layer, pt[b * n_pages + j * pp + r], 0, 0))

    def rspec(r):
        return pl.BlockSpec((None, None, ROPE_DIM, PAGE_SIZE), lambda b, j, pt: (layer, pt[b * n_pages + j * pp + r], 0, 0))

    grid_spec = pltpu.PrefetchScalarGridSpec(
        num_scalar_prefetch=1,
        grid=(nb, nj),
        in_specs=[
            pl.BlockSpec((None, nrow, KV_LORA), lambda b, j, pt: (b, 0, 0)),
            pl.BlockSpec((None, nrow, ROPE_DIM), lambda b, j, pt: (b, 0, 0)),
            pl.BlockSpec((None, LANES, KV_LORA), lambda b, j, pt: (b, 0, 0)),
            pl.BlockSpec((None, ROPE_DIM, LANES), lambda b, j, pt: (b, 0, 0)),
            pl.BlockSpec((D_HEADS * NOPE_DIM, KV_LORA), lambda b, j, pt: (0, 0)),
            pl.BlockSpec((D_HEADS, KV_LORA, D_VDIM), lambda b, j, pt: (0, 0, 0)),
        ] + [cspec(r) for r in range(pp)] + [rspec(r) for r in range(pp)],
        out_specs=pl.BlockSpec((ts, D_HEADS * D_VDIM), lambda b, j, pt: (b, 0)),
        scratch_shapes=[pltpu.VMEM((nrow, 1), F32), pltpu.VMEM((nrow, 1), F32), pltpu.VMEM((nrow, KV_LORA), F32)],
    )
    return pl.pallas_call(
        functools.partial(_mla_sample_body, pp=pp, sub=min(sub, pp), nj=nj, ts=ts),
        grid_spec=grid_spec,
        out_shape=jax.ShapeDtypeStruct((nb * ts, D_HEADS * D_VDIM), F32),
        compiler_params=pltpu.CompilerParams(
            dimension_semantics=("parallel", "arbitrary"), vmem_limit_bytes=VMEM_LIMIT),
        name="mla_sample",
    )(pt_flat, qlat, qpe, cnew, rnew_t, wkt, wv, *([pool_ckv] * pp), *([pool_kpe_t] * pp))


def _mem_attn_body(q_ref, k_ref, v_ref, qn_ref, o_ref, *, nb, tq, n_mem, head_rows):
    def head(ref, b, h):
        if head_rows:
            return ref[b, pl.ds(h, n_mem, stride=MEM_HEADS), :].astype(BF16)
        return ref[b, :, LANES * h:LANES * (h + 1)].astype(BF16)

    for b in range(nb):
        outs = []
        for h in range(MEM_HEADS):
            q = q_ref[tq * b:tq * (b + 1), LANES * h:LANES * (h + 1)]
            q = q * lax.rsqrt(jnp.mean(q * q, axis=-1, keepdims=True) + EPS) * qn_ref[...]
            lg = _dot_nt(q.astype(BF16), head(k_ref, b, h)) * (HEAD_DIM ** -0.5)
            p = _softmax_rows(lg)
            outs.append(jnp.dot(p.astype(BF16), head(v_ref, b, h), preferred_element_type=F32))
        o_ref[tq * b:tq * (b + 1), :] = jnp.concatenate(outs, axis=1)


def mem_attend(q, mk, mv, qn, *, row0, n_batch, tq, nb, n_mem, head_rows, k_index, v_index):
    w = MEM_HEADS * HEAD_DIM
    rb0 = row0 // (nb * tq)
    slab = (n_mem * MEM_HEADS, HEAD_DIM) if head_rows else (n_mem, w)
    kblock = (None,) * (mk.ndim - 3) + (nb,) + slab
    vblock = (None,) * (mv.ndim - 3) + (nb,) + slab
    return pl.pallas_call(
        functools.partial(_mem_attn_body, nb=nb, tq=tq, n_mem=n_mem, head_rows=head_rows),
        grid=(n_batch // nb,),
        in_specs=[pl.BlockSpec((nb * tq, w), lambda i: (rb0 + i, 0)),
                  pl.BlockSpec(kblock, k_index),
                  pl.BlockSpec(vblock, v_index),
                  pl.BlockSpec((1, HEAD_DIM), lambda i: (0, 0))],
        out_specs=pl.BlockSpec((nb * tq, w), lambda i: (i, 0)),
        out_shape=jax.ShapeDtypeStruct((n_batch * tq, w), F32),
        compiler_params=pltpu.CompilerParams(dimension_semantics=("parallel",), vmem_limit_bytes=VMEM_LIMIT),
        name="mem_attend",
    )(q, mk, mv, qn.reshape(1, HEAD_DIM))


def _causal_attn_body(q_ref, k_ref, v_ref, o_ref, *, qb, t, scale):
    i = pl.program_id(2)
    lg = _dot_nt(q_ref[...], k_ref[...]) * scale
    q_pos = i * qb + lax.broadcasted_iota(jnp.int32, (qb, t), 0)
    k_pos = lax.broadcasted_iota(jnp.int32, (qb, t), 1)
    p = _softmax_rows(jnp.where(k_pos <= q_pos, lg, MASK_NEG))
    o_ref[...] = jnp.dot(p.astype(BF16), v_ref[...], preferred_element_type=F32)


def causal_attend(q, k, v, *, nb, t, heads, dv, scale, qb=512):
    nq = t // qb
    dqk = q.shape[1] // heads
    return pl.pallas_call(
        functools.partial(_causal_attn_body, qb=qb, t=t, scale=scale),
        grid=(nb, heads, nq),
        in_specs=[pl.BlockSpec((qb, dqk), lambda b, h, i: (b * nq + i, h)),
                  pl.BlockSpec((t, dqk), lambda b, h, i: (b, h)),
                  pl.BlockSpec((t, dv), lambda b, h, i: (b, h))],
        out_specs=pl.BlockSpec((qb, dv), lambda b, h, i: (b * nq + i, h)),
        out_shape=jax.ShapeDtypeStruct((nb * t, heads * dv), F32),
        compiler_params=pltpu.CompilerParams(
            dimension_semantics=("parallel", "parallel", "arbitrary"), vmem_limit_bytes=VMEM_LIMIT),
        name="causal_attend",
    )(q, k, v)


def _cumsum_rows(x):
    c = x.shape[0]
    row = lax.broadcasted_iota(jnp.int32, x.shape, 0)
    sh = 1
    while sh < c:
        x = x + jnp.where(row >= sh, pltpu.roll(x, sh, axis=0), 0.0)
        sh *= 2
    return x


def _log_sigmoid(x):
    return jnp.minimum(x, 0.0) - jnp.log1p(jnp.exp(-jnp.abs(x)))


def _sigmoid(x):
    return 1.0 / (1.0 + jnp.exp(-x))


def _hgrn2_run(q_ref, f_ref, v_ref, g_ref, lb_ref, on, st_ref, o_ref, *, nh, t, c):
    row = lax.broadcasted_iota(jnp.int32, (c, 1), 0)

    def step(i, carry):
        r0 = pl.multiple_of(i * c, c)
        for h in range(nh):
            sl = slice(LANES * h, LANES * (h + 1))
            lb = lb_ref[h]
            fpre = f_ref[pl.ds(r0, c), sl]
            a = jnp.log(jnp.maximum(lb, LB_FLOOR))
            b = jnp.log1p(-lb) + _log_sigmoid(fpre)
            lf = jnp.maximum(a, b) + jnp.log1p(jnp.exp(-jnp.abs(a - b)))
            kk = (1.0 - lb) * _sigmoid(-fpre)
            qpre = q_ref[pl.ds(r0, c), sl]
            qv = qpre * _sigmoid(qpre)
            vv = v_ref[pl.ds(r0, c), sl]
            cb = _cumsum_rows(lf)
            st = st_ref[h]
            o = _dot_nt((qv * jnp.exp(cb)).astype(BF16), st.astype(BF16))
            for s in range(c):
                dec = jnp.exp(jnp.minimum(cb - cb[s:s + 1, :], 0.0))
                col = jnp.sum(qv * dec * kk[s:s + 1, :], axis=-1, keepdims=True)
                o = o + jnp.where(row >= s, col, 0.0) * vv[s:s + 1, :]
            c_last = cb[c - 1:c, :]
            kd = (kk * jnp.exp(c_last - cb)).astype(BF16)
            st_ref[h] = st * jnp.exp(c_last) + lax.dot_general(vv.astype(BF16), kd, (((0,), (0,)), ((), ())),
                                                               preferred_element_type=F32)
            gpre = g_ref[pl.ds(r0, c), sl]
            o = o * lax.rsqrt(jnp.mean(o * o, axis=-1, keepdims=True) + EPS) * on * (gpre * _sigmoid(gpre))
            o_ref[pl.ds(r0, c), sl] = o
        return carry

    lax.fori_loop(0, t // c, step, 0)


def _hgrn2_prompt_body(q_ref, f_ref, v_ref, g_ref, lb_ref, on_ref, o_ref, s_ref, st_ref, *, nh, t, c):
    st_ref[...] = jnp.zeros(st_ref.shape, F32)
    _hgrn2_run(q_ref, f_ref, v_ref, g_ref, lb_ref, on_ref[...], st_ref, o_ref, nh=nh, t=t, c=c)
    for h in range(nh):
        s_ref[h] = st_ref[h].T


def hgrn2_prompt(proj, lb, on, *, nb, t, nh=4):
    c = math.gcd(16, t)
    w = nh * LANES
    col = lambda c0: (lambda b, j: (b, c0 // w + j))
    return pl.pallas_call(
        functools.partial(_hgrn2_prompt_body, nh=nh, t=t, c=c),
        grid=(nb, B_HEADS // nh),
        in_specs=[pl.BlockSpec((t, w), col(EV_QB)), pl.BlockSpec((t, w), col(EV_FB)),
                  pl.BlockSpec((t, w), col(EV_IB)), pl.BlockSpec((t, w), col(EV_GB)),
                  pl.BlockSpec((nh, 1, B_DK), lambda b, j: (j, 0, 0)),
                  pl.BlockSpec((1, B_DV), lambda b, j: (0, 0))],
        out_specs=[pl.BlockSpec((t, w), lambda b, j: (b, j)),
                   pl.BlockSpec((None, nh, B_DK, B_DV), lambda b, j: (b, j, 0, 0))],
        out_shape=[jax.ShapeDtypeStruct((nb * t, B_HEADS * B_DV), F32),
                   jax.ShapeDtypeStruct((nb, B_HEADS, B_DK, B_DV), F32)],
        scratch_shapes=[pltpu.VMEM((nh, B_DV, B_DK), F32)],
        compiler_params=pltpu.CompilerParams(
            dimension_semantics=("parallel", "parallel"), vmem_limit_bytes=VMEM_LIMIT),
        name="hgrn2_prompt",
    )(proj, proj, proj, proj, lb.reshape(B_HEADS, 1, B_DK), on.reshape(1, B_DV))


def _hgrn2_sample_body(q_ref, f_ref, v_ref, g_ref, lb_ref, on_ref, s0_ref, o_ref, s_ref, st_ref, *, t):
    for h in range(B_HEADS):
        st_ref[h] = s0_ref[h].T
    _hgrn2_run(q_ref, f_ref, v_ref, g_ref, lb_ref, on_ref[...], st_ref, o_ref, nh=B_HEADS, t=t, c=t)
    for h in range(B_HEADS):
        s_ref[h] = st_ref[h].T


def hgrn2_sample(proj, lb, on, s0, *, row0, nb, t):
    rb0 = row0 // t
    w = B_HEADS * LANES
    col = lambda c0: (lambda b: (rb0 + b, c0 // w))
    return pl.pallas_call(
        functools.partial(_hgrn2_sample_body, t=t),
        grid=(nb,),
        in_specs=[pl.BlockSpec((t, w), col(EV_QB)), pl.BlockSpec((t, w), col(EV_FB)),
                  pl.BlockSpec((t, w), col(EV_IB)), pl.BlockSpec((t, w), col(EV_GB)),
                  pl.BlockSpec((B_HEADS, 1, B_DK), lambda b: (0, 0, 0)),
                  pl.BlockSpec((1, B_DV), lambda b: (0, 0)),
                  pl.BlockSpec((None, B_HEADS, B_DK, B_DV), lambda b: (b, 0, 0, 0))],
        out_specs=[pl.BlockSpec((t, w), lambda b: (b, 0)),
                   pl.BlockSpec((None, B_HEADS, B_DK, B_DV), lambda b: (b, 0, 0, 0))],
        out_shape=[jax.ShapeDtypeStruct((nb * t, w), F32),
                   jax.ShapeDtypeStruct((nb, B_HEADS, B_DK, B_DV), F32)],
        scratch_shapes=[pltpu.VMEM((B_HEADS, B_DV, B_DK), F32)],
        compiler_params=pltpu.CompilerParams(dimension_semantics=("parallel",), vmem_limit_bytes=VMEM_LIMIT),
        name="hgrn2_sample",
    )(proj, proj, proj, proj, lb.reshape(B_HEADS, 1, B_DK), on.reshape(1, B_DV), s0)


def _mlstm_chunk(qm, km_h, v, ic_row, ic_col, cb_col, cb_row, ct, n, m, c):
    r = lax.broadcasted_iota(jnp.int32, (c, c), 0)
    s = lax.broadcasted_iota(jnp.int32, (c, c), 1)
    dmat = jnp.where(r >= s, cb_col - cb_row + ic_row, MASK_NEG)
    m_state = cb_col + m
    m_t = jnp.maximum(m_state, jnp.max(dmat, axis=-1, keepdims=True))
    w = jnp.exp(dmat - m_t)
    w0 = jnp.exp(m_state - m_t)
    qb16 = qm.astype(BF16)
    qk = _dot_nt(qb16, km_h.astype(BF16)) * w
    num = (jnp.dot(qk.astype(BF16), v.astype(BF16), preferred_element_type=F32)
           + w0 * jnp.dot(qb16, ct.astype(BF16), preferred_element_type=F32))
    den = jnp.sum(qk, axis=-1, keepdims=True) + w0 * jnp.sum(qm * n, axis=-1, keepdims=True)
    hc = num / jnp.maximum(jnp.abs(den), jnp.exp(-m_t))
    m_last = m_t[c - 1:c, :]
    cb_last = cb_col[c - 1:c, :]
    ws = jnp.exp(cb_last - cb_col + ic_col - m_last)
    fs = jnp.exp(cb_last + m - m_last)
    ct = fs * ct + lax.dot_general(km_h.astype(BF16), (v * ws).astype(BF16), (((0,), (0,)), ((), ())),
                                   preferred_element_type=F32)
    n = fs * n + jnp.sum(ws * km_h, axis=0, keepdims=True)
    return hc, ct, n, m_last


def _mlstm_seq(q_ref, k_ref, v_ref, og_ref, g_ref, gb_ref, on, npairs, states, o_ref, *, t, c):
    lane = lax.broadcasted_iota(jnp.int32, (c, LANES), 1)

    def step(i, carry):
        r0 = pl.multiple_of(i * c, c)
        new = []
        for p in range(npairs):
            psl = slice(LANES * p, LANES * (p + 1))
            g = g_ref[pl.ds(r0, c), psl] + gb_ref[p]
            cb = _cumsum_rows(_log_sigmoid(g))
            gt = g.T
            cbt = cb.T
            q = q_ref[pl.ds(r0, c), psl]
            k = k_ref[pl.ds(r0, c), psl] * (C_DK ** -0.5)
            for half in range(2):
                ct, n, m = carry[2 * p + half]
                keep = (lane < C_DK) if half == 0 else (lane >= C_DK)
                vsl = slice(C_DV * (2 * p + half), C_DV * (2 * p + half + 1))
                hc, ct, n, m = _mlstm_chunk(
                    jnp.where(keep, q, 0.0), jnp.where(keep, k, 0.0), v_ref[pl.ds(r0, c), vsl],
                    gt[half:half + 1, :], g[:, half:half + 1], cb[:, 2 + half:3 + half], cbt[2 + half:3 + half, :],
                    ct, n, m, c)
                og = og_ref[pl.ds(r0, c), vsl]
                hc = hc * lax.rsqrt(jnp.mean(hc * hc, axis=-1, keepdims=True) + EPS) * on * _sigmoid(og)
                o_ref[pl.ds(r0, c), vsl] = hc
                new.append((ct, n, m))
        return tuple(new)

    return lax.fori_loop(0, t // c, step, tuple(states))


def _mlstm_body(*refs, t, c, npairs, has_state):
    q_ref, k_ref, v_ref, og_ref, g_ref, gb_ref, on_ref = refs[:7]
    refs = refs[7:]
    if has_state:
        c0_ref, n0_ref, m0_ref = refs[:3]
        refs = refs[3:]
    o_ref, c_ref, n_ref, m_ref = refs
    states = []
    for h in range(2 * npairs):
        half = h % 2
        if has_state:
            ct_h = c0_ref[h]
            z = jnp.zeros((C_DK, C_DV), F32)
            ct = jnp.concatenate([ct_h, z] if half == 0 else [z, ct_h], axis=0)
            n_h = n0_ref[h:h + 1, :]
            zn = jnp.zeros((1, C_DK), F32)
            n = jnp.concatenate([n_h, zn] if half == 0 else [zn, n_h], axis=1)
            m = m0_ref[:, h:h + 1]
        else:
            ct, n, m = jnp.zeros((2 * C_DK, C_DV), F32), jnp.zeros((1, 2 * C_DK), F32), jnp.zeros((1, 1), F32)
        states.append((ct, n, m))
    out = _mlstm_seq(q_ref, k_ref, v_ref, og_ref, g_ref, gb_ref, on_ref[...], npairs, states, o_ref, t=t, c=c)
    for h, (ct, n, m) in enumerate(out):
        rows = slice(C_DK * (h % 2), C_DK * (h % 2 + 1))
        c_ref[h] = ct[rows, :]
        n_ref[h] = n[:, rows]
        m_ref[h] = jnp.broadcast_to(m, (1, LANES))


def mlstm(proj, gb, on, state, *, row0, nb, t, npairs):
    c = math.gcd(CHUNK, t)
    rb0 = row0 // t
    npg = (C_HEADS // 2) // npairs
    hp = 2 * npairs
    qw, vw = npairs * LANES, npairs * 2 * C_DV
    in_specs = [
        pl.BlockSpec((t, qw), lambda b, j: (rb0 + b, OD_QC // qw + j)),
        pl.BlockSpec((t, qw), lambda b, j: (rb0 + b, OD_KC // qw + j)),
        pl.BlockSpec((t, vw), lambda b, j: (rb0 + b, OD_VC // vw + j)),
        pl.BlockSpec((t, vw), lambda b, j: (rb0 + b, OD_OC // vw + j)),
        pl.BlockSpec((t, qw), lambda b, j: (rb0 + b, OD_G // qw + j)),
        pl.BlockSpec((npairs, 1, LANES), lambda b, j: (j, 0, 0)),
        pl.BlockSpec((1, C_DV), lambda b, j: (0, 0)),
    ]
    args = [proj, proj, proj, proj, proj, gb, on.reshape(1, C_DV)]
    if state is not None:
        in_specs += [pl.BlockSpec((None, hp, C_DK, C_DV), lambda b, j: (b, j, 0, 0)),
                     pl.BlockSpec((None, hp, C_DK), lambda b, j: (b, j, 0)),
                     pl.BlockSpec((None, 1, hp), lambda b, j: (b, 0, j))]
        args += list(state)
    return pl.pallas_call(
        functools.partial(_mlstm_body, t=t, c=c, npairs=npairs, has_state=state is not None),
        grid=(nb, npg),
        in_specs=in_specs,
        out_specs=[pl.BlockSpec((t, vw), lambda b, j: (b, j)),
                   pl.BlockSpec((None, hp, C_DK, C_DV), lambda b, j: (b, j, 0, 0)),
                   pl.BlockSpec((None, hp, 1, C_DK), lambda b, j: (b, j, 0, 0)),
                   pl.BlockSpec((None, hp, 1, LANES), lambda b, j: (b, j, 0, 0))],
        out_shape=[jax.ShapeDtypeStruct((nb * t, C_HEADS * C_DV), F32),
                   jax.ShapeDtypeStruct((nb, C_HEADS, C_DK, C_DV), F32),
                   jax.ShapeDtypeStruct((nb, C_HEADS, 1, C_DK), F32),
                   jax.ShapeDtypeStruct((nb, C_HEADS, 1, LANES), F32)],
        compiler_params=pltpu.CompilerParams(
            dimension_semantics=("parallel", "parallel"), vmem_limit_bytes=VMEM_LIMIT),
        name="mlstm",
    )(*args)


def rms_norm(x, g):
    xf = x.astype(F32)
    y = xf * lax.rsqrt(jnp.mean(xf * xf, axis=-1, keepdims=True) + EPS)
    return (y * g.astype(F32)).astype(x.dtype)


def split_cols(h, sizes):
    cuts = [int(c) for c in np.cumsum(sizes)[:-1]]
    return jnp.split(h, cuts, axis=-1)


def to_blocks(x, nb):
    b, t = x.shape[:2]
    return jnp.swapaxes(x.reshape((b, nb, t // nb) + x.shape[2:]), 0, 1)


def from_blocks(x):
    x = jnp.swapaxes(x, 0, 1)
    return x.reshape((x.shape[0], x.shape[1] * x.shape[2]) + x.shape[3:])


def rel_bucket(rel):
    n = jnp.maximum(rel, 0)
    max_exact = REL_BUCKETS // 2
    nf = jnp.maximum(n, 1).astype(F32)
    large = max_exact + (jnp.log(nf / max_exact) / math.log(REL_MAX_DIST / max_exact)
                         * (REL_BUCKETS - max_exact)).astype(jnp.int32)
    return jnp.where(n < max_exact, n, jnp.minimum(large, REL_BUCKETS - 1))


def rope(x, pos):
    half = ROPE_DIM // 2
    inv = ROPE_THETA ** (-jnp.arange(half, dtype=F32) / half)
    ang = pos.astype(F32)[:, None] * inv[None, :]
    shp = (1, pos.shape[0]) + (1,) * (x.ndim - 3) + (half,)
    cos, sin = jnp.cos(ang).reshape(shp), jnp.sin(ang).reshape(shp)
    xf = x.astype(F32)
    x1, x2 = xf[..., :half], xf[..., half:]
    return jnp.concatenate([x1 * cos - x2 * sin, x2 * cos + x1 * sin], axis=-1).astype(x.dtype)


def dsa_select(q_idx, w_idx, k_idx, q_pos, k_pos, n_sel):
    s = jnp.einsum('bthd,bsd->bths', q_idx.astype(F32), k_idx.astype(F32))
    score = jnp.einsum('bths,bth->bts', jax.nn.relu(s), w_idx.astype(F32))
    allowed = k_pos[None, None, :] <= q_pos[None, :, None]
    score = jnp.where(allowed, score, MASK_NEG)
    _, sel = lax.top_k(score, n_sel)
    valid = sel <= q_pos[None, :, None]
    return sel, valid


def dsa_attend(q, kg, vg, q_pos, sel, valid, rel_bias):
    b, t, h, dh = q.shape
    g = h // A_KV_HEADS
    n_sel = sel.shape[-1]
    qg = q.reshape(b, t, A_KV_HEADS, g, dh)
    lg = jnp.einsum('btkgd,btskd->btkgs', qg, kg).astype(F32) * (dh ** -0.5)
    bias = rel_bias.astype(F32)[rel_bucket(q_pos[None, :, None] - sel)]
    lg = lg + jnp.moveaxis(bias, -1, 2).reshape(b, t, A_KV_HEADS, g, n_sel)
    lg = jnp.where(valid[:, :, None, None, :], lg, MASK_NEG)
    pr = jax.nn.softmax(lg, axis=-1).astype(vg.dtype)
    return jnp.einsum('btkgs,btskd->btkgd', pr, vg).reshape(b, t, h, dh)


def dsa_prompt(q, k, v, q_idx, w_idx, k_idx, rel_bias):
    b, t = q.shape[:2]
    n_sel = min(TOPK_MAX, t // 4)
    qb = math.gcd(Q_BLOCK, t)
    nb = t // qb
    k_pos = jnp.arange(t)
    bidx = jnp.arange(b)[:, None, None]

    def block(args):
        q_b, qi_b, wi_b, qp = args
        sel, valid = dsa_select(qi_b, wi_b, k_idx, qp, k_pos, n_sel)
        return dsa_attend(q_b, k[bidx, sel], v[bidx, sel], qp, sel, valid, rel_bias)

    out = lax.map(block, (to_blocks(q, nb), to_blocks(q_idx, nb), to_blocks(w_idx, nb), k_pos.reshape(nb, qb)))
    return from_blocks(out)


def gather_paged_rows(pool, layer, new_rows, page_table, sel, past):
    b = sel.shape[0]
    in_past = sel < past
    ps = jnp.minimum(sel, past - 1)
    phys = jnp.take_along_axis(page_table, (ps // PAGE_SIZE).reshape(b, -1), axis=1).reshape(sel.shape)
    past_rows = pool[layer, phys, ps % PAGE_SIZE]
    new_idx = jnp.clip(sel - past, 0, new_rows.shape[1] - 1)
    cur_rows = new_rows[jnp.arange(b)[:, None, None], new_idx]
    cond = in_past.reshape(in_past.shape + (1,) * (new_rows.ndim - 2))
    return jnp.where(cond, past_rows, cur_rows)


def dsa_sample(q, k_new, v_new, q_idx, w_idx, kidx_new, pool_k, pool_v, pool_kidx, layer, page_table, rel_bias):
    b, t = q.shape[:2]
    past = page_table.shape[1] * PAGE_SIZE
    n_sel = min(TOPK_MAX, (past + t) // 4)
    kidx_past = pool_kidx[layer, page_table].reshape(b, past, IDX_DIM)
    kidx_all = jnp.concatenate([kidx_past, kidx_new.astype(kidx_past.dtype)], axis=1)
    q_pos = past + jnp.arange(t)
    sel, valid = dsa_select(q_idx, w_idx, kidx_all, q_pos, jnp.arange(past + t), n_sel)
    kg = gather_paged_rows(pool_k, layer, k_new, page_table, sel, past)
    vg = gather_paged_rows(pool_v, layer, v_new, page_table, sel, past)
    return dsa_attend(q, kg, vg, q_pos, sel, valid, rel_bias)


def gla_chunked(q, k, v, log_f, s0):
    b, t, h, dk = q.shape
    c = math.gcd(CHUNK, t)
    nc = t // c
    tri = jnp.tril(jnp.ones((c, c), dtype=bool))

    def step(s, inp):
        qc, kc, vc, lf = inp
        cb = jnp.cumsum(lf, axis=1)
        diff = jnp.where(tri[None, :, :, None, None], cb[:, :, None] - cb[:, None, :], MASK_NEG)
        attn = jnp.einsum('bthd,btshd->bhts', qc, jnp.exp(diff) * kc[:, None])
        o = jnp.einsum('bhts,bshv->bthv', attn, vc) + jnp.einsum('bthd,bhdv->bthv', qc * jnp.exp(cb), s)
        c_last = cb[:, -1]
        s = jnp.exp(c_last)[..., None] * s + jnp.einsum('bshd,bshv->bhdv', kc * jnp.exp(c_last[:, None] - cb), vc)
        return s, o

    xs = tuple(to_blocks(a.astype(F32), nc) for a in (q, k, v, log_f))
    s, o = lax.scan(step, s0.astype(F32), xs)
    return from_blocks(o).astype(v.dtype), s


def mlstm_chunked(q, k, v, i_pre, f_pre, c0, n0, m0):
    b, t, h, dk = q.shape
    c = math.gcd(CHUNK, t)
    nc = t // c
    tri = jnp.tril(jnp.ones((c, c), dtype=bool))
    log_f = jax.nn.log_sigmoid(f_pre.astype(F32))

    def step(carry, inp):
        cm, n, m = carry
        qc, kc, vc, ic, lf = inp
        cb = jnp.cumsum(lf, axis=1)
        dmat = jnp.where(tri[None, :, :, None], cb[:, :, None] - cb[:, None] + ic[:, None], MASK_NEG)
        m_state = cb + m[:, None]
        m_t = jnp.maximum(m_state, jnp.max(dmat, axis=2))
        w = jnp.exp(dmat - m_t[:, :, None])
        w0 = jnp.exp(m_state - m_t)
        qk = jnp.einsum('bthd,bshd->btsh', qc, kc) * w
        num = jnp.einsum('btsh,bshv->bthv', qk, vc) + w0[..., None] * jnp.einsum('bthd,bhvd->bthv', qc, cm)
        den = jnp.sum(qk, axis=2) + w0 * jnp.einsum('bthd,bhd->bth', qc, n)
        hc = num / jnp.maximum(jnp.abs(den), jnp.exp(-m_t))[..., None]
        m_last = m_t[:, -1]
        ws = jnp.exp(cb[:, -1:] - cb + ic - m_last[:, None])
        fs = jnp.exp(cb[:, -1] + m - m_last)
        cm = fs[..., None, None] * cm + jnp.einsum('bshv,bshd->bhvd', vc * ws[..., None], kc)
        n = fs[..., None] * n + jnp.einsum('bsh,bshd->bhd', ws, kc)
        return (cm, n, m_last), hc

    xs = tuple(to_blocks(a.astype(F32), nc) for a in (q, k, v, i_pre, log_f))
    (cm, n, m), hs = lax.scan(step, (c0.astype(F32), n0.astype(F32), m0.astype(F32)), xs)
    return from_blocks(hs).astype(v.dtype), (cm, n, m)


def mla_kv(ckv, kpe, w_kvb, kn):
    b, l, _ = ckv.shape
    kv = (ckv @ w_kvb).reshape(b, l, D_HEADS, NOPE_DIM + D_VDIM)
    k_nope, v = kv[..., :NOPE_DIM], kv[..., NOPE_DIM:]
    k_pe = jnp.broadcast_to(kpe[:, :, None, :], (b, l, D_HEADS, ROPE_DIM)).astype(k_nope.dtype)
    return rms_norm(jnp.concatenate([k_nope, k_pe], axis=-1), kn), v


def causal_attn(q, k, v, q_pos, k_pos):
    b, t, h, dq = q.shape
    scale = dq ** -0.5

    def attend(args):
        q_b, qp = args
        lg = jnp.einsum('bthd,bshd->bhts', q_b, k).astype(F32) * scale
        lg = jnp.where(k_pos[None, None, None, :] <= qp[None, None, :, None], lg, MASK_NEG)
        pr = jax.nn.softmax(lg, axis=-1).astype(v.dtype)
        return jnp.einsum('bhts,bshd->bthd', pr, v)

    qb = math.gcd(Q_BLOCK, t)
    nb = t // qb
    if nb == 1:
        return attend((q, q_pos))
    return from_blocks(lax.map(attend, (to_blocks(q, nb), q_pos.reshape(nb, qb))))


def mla_sample(qd, ckv_new, kpe_new, pool_ckv, pool_kpe, layer, page_table, w_kvb, kn):
    b, t = qd.shape[:2]
    past = page_table.shape[1] * PAGE_SIZE
    q_pos = past + jnp.arange(t)
    k_pos = jnp.arange(past + t)

    def one(args):
        q1, c1, r1, pt = args
        c_all = jnp.concatenate([pool_ckv[layer, pt].reshape(past, KV_LORA), c1.astype(pool_ckv.dtype)], axis=0)[None]
        r_all = jnp.concatenate([pool_kpe[layer, pt].reshape(past, ROPE_DIM), r1.astype(pool_kpe.dtype)], axis=0)[None]
        k, v = mla_kv(c_all, r_all, w_kvb, kn)
        return causal_attn(q1[None], k, v, q_pos, k_pos)[0]

    return lax.map(one, (qd, ckv_new, kpe_new, page_table))


def mem_attend_core(q, mk, mv, qn):
    b, t, _ = q.shape
    q = rms_norm(q.reshape(b, t, MEM_HEADS, HEAD_DIM), qn)
    lg = jnp.einsum('bthd,bshd->bhts', q, mk.astype(q.dtype)).astype(F32) * (HEAD_DIM ** -0.5)
    pr = jax.nn.softmax(lg, axis=-1).astype(q.dtype)
    o = jnp.einsum('bhts,bshd->bthd', pr, mv.astype(q.dtype))
    return o.reshape(b, t, MEM_HEADS * HEAD_DIM)


def hgrn2_mixer(proj, s0, b_on, lower_bound):
    b, t, _ = proj.shape
    qb, fb, ib, gb = (proj[..., c:c + B_HEADS * B_DK] for c in (EV_QB, EV_FB, EV_IB, EV_GB))
    lb = lower_bound.reshape(B_HEADS, B_DK)
    f_pre = fb.reshape(b, t, B_HEADS, B_DK).astype(F32)
    log_f = jnp.logaddexp(jnp.log(jnp.maximum(lb, LB_FLOOR)), jnp.log1p(-lb) + jax.nn.log_sigmoid(f_pre))
    k_b = (1.0 - lb) * jax.nn.sigmoid(-f_pre)
    q_b = jax.nn.silu(qb.reshape(b, t, B_HEADS, B_DK))
    ob, s_new = gla_chunked(q_b, k_b, ib.reshape(b, t, B_HEADS, B_DV), log_f, s0)
    ob = rms_norm(ob, b_on) * jax.nn.silu(gb.reshape(b, t, B_HEADS, B_DV))
    return ob.reshape(b, t, -1), s_new


def dsa_bias_tables(rel_bias, ts, past):
    table = rel_bias.astype(F32)[rel_bucket(jnp.arange(2 * LANES))]
    r = np.arange(LANES)[:, None]
    c = np.arange(LANES)[None, :]
    diag = table[np.clip(r - c, 0, 2 * LANES - 1)]
    prev = table[LANES + r - c]
    far = jnp.broadcast_to(table[2 * LANES - 1], (LANES, LANES, A_HEADS))
    tiles = jnp.moveaxis(jnp.stack([diag, prev, far]), -1, 0)
    n_far = max(past - 2 * LANES, 0)
    rel = past + np.arange(ts)[:, None] - np.arange(n_far, past + LANES)[None, :]
    sample = jnp.concatenate([jnp.broadcast_to(table[2 * LANES - 1], (ts, n_far, A_HEADS)),
                              table[np.clip(rel, 0, 2 * LANES - 1)]], axis=1)
    sample = jnp.moveaxis(sample, -1, 0)
    return tiles, sample


def rope_rows(x, pos):
    half = ROPE_DIM // 2
    inv = ROPE_THETA ** (-jnp.arange(half, dtype=F32) / half)
    ang = pos.astype(F32)[:, None] * inv[None, :]
    shp = (pos.shape[0],) + (1,) * (x.ndim - 2) + (half,)
    cos, sin = jnp.cos(ang).reshape(shp), jnp.sin(ang).reshape(shp)
    x1, x2 = x[..., :half], x[..., half:]
    return jnp.concatenate([x1 * cos - x2 * sin, x2 * cos + x1 * sin], axis=-1)


def mla_sample_pallas(qd, ckv, kpe, st, o, kn, wb):
    b, t = qd.shape[:2]
    qn = (qd[..., :NOPE_DIM] * kn[:NOPE_DIM]).reshape(b * t, D_HEADS * NOPE_DIM)
    qlat = matmul(qn, wb['d_kt_blockdiag'][o])
    qlat = qlat.reshape(b, t, D_HEADS, KV_LORA).transpose(0, 2, 1, 3).reshape(b, D_HEADS * t, KV_LORA)
    qpe = (qd[..., NOPE_DIM:] * kn[NOPE_DIM:]).transpose(0, 2, 1, 3).reshape(b, D_HEADS * t, ROPE_DIM)
    cnew = jnp.pad(ckv, ((0, 0), (0, LANES - t), (0, 0)))
    rnew_t = jnp.swapaxes(jnp.pad(kpe, ((0, 0), (0, LANES - t), (0, 0))), 1, 2)
    od = mla_sample_attend(qlat.astype(BF16), qpe.astype(BF16), cnew, rnew_t, wb['d_kt'][o], wb['d_v'][o],
                           st['cache_d_ckv'], st['pool_kpe_t'], st['pt_flat'], o,
                           nb=b, ts=t, n_pages=st['page_table'].shape[1])
    return od.reshape(b, t, D_HEADS, D_VDIM)


def odd_mixer(proj, pos, mode, st, p, o, wb):
    b, t, _ = proj.shape
    qc, kc, vc, ic, fc, oc, qa, ckv, kpe = split_cols(proj, ODD_SPLITS)
    gate_b = p['c_gate_b'][o].astype(F32)
    i_pre = ic.astype(F32) + gate_b[0]
    f_pre = fc.astype(F32) + gate_b[1]
    q_c = qc.reshape(b, t, C_HEADS, C_DK)
    k_c = kc.reshape(b, t, C_HEADS, C_DK) * (C_DK ** -0.5)
    v_c = vc.reshape(b, t, C_HEADS, C_DV)
    if mode == 'prompt':
        c0 = jnp.zeros((b, C_HEADS, C_DV, C_DK), F32)
        n0 = jnp.zeros((b, C_HEADS, C_DK), F32)
        m0 = jnp.zeros((b, C_HEADS), F32)
    else:
        c0, n0, m0 = st['state_c_C'][o], st['state_c_n'][o], st['state_c_m'][o]
    hc, (c1, n1, m1) = mlstm_chunked(q_c, k_c, v_c, i_pre, f_pre, c0, n0, m0)
    hc = rms_norm(hc, p['c_on'][o]) * jax.nn.sigmoid(oc.reshape(b, t, C_HEADS, C_DV))
    qf = matmul(qa.reshape(b * t, Q_LORA), wb['w_d_qb'][o], gain=p['d_qa_g'][o])
    qf = qf.reshape(b, t, D_HEADS, NOPE_DIM + ROPE_DIM)
    qd = rms_norm(jnp.concatenate([qf[..., :NOPE_DIM], rope(qf[..., NOPE_DIM:], pos)], axis=-1), p['d_qn'][o])
    ckv = rms_norm(ckv, p['d_kv_g'][o])
    kpe = rope(kpe, pos)
    if mode == 'prompt':
        kd, vd = mla_kv(ckv, kpe, p['w_d_kvb'][o], p['d_kn'][o])
        od = causal_attn(qd, kd, vd, pos, pos)
    else:
        od = mla_sample_pallas(qd, ckv, kpe, st, o, p['d_kn'][o], wb)
    mixed = jnp.concatenate([hc.reshape(b, t, -1).astype(od.dtype), od.reshape(b, t, -1)], axis=-1)
    return mixed, (ckv, kpe, c1, n1, m1)


def kernel(x_prompt, x_sample, cache_a_k, cache_a_v, cache_a_kidx, state_b, state_c_C, state_c_n, state_c_m, cache_d_ckv, cache_d_kpe, cache_mem_k, cache_mem_v, page_table, mem_prompt, g_ffn1, w_ffn1_gu, w_ffn1_dn, g_mix, w_in_even, w_in_odd, w_mix_out, rel_bias, a_qn, a_kn, b_lb, b_on, c_gate_b, c_on, d_qa_g, d_kv_g, w_d_qb, w_d_kvb, d_qn, d_kn, g_mem, w_mem_q, w_mem_kv, w_mem_o, mem_qn, mem_kn, g_ffn2, w_ffn2_gu, w_ffn2_dn):
    p = {'rel_bias': rel_bias, 'a_qn': a_qn, 'a_kn': a_kn, 'b_on': b_on, 'c_gate_b': c_gate_b, 'c_on': c_on,
         'd_qa_g': d_qa_g, 'd_kv_g': d_kv_g, 'w_d_kvb': w_d_kvb, 'd_qn': d_qn, 'd_kn': d_kn,
         'mem_qn': mem_qn, 'mem_kn': mem_kn}
    bp, tp, d = x_prompt.shape
    bs, ts, _ = x_sample.shape
    n_p, n_s = bp * tp, bs * ts
    n_mem = mem_prompt.shape[1]
    past = page_table.shape[1] * PAGE_SIZE

    wo = w_in_odd.astype(BF16)
    n_odd = wo.shape[0]
    oc = np.cumsum([0] + ODD_SPLITS)
    gate_cols = []
    for pr in range(C_HEADS // 2):
        gate_cols += [wo[:, :, oc[3] + 2 * pr:oc[3] + 2 * pr + 2], wo[:, :, oc[4] + 2 * pr:oc[4] + 2 * pr + 2],
                      jnp.zeros((n_odd, d, LANES - 4), BF16)]
    in_odd = jnp.concatenate(
        [wo[:, :, oc[0]:oc[3]], wo[:, :, oc[5]:oc[6]]] + gate_cols
        + [wo[:, :, oc[6]:oc[9]], jnp.zeros((n_odd, d, LANES - ROPE_DIM), BF16)], axis=-1)
    gb = c_gate_b.astype(F32)
    gate_bias = jnp.concatenate(
        [gb[:, 0].reshape(n_odd, C_HEADS // 2, 2), gb[:, 1].reshape(n_odd, C_HEADS // 2, 2),
         jnp.zeros((n_odd, C_HEADS // 2, LANES - 4), F32)], axis=-1)[:, :, None, :]
    we = w_in_even.astype(BF16)
    c_wi = sum(EVEN_SPLITS[:4])
    c_ki = c_wi + IDX_HEADS
    c_qb = c_ki + IDX_DIM
    n_even = we.shape[0]
    c_ka = EVEN_SPLITS[0]
    in_even = jnp.concatenate(
        [we[:, :, :c_ka], we[:, :, c_qb:], we[:, :, c_ka:c_wi], we[:, :, c_ki:c_qb], we[:, :, c_ki:c_qb],
         we[:, :, c_wi:c_ki], jnp.zeros((n_even, d, LANES - IDX_HEADS), BF16)], axis=-1)
    kvb = w_d_kvb.astype(BF16).reshape(-1, KV_LORA, D_HEADS, NOPE_DIM + D_VDIM)
    d_kt3 = kvb[..., :NOPE_DIM].transpose(0, 2, 3, 1)
    eye = jnp.eye(D_HEADS, dtype=BF16)
    d_kt_bd = (d_kt3[:, :, :, None, :] * eye[None, :, None, :, None]).reshape(
        -1, D_HEADS * NOPE_DIM, D_HEADS * KV_LORA)
    wb = {
        'ffn1_gu': w_ffn1_gu.astype(BF16), 'ffn1_dn': w_ffn1_dn.astype(BF16),
        'ffn2_gu': w_ffn2_gu.astype(BF16), 'ffn2_dn': w_ffn2_dn.astype(BF16),
        'in_even': in_even,
        'in_odd': in_odd,
        'mix_out': w_mix_out.astype(BF16), 'w_d_qb': w_d_qb.astype(BF16), 'w_d_kvb': w_d_kvb.astype(BF16),
        'mem_q': w_mem_q.astype(BF16), 'mem_kv': w_mem_kv.astype(BF16), 'mem_o': w_mem_o.astype(BF16),
        'd_kt': d_kt3.reshape(-1, D_HEADS * NOPE_DIM, KV_LORA), 'd_kt_blockdiag': d_kt_bd,
        'd_v': kvb[..., NOPE_DIM:].transpose(0, 2, 1, 3),
    }
    bias_tiles, bias_sample = dsa_bias_tables(rel_bias, ts, past)
    n_pages = page_table.shape[1]
    n_phys = cache_a_k.shape[1]
    pool_k = cache_a_k.reshape(-1, n_phys, PAGE_SIZE * A_KV_HEADS, HEAD_DIM)
    pool_v = cache_a_v.reshape(-1, n_phys, PAGE_SIZE * A_KV_HEADS, HEAD_DIM)
    pool_kidx_t = jnp.swapaxes(cache_a_kidx, 2, 3)
    pool_kpe_t = jnp.swapaxes(cache_d_kpe, 2, 3)
    mem_k_rows = cache_mem_k.reshape(DEPTH, bs, n_mem * MEM_HEADS, HEAD_DIM)
    mem_v_rows = cache_mem_v.reshape(DEPTH, bs, n_mem * MEM_HEADS, HEAD_DIM)
    pt_flat = page_table.reshape(-1)

    lb_soft = jax.nn.softmax(b_lb.astype(F32), axis=0)
    lower_bound = jnp.cumsum(lb_soft, axis=0) - lb_soft[0]

    pos_all = jnp.concatenate([jnp.tile(jnp.arange(tp), bp), jnp.tile(past + jnp.arange(ts), bs)])
    st = {'cache_a_k': cache_a_k, 'cache_a_v': cache_a_v, 'cache_a_kidx': cache_a_kidx, 'state_b': state_b,
          'state_c_C': state_c_C, 'state_c_n': state_c_n, 'state_c_m': state_c_m,
          'cache_d_ckv': cache_d_ckv, 'pool_kpe_t': pool_kpe_t, 'page_table': page_table, 'pt_flat': pt_flat}

    x = jnp.concatenate([x_prompt.reshape(n_p, d), x_sample.reshape(n_s, d)], axis=0)
    mem2d = mem_prompt.reshape(bp * n_mem, d)
    new_p = {n: [] for n in ('a_k', 'a_v', 'a_kidx', 'b_S', 'c_C', 'c_n', 'c_m', 'd_ckv', 'd_kpe', 'mem_k', 'mem_v')}
    new_s = {n: [] for n in ('a_k', 'a_v', 'a_kidx', 'b_S', 'c_C', 'c_n', 'c_m', 'd_ckv', 'd_kpe')}

    for l in range(DEPTH):
        x = ffn_half_step(x, g_ffn1[l], wb['ffn1_gu'], wb['ffn1_dn'], l)
        if l % 2 == 0:
            e = l // 2
            proj = matmul(x, wb['in_even'], layer=e, gain=g_mix[l], tn=1280)
            ka32, ka16 = head_norm(proj, EV_KA // (A_KV_HEADS * HEAD_DIM), A_KV_HEADS, a_kn[e])
            oa_p = dsa_prompt_attend(proj, ka16, bias_tiles, a_qn[e], nb=bp, t=tp)
            sel_past, sel_new = dsa_sample_select(proj, pool_kidx_t, pt_flat, e, row0=n_p, nb=bs, ts=ts,
                                                  n_pages=n_pages)
            oa_s = dsa_sample_attend(proj, ka32, sel_past, sel_new, bias_sample, pool_k, pool_v, pt_flat, e, a_qn[e],
                                     row0=n_p, nb=bs, ts=ts, n_pages=n_pages)
            lb = lower_bound[e].reshape(B_HEADS, B_DK)
            ob_p, s_p = hgrn2_prompt(proj, lb, b_on[e], nb=bp, t=tp)
            ob_s, s_s = hgrn2_sample(proj, lb, b_on[e], state_b[e], row0=n_p, nb=bs, t=ts)
            mixed_p = jnp.concatenate([oa_p, ob_p], axis=-1)
            mixed_s = jnp.concatenate([oa_s, ob_s], axis=-1)
            va = proj[:, EV_VA:EV_VA + A_KV_HEADS * HEAD_DIM]
            ki = proj[:, EV_KIKI:EV_KIKI + IDX_DIM]
            for new, rows, nb_, t_, s_new in ((new_p, slice(0, n_p), bp, tp, s_p), (new_s, slice(n_p, None), bs, ts, s_s)):
                new['a_k'].append(ka32[rows].reshape(nb_, t_, A_KV_HEADS, HEAD_DIM))
                new['a_v'].append(va[rows].reshape(nb_, t_, A_KV_HEADS, HEAD_DIM))
                new['a_kidx'].append(ki[rows].reshape(nb_, t_, IDX_DIM))
                new['b_S'].append(s_new)
        else:
            o = l // 2
            proj = matmul(x, wb['in_odd'], layer=o, gain=g_mix[l], tn=896)
            hc_p, cc_p, cn_p, cm_p = mlstm(proj, gate_bias[o], c_on[o], None, row0=0, nb=bp, t=tp, npairs=2)
            hc_s, cc_s, cn_s, cm_s = mlstm(
                proj, gate_bias[o], c_on[o],
                (jnp.swapaxes(state_c_C[o], -1, -2), state_c_n[o], state_c_m[o].reshape(bs, 1, C_HEADS)),
                row0=n_p, nb=bs, t=ts, npairs=C_HEADS // 2)
            qf = matmul(proj, wb['w_d_qb'], layer=o, gain=d_qa_g[o], a_col=OD_QA // Q_LORA)
            qf = qf.reshape(n_p + n_s, D_HEADS, NOPE_DIM + ROPE_DIM)
            qd = rms_norm(jnp.concatenate([qf[..., :NOPE_DIM], rope_rows(qf[..., NOPE_DIM:], pos_all)], axis=-1),
                          d_qn[o])
            ckv = rms_norm(proj[:, OD_CKV:OD_CKV + KV_LORA], d_kv_g[o])
            kpe = rope_rows(proj[:, OD_KPE:OD_KPE + ROPE_DIM], pos_all)
            kv = matmul(ckv[:n_p], wb['w_d_kvb'], layer=o).reshape(n_p, D_HEADS, NOPE_DIM + D_VDIM)
            k_pe = jnp.broadcast_to(kpe[:n_p, None, :], (n_p, D_HEADS, ROPE_DIM))
            kd = rms_norm(jnp.concatenate([kv[..., :NOPE_DIM], k_pe], axis=-1), d_kn[o])
            pad = ((0, 0), (0, 0), (0, 2 * LANES - NOPE_DIM - ROPE_DIM))
            q16 = jnp.pad(qd[:n_p], pad).astype(BF16).reshape(n_p, D_HEADS * 2 * LANES)
            k16 = jnp.pad(kd, pad).astype(BF16).reshape(n_p, D_HEADS * 2 * LANES)
            v16 = kv[..., NOPE_DIM:].astype(BF16).reshape(n_p, D_HEADS * D_VDIM)
            od_p = causal_attend(q16, k16, v16, nb=bp, t=tp, heads=D_HEADS, dv=D_VDIM,
                                 scale=(NOPE_DIM + ROPE_DIM) ** -0.5)
            od_s = mla_sample_pallas(qd[n_p:].reshape(bs, ts, D_HEADS, NOPE_DIM + ROPE_DIM),
                                     ckv[n_p:].reshape(bs, ts, KV_LORA), kpe[n_p:].reshape(bs, ts, ROPE_DIM),
                                     st, o, d_kn[o], wb).reshape(n_s, D_HEADS * D_VDIM)
            mixed_p = jnp.concatenate([hc_p, od_p], axis=-1)
            mixed_s = jnp.concatenate([hc_s, od_s], axis=-1)
            for new, rows, nb_, t_, cc, cn, cm in ((new_p, slice(0, n_p), bp, tp, cc_p, cn_p, cm_p),
                                                   (new_s, slice(n_p, None), bs, ts, cc_s, cn_s, cm_s)):
                new['d_ckv'].append(ckv[rows].reshape(nb_, t_, KV_LORA))
                new['d_kpe'].append(kpe[rows].reshape(nb_, t_, ROPE_DIM))
                new['c_C'].append(jnp.swapaxes(cc, -1, -2))
                new['c_n'].append(cn[:, :, 0, :])
                new['c_m'].append(cm[:, :, 0, 0])
        mixed = jnp.concatenate([mixed_p, mixed_s], axis=0)
        x = matmul(mixed, wb['mix_out'], layer=l, res=x)

        kvm = matmul(mem2d, wb['mem_kv'], layer=l)
        mk32, mk16 = head_norm(kvm, 0, MEM_HEADS, mem_kn[l], tm=bp * n_mem)
        wm = MEM_HEADS * HEAD_DIM
        new_p['mem_k'].append(mk32.reshape(bp, n_mem, MEM_HEADS, HEAD_DIM))
        new_p['mem_v'].append(kvm[:, wm:].reshape(bp, n_mem, MEM_HEADS, HEAD_DIM))
        q = matmul(x, wb['mem_q'], layer=l, gain=g_mem[l])
        nqb = 4
        o_p = mem_attend(q, mk16.reshape(bp, n_mem, wm), kvm.reshape(bp, n_mem, 2 * wm), mem_qn[l],
                         row0=0, n_batch=bp * nqb, tq=tp // nqb, nb=1, n_mem=n_mem, head_rows=False,
                         k_index=lambda i: (i // nqb, 0, 0), v_index=lambda i: (i // nqb, 0, 1))
        o_s = mem_attend(q, mem_k_rows, mem_v_rows, mem_qn[l], row0=n_p, n_batch=bs, tq=ts, nb=8,
                         n_mem=n_mem, head_rows=True,
                         k_index=lambda i, l=l: (l, i, 0, 0), v_index=lambda i, l=l: (l, i, 0, 0))
        o_all = jnp.concatenate([o_p, o_s], axis=0)
        x = matmul(o_all, wb['mem_o'], layer=l, res=x)
        x = ffn_half_step(x, g_ffn2[l], wb['ffn2_gu'], wb['ffn2_dn'], l)

    sp = {n: jnp.stack(v) for n, v in new_p.items()}
    ss = {n: jnp.stack(v) for n, v in new_s.items()}
    return (x[:n_p].reshape(bp, tp, d), x[n_p:].reshape(bs, ts, d),
            sp['a_k'], sp['a_v'], sp['a_kidx'], sp['b_S'], sp['c_C'], sp['c_n'], sp['c_m'], sp['d_ckv'], sp['d_kpe'],
            sp['mem_k'], sp['mem_v'],
            ss['a_k'], ss['a_v'], ss['a_kidx'], ss['b_S'], ss['c_C'], ss['c_n'], ss['c_m'], ss['d_ckv'], ss['d_kpe'])
```

```python
import functools
import math

import jax
import jax.numpy as jnp
import numpy as np
from jax import lax
from jax.experimental import pallas as pl
from jax.experimental.pallas import tpu as pltpu

D_MODEL = 2048
DEPTH = 4
PAGE_SIZE = 128
HEAD_DIM = 128
A_HEADS = 8
A_KV_HEADS = 2
IDX_HEADS = 8
IDX_DIM = 64
TOPK_MAX = 256
REL_BUCKETS = 32
REL_MAX_DIST = 128
B_HEADS = 8
B_DK = 128
B_DV = 128
C_HEADS = 8
C_DK = 64
C_DV = 128
D_HEADS = 8
Q_LORA = 512
KV_LORA = 256
NOPE_DIM = 128
ROPE_DIM = 64
D_VDIM = 128
ROPE_THETA = 10000.0
MEM_HEADS = 4
D_FF = 5632
CHUNK = 64
Q_BLOCK = 128
EPS = 1e-6
MASK_NEG = -1e30
LB_FLOOR = 1e-20
F32 = jnp.float32
BF16 = jnp.bfloat16

EVEN_SPLITS = [A_HEADS * HEAD_DIM, A_KV_HEADS * HEAD_DIM, A_KV_HEADS * HEAD_DIM, IDX_HEADS * IDX_DIM, IDX_HEADS,
               IDX_DIM, B_HEADS * B_DK, B_HEADS * B_DK, B_HEADS * B_DV, B_HEADS * B_DV]
ODD_SPLITS = [C_HEADS * C_DK, C_HEADS * C_DK, C_HEADS * C_DV, C_HEADS, C_HEADS, C_HEADS * C_DV, Q_LORA, KV_LORA,
              ROPE_DIM]
EVEN_IN = sum(EVEN_SPLITS)
ODD_IN = sum(ODD_SPLITS)

EV_QA, EV_QB, EV_FB, EV_IB, EV_GB, EV_KA, EV_VA, EV_QI, EV_KIKI, EV_WI, EVEN_PAD = (
    0, 1024, 2048, 3072, 4096, 5120, 5376, 5632, 6144, 6272, 6400)
OD_QC, OD_KC, OD_VC, OD_OC, OD_G, OD_QA, OD_CKV, OD_KPE, ODD_PAD = (
    0, 512, 1024, 2048, 3072, 3584, 4096, 4352, 4480)

LANES = 128
VMEM_LIMIT = 56 * 1024 * 1024


def _round_up(n, m):
    return (n + m - 1) // m * m


def _ffn_body(x_ref, g_ref, wg_ref, wu_ref, wd_ref, o_ref, n_ref):
    f = pl.program_id(1)

    @pl.when(f == 0)
    def _():
        x = x_ref[...]
        ms = jnp.mean(x * x, axis=-1, keepdims=True)
        n_ref[...] = (x * lax.rsqrt(ms + EPS) * g_ref[...]).astype(BF16)
        o_ref[...] = x

    n = n_ref[...]
    a = jnp.dot(n, wg_ref[...], preferred_element_type=F32)
    b = jnp.dot(n, wu_ref[...], preferred_element_type=F32)
    h = (a * jax.nn.sigmoid(a) * b).astype(BF16)
    o_ref[...] += 0.5 * jnp.dot(h, wd_ref[...], preferred_element_type=F32)


def ffn_half_step(x, g, w_gu, w_dn, layer, *, tm=768, tf=512):
    m, d = x.shape
    ff = w_dn.shape[1]
    nf = ff // tf
    assert m % tm == 0 and ff % tf == 0, (m, tm, ff, tf)
    return pl.pallas_call(
        _ffn_body,
        grid=(m // tm, nf),
        in_specs=[
            pl.BlockSpec((tm, d), lambda i, f: (i, 0)),
            pl.BlockSpec((1, d), lambda i, f: (0, 0)),
            pl.BlockSpec((None, d, tf), lambda i, f: (layer, 0, f)),
            pl.BlockSpec((None, d, tf), lambda i, f: (layer, 0, f + nf)),
            pl.BlockSpec((None, tf, d), lambda i, f: (layer, f, 0)),
        ],
        out_specs=pl.BlockSpec((tm, d), lambda i, f: (i, 0)),
        out_shape=jax.ShapeDtypeStruct((m, d), F32),
        scratch_shapes=[pltpu.VMEM((tm, d), BF16)],
        compiler_params=pltpu.CompilerParams(
            dimension_semantics=("parallel", "arbitrary"), vmem_limit_bytes=VMEM_LIMIT),
        name="ffn_half_step",
    )(x, g.reshape(1, d), w_gu, w_gu, w_dn)


def _mm_body(*refs, has_gain, has_res, alpha):
    a_ref, w_ref = refs[0], refs[1]
    k = 2
    g_ref = res_ref = None
    if has_gain:
        g_ref = refs[k]
        k += 1
    if has_res:
        res_ref = refs[k]
        k += 1
    o_ref, n_ref = refs[k], refs[k + 1]

    @pl.when(pl.program_id(1) == 0)
    def _():
        a = a_ref[...]
        if has_gain:
            ms = jnp.mean(a * a, axis=-1, keepdims=True)
            a = a * lax.rsqrt(ms + EPS) * g_ref[...]
        n_ref[...] = a.astype(BF16)

    acc = jnp.dot(n_ref[...], w_ref[...], preferred_element_type=F32)
    if has_res:
        acc = res_ref[...] + alpha * acc
    o_ref[...] = acc


def matmul(a, w, *, gain=None, res=None, alpha=1.0, tm=512, tn=512, a_col=0, layer=None):
    m = a.shape[0]
    kdim, n = w.shape[-2:]
    tm = min(tm, m)
    tn = min(tn, n)
    assert m % tm == 0 and n % tn == 0, (m, tm, n, tn)
    w_spec = (pl.BlockSpec((kdim, tn), lambda i, j: (0, j)) if layer is None
              else pl.BlockSpec((None, kdim, tn), lambda i, j: (layer, 0, j)))
    in_specs = [pl.BlockSpec((tm, kdim), lambda i, j: (i, a_col)), w_spec]
    args = [a, w]
    if gain is not None:
        in_specs.append(pl.BlockSpec((1, kdim), lambda i, j: (0, 0)))
        args.append(gain.reshape(1, kdim))
    if res is not None:
        in_specs.append(pl.BlockSpec((tm, tn), lambda i, j: (i, j)))
        args.append(res)
    return pl.pallas_call(
        functools.partial(_mm_body, has_gain=gain is not None, has_res=res is not None, alpha=alpha),
        grid=(m // tm, n // tn),
        in_specs=in_specs,
        out_specs=pl.BlockSpec((tm, tn), lambda i, j: (i, j)),
        out_shape=jax.ShapeDtypeStruct((m, n), F32),
        scratch_shapes=[pltpu.VMEM((tm, kdim), BF16)],
        compiler_params=pltpu.CompilerParams(
            dimension_semantics=("parallel", "arbitrary"), vmem_limit_bytes=VMEM_LIMIT),
        name="matmul",
    )(*args)


def _head_norm_body(x_ref, g_ref, o32_ref, o16_ref, *, heads):
    outs = []
    for h in range(heads):
        x = x_ref[:, LANES * h:LANES * (h + 1)]
        outs.append(x * lax.rsqrt(jnp.mean(x * x, axis=-1, keepdims=True) + EPS) * g_ref[...])
    y = jnp.concatenate(outs, axis=1)
    o32_ref[...] = y
    o16_ref[...] = y.astype(BF16)


def head_norm(x, col_block, heads, gain, *, tm=512):
    m = x.shape[0]
    w = heads * LANES
    return pl.pallas_call(
        functools.partial(_head_norm_body, heads=heads),
        grid=(m // tm,),
        in_specs=[pl.BlockSpec((tm, w), lambda i: (i, col_block)),
                  pl.BlockSpec((1, LANES), lambda i: (0, 0))],
        out_specs=[pl.BlockSpec((tm, w), lambda i: (i, 0)), pl.BlockSpec((tm, w), lambda i: (i, 0))],
        out_shape=[jax.ShapeDtypeStruct((m, w), F32), jax.ShapeDtypeStruct((m, w), BF16)],
        compiler_params=pltpu.CompilerParams(dimension_semantics=("parallel",)),
        name="head_norm",
    )(x, gain.reshape(1, LANES))


INT_MIN = -2 ** 31
INT_MAX = 2 ** 31 - 1
_NT = (((1,), (1,)), ((), ()))


def _dot_nt(a, b):
    return lax.dot_general(a, b, _NT, preferred_element_type=F32)


def _sort_key(x):
    b = lax.bitcast_convert_type(x + 0.0, jnp.int32)
    return jnp.where(b >= 0, b, b ^ jnp.int32(INT_MAX))


def _row_count(cond):
    c = jnp.where(cond, 1.0, 0.0)
    if c.ndim == 3:
        c = jnp.sum(c, axis=0)
    return jnp.sum(c, axis=-1, keepdims=True)


def _topk_masks(pieces, k, idx_bits):
    kf = float(k)

    def count(fn):
        tot = None
        for key, idx in pieces:
            c = _row_count(fn(key, idx))
            tot = c if tot is None else tot + c
        return tot

    t0 = jnp.where(count(lambda key, idx: key >= 0) >= kf, jnp.int32(0), jnp.int32(INT_MIN))

    def body(i, t):
        cand = t | jnp.left_shift(jnp.int32(1), 30 - i)
        return jnp.where(count(lambda key, idx: key >= cand) >= kf, cand, t)

    thr = lax.fori_loop(0, 31, body, t0)
    need = kf - count(lambda key, idx: key > thr)
    ties = [(jnp.where(key == thr, idx, INT_MAX), idx) for key, idx in pieces]

    def count_ties(cand):
        tot = None
        for tie, _ in ties:
            c = _row_count(tie < cand)
            tot = c if tot is None else tot + c
        return tot

    def body2(i, cur):
        cand = cur + jnp.left_shift(jnp.int32(1), idx_bits - 1 - i)
        return jnp.where(count_ties(cand) < need, cand, cur)

    cut = lax.fori_loop(0, idx_bits, body2, jnp.zeros_like(thr))
    return [jnp.where(key > thr, 1.0, jnp.where(tie <= cut, 1.0, 0.0))
            for (key, _), (tie, _) in zip(pieces, ties)]


def _softmax_rows(lg):
    mx = jnp.max(lg, axis=-1, keepdims=True)
    p = jnp.exp(lg - mx)
    return p / jnp.sum(p, axis=-1, keepdims=True)


def _dsa_prompt_body(qa_ref, qi_ref, wi_ref, kiki_ref, k_ref, v_ref, bias_ref, qn_ref, o_ref, *, qb, t, n_sel, i0, kw):
    i = i0 + pl.program_id(1)
    nsub = qb // LANES

    def run(kw):
        nk = kw // LANES
        kiki = kiki_ref[0:kw, :].astype(BF16)
        qi = qi_ref[...] * (IDX_DIM ** -0.5)
        w = wi_ref[...] * (IDX_HEADS ** -0.5)
        lane = lax.broadcasted_iota(jnp.int32, (qb, LANES), 1)
        score = jnp.zeros((qb, kw), F32)
        for h in range(IDX_HEADS):
            blk = qi[:, LANES * (h // 2):LANES * (h // 2 + 1)]
            keep = (lane < IDX_DIM) if h % 2 == 0 else (lane >= IDX_DIM)
            s = _dot_nt(jnp.where(keep, blk, 0.0).astype(BF16), kiki)
            score = score + jnp.maximum(s, 0.0) * w[:, h:h + 1]
        q_pos = i * qb + lax.broadcasted_iota(jnp.int32, (qb, kw), 0)
        k_pos = lax.broadcasted_iota(jnp.int32, (qb, kw), 1)
        allowed = k_pos <= q_pos
        score = jnp.where(allowed, score, MASK_NEG)
        pieces = [(_sort_key(score), k_pos)]
        nv = _round_up(n_sel, LANES) if kw < t else 0
        if nv:
            pieces.append((_sort_key(jnp.full((qb, nv), MASK_NEG, F32)),
                           kw + lax.broadcasted_iota(jnp.int32, (qb, nv), 1)))
        selm = _topk_masks(pieces, n_sel, max(1, (kw + nv - 1).bit_length()))[0]
        sel = jnp.where(allowed, selm, 0.0) > 0.5

        k = k_ref[0:kw, :]
        v = v_ref[0:kw, :].astype(BF16)
        outs = []
        for h in range(A_HEADS):
            g = h // (A_HEADS // A_KV_HEADS)
            rows = []
            for a in range(nsub):
                d0 = i * nsub + a
                tiles = []
                for j in range(nk):
                    d = d0 - j
                    tiles.append(jnp.where(d == 0, bias_ref[h, 0],
                                           jnp.where(d == 1, bias_ref[h, 1], bias_ref[h, 2])))
                rows.append(jnp.concatenate(tiles, axis=1))
            bias = jnp.concatenate(rows, axis=0) if nsub > 1 else rows[0]
            q = qa_ref[:, LANES * h:LANES * (h + 1)]
            q = q * lax.rsqrt(jnp.mean(q * q, axis=-1, keepdims=True) + EPS) * qn_ref[...]
            lg = _dot_nt(q.astype(BF16), k[:, LANES * g:LANES * (g + 1)]) * (HEAD_DIM ** -0.5) + bias
            p = _softmax_rows(jnp.where(sel, lg, MASK_NEG))
            outs.append(jnp.dot(p.astype(BF16), v[:, LANES * g:LANES * (g + 1)], preferred_element_type=F32))
        o_ref[...] = jnp.concatenate(outs, axis=1)

    run(kw)


def dsa_prompt_attend(proj, ka16, bias_tiles, qn, *, nb, t, qb=256, splits=2):
    nq = t // qb
    wq = A_HEADS * HEAD_DIM
    per = -(-nq // splits)
    outs = []
    for i0 in range(0, nq, per):
        ni = min(per, nq - i0)
        o = _dsa_prompt_call(proj, ka16, bias_tiles, qn, nb=nb, t=t, qb=qb, i0=i0, ni=ni, kw=min(t, (i0 + ni) * qb))
        outs.append(o.reshape(nb, ni * qb, wq))
    return jnp.concatenate(outs, axis=1).reshape(nb * t, wq)


def _dsa_prompt_call(proj, ka16, bias_tiles, qn, *, nb, t, qb, i0, ni, kw):
    nq = t // qb
    n_sel = min(TOPK_MAX, t // 4)
    wq = A_HEADS * HEAD_DIM
    assert t % kw == 0, (t, kw)
    return pl.pallas_call(
        functools.partial(_dsa_prompt_body, qb=qb, t=t, n_sel=n_sel, i0=i0, kw=kw),
        grid=(nb, ni),
        in_specs=[
            pl.BlockSpec((qb, wq), lambda b, i: (b * nq + i0 + i, 0)),
            pl.BlockSpec((qb, IDX_HEADS * IDX_DIM), lambda b, i: (b * nq + i0 + i, EV_QI // (IDX_HEADS * IDX_DIM))),
            pl.BlockSpec((qb, LANES), lambda b, i: (b * nq + i0 + i, EV_WI // LANES)),
            pl.BlockSpec((kw, LANES), lambda b, i: (b * (t // kw), EV_KIKI // LANES)),
            pl.BlockSpec((kw, A_KV_HEADS * HEAD_DIM), lambda b, i: (b * (t // kw), 0)),
            pl.BlockSpec((kw, A_KV_HEADS * HEAD_DIM), lambda b, i: (b * (t // kw), EV_VA // (A_KV_HEADS * HEAD_DIM))),
            pl.BlockSpec((A_HEADS, 3, LANES, LANES), lambda b, i: (0, 0, 0, 0)),
            pl.BlockSpec((1, HEAD_DIM), lambda b, i: (0, 0)),
        ],
        out_specs=pl.BlockSpec((qb, wq), lambda b, i: (b * ni + i, 0)),
        out_shape=jax.ShapeDtypeStruct((nb * ni * qb, wq), F32),
        compiler_params=pltpu.CompilerParams(
            dimension_semantics=("parallel", "arbitrary"), vmem_limit_bytes=VMEM_LIMIT),
        name="dsa_prompt",
    )(proj, proj, proj, proj, ka16, proj, bias_tiles, qn.reshape(1, HEAD_DIM))


def _idx_head_sum(s, w):
    tot = None
    for h in range(IDX_HEADS):
        c = jnp.maximum(s[8 * h:8 * (h + 1)], 0.0) * w[:, h:h + 1]
        tot = c if tot is None else tot + c
    return tot


def _dsa_score_body(pt_ref, qi_ref, wi_ref, knew_ref, *rest, pp, ts):
    pages, sp_ref, sn_ref = rest[:pp], rest[pp], rest[pp + 1]
    qi = qi_ref[...] * (IDX_DIM ** -0.5)
    a = jnp.concatenate([qi[:, IDX_DIM * h:IDX_DIM * (h + 1)] for h in range(IDX_HEADS)], axis=0).astype(BF16)
    w = wi_ref[...] * (IDX_HEADS ** -0.5)
    kt = jnp.concatenate([p[...] for p in pages], axis=1).astype(BF16)
    sp_ref[...] = _idx_head_sum(jnp.dot(a, kt, preferred_element_type=F32), w)

    @pl.when(pl.program_id(1) == 0)
    def _():
        knew = knew_ref[...][:, :IDX_DIM]
        kn = jnp.concatenate([knew, jnp.zeros((LANES - ts, IDX_DIM), F32)], axis=0).astype(BF16)
        sn_ref[...] = _idx_head_sum(_dot_nt(a, kn), w)


def _dsa_mask_body(sp_ref, sn_ref, mp_ref, mn_ref, *, ts, n_sel):
    rows, past = sp_ref.shape
    lane = lax.broadcasted_iota(jnp.int32, (rows // ts, ts, LANES), 2).reshape(rows, LANES)
    t_row = lax.broadcasted_iota(jnp.int32, (rows // ts, ts, LANES), 1).reshape(rows, LANES)
    causal = lane <= t_row
    snew = jnp.where(causal, sn_ref[...], MASK_NEG)
    key_new = jnp.where(lane < ts, _sort_key(snew), INT_MIN)
    idx_past = lax.broadcasted_iota(jnp.int32, (rows, past), 1)
    m_past, m_new = _topk_masks([(_sort_key(sp_ref[...]), idx_past), (key_new, past + lane)], n_sel,
                                (past + LANES - 1).bit_length())
    mp_ref[...] = m_past
    mn_ref[...] = jnp.where(causal, m_new, 0.0)


def dsa_sample_select(proj, pool_kidx_t, pt_flat, layer, *, row0, nb, ts, n_pages, pp=32, rows_per_step=128):
    pp = min(pp, n_pages)
    nj = n_pages // pp
    tk = pp * PAGE_SIZE
    past = n_pages * PAGE_SIZE
    n_sel = min(TOPK_MAX, (past + ts) // 4)
    rb0 = row0 // ts
    rows = nb * ts
    rstep = min(rows_per_step, rows)
    sp, sn = _dsa_sample_scores(proj, pool_kidx_t, pt_flat, layer, rb0=rb0, nb=nb, ts=ts, n_pages=n_pages, pp=pp)
    mp, mn = pl.pallas_call(
        functools.partial(_dsa_mask_body, ts=ts, n_sel=n_sel),
        grid=(rows // rstep,),
        in_specs=[pl.BlockSpec((rstep, past), lambda i: (i, 0)), pl.BlockSpec((rstep, LANES), lambda i: (i, 0))],
        out_specs=[pl.BlockSpec((rstep, past), lambda i: (i, 0)), pl.BlockSpec((rstep, LANES), lambda i: (i, 0))],
        out_shape=[jax.ShapeDtypeStruct((rows, past), F32), jax.ShapeDtypeStruct((rows, LANES), F32)],
        compiler_params=pltpu.CompilerParams(dimension_semantics=("parallel",), vmem_limit_bytes=VMEM_LIMIT),
        name="dsa_sample_mask",
    )(sp.reshape(rows, past), sn.reshape(rows, LANES))
    return mp.reshape(nb, ts, past), mn.reshape(nb, ts, LANES)


def _dsa_sample_scores(proj, pool_kidx_t, pt_flat, layer, *, rb0, nb, ts, n_pages, pp):
    pp = min(pp, n_pages)
    nj = n_pages // pp
    tk = pp * PAGE_SIZE
    past = n_pages * PAGE_SIZE

    def page_spec(r):
        return pl.BlockSpec((None, None, IDX_DIM, PAGE_SIZE),
                            lambda b, j, pt: (layer, pt[b * n_pages + j * pp + r], 0, 0))

    grid_spec = pltpu.PrefetchScalarGridSpec(
        num_scalar_prefetch=1,
        grid=(nb, nj),
        in_specs=[
            pl.BlockSpec((ts, IDX_HEADS * IDX_DIM), lambda b, j, pt: (rb0 + b, EV_QI // (IDX_HEADS * IDX_DIM))),
            pl.BlockSpec((ts, LANES), lambda b, j, pt: (rb0 + b, EV_WI // LANES)),
            pl.BlockSpec((ts, LANES), lambda b, j, pt: (rb0 + b, EV_KIKI // LANES)),
        ] + [page_spec(r) for r in range(pp)],
        out_specs=[pl.BlockSpec((None, ts, tk), lambda b, j, pt: (b, 0, j)),
                   pl.BlockSpec((None, ts, LANES), lambda b, j, pt: (b, 0, 0))],
    )
    return pl.pallas_call(
        functools.partial(_dsa_score_body, pp=pp, ts=ts),
        grid_spec=grid_spec,
        out_shape=[jax.ShapeDtypeStruct((nb, ts, past), F32), jax.ShapeDtypeStruct((nb, ts, LANES), F32)],
        compiler_params=pltpu.CompilerParams(
            dimension_semantics=("parallel", "arbitrary"), vmem_limit_bytes=VMEM_LIMIT),
        name="dsa_sample_scores",
    )(pt_flat, proj, proj, proj, *([pool_kidx_t] * pp))


def _dsa_att_body(pt_ref, q_ref, knew_ref, vnew_ref, qn_ref, mask_ref, mnew_ref, bias_ref, bnew_ref, *rest,
                  pp, sub, nj, ts):
    kpages, vpages = rest[:pp], rest[pp:2 * pp]
    o_ref, qs_ref, m_ref, l_ref, acc_ref = rest[2 * pp:]
    j = pl.program_id(1)
    gsz = A_HEADS // A_KV_HEADS
    rows_g = gsz * ts

    @pl.when(j == 0)
    def _():
        qs = []
        for h in range(A_HEADS):
            q = q_ref[:, LANES * h:LANES * (h + 1)]
            qs.append(q * lax.rsqrt(jnp.mean(q * q, axis=-1, keepdims=True) + EPS) * qn_ref[...])
        qs_ref[...] = jnp.concatenate(qs, axis=0).astype(BF16)
        m_ref[...] = jnp.full(m_ref.shape, MASK_NEG, F32)
        l_ref[...] = jnp.zeros(l_ref.shape, F32)
        acc_ref[...] = jnp.zeros(acc_ref.shape, F32)

    def partial(k_tiles, v_tiles, mask, bias):
        qs = qs_ref[...]
        lg = jnp.concatenate(
            [_dot_nt(qs[rows_g * g:rows_g * (g + 1)], k_tiles[g])
             for g in range(A_KV_HEADS)], axis=0) * (HEAD_DIM ** -0.5) + bias
        sel = jnp.concatenate([mask] * A_HEADS, axis=0) > 0.5
        lg = jnp.where(sel, lg, MASK_NEG)
        m = jnp.max(lg, axis=-1, keepdims=True)
        p = jnp.where(sel, jnp.exp(lg - m), 0.0)
        pb = p.astype(BF16)
        pv = jnp.concatenate(
            [jnp.dot(pb[rows_g * g:rows_g * (g + 1)], v_tiles[g], preferred_element_type=F32)
             for g in range(A_KV_HEADS)], axis=0)
        return m, jnp.sum(p, axis=-1, keepdims=True), pv

    def merge(parts):
        m_old = m_ref[...]
        m_new = m_old
        for m, _, _ in parts:
            m_new = jnp.maximum(m_new, m)
        alpha = jnp.exp(m_old - m_new)
        l_new = alpha * l_ref[...]
        acc = alpha * acc_ref[...]
        for m, l, pv in parts:
            a = jnp.exp(m - m_new)
            l_new = l_new + a * l
            acc = acc + a * pv
        m_ref[...] = m_new
        l_ref[...] = l_new
        acc_ref[...] = acc

    def head_rows(pages, g):
        return jnp.concatenate([p[pl.ds(g, PAGE_SIZE, stride=A_KV_HEADS), :] for p in pages], axis=0).astype(BF16)

    bias_all = bias_ref[...].reshape(A_HEADS * ts, pp * PAGE_SIZE)
    mask_all = mask_ref[...]
    parts = []
    for s0 in range(0, pp, sub):
        kp, vp = kpages[s0:s0 + sub], vpages[s0:s0 + sub]
        cols = slice(s0 * PAGE_SIZE, (s0 + sub) * PAGE_SIZE)
        parts.append(partial([head_rows(kp, g) for g in range(A_KV_HEADS)],
                             [head_rows(vp, g) for g in range(A_KV_HEADS)], mask_all[:, cols], bias_all[:, cols]))
    merge(parts)

    @pl.when(j == nj - 1)
    def _():
        pad = jnp.zeros((LANES - ts, A_KV_HEADS * HEAD_DIM), F32)
        kn = jnp.concatenate([knew_ref[...], pad], axis=0).astype(BF16)
        vn = jnp.concatenate([vnew_ref[...], pad], axis=0).astype(BF16)
        merge([partial([kn[:, LANES * g:LANES * (g + 1)] for g in range(A_KV_HEADS)],
                       [vn[:, LANES * g:LANES * (g + 1)] for g in range(A_KV_HEADS)],
                       mnew_ref[...], bnew_ref[...].reshape(A_HEADS * ts, LANES))])
        out = acc_ref[...] / l_ref[...]
        o_ref[...] = jnp.concatenate([out[ts * h:ts * (h + 1)] for h in range(A_HEADS)], axis=1)


def dsa_sample_attend(proj, ka32, mask_past, mask_new, bias_s, pool_k, pool_v, pt_flat, layer, qn, *, row0, nb, ts,
                      n_pages, pp=32, sub=8):
    pp = min(pp, n_pages)
    nj = n_pages // pp
    tk = pp * PAGE_SIZE
    past = n_pages * PAGE_SIZE
    rb0 = row0 // ts
    wkv = A_KV_HEADS * HEAD_DIM
    wq = A_HEADS * HEAD_DIM

    def page_spec(r):
        return pl.BlockSpec((None, None, PAGE_SIZE * A_KV_HEADS, HEAD_DIM),
                            lambda b, j, pt: (layer, pt[b * n_pages + j * pp + r], 0, 0))

    grid_spec = pltpu.PrefetchScalarGridSpec(
        num_scalar_prefetch=1,
        grid=(nb, nj),
        in_specs=[
            pl.BlockSpec((ts, wq), lambda b, j, pt: (rb0 + b, 0)),
            pl.BlockSpec((ts, wkv), lambda b, j, pt: (rb0 + b, 0)),
            pl.BlockSpec((ts, wkv), lambda b, j, pt: (rb0 + b, EV_VA // wkv)),
            pl.BlockSpec((1, HEAD_DIM), lambda b, j, pt: (0, 0)),
            pl.BlockSpec((None, ts, tk), lambda b, j, pt: (b, 0, j)),
            pl.BlockSpec((None, ts, LANES), lambda b, j, pt: (b, 0, 0)),
            pl.BlockSpec((A_HEADS, ts, tk), lambda b, j, pt: (0, 0, j)),
            pl.BlockSpec((A_HEADS, ts, LANES), lambda b, j, pt: (0, 0, past // LANES)),
        ] + [page_spec(r) for r in range(pp)] * 2,
        out_specs=pl.BlockSpec((ts, wq), lambda b, j, pt: (b, 0)),
        scratch_shapes=[pltpu.VMEM((A_HEADS * ts, HEAD_DIM), BF16), pltpu.VMEM((A_HEADS * ts, 1), F32),
                        pltpu.VMEM((A_HEADS * ts, 1), F32), pltpu.VMEM((A_HEADS * ts, HEAD_DIM), F32)],
    )
    return pl.pallas_call(
        functools.partial(_dsa_att_body, pp=pp, sub=min(sub, pp), nj=nj, ts=ts),
        grid_spec=grid_spec,
        out_shape=jax.ShapeDtypeStruct((nb * ts, wq), F32),
        compiler_params=pltpu.CompilerParams(
            dimension_semantics=("parallel", "arbitrary"), vmem_limit_bytes=VMEM_LIMIT),
        name="dsa_sample_attend",
    )(pt_flat, proj, ka32, proj, qn.reshape(1, HEAD_DIM), mask_past, mask_new, bias_s, bias_s,
      *([pool_k] * pp), *([pool_v] * pp))


def _mla_sample_body(pt_ref, qlat_ref, qpe_ref, cnew_ref, rnew_ref, wkt_ref, wv_ref, *rest, pp, sub, nj, ts):
    cpages, rpages = rest[:pp], rest[pp:2 * pp]
    o_ref, m_ref, l_ref, acc_ref = rest[2 * pp:]
    j = pl.program_id(1)
    dq = NOPE_DIM + ROPE_DIM
    nrow = D_HEADS * ts

    @pl.when(j == 0)
    def _():
        m_ref[...] = jnp.full(m_ref.shape, MASK_NEG, F32)
        l_ref[...] = jnp.zeros(l_ref.shape, F32)
        acc_ref[...] = jnp.zeros(acc_ref.shape, F32)

    def update(c, rt, sel):
        n = c.shape[0]
        cb = c.astype(BF16)
        both = _dot_nt(jnp.concatenate([wkt_ref[...], qlat_ref[...]], axis=0), cb)
        kt = both[:D_HEADS * NOPE_DIM]
        ss = jnp.sum((kt * kt).reshape(D_HEADS, NOPE_DIM, n), axis=1)
        rss = jnp.sum(rt * rt, axis=0, keepdims=True)
        rinv = lax.rsqrt((ss + rss) * (1.0 / dq) + EPS) * (dq ** -0.5)
        lg = both[D_HEADS * NOPE_DIM:] + jnp.dot(qpe_ref[...], rt.astype(BF16),
                                                 preferred_element_type=F32)
        lg = (lg.reshape(D_HEADS, ts, n) * rinv[:, None, :]).reshape(nrow, n)
        if sel is not None:
            lg = jnp.where(sel, lg, MASK_NEG)
        m = jnp.max(lg, axis=-1, keepdims=True)
        p = jnp.exp(lg - m)
        if sel is not None:
            p = jnp.where(sel, p, 0.0)
        return m, jnp.sum(p, axis=-1, keepdims=True), jnp.dot(p.astype(BF16), cb, preferred_element_type=F32)

    def merge(parts):
        m_old = m_ref[...]
        m_new = m_old
        for m, _, _ in parts:
            m_new = jnp.maximum(m_new, m)
        alpha = jnp.exp(m_old - m_new)
        l_new = alpha * l_ref[...]
        acc = alpha * acc_ref[...]
        for m, l, pv in parts:
            a = jnp.exp(m - m_new)
            l_new = l_new + a * l
            acc = acc + a * pv
        m_ref[...] = m_new
        l_ref[...] = l_new
        acc_ref[...] = acc

    parts = []
    for s0 in range(0, pp, sub):
        c = jnp.concatenate([p[...] for p in cpages[s0:s0 + sub]], axis=0)
        rt = jnp.concatenate([p[...] for p in rpages[s0:s0 + sub]], axis=1)
        parts.append(update(c, rt, None))
    merge(parts)

    @pl.when(j == nj - 1)
    def _():
        lane = lax.broadcasted_iota(jnp.int32, (D_HEADS, ts, LANES), 2).reshape(nrow, LANES)
        row_t = lax.broadcasted_iota(jnp.int32, (D_HEADS, ts, LANES), 1).reshape(nrow, LANES)
        merge([update(cnew_ref[...], rnew_ref[...], lane <= row_t)])
        lat = (acc_ref[...] / l_ref[...]).astype(BF16)
        o_ref[...] = jnp.concatenate(
            [jnp.dot(lat[ts * h:ts * (h + 1)], wv_ref[h], preferred_element_type=F32) for h in range(D_HEADS)],
            axis=1)


def mla_sample_attend(qlat, qpe, cnew, rnew_t, wkt, wv, pool_ckv, pool_kpe_t, pt_flat, layer, *, nb, ts, n_pages,
                      pp=32, sub=16):
    pp = min(pp, n_pages)
    nj = n_pages // pp
    nrow = D_HEADS * ts

    def cspec(r):
        return pl.BlockSpec((None, None, PAGE_SIZE, KV_LORA), lambda b, j, pt: (layer, pt[b * n_pages + j * pp + r], 0, 0))

    def rspec(r):
        return pl.BlockSpec((None, None, ROPE_DIM, PAGE_SIZE), lambda b, j, pt: (layer, pt[b * n_pages + j * pp + r], 0, 0))

    grid_spec = pltpu.PrefetchScalarGridSpec(
        num_scalar_prefetch=1,
        grid=(nb, nj),
        in_specs=[
            pl.BlockSpec((None, nrow, KV_LORA), lambda b, j, pt: (b, 0, 0)),
            pl.BlockSpec((None, nrow, ROPE_DIM), lambda b, j, pt: (b, 0, 0)),
            pl.BlockSpec((None, LANES, KV_LORA), lambda b, j, pt: (b, 0, 0)),
            pl.BlockSpec((None, ROPE_DIM, LANES), lambda b, j, pt: (b, 0, 0)),
            pl.BlockSpec((D_HEADS * NOPE_DIM, KV_LORA), lambda b, j, pt: (0, 0)),
            pl.BlockSpec((D_HEADS, KV_LORA, D_VDIM), lambda b, j, pt: (0, 0, 0)),
        ] + [cspec(r) for r in range(pp)] + [rspec(r) for r in range(pp)],
        out_specs=pl.BlockSpec((ts, D_HEADS * D_VDIM), lambda b, j, pt: (b, 0)),
        scratch_shapes=[pltpu.VMEM((nrow, 1), F32), pltpu.VMEM((nrow, 1), F32), pltpu.VMEM((nrow, KV_LORA), F32)],
    )
    return pl.pallas_call(
        functools.partial(_mla_sample_body, pp=pp, sub=min(sub, pp), nj=nj, ts=ts),
        grid_spec=grid_spec,
        out_shape=jax.ShapeDtypeStruct((nb * ts, D_HEADS * D_VDIM), F32),
        compiler_params=pltpu.CompilerParams(
            dimension_semantics=("parallel", "arbitrary"), vmem_limit_bytes=VMEM_LIMIT),
        name="mla_sample",
    )(pt_flat, qlat, qpe, cnew, rnew_t, wkt, wv, *([pool_ckv] * pp), *([pool_kpe_t] * pp))


def _mem_attn_body(q_ref, k_ref, v_ref, qn_ref, o_ref, *, nb, tq, n_mem, head_rows):
    def head(ref, b, h):
        if head_rows:
            return ref[b, pl.ds(h, n_mem, stride=MEM_HEADS), :].astype(BF16)
        return ref[b, :, LANES * h:LANES * (h + 1)].astype(BF16)

    for b in range(nb):
        outs = []
        for h in range(MEM_HEADS):
            q = q_ref[tq * b:tq * (b + 1), LANES * h:LANES * (h + 1)]
            q = q * lax.rsqrt(jnp.mean(q * q, axis=-1, keepdims=True) + EPS) * qn_ref[...]
            lg = _dot_nt(q.astype(BF16), head(k_ref, b, h)) * (HEAD_DIM ** -0.5)
            p = _softmax_rows(lg)
            outs.append(jnp.dot(p.astype(BF16), head(v_ref, b, h), preferred_element_type=F32))
        o_ref[tq * b:tq * (b + 1), :] = jnp.concatenate(outs, axis=1)


def mem_attend(q, mk, mv, qn, *, row0, n_batch, tq, nb, n_mem, head_rows, k_index, v_index):
    w = MEM_HEADS * HEAD_DIM
    rb0 = row0 // (nb * tq)
    slab = (n_mem * MEM_HEADS, HEAD_DIM) if head_rows else (n_mem, w)
    kblock = (None,) * (mk.ndim - 3) + (nb,) + slab
    vblock = (None,) * (mv.ndim - 3) + (nb,) + slab
    return pl.pallas_call(
        functools.partial(_mem_attn_body, nb=nb, tq=tq, n_mem=n_mem, head_rows=head_rows),
        grid=(n_batch // nb,),
        in_specs=[pl.BlockSpec((nb * tq, w), lambda i: (rb0 + i, 0)),
                  pl.BlockSpec(kblock, k_index),
                  pl.BlockSpec(vblock, v_index),
                  pl.BlockSpec((1, HEAD_DIM), lambda i: (0, 0))],
        out_specs=pl.BlockSpec((nb * tq, w), lambda i: (i, 0)),
        out_shape=jax.ShapeDtypeStruct((n_batch * tq, w), F32),
        compiler_params=pltpu.CompilerParams(dimension_semantics=("parallel",), vmem_limit_bytes=VMEM_LIMIT),
        name="mem_attend",
    )(q, mk, mv, qn.reshape(1, HEAD_DIM))


def _causal_attn_body(q_ref, k_ref, v_ref, o_ref, *, qb, kw, i0, scale):
    i = i0 + pl.program_id(2)
    lg = _dot_nt(q_ref[...], k_ref[...]) * scale
    q_pos = i * qb + lax.broadcasted_iota(jnp.int32, (qb, kw), 0)
    k_pos = lax.broadcasted_iota(jnp.int32, (qb, kw), 1)
    p = _softmax_rows(jnp.where(k_pos <= q_pos, lg, MASK_NEG))
    o_ref[...] = jnp.dot(p.astype(BF16), v_ref[...], preferred_element_type=F32)


def causal_attend(q, k, v, *, nb, t, heads, dv, scale, qb=512, splits=2):
    nq = t // qb
    dqk = q.shape[1] // heads
    per = -(-nq // splits)
    outs = []
    for i0 in range(0, nq, per):
        ni = min(per, nq - i0)
        kw = min(t, (i0 + ni) * qb)
        assert t % kw == 0, (t, kw)
        o = pl.pallas_call(
            functools.partial(_causal_attn_body, qb=qb, kw=kw, i0=i0, scale=scale),
            grid=(nb, heads, ni),
            in_specs=[pl.BlockSpec((qb, dqk), lambda b, h, i, i0=i0: (b * nq + i0 + i, h)),
                      pl.BlockSpec((kw, dqk), lambda b, h, i, kw=kw: (b * (t // kw), h)),
                      pl.BlockSpec((kw, dv), lambda b, h, i, kw=kw: (b * (t // kw), h))],
            out_specs=pl.BlockSpec((qb, dv), lambda b, h, i, ni=ni: (b * ni + i, h)),
            out_shape=jax.ShapeDtypeStruct((nb * ni * qb, heads * dv), F32),
            compiler_params=pltpu.CompilerParams(
                dimension_semantics=("parallel", "parallel", "arbitrary"), vmem_limit_bytes=VMEM_LIMIT),
            name="causal_attend",
        )(q, k, v)
        outs.append(o.reshape(nb, ni * qb, heads * dv))
    return jnp.concatenate(outs, axis=1).reshape(nb * t, heads * dv)


def _cumsum_rows(x):
    c = x.shape[0]
    row = lax.broadcasted_iota(jnp.int32, x.shape, 0)
    sh = 1
    while sh < c:
        x = x + jnp.where(row >= sh, pltpu.roll(x, sh, axis=0), 0.0)
        sh *= 2
    return x


def _log_sigmoid(x):
    return jnp.minimum(x, 0.0) - jnp.log1p(jnp.exp(-jnp.abs(x)))


def _sigmoid(x):
    return 1.0 / (1.0 + jnp.exp(-x))


def _hgrn2_run(q_ref, f_ref, v_ref, g_ref, lb_ref, on, st_ref, o_ref, *, nh, t, c):
    row = lax.broadcasted_iota(jnp.int32, (c, 1), 0)

    def step(i, carry):
        r0 = pl.multiple_of(i * c, c)
        for h in range(nh):
            sl = slice(LANES * h, LANES * (h + 1))
            lb = lb_ref[h]
            fpre = f_ref[pl.ds(r0, c), sl]
            a = jnp.log(jnp.maximum(lb, LB_FLOOR))
            b = jnp.log1p(-lb) + _log_sigmoid(fpre)
            lf = jnp.maximum(a, b) + jnp.log1p(jnp.exp(-jnp.abs(a - b)))
            kk = (1.0 - lb) * _sigmoid(-fpre)
            qpre = q_ref[pl.ds(r0, c), sl]
            qv = qpre * _sigmoid(qpre)
            vv = v_ref[pl.ds(r0, c), sl]
            cb = _cumsum_rows(lf)
            st = st_ref[h]
            o = _dot_nt((qv * jnp.exp(cb)).astype(BF16), st.astype(BF16))
            for s in range(c):
                dec = jnp.exp(jnp.minimum(cb - cb[s:s + 1, :], 0.0))
                col = jnp.sum(qv * dec * kk[s:s + 1, :], axis=-1, keepdims=True)
                o = o + jnp.where(row >= s, col, 0.0) * vv[s:s + 1, :]
            c_last = cb[c - 1:c, :]
            kd = (kk * jnp.exp(c_last - cb)).astype(BF16)
            st_ref[h] = st * jnp.exp(c_last) + lax.dot_general(vv.astype(BF16), kd, (((0,), (0,)), ((), ())),
                                                               preferred_element_type=F32)
            gpre = g_ref[pl.ds(r0, c), sl]
            o = o * lax.rsqrt(jnp.mean(o * o, axis=-1, keepdims=True) + EPS) * on * (gpre * _sigmoid(gpre))
            o_ref[pl.ds(r0, c), sl] = o
        return carry

    lax.fori_loop(0, t // c, step, 0)


def _hgrn2_prompt_body(q_ref, f_ref, v_ref, g_ref, lb_ref, on_ref, o_ref, s_ref, st_ref, *, nh, t, c):
    st_ref[...] = jnp.zeros(st_ref.shape, F32)
    _hgrn2_run(q_ref, f_ref, v_ref, g_ref, lb_ref, on_ref[...], st_ref, o_ref, nh=nh, t=t, c=c)
    for h in range(nh):
        s_ref[h] = st_ref[h].T


def hgrn2_prompt(proj, lb, on, *, nb, t, nh=4):
    c = math.gcd(16, t)
    w = nh * LANES
    col = lambda c0: (lambda b, j: (b, c0 // w + j))
    return pl.pallas_call(
        functools.partial(_hgrn2_prompt_body, nh=nh, t=t, c=c),
        grid=(nb, B_HEADS // nh),
        in_specs=[pl.BlockSpec((t, w), col(EV_QB)), pl.BlockSpec((t, w), col(EV_FB)),
                  pl.BlockSpec((t, w), col(EV_IB)), pl.BlockSpec((t, w), col(EV_GB)),
                  pl.BlockSpec((nh, 1, B_DK), lambda b, j: (j, 0, 0)),
                  pl.BlockSpec((1, B_DV), lambda b, j: (0, 0))],
        out_specs=[pl.BlockSpec((t, w), lambda b, j: (b, j)),
                   pl.BlockSpec((None, nh, B_DK, B_DV), lambda b, j: (b, j, 0, 0))],
        out_shape=[jax.ShapeDtypeStruct((nb * t, B_HEADS * B_DV), F32),
                   jax.ShapeDtypeStruct((nb, B_HEADS, B_DK, B_DV), F32)],
        scratch_shapes=[pltpu.VMEM((nh, B_DV, B_DK), F32)],
        compiler_params=pltpu.CompilerParams(
            dimension_semantics=("parallel", "parallel"), vmem_limit_bytes=VMEM_LIMIT),
        name="hgrn2_prompt",
    )(proj, proj, proj, proj, lb.reshape(B_HEADS, 1, B_DK), on.reshape(1, B_DV))


def _hgrn2_sample_body(q_ref, f_ref, v_ref, g_ref, lb_ref, on_ref, s0_ref, o_ref, s_ref, st_ref, *, t):
    for h in range(B_HEADS):
        st_ref[h] = s0_ref[h].T
    _hgrn2_run(q_ref, f_ref, v_ref, g_ref, lb_ref, on_ref[...], st_ref, o_ref, nh=B_HEADS, t=t, c=t)
    for h in range(B_HEADS):
        s_ref[h] = st_ref[h].T


def hgrn2_sample(proj, lb, on, s0, *, row0, nb, t):
    rb0 = row0 // t
    w = B_HEADS * LANES
    col = lambda c0: (lambda b: (rb0 + b, c0 // w))
    return pl.pallas_call(
        functools.partial(_hgrn2_sample_body, t=t),
        grid=(nb,),
        in_specs=[pl.BlockSpec((t, w), col(EV_QB)), pl.BlockSpec((t, w), col(EV_FB)),
                  pl.BlockSpec((t, w), col(EV_IB)), pl.BlockSpec((t, w), col(EV_GB)),
                  pl.BlockSpec((B_HEADS, 1, B_DK), lambda b: (0, 0, 0)),
                  pl.BlockSpec((1, B_DV), lambda b: (0, 0)),
                  pl.BlockSpec((None, B_HEADS, B_DK, B_DV), lambda b: (b, 0, 0, 0))],
        out_specs=[pl.BlockSpec((t, w), lambda b: (b, 0)),
                   pl.BlockSpec((None, B_HEADS, B_DK, B_DV), lambda b: (b, 0, 0, 0))],
        out_shape=[jax.ShapeDtypeStruct((nb * t, w), F32),
                   jax.ShapeDtypeStruct((nb, B_HEADS, B_DK, B_DV), F32)],
        scratch_shapes=[pltpu.VMEM((B_HEADS, B_DV, B_DK), F32)],
        compiler_params=pltpu.CompilerParams(dimension_semantics=("parallel",), vmem_limit_bytes=VMEM_LIMIT),
        name="hgrn2_sample",
    )(proj, proj, proj, proj, lb.reshape(B_HEADS, 1, B_DK), on.reshape(1, B_DV), s0)


def _mlstm_chunk(qm, km_h, v, ic_row, ic_col, cb_col, cb_row, ct, n, m, c):
    r = lax.broadcasted_iota(jnp.int32, (c, c), 0)
    s = lax.broadcasted_iota(jnp.int32, (c, c), 1)
    dmat = jnp.where(r >= s, cb_col - cb_row + ic_row, MASK_NEG)
    m_state = cb_col + m
    m_t = jnp.maximum(m_state, jnp.max(dmat, axis=-1, keepdims=True))
    w = jnp.exp(dmat - m_t)
    w0 = jnp.exp(m_state - m_t)
    qb16 = qm.astype(BF16)
    qk = _dot_nt(qb16, km_h.astype(BF16)) * w
    num = (jnp.dot(qk.astype(BF16), v.astype(BF16), preferred_element_type=F32)
           + w0 * jnp.dot(qb16, ct.astype(BF16), preferred_element_type=F32))
    den = jnp.sum(qk, axis=-1, keepdims=True) + w0 * jnp.sum(qm * n, axis=-1, keepdims=True)
    hc = num / jnp.maximum(jnp.abs(den), jnp.exp(-m_t))
    m_last = m_t[c - 1:c, :]
    cb_last = cb_col[c - 1:c, :]
    ws = jnp.exp(cb_last - cb_col + ic_col - m_last)
    fs = jnp.exp(cb_last + m - m_last)
    ct = fs * ct + lax.dot_general(km_h.astype(BF16), (v * ws).astype(BF16), (((0,), (0,)), ((), ())),
                                   preferred_element_type=F32)
    n = fs * n + jnp.sum(ws * km_h, axis=0, keepdims=True)
    return hc, ct, n, m_last


def _mlstm_seq(q_ref, k_ref, v_ref, og_ref, g_ref, gb_ref, on, npairs, states, o_ref, *, t, c):
    lane = lax.broadcasted_iota(jnp.int32, (c, LANES), 1)

    def step(i, carry):
        r0 = pl.multiple_of(i * c, c)
        new = []
        for p in range(npairs):
            psl = slice(LANES * p, LANES * (p + 1))
            g = g_ref[pl.ds(r0, c), psl] + gb_ref[p]
            cb = _cumsum_rows(_log_sigmoid(g))
            gt = g.T
            cbt = cb.T
            q = q_ref[pl.ds(r0, c), psl]
            k = k_ref[pl.ds(r0, c), psl] * (C_DK ** -0.5)
            for half in range(2):
                ct, n, m = carry[2 * p + half]
                keep = (lane < C_DK) if half == 0 else (lane >= C_DK)
                vsl = slice(C_DV * (2 * p + half), C_DV * (2 * p + half + 1))
                hc, ct, n, m = _mlstm_chunk(
                    jnp.where(keep, q, 0.0), jnp.where(keep, k, 0.0), v_ref[pl.ds(r0, c), vsl],
                    gt[half:half + 1, :], g[:, half:half + 1], cb[:, 2 + half:3 + half], cbt[2 + half:3 + half, :],
                    ct, n, m, c)
                og = og_ref[pl.ds(r0, c), vsl]
                hc = hc * lax.rsqrt(jnp.mean(hc * hc, axis=-1, keepdims=True) + EPS) * on * _sigmoid(og)
                o_ref[pl.ds(r0, c), vsl] = hc
                new.append((ct, n, m))
        return tuple(new)

    return lax.fori_loop(0, t // c, step, tuple(states))


def _mlstm_body(*refs, t, c, npairs, has_state):
    q_ref, k_ref, v_ref, og_ref, g_ref, gb_ref, on_ref = refs[:7]
    refs = refs[7:]
    if has_state:
        c0_ref, n0_ref, m0_ref = refs[:3]
        refs = refs[3:]
    o_ref, c_ref, n_ref, m_ref = refs
    states = []
    for h in range(2 * npairs):
        half = h % 2
        if has_state:
            ct_h = c0_ref[h]
            z = jnp.zeros((C_DK, C_DV), F32)
            ct = jnp.concatenate([ct_h, z] if half == 0 else [z, ct_h], axis=0)
            n_h = n0_ref[h:h + 1, :]
            zn = jnp.zeros((1, C_DK), F32)
            n = jnp.concatenate([n_h, zn] if half == 0 else [zn, n_h], axis=1)
            m = m0_ref[:, h:h + 1]
        else:
            ct, n, m = jnp.zeros((2 * C_DK, C_DV), F32), jnp.zeros((1, 2 * C_DK), F32), jnp.zeros((1, 1), F32)
        states.append((ct, n, m))
    out = _mlstm_seq(q_ref, k_ref, v_ref, og_ref, g_ref, gb_ref, on_ref[...], npairs, states, o_ref, t=t, c=c)
    for h, (ct, n, m) in enumerate(out):
        rows = slice(C_DK * (h % 2), C_DK * (h % 2 + 1))
        c_ref[h] = ct[rows, :]
        n_ref[h] = n[:, rows]
        m_ref[h] = jnp.broadcast_to(m, (1, LANES))


def mlstm(proj, gb, on, state, *, row0, nb, t, npairs):
    c = math.gcd(CHUNK, t)
    rb0 = row0 // t
    npg = (C_HEADS // 2) // npairs
    hp = 2 * npairs
    qw, vw = npairs * LANES, npairs * 2 * C_DV
    in_specs = [
        pl.BlockSpec((t, qw), lambda b, j: (rb0 + b, OD_QC // qw + j)),
        pl.BlockSpec((t, qw), lambda b, j: (rb0 + b, OD_KC // qw + j)),
        pl.BlockSpec((t, vw), lambda b, j: (rb0 + b, OD_VC // vw + j)),
        pl.BlockSpec((t, vw), lambda b, j: (rb0 + b, OD_OC // vw + j)),
        pl.BlockSpec((t, qw), lambda b, j: (rb0 + b, OD_G // qw + j)),
        pl.BlockSpec((npairs, 1, LANES), lambda b, j: (j, 0, 0)),
        pl.BlockSpec((1, C_DV), lambda b, j: (0, 0)),
    ]
    args = [proj, proj, proj, proj, proj, gb, on.reshape(1, C_DV)]
    if state is not None:
        in_specs += [pl.BlockSpec((None, hp, C_DK, C_DV), lambda b, j: (b, j, 0, 0)),
                     pl.BlockSpec((None, hp, C_DK), lambda b, j: (b, j, 0)),
                     pl.BlockSpec((None, 1, hp), lambda b, j: (b, 0, j))]
        args += list(state)
    return pl.pallas_call(
        functools.partial(_mlstm_body, t=t, c=c, npairs=npairs, has_state=state is not None),
        grid=(nb, npg),
        in_specs=in_specs,
        out_specs=[pl.BlockSpec((t, vw), lambda b, j: (b, j)),
                   pl.BlockSpec((None, hp, C_DK, C_DV), lambda b, j: (b, j, 0, 0)),
                   pl.BlockSpec((None, hp, 1, C_DK), lambda b, j: (b, j, 0, 0)),
                   pl.BlockSpec((None, hp, 1, LANES), lambda b, j: (b, j, 0, 0))],
        out_shape=[jax.ShapeDtypeStruct((nb * t, C_HEADS * C_DV), F32),
                   jax.ShapeDtypeStruct((nb, C_HEADS, C_DK, C_DV), F32),
                   jax.ShapeDtypeStruct((nb, C_HEADS, 1, C_DK), F32),
                   jax.ShapeDtypeStruct((nb, C_HEADS, 1, LANES), F32)],
        compiler_params=pltpu.CompilerParams(
            dimension_semantics=("parallel", "parallel"), vmem_limit_bytes=VMEM_LIMIT),
        name="mlstm",
    )(*args)


def rms_norm(x, g):
    xf = x.astype(F32)
    y = xf * lax.rsqrt(jnp.mean(xf * xf, axis=-1, keepdims=True) + EPS)
    return (y * g.astype(F32)).astype(x.dtype)


def split_cols(h, sizes):
    cuts = [int(c) for c in np.cumsum(sizes)[:-1]]
    return jnp.split(h, cuts, axis=-1)


def to_blocks(x, nb):
    b, t = x.shape[:2]
    return jnp.swapaxes(x.reshape((b, nb, t // nb) + x.shape[2:]), 0, 1)


def from_blocks(x):
    x = jnp.swapaxes(x, 0, 1)
    return x.reshape((x.shape[0], x.shape[1] * x.shape[2]) + x.shape[3:])


def rel_bucket(rel):
    n = jnp.maximum(rel, 0)
    max_exact = REL_BUCKETS // 2
    nf = jnp.maximum(n, 1).astype(F32)
    large = max_exact + (jnp.log(nf / max_exact) / math.log(REL_MAX_DIST / max_exact)
                         * (REL_BUCKETS - max_exact)).astype(jnp.int32)
    return jnp.where(n < max_exact, n, jnp.minimum(large, REL_BUCKETS - 1))


def rope(x, pos):
    half = ROPE_DIM // 2
    inv = ROPE_THETA ** (-jnp.arange(half, dtype=F32) / half)
    ang = pos.astype(F32)[:, None] * inv[None, :]
    shp = (1, pos.shape[0]) + (1,) * (x.ndim - 3) + (half,)
    cos, sin = jnp.cos(ang).reshape(shp), jnp.sin(ang).reshape(shp)
    xf = x.astype(F32)
    x1, x2 = xf[..., :half], xf[..., half:]
    return jnp.concatenate([x1 * cos - x2 * sin, x2 * cos + x1 * sin], axis=-1).astype(x.dtype)


def dsa_select(q_idx, w_idx, k_idx, q_pos, k_pos, n_sel):
    s = jnp.einsum('bthd,bsd->bths', q_idx.astype(F32), k_idx.astype(F32))
    score = jnp.einsum('bths,bth->bts', jax.nn.relu(s), w_idx.astype(F32))
    allowed = k_pos[None, None, :] <= q_pos[None, :, None]
    score = jnp.where(allowed, score, MASK_NEG)
    _, sel = lax.top_k(score, n_sel)
    valid = sel <= q_pos[None, :, None]
    return sel, valid


def dsa_attend(q, kg, vg, q_pos, sel, valid, rel_bias):
    b, t, h, dh = q.shape
    g = h // A_KV_HEADS
    n_sel = sel.shape[-1]
    qg = q.reshape(b, t, A_KV_HEADS, g, dh)
    lg = jnp.einsum('btkgd,btskd->btkgs', qg, kg).astype(F32) * (dh ** -0.5)
    bias = rel_bias.astype(F32)[rel_bucket(q_pos[None, :, None] - sel)]
    lg = lg + jnp.moveaxis(bias, -1, 2).reshape(b, t, A_KV_HEADS, g, n_sel)
    lg = jnp.where(valid[:, :, None, None, :], lg, MASK_NEG)
    pr = jax.nn.softmax(lg, axis=-1).astype(vg.dtype)
    return jnp.einsum('btkgs,btskd->btkgd', pr, vg).reshape(b, t, h, dh)


def dsa_prompt(q, k, v, q_idx, w_idx, k_idx, rel_bias):
    b, t = q.shape[:2]
    n_sel = min(TOPK_MAX, t // 4)
    qb = math.gcd(Q_BLOCK, t)
    nb = t // qb
    k_pos = jnp.arange(t)
    bidx = jnp.arange(b)[:, None, None]

    def block(args):
        q_b, qi_b, wi_b, qp = args
        sel, valid = dsa_select(qi_b, wi_b, k_idx, qp, k_pos, n_sel)
        return dsa_attend(q_b, k[bidx, sel], v[bidx, sel], qp, sel, valid, rel_bias)

    out = lax.map(block, (to_blocks(q, nb), to_blocks(q_idx, nb), to_blocks(w_idx, nb), k_pos.reshape(nb, qb)))
    return from_blocks(out)


def gather_paged_rows(pool, layer, new_rows, page_table, sel, past):
    b = sel.shape[0]
    in_past = sel < past
    ps = jnp.minimum(sel, past - 1)
    phys = jnp.take_along_axis(page_table, (ps // PAGE_SIZE).reshape(b, -1), axis=1).reshape(sel.shape)
    past_rows = pool[layer, phys, ps % PAGE_SIZE]
    new_idx = jnp.clip(sel - past, 0, new_rows.shape[1] - 1)
    cur_rows = new_rows[jnp.arange(b)[:, None, None], new_idx]
    cond = in_past.reshape(in_past.shape + (1,) * (new_rows.ndim - 2))
    return jnp.where(cond, past_rows, cur_rows)


def dsa_sample(q, k_new, v_new, q_idx, w_idx, kidx_new, pool_k, pool_v, pool_kidx, layer, page_table, rel_bias):
    b, t = q.shape[:2]
    past = page_table.shape[1] * PAGE_SIZE
    n_sel = min(TOPK_MAX, (past + t) // 4)
    kidx_past = pool_kidx[layer, page_table].reshape(b, past, IDX_DIM)
    kidx_all = jnp.concatenate([kidx_past, kidx_new.astype(kidx_past.dtype)], axis=1)
    q_pos = past + jnp.arange(t)
    sel, valid = dsa_select(q_idx, w_idx, kidx_all, q_pos, jnp.arange(past + t), n_sel)
    kg = gather_paged_rows(pool_k, layer, k_new, page_table, sel, past)
    vg = gather_paged_rows(pool_v, layer, v_new, page_table, sel, past)
    return dsa_attend(q, kg, vg, q_pos, sel, valid, rel_bias)


def gla_chunked(q, k, v, log_f, s0):
    b, t, h, dk = q.shape
    c = math.gcd(CHUNK, t)
    nc = t // c
    tri = jnp.tril(jnp.ones((c, c), dtype=bool))

    def step(s, inp):
        qc, kc, vc, lf = inp
        cb = jnp.cumsum(lf, axis=1)
        diff = jnp.where(tri[None, :, :, None, None], cb[:, :, None] - cb[:, None, :], MASK_NEG)
        attn = jnp.einsum('bthd,btshd->bhts', qc, jnp.exp(diff) * kc[:, None])
        o = jnp.einsum('bhts,bshv->bthv', attn, vc) + jnp.einsum('bthd,bhdv->bthv', qc * jnp.exp(cb), s)
        c_last = cb[:, -1]
        s = jnp.exp(c_last)[..., None] * s + jnp.einsum('bshd,bshv->bhdv', kc * jnp.exp(c_last[:, None] - cb), vc)
        return s, o

    xs = tuple(to_blocks(a.astype(F32), nc) for a in (q, k, v, log_f))
    s, o = lax.scan(step, s0.astype(F32), xs)
    return from_blocks(o).astype(v.dtype), s


def mlstm_chunked(q, k, v, i_pre, f_pre, c0, n0, m0):
    b, t, h, dk = q.shape
    c = math.gcd(CHUNK, t)
    nc = t // c
    tri = jnp.tril(jnp.ones((c, c), dtype=bool))
    log_f = jax.nn.log_sigmoid(f_pre.astype(F32))

    def step(carry, inp):
        cm, n, m = carry
        qc, kc, vc, ic, lf = inp
        cb = jnp.cumsum(lf, axis=1)
        dmat = jnp.where(tri[None, :, :, None], cb[:, :, None] - cb[:, None] + ic[:, None], MASK_NEG)
        m_state = cb + m[:, None]
        m_t = jnp.maximum(m_state, jnp.max(dmat, axis=2))
        w = jnp.exp(dmat - m_t[:, :, None])
        w0 = jnp.exp(m_state - m_t)
        qk = jnp.einsum('bthd,bshd->btsh', qc, kc) * w
        num = jnp.einsum('btsh,bshv->bthv', qk, vc) + w0[..., None] * jnp.einsum('bthd,bhvd->bthv', qc, cm)
        den = jnp.sum(qk, axis=2) + w0 * jnp.einsum('bthd,bhd->bth', qc, n)
        hc = num / jnp.maximum(jnp.abs(den), jnp.exp(-m_t))[..., None]
        m_last = m_t[:, -1]
        ws = jnp.exp(cb[:, -1:] - cb + ic - m_last[:, None])
        fs = jnp.exp(cb[:, -1] + m - m_last)
        cm = fs[..., None, None] * cm + jnp.einsum('bshv,bshd->bhvd', vc * ws[..., None], kc)
        n = fs[..., None] * n + jnp.einsum('bsh,bshd->bhd', ws, kc)
        return (cm, n, m_last), hc

    xs = tuple(to_blocks(a.astype(F32), nc) for a in (q, k, v, i_pre, log_f))
    (cm, n, m), hs = lax.scan(step, (c0.astype(F32), n0.astype(F32), m0.astype(F32)), xs)
    return from_blocks(hs).astype(v.dtype), (cm, n, m)


def mla_kv(ckv, kpe, w_kvb, kn):
    b, l, _ = ckv.shape
    kv = (ckv @ w_kvb).reshape(b, l, D_HEADS, NOPE_DIM + D_VDIM)
    k_nope, v = kv[..., :NOPE_DIM], kv[..., NOPE_DIM:]
    k_pe = jnp.broadcast_to(kpe[:, :, None, :], (b, l, D_HEADS, ROPE_DIM)).astype(k_nope.dtype)
    return rms_norm(jnp.concatenate([k_nope, k_pe], axis=-1), kn), v


def causal_attn(q, k, v, q_pos, k_pos):
    b, t, h, dq = q.shape
    scale = dq ** -0.5

    def attend(args):
        q_b, qp = args
        lg = jnp.einsum('bthd,bshd->bhts', q_b, k).astype(F32) * scale
        lg = jnp.where(k_pos[None, None, None, :] <= qp[None, None, :, None], lg, MASK_NEG)
        pr = jax.nn.softmax(lg, axis=-1).astype(v.dtype)
        return jnp.einsum('bhts,bshd->bthd', pr, v)

    qb = math.gcd(Q_BLOCK, t)
    nb = t // qb
    if nb == 1:
        return attend((q, q_pos))
    return from_blocks(lax.map(attend, (to_blocks(q, nb), q_pos.reshape(nb, qb))))


def mla_sample(qd, ckv_new, kpe_new, pool_ckv, pool_kpe, layer, page_table, w_kvb, kn):
    b, t = qd.shape[:2]
    past = page_table.shape[1] * PAGE_SIZE
    q_pos = past + jnp.arange(t)
    k_pos = jnp.arange(past + t)

    def one(args):
        q1, c1, r1, pt = args
        c_all = jnp.concatenate([pool_ckv[layer, pt].reshape(past, KV_LORA), c1.astype(pool_ckv.dtype)], axis=0)[None]
        r_all = jnp.concatenate([pool_kpe[layer, pt].reshape(past, ROPE_DIM), r1.astype(pool_kpe.dtype)], axis=0)[None]
        k, v = mla_kv(c_all, r_all, w_kvb, kn)
        return causal_attn(q1[None], k, v, q_pos, k_pos)[0]

    return lax.map(one, (qd, ckv_new, kpe_new, page_table))


def mem_attend_core(q, mk, mv, qn):
    b, t, _ = q.shape
    q = rms_norm(q.reshape(b, t, MEM_HEADS, HEAD_DIM), qn)
    lg = jnp.einsum('bthd,bshd->bhts', q, mk.astype(q.dtype)).astype(F32) * (HEAD_DIM ** -0.5)
    pr = jax.nn.softmax(lg, axis=-1).astype(q.dtype)
    o = jnp.einsum('bhts,bshd->bthd', pr, mv.astype(q.dtype))
    return o.reshape(b, t, MEM_HEADS * HEAD_DIM)


def hgrn2_mixer(proj, s0, b_on, lower_bound):
    b, t, _ = proj.shape
    qb, fb, ib, gb = (proj[..., c:c + B_HEADS * B_DK] for c in (EV_QB, EV_FB, EV_IB, EV_GB))
    lb = lower_bound.reshape(B_HEADS, B_DK)
    f_pre = fb.reshape(b, t, B_HEADS, B_DK).astype(F32)
    log_f = jnp.logaddexp(jnp.log(jnp.maximum(lb, LB_FLOOR)), jnp.log1p(-lb) + jax.nn.log_sigmoid(f_pre))
    k_b = (1.0 - lb) * jax.nn.sigmoid(-f_pre)
    q_b = jax.nn.silu(qb.reshape(b, t, B_HEADS, B_DK))
    ob, s_new = gla_chunked(q_b, k_b, ib.reshape(b, t, B_HEADS, B_DV), log_f, s0)
    ob = rms_norm(ob, b_on) * jax.nn.silu(gb.reshape(b, t, B_HEADS, B_DV))
    return ob.reshape(b, t, -1), s_new


def dsa_bias_tables(rel_bias, ts, past):
    table = rel_bias.astype(F32)[rel_bucket(jnp.arange(2 * LANES))]
    r = np.arange(LANES)[:, None]
    c = np.arange(LANES)[None, :]
    diag = table[np.clip(r - c, 0, 2 * LANES - 1)]
    prev = table[LANES + r - c]
    far = jnp.broadcast_to(table[2 * LANES - 1], (LANES, LANES, A_HEADS))
    tiles = jnp.moveaxis(jnp.stack([diag, prev, far]), -1, 0)
    n_far = max(past - 2 * LANES, 0)
    rel = past + np.arange(ts)[:, None] - np.arange(n_far, past + LANES)[None, :]
    sample = jnp.concatenate([jnp.broadcast_to(table[2 * LANES - 1], (ts, n_far, A_HEADS)),
                              table[np.clip(rel, 0, 2 * LANES - 1)]], axis=1)
    sample = jnp.moveaxis(sample, -1, 0)
    return tiles, sample


def rope_tables(pos):
    half = ROPE_DIM // 2
    inv = ROPE_THETA ** (-jnp.arange(half, dtype=F32) / half)
    ang = pos.astype(F32)[:, None] * inv[None, :]
    return jnp.cos(ang), jnp.sin(ang)


def rope_rows(x, tables):
    half = ROPE_DIM // 2
    shp = (x.shape[0],) + (1,) * (x.ndim - 2) + (half,)
    cos, sin = tables[0].reshape(shp), tables[1].reshape(shp)
    x1, x2 = x[..., :half], x[..., half:]
    return jnp.concatenate([x1 * cos - x2 * sin, x2 * cos + x1 * sin], axis=-1)


def mla_sample_pallas(qd, ckv, kpe, st, o, kn, wb):
    b, t = qd.shape[:2]
    qn = (qd[..., :NOPE_DIM] * kn[:NOPE_DIM]).reshape(b * t, D_HEADS * NOPE_DIM)
    qlat = matmul(qn, wb['d_kt_blockdiag'][o])
    qlat = qlat.reshape(b, t, D_HEADS, KV_LORA).transpose(0, 2, 1, 3).reshape(b, D_HEADS * t, KV_LORA)
    qpe = (qd[..., NOPE_DIM:] * kn[NOPE_DIM:]).transpose(0, 2, 1, 3).reshape(b, D_HEADS * t, ROPE_DIM)
    cnew = jnp.pad(ckv, ((0, 0), (0, LANES - t), (0, 0)))
    rnew_t = jnp.swapaxes(jnp.pad(kpe, ((0, 0), (0, LANES - t), (0, 0))), 1, 2)
    od = mla_sample_attend(qlat.astype(BF16), qpe.astype(BF16), cnew, rnew_t, wb['d_kt'][o], wb['d_v'][o],
                           st['cache_d_ckv'], st['pool_kpe_t'], st['pt_flat'], o,
                           nb=b, ts=t, n_pages=st['page_table'].shape[1])
    return od.reshape(b, t, D_HEADS, D_VDIM)


def odd_mixer(proj, pos, mode, st, p, o, wb):
    b, t, _ = proj.shape
    qc, kc, vc, ic, fc, oc, qa, ckv, kpe = split_cols(proj, ODD_SPLITS)
    gate_b = p['c_gate_b'][o].astype(F32)
    i_pre = ic.astype(F32) + gate_b[0]
    f_pre = fc.astype(F32) + gate_b[1]
    q_c = qc.reshape(b, t, C_HEADS, C_DK)
    k_c = kc.reshape(b, t, C_HEADS, C_DK) * (C_DK ** -0.5)
    v_c = vc.reshape(b, t, C_HEADS, C_DV)
    if mode == 'prompt':
        c0 = jnp.zeros((b, C_HEADS, C_DV, C_DK), F32)
        n0 = jnp.zeros((b, C_HEADS, C_DK), F32)
        m0 = jnp.zeros((b, C_HEADS), F32)
    else:
        c0, n0, m0 = st['state_c_C'][o], st['state_c_n'][o], st['state_c_m'][o]
    hc, (c1, n1, m1) = mlstm_chunked(q_c, k_c, v_c, i_pre, f_pre, c0, n0, m0)
    hc = rms_norm(hc, p['c_on'][o]) * jax.nn.sigmoid(oc.reshape(b, t, C_HEADS, C_DV))
    qf = matmul(qa.reshape(b * t, Q_LORA), wb['w_d_qb'][o], gain=p['d_qa_g'][o])
    qf = qf.reshape(b, t, D_HEADS, NOPE_DIM + ROPE_DIM)
    qd = rms_norm(jnp.concatenate([qf[..., :NOPE_DIM], rope(qf[..., NOPE_DIM:], pos)], axis=-1), p['d_qn'][o])
    ckv = rms_norm(ckv, p['d_kv_g'][o])
    kpe = rope(kpe, pos)
    if mode == 'prompt':
        kd, vd = mla_kv(ckv, kpe, p['w_d_kvb'][o], p['d_kn'][o])
        od = causal_attn(qd, kd, vd, pos, pos)
    else:
        od = mla_sample_pallas(qd, ckv, kpe, st, o, p['d_kn'][o], wb)
    mixed = jnp.concatenate([hc.reshape(b, t, -1).astype(od.dtype), od.reshape(b, t, -1)], axis=-1)
    return mixed, (ckv, kpe, c1, n1, m1)


def kernel(x_prompt, x_sample, cache_a_k, cache_a_v, cache_a_kidx, state_b, state_c_C, state_c_n, state_c_m, cache_d_ckv, cache_d_kpe, cache_mem_k, cache_mem_v, page_table, mem_prompt, g_ffn1, w_ffn1_gu, w_ffn1_dn, g_mix, w_in_even, w_in_odd, w_mix_out, rel_bias, a_qn, a_kn, b_lb, b_on, c_gate_b, c_on, d_qa_g, d_kv_g, w_d_qb, w_d_kvb, d_qn, d_kn, g_mem, w_mem_q, w_mem_kv, w_mem_o, mem_qn, mem_kn, g_ffn2, w_ffn2_gu, w_ffn2_dn):
    p = {'rel_bias': rel_bias, 'a_qn': a_qn, 'a_kn': a_kn, 'b_on': b_on, 'c_gate_b': c_gate_b, 'c_on': c_on,
         'd_qa_g': d_qa_g, 'd_kv_g': d_kv_g, 'w_d_kvb': w_d_kvb, 'd_qn': d_qn, 'd_kn': d_kn,
         'mem_qn': mem_qn, 'mem_kn': mem_kn}
    bp, tp, d = x_prompt.shape
    bs, ts, _ = x_sample.shape
    n_p, n_s = bp * tp, bs * ts
    n_mem = mem_prompt.shape[1]
    past = page_table.shape[1] * PAGE_SIZE

    wo = w_in_odd.astype(BF16)
    n_odd = wo.shape[0]
    oc = np.cumsum([0] + ODD_SPLITS)
    gate_cols = []
    for pr in range(C_HEADS // 2):
        gate_cols += [wo[:, :, oc[3] + 2 * pr:oc[3] + 2 * pr + 2], wo[:, :, oc[4] + 2 * pr:oc[4] + 2 * pr + 2],
                      jnp.zeros((n_odd, d, LANES - 4), BF16)]
    in_odd = jnp.concatenate(
        [wo[:, :, oc[0]:oc[3]], wo[:, :, oc[5]:oc[6]]] + gate_cols
        + [wo[:, :, oc[6]:oc[9]], jnp.zeros((n_odd, d, LANES - ROPE_DIM), BF16)], axis=-1)
    gb = c_gate_b.astype(F32)
    gate_bias = jnp.concatenate(
        [gb[:, 0].reshape(n_odd, C_HEADS // 2, 2), gb[:, 1].reshape(n_odd, C_HEADS // 2, 2),
         jnp.zeros((n_odd, C_HEADS // 2, LANES - 4), F32)], axis=-1)[:, :, None, :]
    we = w_in_even.astype(BF16)
    c_wi = sum(EVEN_SPLITS[:4])
    c_ki = c_wi + IDX_HEADS
    c_qb = c_ki + IDX_DIM
    n_even = we.shape[0]
    c_ka = EVEN_SPLITS[0]
    in_even = jnp.concatenate(
        [we[:, :, :c_ka], we[:, :, c_qb:], we[:, :, c_ka:c_wi], we[:, :, c_ki:c_qb], we[:, :, c_ki:c_qb],
         we[:, :, c_wi:c_ki], jnp.zeros((n_even, d, LANES - IDX_HEADS), BF16)], axis=-1)
    kvb = w_d_kvb.astype(BF16).reshape(-1, KV_LORA, D_HEADS, NOPE_DIM + D_VDIM)
    d_kt3 = kvb[..., :NOPE_DIM].transpose(0, 2, 3, 1)
    eye = jnp.eye(D_HEADS, dtype=BF16)
    d_kt_bd = (d_kt3[:, :, :, None, :] * eye[None, :, None, :, None]).reshape(
        -1, D_HEADS * NOPE_DIM, D_HEADS * KV_LORA)
    wb = {
        'ffn1_gu': w_ffn1_gu.astype(BF16), 'ffn1_dn': w_ffn1_dn.astype(BF16),
        'ffn2_gu': w_ffn2_gu.astype(BF16), 'ffn2_dn': w_ffn2_dn.astype(BF16),
        'in_even': in_even,
        'in_odd': in_odd,
        'mix_out': w_mix_out.astype(BF16), 'w_d_qb': w_d_qb.astype(BF16), 'w_d_kvb': w_d_kvb.astype(BF16),
        'mem_q': w_mem_q.astype(BF16), 'mem_kv': w_mem_kv.astype(BF16), 'mem_o': w_mem_o.astype(BF16),
        'd_kt': d_kt3.reshape(-1, D_HEADS * NOPE_DIM, KV_LORA), 'd_kt_blockdiag': d_kt_bd,
        'd_v': kvb[..., NOPE_DIM:].transpose(0, 2, 1, 3),
    }
    bias_tiles, bias_sample = dsa_bias_tables(rel_bias, ts, past)
    n_pages = page_table.shape[1]
    n_phys = cache_a_k.shape[1]
    pool_k = cache_a_k.reshape(-1, n_phys, PAGE_SIZE * A_KV_HEADS, HEAD_DIM)
    pool_v = cache_a_v.reshape(-1, n_phys, PAGE_SIZE * A_KV_HEADS, HEAD_DIM)
    pool_kidx_t = jnp.swapaxes(cache_a_kidx, 2, 3)
    pool_kpe_t = jnp.swapaxes(cache_d_kpe, 2, 3)
    mem_k_rows = cache_mem_k.reshape(DEPTH, bs, n_mem * MEM_HEADS, HEAD_DIM)
    mem_v_rows = cache_mem_v.reshape(DEPTH, bs, n_mem * MEM_HEADS, HEAD_DIM)
    pt_flat = page_table.reshape(-1)

    lb_soft = jax.nn.softmax(b_lb.astype(F32), axis=0)
    lower_bound = jnp.cumsum(lb_soft, axis=0) - lb_soft[0]

    pos_all = rope_tables(jnp.concatenate([jnp.tile(jnp.arange(tp), bp), jnp.tile(past + jnp.arange(ts), bs)]))
    st = {'cache_a_k': cache_a_k, 'cache_a_v': cache_a_v, 'cache_a_kidx': cache_a_kidx, 'state_b': state_b,
          'state_c_C': state_c_C, 'state_c_n': state_c_n, 'state_c_m': state_c_m,
          'cache_d_ckv': cache_d_ckv, 'pool_kpe_t': pool_kpe_t, 'page_table': page_table, 'pt_flat': pt_flat}

    x = jnp.concatenate([x_prompt.reshape(n_p, d), x_sample.reshape(n_s, d)], axis=0)
    mem2d = mem_prompt.reshape(bp * n_mem, d)
    new_p = {n: [] for n in ('a_k', 'a_v', 'a_kidx', 'b_S', 'c_C', 'c_n', 'c_m', 'd_ckv', 'd_kpe', 'mem_k', 'mem_v')}
    new_s = {n: [] for n in ('a_k', 'a_v', 'a_kidx', 'b_S', 'c_C', 'c_n', 'c_m', 'd_ckv', 'd_kpe')}

    for l in range(DEPTH):
        x = ffn_half_step(x, g_ffn1[l], wb['ffn1_gu'], wb['ffn1_dn'], l)
        if l % 2 == 0:
            e = l // 2
            proj = matmul(x, wb['in_even'], layer=e, gain=g_mix[l], tn=1280)
            ka32, ka16 = head_norm(proj, EV_KA // (A_KV_HEADS * HEAD_DIM), A_KV_HEADS, a_kn[e])
            oa_p = dsa_prompt_attend(proj, ka16, bias_tiles, a_qn[e], nb=bp, t=tp)
            sel_past, sel_new = dsa_sample_select(proj, pool_kidx_t, pt_flat, e, row0=n_p, nb=bs, ts=ts,
                                                  n_pages=n_pages)
            oa_s = dsa_sample_attend(proj, ka32, sel_past, sel_new, bias_sample, pool_k, pool_v, pt_flat, e, a_qn[e],
                                     row0=n_p, nb=bs, ts=ts, n_pages=n_pages)
            lb = lower_bound[e].reshape(B_HEADS, B_DK)
            ob_p, s_p = hgrn2_prompt(proj, lb, b_on[e], nb=bp, t=tp)
            ob_s, s_s = hgrn2_sample(proj, lb, b_on[e], state_b[e], row0=n_p, nb=bs, t=ts)
            mixed_p = jnp.concatenate([oa_p, ob_p], axis=-1)
            mixed_s = jnp.concatenate([oa_s, ob_s], axis=-1)
            va = proj[:, EV_VA:EV_VA + A_KV_HEADS * HEAD_DIM]
            ki = proj[:, EV_KIKI:EV_KIKI + IDX_DIM]
            for new, rows, nb_, t_, s_new in ((new_p, slice(0, n_p), bp, tp, s_p), (new_s, slice(n_p, None), bs, ts, s_s)):
                new['a_k'].append(ka32[rows].reshape(nb_, t_, A_KV_HEADS, HEAD_DIM))
                new['a_v'].append(va[rows].reshape(nb_, t_, A_KV_HEADS, HEAD_DIM))
                new['a_kidx'].append(ki[rows].reshape(nb_, t_, IDX_DIM))
                new['b_S'].append(s_new)
        else:
            o = l // 2
            proj = matmul(x, wb['in_odd'], layer=o, gain=g_mix[l], tn=896)
            hc_p, cc_p, cn_p, cm_p = mlstm(proj, gate_bias[o], c_on[o], None, row0=0, nb=bp, t=tp, npairs=2)
            hc_s, cc_s, cn_s, cm_s = mlstm(
                proj, gate_bias[o], c_on[o],
                (jnp.swapaxes(state_c_C[o], -1, -2), state_c_n[o], state_c_m[o].reshape(bs, 1, C_HEADS)),
                row0=n_p, nb=bs, t=ts, npairs=C_HEADS // 2)
            qf = matmul(proj, wb['w_d_qb'], layer=o, gain=d_qa_g[o], a_col=OD_QA // Q_LORA)
            qf = qf.reshape(n_p + n_s, D_HEADS, NOPE_DIM + ROPE_DIM)
            qd = rms_norm(jnp.concatenate([qf[..., :NOPE_DIM], rope_rows(qf[..., NOPE_DIM:], pos_all)], axis=-1),
                          d_qn[o])
            ckv = rms_norm(proj[:, OD_CKV:OD_CKV + KV_LORA], d_kv_g[o])
            kpe = rope_rows(proj[:, OD_KPE:OD_KPE + ROPE_DIM], pos_all)
            kv = matmul(ckv[:n_p], wb['w_d_kvb'], layer=o).reshape(n_p, D_HEADS, NOPE_DIM + D_VDIM)
            k_pe = jnp.broadcast_to(kpe[:n_p, None, :], (n_p, D_HEADS, ROPE_DIM))
            kd = rms_norm(jnp.concatenate([kv[..., :NOPE_DIM], k_pe], axis=-1), d_kn[o])
            pad = ((0, 0), (0, 0), (0, 2 * LANES - NOPE_DIM - ROPE_DIM))
            q16 = jnp.pad(qd[:n_p], pad).astype(BF16).reshape(n_p, D_HEADS * 2 * LANES)
            k16 = jnp.pad(kd, pad).astype(BF16).reshape(n_p, D_HEADS * 2 * LANES)
            v16 = kv[..., NOPE_DIM:].astype(BF16).reshape(n_p, D_HEADS * D_VDIM)
            od_p = causal_attend(q16, k16, v16, nb=bp, t=tp, heads=D_HEADS, dv=D_VDIM,
                                 scale=(NOPE_DIM + ROPE_DIM) ** -0.5)
            od_s = mla_sample_pallas(qd[n_p:].reshape(bs, ts, D_HEADS, NOPE_DIM + ROPE_DIM),
                                     ckv[n_p:].reshape(bs, ts, KV_LORA), kpe[n_p:].reshape(bs, ts, ROPE_DIM),
                                     st, o, d_kn[o], wb).reshape(n_s, D_HEADS * D_VDIM)
            mixed_p = jnp.concatenate([hc_p, od_p], axis=-1)
            mixed_s = jnp.concatenate([hc_s, od_s], axis=-1)
            for new, rows, nb_, t_, cc, cn, cm in ((new_p, slice(0, n_p), bp, tp, cc_p, cn_p, cm_p),
                                                   (new_s, slice(n_p, None), bs, ts, cc_s, cn_s, cm_s)):
                new['d_ckv'].append(ckv[rows].reshape(nb_, t_, KV_LORA))
                new['d_kpe'].append(kpe[rows].reshape(nb_, t_, ROPE_DIM))
                new['c_C'].append(jnp.swapaxes(cc, -1, -2))
                new['c_n'].append(cn[:, :, 0, :])
                new['c_m'].append(cm[:, :, 0, 0])
        mixed = jnp.concatenate([mixed_p, mixed_s], axis=0)
        x = matmul(mixed, wb['mix_out'], layer=l, res=x)

        kvm = matmul(mem2d, wb['mem_kv'], layer=l)
        mk32, mk16 = head_norm(kvm, 0, MEM_HEADS, mem_kn[l], tm=bp * n_mem)
        wm = MEM_HEADS * HEAD_DIM
        new_p['mem_k'].append(mk32.reshape(bp, n_mem, MEM_HEADS, HEAD_DIM))
        new_p['mem_v'].append(kvm[:, wm:].reshape(bp, n_mem, MEM_HEADS, HEAD_DIM))
        q = matmul(x, wb['mem_q'], layer=l, gain=g_mem[l])
        nqb = 4
        o_p = mem_attend(q, mk16.reshape(bp, n_mem, wm), kvm.reshape(bp, n_mem, 2 * wm), mem_qn[l],
                         row0=0, n_batch=bp * nqb, tq=tp // nqb, nb=1, n_mem=n_mem, head_rows=False,
                         k_index=lambda i: (i // nqb, 0, 0), v_index=lambda i: (i // nqb, 0, 1))
        o_s = mem_attend(q, mem_k_rows, mem_v_rows, mem_qn[l], row0=n_p, n_batch=bs, tq=ts, nb=8,
                         n_mem=n_mem, head_rows=True,
                         k_index=lambda i, l=l: (l, i, 0, 0), v_index=lambda i, l=l: (l, i, 0, 0))
        o_all = jnp.concatenate([o_p, o_s], axis=0)
        x = matmul(o_all, wb['mem_o'], layer=l, res=x)
        x = ffn_half_step(x, g_ffn2[l], wb['ffn2_gu'], wb['ffn2_dn'], l)

    sp = {n: jnp.stack(v) for n, v in new_p.items()}
    ss = {n: jnp.stack(v) for n, v in new_s.items()}
    return (x[:n_p].reshape(bp, tp, d), x[n_p:].reshape(bs, ts, d),
            sp['a_k'], sp['a_v'], sp['a_kidx'], sp['b_S'], sp['c_C'], sp['c_n'], sp['c_m'], sp['d_ckv'], sp['d_kpe'],
            sp['mem_k'], sp['mem_v'],
            ss['a_k'], ss['a_v'], ss['a_kidx'], ss['b_S'], ss['c_C'], ss['c_n'], ss['c_m'], ss['d_ckv'], ss['d_kpe'])
```

```python
import functools
import math

import jax
import jax.numpy as jnp
import numpy as np
from jax import lax
from jax.experimental import pallas as pl
from jax.experimental.pallas import tpu as pltpu

D_MODEL = 2048
DEPTH = 4
PAGE_SIZE = 128
HEAD_DIM = 128
A_HEADS = 8
A_KV_HEADS = 2
IDX_HEADS = 8
IDX_DIM = 64
TOPK_MAX = 256
REL_BUCKETS = 32
REL_MAX_DIST = 128
B_HEADS = 8
B_DK = 128
B_DV = 128
C_HEADS = 8
C_DK = 64
C_DV = 128
D_HEADS = 8
Q_LORA = 512
KV_LORA = 256
NOPE_DIM = 128
ROPE_DIM = 64
D_VDIM = 128
ROPE_THETA = 10000.0
MEM_HEADS = 4
D_FF = 5632
CHUNK = 64
Q_BLOCK = 128
EPS = 1e-6
MASK_NEG = -1e30
LB_FLOOR = 1e-20
F32 = jnp.float32
BF16 = jnp.bfloat16

EVEN_SPLITS = [A_HEADS * HEAD_DIM, A_KV_HEADS * HEAD_DIM, A_KV_HEADS * HEAD_DIM, IDX_HEADS * IDX_DIM, IDX_HEADS,
               IDX_DIM, B_HEADS * B_DK, B_HEADS * B_DK, B_HEADS * B_DV, B_HEADS * B_DV]
ODD_SPLITS = [C_HEADS * C_DK, C_HEADS * C_DK, C_HEADS * C_DV, C_HEADS, C_HEADS, C_HEADS * C_DV, Q_LORA, KV_LORA,
              ROPE_DIM]
EVEN_IN = sum(EVEN_SPLITS)
ODD_IN = sum(ODD_SPLITS)

EV_QA, EV_QB, EV_FB, EV_IB, EV_GB, EV_KA, EV_VA, EV_QI, EV_KIKI, EV_WI, EVEN_PAD = (
    0, 1024, 2048, 3072, 4096, 5120, 5376, 5632, 6144, 6272, 6400)
OD_QC, OD_KC, OD_VC, OD_OC, OD_G, OD_QA, OD_CKV, OD_KPE, ODD_PAD = (
    0, 512, 1024, 2048, 3072, 3584, 4096, 4352, 4480)

LANES = 128
VMEM_LIMIT = 56 * 1024 * 1024


def _round_up(n, m):
    return (n + m - 1) // m * m


def _ffn_body(x_ref, g_ref, wg_ref, wu_ref, wd_ref, o_ref, n_ref):
    f = pl.program_id(1)

    @pl.when(f == 0)
    def _():
        x = x_ref[...]
        ms = jnp.mean(x * x, axis=-1, keepdims=True)
        n_ref[...] = (x * lax.rsqrt(ms + EPS) * g_ref[...]).astype(BF16)
        o_ref[...] = x

    n = n_ref[...]
    a = jnp.dot(n, wg_ref[...], preferred_element_type=F32)
    b = jnp.dot(n, wu_ref[...], preferred_element_type=F32)
    h = (a * jax.nn.sigmoid(a) * b).astype(BF16)
    o_ref[...] += 0.5 * jnp.dot(h, wd_ref[...], preferred_element_type=F32)


def ffn_half_step(x, g, w_gu, w_dn, layer, *, tm=768, tf=512):
    m, d = x.shape
    ff = w_dn.shape[1]
    nf = ff // tf
    assert m % tm == 0 and ff % tf == 0, (m, tm, ff, tf)
    return pl.pallas_call(
        _ffn_body,
        grid=(m // tm, nf),
        in_specs=[
            pl.BlockSpec((tm, d), lambda i, f: (i, 0)),
            pl.BlockSpec((1, d), lambda i, f: (0, 0)),
            pl.BlockSpec((None, d, tf), lambda i, f: (layer, 0, f)),
            pl.BlockSpec((None, d, tf), lambda i, f: (layer, 0, f + nf)),
            pl.BlockSpec((None, tf, d), lambda i, f: (layer, f, 0)),
        ],
        out_specs=pl.BlockSpec((tm, d), lambda i, f: (i, 0)),
        out_shape=jax.ShapeDtypeStruct((m, d), F32),
        scratch_shapes=[pltpu.VMEM((tm, d), BF16)],
        compiler_params=pltpu.CompilerParams(
            dimension_semantics=("parallel", "arbitrary"), vmem_limit_bytes=VMEM_LIMIT),
        name="ffn_half_step",
    )(x, g.reshape(1, d), w_gu, w_gu, w_dn)


def _mm_body(*refs, has_gain, has_res, alpha):
    a_ref, w_ref = refs[0], refs[1]
    k = 2
    g_ref = res_ref = None
    if has_gain:
        g_ref = refs[k]
        k += 1
    if has_res:
        res_ref = refs[k]
        k += 1
    o_ref, n_ref = refs[k], refs[k + 1]

    @pl.when(pl.program_id(1) == 0)
    def _():
        a = a_ref[...]
        if has_gain:
            ms = jnp.mean(a * a, axis=-1, keepdims=True)
            a = a * lax.rsqrt(ms + EPS) * g_ref[...]
        n_ref[...] = a.astype(BF16)

    acc = jnp.dot(n_ref[...], w_ref[...], preferred_element_type=F32)
    if has_res:
        acc = res_ref[...] + alpha * acc
    o_ref[...] = acc


def matmul(a, w, *, gain=None, res=None, alpha=1.0, tm=512, tn=512, a_col=0, layer=None):
    m = a.shape[0]
    kdim, n = w.shape[-2:]
    tm = min(tm, m)
    tn = min(tn, n)
    assert m % tm == 0 and n % tn == 0, (m, tm, n, tn)
    w_spec = (pl.BlockSpec((kdim, tn), lambda i, j: (0, j)) if layer is None
              else pl.BlockSpec((None, kdim, tn), lambda i, j: (layer, 0, j)))
    in_specs = [pl.BlockSpec((tm, kdim), lambda i, j: (i, a_col)), w_spec]
    args = [a, w]
    if gain is not None:
        in_specs.append(pl.BlockSpec((1, kdim), lambda i, j: (0, 0)))
        args.append(gain.reshape(1, kdim))
    if res is not None:
        in_specs.append(pl.BlockSpec((tm, tn), lambda i, j: (i, j)))
        args.append(res)
    return pl.pallas_call(
        functools.partial(_mm_body, has_gain=gain is not None, has_res=res is not None, alpha=alpha),
        grid=(m // tm, n // tn),
        in_specs=in_specs,
        out_specs=pl.BlockSpec((tm, tn), lambda i, j: (i, j)),
        out_shape=jax.ShapeDtypeStruct((m, n), F32),
        scratch_shapes=[pltpu.VMEM((tm, kdim), BF16)],
        compiler_params=pltpu.CompilerParams(
            dimension_semantics=("parallel", "arbitrary"), vmem_limit_bytes=VMEM_LIMIT),
        name="matmul",
    )(*args)


def _head_norm_body(x_ref, g_ref, o32_ref, o16_ref, *, heads):
    outs = []
    for h in range(heads):
        x = x_ref[:, LANES * h:LANES * (h + 1)]
        outs.append(x * lax.rsqrt(jnp.mean(x * x, axis=-1, keepdims=True) + EPS) * g_ref[...])
    y = jnp.concatenate(outs, axis=1)
    o32_ref[...] = y
    o16_ref[...] = y.astype(BF16)


def head_norm(x, col_block, heads, gain, *, tm=512):
    m = x.shape[0]
    w = heads * LANES
    return pl.pallas_call(
        functools.partial(_head_norm_body, heads=heads),
        grid=(m // tm,),
        in_specs=[pl.BlockSpec((tm, w), lambda i: (i, col_block)),
                  pl.BlockSpec((1, LANES), lambda i: (0, 0))],
        out_specs=[pl.BlockSpec((tm, w), lambda i: (i, 0)), pl.BlockSpec((tm, w), lambda i: (i, 0))],
        out_shape=[jax.ShapeDtypeStruct((m, w), F32), jax.ShapeDtypeStruct((m, w), BF16)],
        compiler_params=pltpu.CompilerParams(dimension_semantics=("parallel",)),
        name="head_norm",
    )(x, gain.reshape(1, LANES))


INT_MIN = -2 ** 31
INT_MAX = 2 ** 31 - 1
_NT = (((1,), (1,)), ((), ()))


def _dot_nt(a, b):
    return lax.dot_general(a, b, _NT, preferred_element_type=F32)


def _sort_key(x):
    b = lax.bitcast_convert_type(x + 0.0, jnp.int32)
    return jnp.where(b >= 0, b, b ^ jnp.int32(INT_MAX))


def _row_count(cond):
    c = jnp.where(cond, 1.0, 0.0)
    if c.ndim == 3:
        c = jnp.sum(c, axis=0)
    return jnp.sum(c, axis=-1, keepdims=True)


def _topk_masks(pieces, k, idx_bits):
    kf = float(k)

    def count(fn):
        tot = None
        for key, idx in pieces:
            c = _row_count(fn(key, idx))
            tot = c if tot is None else tot + c
        return tot

    t0 = jnp.where(count(lambda key, idx: key >= 0) >= kf, jnp.int32(0), jnp.int32(INT_MIN))

    def body(i, t):
        cand = t | jnp.left_shift(jnp.int32(1), 30 - i)
        return jnp.where(count(lambda key, idx: key >= cand) >= kf, cand, t)

    thr = lax.fori_loop(0, 31, body, t0)
    need = kf - count(lambda key, idx: key > thr)
    ties = [(jnp.where(key == thr, idx, INT_MAX), idx) for key, idx in pieces]

    def count_ties(cand):
        tot = None
        for tie, _ in ties:
            c = _row_count(tie < cand)
            tot = c if tot is None else tot + c
        return tot

    def body2(i, cur):
        cand = cur + jnp.left_shift(jnp.int32(1), idx_bits - 1 - i)
        return jnp.where(count_ties(cand) < need, cand, cur)

    cut = lax.fori_loop(0, idx_bits, body2, jnp.zeros_like(thr))
    return [jnp.where(key > thr, 1.0, jnp.where(tie <= cut, 1.0, 0.0))
            for (key, _), (tie, _) in zip(pieces, ties)]


def _softmax_rows(lg):
    mx = jnp.max(lg, axis=-1, keepdims=True)
    p = jnp.exp(lg - mx)
    return p / jnp.sum(p, axis=-1, keepdims=True)


def _dsa_prompt_body(qa_ref, qi_ref, wi_ref, kiki_ref, k_ref, v_ref, bias_ref, qn_ref, o_ref, *, qb, t, n_sel, i0, kw):
    i = i0 + pl.program_id(1)
    nsub = qb // LANES

    def run(kw):
        nk = kw // LANES
        kiki = kiki_ref[0:kw, :].astype(BF16)
        qi = qi_ref[...] * (IDX_DIM ** -0.5)
        w = wi_ref[...] * (IDX_HEADS ** -0.5)
        lane = lax.broadcasted_iota(jnp.int32, (qb, LANES), 1)
        score = jnp.zeros((qb, kw), F32)
        for h in range(IDX_HEADS):
            blk = qi[:, LANES * (h // 2):LANES * (h // 2 + 1)]
            keep = (lane < IDX_DIM) if h % 2 == 0 else (lane >= IDX_DIM)
            s = _dot_nt(jnp.where(keep, blk, 0.0).astype(BF16), kiki)
            score = score + jnp.maximum(s, 0.0) * w[:, h:h + 1]
        q_pos = i * qb + lax.broadcasted_iota(jnp.int32, (qb, kw), 0)
        k_pos = lax.broadcasted_iota(jnp.int32, (qb, kw), 1)
        allowed = k_pos <= q_pos
        score = jnp.where(allowed, score, MASK_NEG)
        pieces = [(_sort_key(score), k_pos)]
        nv = _round_up(n_sel, LANES) if kw < t else 0
        if nv:
            pieces.append((_sort_key(jnp.full((qb, nv), MASK_NEG, F32)),
                           kw + lax.broadcasted_iota(jnp.int32, (qb, nv), 1)))
        selm = _topk_masks(pieces, n_sel, max(1, (kw + nv - 1).bit_length()))[0]
        sel = jnp.where(allowed, selm, 0.0) > 0.5

        k = k_ref[0:kw, :]
        v = v_ref[0:kw, :].astype(BF16)
        outs = []
        for h in range(A_HEADS):
            g = h // (A_HEADS // A_KV_HEADS)
            rows = []
            for a in range(nsub):
                d0 = i * nsub + a
                tiles = []
                for j in range(nk):
                    d = d0 - j
                    tiles.append(jnp.where(d == 0, bias_ref[h, 0],
                                           jnp.where(d == 1, bias_ref[h, 1], bias_ref[h, 2])))
                rows.append(jnp.concatenate(tiles, axis=1))
            bias = jnp.concatenate(rows, axis=0) if nsub > 1 else rows[0]
            q = qa_ref[:, LANES * h:LANES * (h + 1)]
            q = q * lax.rsqrt(jnp.mean(q * q, axis=-1, keepdims=True) + EPS) * qn_ref[...]
            lg = _dot_nt(q.astype(BF16), k[:, LANES * g:LANES * (g + 1)]) * (HEAD_DIM ** -0.5) + bias
            p = _softmax_rows(jnp.where(sel, lg, MASK_NEG))
            outs.append(jnp.dot(p.astype(BF16), v[:, LANES * g:LANES * (g + 1)], preferred_element_type=F32))
        o_ref[...] = jnp.concatenate(outs, axis=1)

    run(kw)


def dsa_prompt_attend(proj, ka16, bias_tiles, qn, *, nb, t, qb=256, splits=2):
    nq = t // qb
    wq = A_HEADS * HEAD_DIM
    per = -(-nq // splits)
    outs = []
    for i0 in range(0, nq, per):
        ni = min(per, nq - i0)
        o = _dsa_prompt_call(proj, ka16, bias_tiles, qn, nb=nb, t=t, qb=qb, i0=i0, ni=ni, kw=min(t, (i0 + ni) * qb))
        outs.append(o.reshape(nb, ni * qb, wq))
    return jnp.concatenate(outs, axis=1).reshape(nb * t, wq)


def _dsa_prompt_call(proj, ka16, bias_tiles, qn, *, nb, t, qb, i0, ni, kw):
    nq = t // qb
    n_sel = min(TOPK_MAX, t // 4)
    wq = A_HEADS * HEAD_DIM
    assert t % kw == 0, (t, kw)
    return pl.pallas_call(
        functools.partial(_dsa_prompt_body, qb=qb, t=t, n_sel=n_sel, i0=i0, kw=kw),
        grid=(nb, ni),
        in_specs=[
            pl.BlockSpec((qb, wq), lambda b, i: (b * nq + i0 + i, 0)),
            pl.BlockSpec((qb, IDX_HEADS * IDX_DIM), lambda b, i: (b * nq + i0 + i, EV_QI // (IDX_HEADS * IDX_DIM))),
            pl.BlockSpec((qb, LANES), lambda b, i: (b * nq + i0 + i, EV_WI // LANES)),
            pl.BlockSpec((kw, LANES), lambda b, i: (b * (t // kw), EV_KIKI // LANES)),
            pl.BlockSpec((kw, A_KV_HEADS * HEAD_DIM), lambda b, i: (b * (t // kw), 0)),
            pl.BlockSpec((kw, A_KV_HEADS * HEAD_DIM), lambda b, i: (b * (t // kw), EV_VA // (A_KV_HEADS * HEAD_DIM))),
            pl.BlockSpec((A_HEADS, 3, LANES, LANES), lambda b, i: (0, 0, 0, 0)),
            pl.BlockSpec((1, HEAD_DIM), lambda b, i: (0, 0)),
        ],
        out_specs=pl.BlockSpec((qb, wq), lambda b, i: (b * ni + i, 0)),
        out_shape=jax.ShapeDtypeStruct((nb * ni * qb, wq), F32),
        compiler_params=pltpu.CompilerParams(
            dimension_semantics=("parallel", "arbitrary"), vmem_limit_bytes=VMEM_LIMIT),
        name="dsa_prompt",
    )(proj, proj, proj, proj, ka16, proj, bias_tiles, qn.reshape(1, HEAD_DIM))


def _idx_head_sum(s, w):
    tot = None
    for h in range(IDX_HEADS):
        c = jnp.maximum(s[8 * h:8 * (h + 1)], 0.0) * w[:, h:h + 1]
        tot = c if tot is None else tot + c
    return tot


def _dsa_score_body(pt_ref, qi_ref, wi_ref, knew_ref, *rest, pp, ts):
    pages, sp_ref, sn_ref = rest[:pp], rest[pp], rest[pp + 1]
    qi = qi_ref[...] * (IDX_DIM ** -0.5)
    a = jnp.concatenate([qi[:, IDX_DIM * h:IDX_DIM * (h + 1)] for h in range(IDX_HEADS)], axis=0).astype(BF16)
    w = wi_ref[...] * (IDX_HEADS ** -0.5)
    kt = jnp.concatenate([p[...] for p in pages], axis=1).astype(BF16)
    sp_ref[...] = _idx_head_sum(jnp.dot(a, kt, preferred_element_type=F32), w)

    @pl.when(pl.program_id(1) == 0)
    def _():
        knew = knew_ref[...][:, :IDX_DIM]
        kn = jnp.concatenate([knew, jnp.zeros((LANES - ts, IDX_DIM), F32)], axis=0).astype(BF16)
        sn_ref[...] = _idx_head_sum(_dot_nt(a, kn), w)


def _dsa_mask_body(sp_ref, sn_ref, mp_ref, mn_ref, *, ts, n_sel):
    rows, past = sp_ref.shape
    lane = lax.broadcasted_iota(jnp.int32, (rows // ts, ts, LANES), 2).reshape(rows, LANES)
    t_row = lax.broadcasted_iota(jnp.int32, (rows // ts, ts, LANES), 1).reshape(rows, LANES)
    causal = lane <= t_row
    snew = jnp.where(causal, sn_ref[...], MASK_NEG)
    key_new = jnp.where(lane < ts, _sort_key(snew), INT_MIN)
    idx_past = lax.broadcasted_iota(jnp.int32, (rows, past), 1)
    m_past, m_new = _topk_masks([(_sort_key(sp_ref[...]), idx_past), (key_new, past + lane)], n_sel,
                                (past + LANES - 1).bit_length())
    mp_ref[...] = m_past
    mn_ref[...] = jnp.where(causal, m_new, 0.0)


def dsa_sample_select(proj, pool_kidx_t, pt_flat, layer, *, row0, nb, ts, n_pages, pp=32, rows_per_step=128):
    pp = min(pp, n_pages)
    nj = n_pages // pp
    tk = pp * PAGE_SIZE
    past = n_pages * PAGE_SIZE
    n_sel = min(TOPK_MAX, (past + ts) // 4)
    rb0 = row0 // ts
    rows = nb * ts
    rstep = min(rows_per_step, rows)
    sp, sn = _dsa_sample_scores(proj, pool_kidx_t, pt_flat, layer, rb0=rb0, nb=nb, ts=ts, n_pages=n_pages, pp=pp)
    mp, mn = pl.pallas_call(
        functools.partial(_dsa_mask_body, ts=ts, n_sel=n_sel),
        grid=(rows // rstep,),
        in_specs=[pl.BlockSpec((rstep, past), lambda i: (i, 0)), pl.BlockSpec((rstep, LANES), lambda i: (i, 0))],
        out_specs=[pl.BlockSpec((rstep, past), lambda i: (i, 0)), pl.BlockSpec((rstep, LANES), lambda i: (i, 0))],
        out_shape=[jax.ShapeDtypeStruct((rows, past), F32), jax.ShapeDtypeStruct((rows, LANES), F32)],
        compiler_params=pltpu.CompilerParams(dimension_semantics=("parallel",), vmem_limit_bytes=VMEM_LIMIT),
        name="dsa_sample_mask",
    )(sp.reshape(rows, past), sn.reshape(rows, LANES))
    return mp.reshape(nb, ts, past), mn.reshape(nb, ts, LANES)


def _dsa_sample_scores(proj, pool_kidx_t, pt_flat, layer, *, rb0, nb, ts, n_pages, pp):
    pp = min(pp, n_pages)
    nj = n_pages // pp
    tk = pp * PAGE_SIZE
    past = n_pages * PAGE_SIZE

    def page_spec(r):
        return pl.BlockSpec((None, None, IDX_DIM, PAGE_SIZE),
                            lambda b, j, pt: (layer, pt[b * n_pages + j * pp + r], 0, 0))

    grid_spec = pltpu.PrefetchScalarGridSpec(
        num_scalar_prefetch=1,
        grid=(nb, nj),
        in_specs=[
            pl.BlockSpec((ts, IDX_HEADS * IDX_DIM), lambda b, j, pt: (rb0 + b, EV_QI // (IDX_HEADS * IDX_DIM))),
            pl.BlockSpec((ts, LANES), lambda b, j, pt: (rb0 + b, EV_WI // LANES)),
            pl.BlockSpec((ts, LANES), lambda b, j, pt: (rb0 + b, EV_KIKI // LANES)),
        ] + [page_spec(r) for r in range(pp)],
        out_specs=[pl.BlockSpec((None, ts, tk), lambda b, j, pt: (b, 0, j)),
                   pl.BlockSpec((None, ts, LANES), lambda b, j, pt: (b, 0, 0))],
    )
    return pl.pallas_call(
        functools.partial(_dsa_score_body, pp=pp, ts=ts),
        grid_spec=grid_spec,
        out_shape=[jax.ShapeDtypeStruct((nb, ts, past), F32), jax.ShapeDtypeStruct((nb, ts, LANES), F32)],
        compiler_params=pltpu.CompilerParams(
            dimension_semantics=("parallel", "arbitrary"), vmem_limit_bytes=VMEM_LIMIT),
        name="dsa_sample_scores",
    )(pt_flat, proj, proj, proj, *([pool_kidx_t] * pp))


def _dsa_att_body(pt_ref, q_ref, knew_ref, vnew_ref, qn_ref, mask_ref, mnew_ref, bias_ref, bnew_ref, *rest,
                  pp, sub, nj, ts):
    kpages, vpages = rest[:pp], rest[pp:2 * pp]
    o_ref, qs_ref, m_ref, l_ref, acc_ref = rest[2 * pp:]
    j = pl.program_id(1)
    gsz = A_HEADS // A_KV_HEADS
    rows_g = gsz * ts

    @pl.when(j == 0)
    def _():
        qs = []
        for h in range(A_HEADS):
            q = q_ref[:, LANES * h:LANES * (h + 1)]
            qs.append(q * lax.rsqrt(jnp.mean(q * q, axis=-1, keepdims=True) + EPS) * qn_ref[...])
        qs_ref[...] = jnp.concatenate(qs, axis=0).astype(BF16)
        m_ref[...] = jnp.full(m_ref.shape, MASK_NEG, F32)
        l_ref[...] = jnp.zeros(l_ref.shape, F32)
        acc_ref[...] = jnp.zeros(acc_ref.shape, F32)

    def partial(k_tiles, v_tiles, mask, bias):
        qs = qs_ref[...]
        lg = jnp.concatenate(
            [_dot_nt(qs[rows_g * g:rows_g * (g + 1)], k_tiles[g])
             for g in range(A_KV_HEADS)], axis=0) * (HEAD_DIM ** -0.5) + bias
        sel = jnp.concatenate([mask] * A_HEADS, axis=0) > 0.5
        lg = jnp.where(sel, lg, MASK_NEG)
        m = jnp.max(lg, axis=-1, keepdims=True)
        p = jnp.where(sel, jnp.exp(lg - m), 0.0)
        pb = p.astype(BF16)
        pv = jnp.concatenate(
            [jnp.dot(pb[rows_g * g:rows_g * (g + 1)], v_tiles[g], preferred_element_type=F32)
             for g in range(A_KV_HEADS)], axis=0)
        return m, jnp.sum(p, axis=-1, keepdims=True), pv

    def merge(parts):
        m_old = m_ref[...]
        m_new = m_old
        for m, _, _ in parts:
            m_new = jnp.maximum(m_new, m)
        alpha = jnp.exp(m_old - m_new)
        l_new = alpha * l_ref[...]
        acc = alpha * acc_ref[...]
        for m, l, pv in parts:
            a = jnp.exp(m - m_new)
            l_new = l_new + a * l
            acc = acc + a * pv
        m_ref[...] = m_new
        l_ref[...] = l_new
        acc_ref[...] = acc

    def head_rows(pages, g):
        return jnp.concatenate([p[pl.ds(g, PAGE_SIZE, stride=A_KV_HEADS), :] for p in pages], axis=0).astype(BF16)

    bias_all = bias_ref[...].reshape(A_HEADS * ts, pp * PAGE_SIZE)
    mask_all = mask_ref[...]
    parts = []
    for s0 in range(0, pp, sub):
        kp, vp = kpages[s0:s0 + sub], vpages[s0:s0 + sub]
        cols = slice(s0 * PAGE_SIZE, (s0 + sub) * PAGE_SIZE)
        parts.append(partial([head_rows(kp, g) for g in range(A_KV_HEADS)],
                             [head_rows(vp, g) for g in range(A_KV_HEADS)], mask_all[:, cols], bias_all[:, cols]))
    merge(parts)

    @pl.when(j == nj - 1)
    def _():
        pad = jnp.zeros((LANES - ts, A_KV_HEADS * HEAD_DIM), F32)
        kn = jnp.concatenate([knew_ref[...], pad], axis=0).astype(BF16)
        vn = jnp.concatenate([vnew_ref[...], pad], axis=0).astype(BF16)
        merge([partial([kn[:, LANES * g:LANES * (g + 1)] for g in range(A_KV_HEADS)],
                       [vn[:, LANES * g:LANES * (g + 1)] for g in range(A_KV_HEADS)],
                       mnew_ref[...], bnew_ref[...].reshape(A_HEADS * ts, LANES))])
        out = acc_ref[...] / l_ref[...]
        o_ref[...] = jnp.concatenate([out[ts * h:ts * (h + 1)] for h in range(A_HEADS)], axis=1)


def dsa_sample_attend(proj, ka32, mask_past, mask_new, bias_s, pool_k, pool_v, pt_flat, layer, qn, *, row0, nb, ts,
                      n_pages, pp=32, sub=16):
    pp = min(pp, n_pages)
    nj = n_pages // pp
    tk = pp * PAGE_SIZE
    past = n_pages * PAGE_SIZE
    rb0 = row0 // ts
    wkv = A_KV_HEADS * HEAD_DIM
    wq = A_HEADS * HEAD_DIM

    def page_spec(r):
        return pl.BlockSpec((None, None, PAGE_SIZE * A_KV_HEADS, HEAD_DIM),
                            lambda b, j, pt: (layer, pt[b * n_pages + j * pp + r], 0, 0))

    grid_spec = pltpu.PrefetchScalarGridSpec(
        num_scalar_prefetch=1,
        grid=(nb, nj),
        in_specs=[
            pl.BlockSpec((ts, wq), lambda b, j, pt: (rb0 + b, 0)),
            pl.BlockSpec((ts, wkv), lambda b, j, pt: (rb0 + b, 0)),
            pl.BlockSpec((ts, wkv), lambda b, j, pt: (rb0 + b, EV_VA // wkv)),
            pl.BlockSpec((1, HEAD_DIM), lambda b, j, pt: (0, 0)),
            pl.BlockSpec((None, ts, tk), lambda b, j, pt: (b, 0, j)),
            pl.BlockSpec((None, ts, LANES), lambda b, j, pt: (b, 0, 0)),
            pl.BlockSpec((A_HEADS, ts, tk), lambda b, j, pt: (0, 0, j)),
            pl.BlockSpec((A_HEADS, ts, LANES), lambda b, j, pt: (0, 0, past // LANES)),
        ] + [page_spec(r) for r in range(pp)] * 2,
        out_specs=pl.BlockSpec((ts, wq), lambda b, j, pt: (b, 0)),
        scratch_shapes=[pltpu.VMEM((A_HEADS * ts, HEAD_DIM), BF16), pltpu.VMEM((A_HEADS * ts, 1), F32),
                        pltpu.VMEM((A_HEADS * ts, 1), F32), pltpu.VMEM((A_HEADS * ts, HEAD_DIM), F32)],
    )
    return pl.pallas_call(
        functools.partial(_dsa_att_body, pp=pp, sub=min(sub, pp), nj=nj, ts=ts),
        grid_spec=grid_spec,
        out_shape=jax.ShapeDtypeStruct((nb * ts, wq), F32),
        compiler_params=pltpu.CompilerParams(
            dimension_semantics=("parallel", "arbitrary"), vmem_limit_bytes=VMEM_LIMIT),
        name="dsa_sample_attend",
    )(pt_flat, proj, ka32, proj, qn.reshape(1, HEAD_DIM), mask_past, mask_new, bias_s, bias_s,
      *([pool_k] * pp), *([pool_v] * pp))


def _mla_sample_body(pt_ref, qlat_ref, qpe_ref, cnew_ref, rnew_ref, wkt_ref, wv_ref, *rest, pp, sub, nj, ts):
    cpages, rpages = rest[:pp], rest[pp:2 * pp]
    o_ref, m_ref, l_ref, acc_ref = rest[2 * pp:]
    j = pl.program_id(1)
    dq = NOPE_DIM + ROPE_DIM
    nrow = D_HEADS * ts

    @pl.when(j == 0)
    def _():
        m_ref[...] = jnp.full(m_ref.shape, MASK_NEG, F32)
        l_ref[...] = jnp.zeros(l_ref.shape, F32)
        acc_ref[...] = jnp.zeros(acc_ref.shape, F32)

    def update(c, rt, sel):
        n = c.shape[0]
        cb = c.astype(BF16)
        both = _dot_nt(jnp.concatenate([wkt_ref[...], qlat_ref[...]], axis=0), cb)
        kt = both[:D_HEADS * NOPE_DIM]
        ss = jnp.sum((kt * kt).reshape(D_HEADS, NOPE_DIM, n), axis=1)
        rss = jnp.sum(rt * rt, axis=0, keepdims=True)
        rinv = lax.rsqrt((ss + rss) * (1.0 / dq) + EPS) * (dq ** -0.5)
        lg = both[D_HEADS * NOPE_DIM:] + jnp.dot(qpe_ref[...], rt.astype(BF16),
                                                 preferred_element_type=F32)
        lg = (lg.reshape(D_HEADS, ts, n) * rinv[:, None, :]).reshape(nrow, n)
        if sel is not None:
            lg = jnp.where(sel, lg, MASK_NEG)
        m = jnp.max(lg, axis=-1, keepdims=True)
        p = jnp.exp(lg - m)
        if sel is not None:
            p = jnp.where(sel, p, 0.0)
        return m, jnp.sum(p, axis=-1, keepdims=True), jnp.dot(p.astype(BF16), cb, preferred_element_type=F32)

    def merge(parts):
        m_old = m_ref[...]
        m_new = m_old
        for m, _, _ in parts:
            m_new = jnp.maximum(m_new, m)
        alpha = jnp.exp(m_old - m_new)
        l_new = alpha * l_ref[...]
        acc = alpha * acc_ref[...]
        for m, l, pv in parts:
            a = jnp.exp(m - m_new)
            l_new = l_new + a * l
            acc = acc + a * pv
        m_ref[...] = m_new
        l_ref[...] = l_new
        acc_ref[...] = acc

    parts = []
    for s0 in range(0, pp, sub):
        c = jnp.concatenate([p[...] for p in cpages[s0:s0 + sub]], axis=0)
        rt = jnp.concatenate([p[...] for p in rpages[s0:s0 + sub]], axis=1)
        parts.append(update(c, rt, None))
    merge(parts)

    @pl.when(j == nj - 1)
    def _():
        lane = lax.broadcasted_iota(jnp.int32, (D_HEADS, ts, LANES), 2).reshape(nrow, LANES)
        row_t = lax.broadcasted_iota(jnp.int32, (D_HEADS, ts, LANES), 1).reshape(nrow, LANES)
        merge([update(cnew_ref[...], rnew_ref[...], lane <= row_t)])
        lat = (acc_ref[...] / l_ref[...]).astype(BF16)
        o_ref[...] = jnp.concatenate(
            [jnp.dot(lat[ts * h:ts * (h + 1)], wv_ref[h], preferred_element_type=F32) for h in range(D_HEADS)],
            axis=1)


def mla_sample_attend(qlat, qpe, cnew, rnew_t, wkt, wv, pool_ckv, pool_kpe_t, pt_flat, layer, *, nb, ts, n_pages,
                      pp=32, sub=16):
    pp = min(pp, n_pages)
    nj = n_pages // pp
    nrow = D_HEADS * ts

    def cspec(r):
        return pl.BlockSpec((None, None, PAGE_SIZE, KV_LORA), lambda b, j, pt: (layer, pt[b * n_pages + j * pp + r], 0, 0))

    def rspec(r):
        return pl.BlockSpec((None, None, ROPE_DIM, PAGE_SIZE), lambda b, j, pt: (layer, pt[b * n_pages + j * pp + r], 0, 0))

    grid_spec = pltpu.PrefetchScalarGridSpec(
        num_scalar_prefetch=1,
        grid=(nb, nj),
        in_specs=[
            pl.BlockSpec((None, nrow, KV_LORA), lambda b, j, pt: (b, 0, 0)),
            pl.BlockSpec((None, nrow, ROPE_DIM), lambda b, j, pt: (b, 0, 0)),
            pl.BlockSpec((None, LANES, KV_LORA), lambda b, j, pt: (b, 0, 0)),
            pl.BlockSpec((None, ROPE_DIM, LANES), lambda b, j, pt: (b, 0, 0)),
            pl.BlockSpec((D_HEADS * NOPE_DIM, KV_LORA), lambda b, j, pt: (0, 0)),
            pl.BlockSpec((D_HEADS, KV_LORA, D_VDIM), lambda b, j, pt: (0, 0, 0)),
        ] + [cspec(r) for r in range(pp)] + [rspec(r) for r in range(pp)],
        out_specs=pl.BlockSpec((ts, D_HEADS * D_VDIM), lambda b, j, pt: (b, 0)),
        scratch_shapes=[pltpu.VMEM((nrow, 1), F32), pltpu.VMEM((nrow, 1), F32), pltpu.VMEM((nrow, KV_LORA), F32)],
    )
    return pl.pallas_call(
        functools.partial(_mla_sample_body, pp=pp, sub=min(sub, pp), nj=nj, ts=ts),
        grid_spec=grid_spec,
        out_shape=jax.ShapeDtypeStruct((nb * ts, D_HEADS * D_VDIM), F32),
        compiler_params=pltpu.CompilerParams(
            dimension_semantics=("parallel", "arbitrary"), vmem_limit_bytes=VMEM_LIMIT),
        name="mla_sample",
    )(pt_flat, qlat, qpe, cnew, rnew_t, wkt, wv, *([pool_ckv] * pp), *([pool_kpe_t] * pp))


def _mem_attn_body(q_ref, k_ref, v_ref, qn_ref, o_ref, *, nb, tq, n_mem, head_rows):
    def head(ref, b, h):
        if head_rows:
            return ref[b, pl.ds(h, n_mem, stride=MEM_HEADS), :].astype(BF16)
        return ref[b, :, LANES * h:LANES * (h + 1)].astype(BF16)

    for b in range(nb):
        outs = []
        for h in range(MEM_HEADS):
            q = q_ref[tq * b:tq * (b + 1), LANES * h:LANES * (h + 1)]
            q = q * lax.rsqrt(jnp.mean(q * q, axis=-1, keepdims=True) + EPS) * qn_ref[...]
            lg = _dot_nt(q.astype(BF16), head(k_ref, b, h)) * (HEAD_DIM ** -0.5)
            p = _softmax_rows(lg)
            outs.append(jnp.dot(p.astype(BF16), head(v_ref, b, h), preferred_element_type=F32))
        o_ref[tq * b:tq * (b + 1), :] = jnp.concatenate(outs, axis=1)


def mem_attend(q, mk, mv, qn, *, row0, n_batch, tq, nb, n_mem, head_rows, k_index, v_index):
    w = MEM_HEADS * HEAD_DIM
    rb0 = row0 // (nb * tq)
    slab = (n_mem * MEM_HEADS, HEAD_DIM) if head_rows else (n_mem, w)
    kblock = (None,) * (mk.ndim - 3) + (nb,) + slab
    vblock = (None,) * (mv.ndim - 3) + (nb,) + slab
    return pl.pallas_call(
        functools.partial(_mem_attn_body, nb=nb, tq=tq, n_mem=n_mem, head_rows=head_rows),
        grid=(n_batch // nb,),
        in_specs=[pl.BlockSpec((nb * tq, w), lambda i: (rb0 + i, 0)),
                  pl.BlockSpec(kblock, k_index),
                  pl.BlockSpec(vblock, v_index),
                  pl.BlockSpec((1, HEAD_DIM), lambda i: (0, 0))],
        out_specs=pl.BlockSpec((nb * tq, w), lambda i: (i, 0)),
        out_shape=jax.ShapeDtypeStruct((n_batch * tq, w), F32),
        compiler_params=pltpu.CompilerParams(dimension_semantics=("parallel",), vmem_limit_bytes=VMEM_LIMIT),
        name="mem_attend",
    )(q, mk, mv, qn.reshape(1, HEAD_DIM))


def _causal_attn_body(q_ref, k_ref, v_ref, o_ref, *, qb, kw, i0, scale):
    i = i0 + pl.program_id(2)
    lg = _dot_nt(q_ref[...], k_ref[...]) * scale
    q_pos = i * qb + lax.broadcasted_iota(jnp.int32, (qb, kw), 0)
    k_pos = lax.broadcasted_iota(jnp.int32, (qb, kw), 1)
    p = _softmax_rows(jnp.where(k_pos <= q_pos, lg, MASK_NEG))
    o_ref[...] = jnp.dot(p.astype(BF16), v_ref[...], preferred_element_type=F32)


def causal_attend(q, k, v, *, nb, t, heads, dv, scale, qb=512, splits=2):
    nq = t // qb
    dqk = q.shape[1] // heads
    per = -(-nq // splits)
    outs = []
    for i0 in range(0, nq, per):
        ni = min(per, nq - i0)
        kw = min(t, (i0 + ni) * qb)
        assert t % kw == 0, (t, kw)
        o = pl.pallas_call(
            functools.partial(_causal_attn_body, qb=qb, kw=kw, i0=i0, scale=scale),
            grid=(nb, heads, ni),
            in_specs=[pl.BlockSpec((qb, dqk), lambda b, h, i, i0=i0: (b * nq + i0 + i, h)),
                      pl.BlockSpec((kw, dqk), lambda b, h, i, kw=kw: (b * (t // kw), h)),
                      pl.BlockSpec((kw, dv), lambda b, h, i, kw=kw: (b * (t // kw), h))],
            out_specs=pl.BlockSpec((qb, dv), lambda b, h, i, ni=ni: (b * ni + i, h)),
            out_shape=jax.ShapeDtypeStruct((nb * ni * qb, heads * dv), F32),
            compiler_params=pltpu.CompilerParams(
                dimension_semantics=("parallel", "parallel", "arbitrary"), vmem_limit_bytes=VMEM_LIMIT),
            name="causal_attend",
        )(q, k, v)
        outs.append(o.reshape(nb, ni * qb, heads * dv))
    return jnp.concatenate(outs, axis=1).reshape(nb * t, heads * dv)


def _cumsum_rows(x):
    c = x.shape[0]
    row = lax.broadcasted_iota(jnp.int32, x.shape, 0)
    sh = 1
    while sh < c:
        x = x + jnp.where(row >= sh, pltpu.roll(x, sh, axis=0), 0.0)
        sh *= 2
    return x


def _log_sigmoid(x):
    return jnp.minimum(x, 0.0) - jnp.log1p(jnp.exp(-jnp.abs(x)))


def _sigmoid(x):
    return 1.0 / (1.0 + jnp.exp(-x))


def _hgrn2_run(q_ref, f_ref, v_ref, g_ref, lb_ref, on, st_ref, o_ref, *, nh, t, c):
    row = lax.broadcasted_iota(jnp.int32, (c, 1), 0)

    def step(i, carry):
        r0 = pl.multiple_of(i * c, c)
        for h in range(nh):
            sl = slice(LANES * h, LANES * (h + 1))
            lb = lb_ref[h]
            fpre = f_ref[pl.ds(r0, c), sl]
            a = jnp.log(jnp.maximum(lb, LB_FLOOR))
            b = jnp.log1p(-lb) + _log_sigmoid(fpre)
            lf = jnp.maximum(a, b) + jnp.log1p(jnp.exp(-jnp.abs(a - b)))
            kk = (1.0 - lb) * _sigmoid(-fpre)
            qpre = q_ref[pl.ds(r0, c), sl]
            qv = qpre * _sigmoid(qpre)
            vv = v_ref[pl.ds(r0, c), sl]
            cb = _cumsum_rows(lf)
            st = st_ref[h]
            o = _dot_nt((qv * jnp.exp(cb)).astype(BF16), st.astype(BF16))
            for s in range(c):
                dec = jnp.exp(jnp.minimum(cb - cb[s:s + 1, :], 0.0))
                col = jnp.sum(qv * dec * kk[s:s + 1, :], axis=-1, keepdims=True)
                o = o + jnp.where(row >= s, col, 0.0) * vv[s:s + 1, :]
            c_last = cb[c - 1:c, :]
            kd = (kk * jnp.exp(c_last - cb)).astype(BF16)
            st_ref[h] = st * jnp.exp(c_last) + lax.dot_general(vv.astype(BF16), kd, (((0,), (0,)), ((), ())),
                                                               preferred_element_type=F32)
            gpre = g_ref[pl.ds(r0, c), sl]
            o = o * lax.rsqrt(jnp.mean(o * o, axis=-1, keepdims=True) + EPS) * on * (gpre * _sigmoid(gpre))
            o_ref[pl.ds(r0, c), sl] = o
        return carry

    lax.fori_loop(0, t // c, step, 0)


def _hgrn2_prompt_body(q_ref, f_ref, v_ref, g_ref, lb_ref, on_ref, o_ref, s_ref, st_ref, *, nh, t, c):
    st_ref[...] = jnp.zeros(st_ref.shape, F32)
    _hgrn2_run(q_ref, f_ref, v_ref, g_ref, lb_ref, on_ref[...], st_ref, o_ref, nh=nh, t=t, c=c)
    for h in range(nh):
        s_ref[h] = st_ref[h].T


def hgrn2_prompt(proj, lb, on, *, nb, t, nh=4):
    c = math.gcd(16, t)
    w = nh * LANES
    col = lambda c0: (lambda b, j: (b, c0 // w + j))
    return pl.pallas_call(
        functools.partial(_hgrn2_prompt_body, nh=nh, t=t, c=c),
        grid=(nb, B_HEADS // nh),
        in_specs=[pl.BlockSpec((t, w), col(EV_QB)), pl.BlockSpec((t, w), col(EV_FB)),
                  pl.BlockSpec((t, w), col(EV_IB)), pl.BlockSpec((t, w), col(EV_GB)),
                  pl.BlockSpec((nh, 1, B_DK), lambda b, j: (j, 0, 0)),
                  pl.BlockSpec((1, B_DV), lambda b, j: (0, 0))],
        out_specs=[pl.BlockSpec((t, w), lambda b, j: (b, j)),
                   pl.BlockSpec((None, nh, B_DK, B_DV), lambda b, j: (b, j, 0, 0))],
        out_shape=[jax.ShapeDtypeStruct((nb * t, B_HEADS * B_DV), F32),
                   jax.ShapeDtypeStruct((nb, B_HEADS, B_DK, B_DV), F32)],
        scratch_shapes=[pltpu.VMEM((nh, B_DV, B_DK), F32)],
        compiler_params=pltpu.CompilerParams(
            dimension_semantics=("parallel", "parallel"), vmem_limit_bytes=VMEM_LIMIT),
        name="hgrn2_prompt",
    )(proj, proj, proj, proj, lb.reshape(B_HEADS, 1, B_DK), on.reshape(1, B_DV))


def _hgrn2_sample_body(q_ref, f_ref, v_ref, g_ref, lb_ref, on_ref, s0_ref, o_ref, s_ref, st_ref, *, t):
    for h in range(B_HEADS):
        st_ref[h] = s0_ref[h].T
    _hgrn2_run(q_ref, f_ref, v_ref, g_ref, lb_ref, on_ref[...], st_ref, o_ref, nh=B_HEADS, t=t, c=t)
    for h in range(B_HEADS):
        s_ref[h] = st_ref[h].T


def hgrn2_sample(proj, lb, on, s0, *, row0, nb, t):
    rb0 = row0 // t
    w = B_HEADS * LANES
    col = lambda c0: (lambda b: (rb0 + b, c0 // w))
    return pl.pallas_call(
        functools.partial(_hgrn2_sample_body, t=t),
        grid=(nb,),
        in_specs=[pl.BlockSpec((t, w), col(EV_QB)), pl.BlockSpec((t, w), col(EV_FB)),
                  pl.BlockSpec((t, w), col(EV_IB)), pl.BlockSpec((t, w), col(EV_GB)),
                  pl.BlockSpec((B_HEADS, 1, B_DK), lambda b: (0, 0, 0)),
                  pl.BlockSpec((1, B_DV), lambda b: (0, 0)),
                  pl.BlockSpec((None, B_HEADS, B_DK, B_DV), lambda b: (b, 0, 0, 0))],
        out_specs=[pl.BlockSpec((t, w), lambda b: (b, 0)),
                   pl.BlockSpec((None, B_HEADS, B_DK, B_DV), lambda b: (b, 0, 0, 0))],
        out_shape=[jax.ShapeDtypeStruct((nb * t, w), F32),
                   jax.ShapeDtypeStruct((nb, B_HEADS, B_DK, B_DV), F32)],
        scratch_shapes=[pltpu.VMEM((B_HEADS, B_DV, B_DK), F32)],
        compiler_params=pltpu.CompilerParams(dimension_semantics=("parallel",), vmem_limit_bytes=VMEM_LIMIT),
        name="hgrn2_sample",
    )(proj, proj, proj, proj, lb.reshape(B_HEADS, 1, B_DK), on.reshape(1, B_DV), s0)


def _mlstm_chunk(qm, km_h, v, ic_row, ic_col, cb_col, cb_row, ct, n, m, c):
    r = lax.broadcasted_iota(jnp.int32, (c, c), 0)
    s = lax.broadcasted_iota(jnp.int32, (c, c), 1)
    dmat = jnp.where(r >= s, cb_col - cb_row + ic_row, MASK_NEG)
    m_state = cb_col + m
    m_t = jnp.maximum(m_state, jnp.max(dmat, axis=-1, keepdims=True))
    w = jnp.exp(dmat - m_t)
    w0 = jnp.exp(m_state - m_t)
    qb16 = qm.astype(BF16)
    qk = _dot_nt(qb16, km_h.astype(BF16)) * w
    num = (jnp.dot(qk.astype(BF16), v.astype(BF16), preferred_element_type=F32)
           + w0 * jnp.dot(qb16, ct.astype(BF16), preferred_element_type=F32))
    den = jnp.sum(qk, axis=-1, keepdims=True) + w0 * jnp.sum(qm * n, axis=-1, keepdims=True)
    hc = num / jnp.maximum(jnp.abs(den), jnp.exp(-m_t))
    m_last = m_t[c - 1:c, :]
    cb_last = cb_col[c - 1:c, :]
    ws = jnp.exp(cb_last - cb_col + ic_col - m_last)
    fs = jnp.exp(cb_last + m - m_last)
    ct = fs * ct + lax.dot_general(km_h.astype(BF16), (v * ws).astype(BF16), (((0,), (0,)), ((), ())),
                                   preferred_element_type=F32)
    n = fs * n + jnp.sum(ws * km_h, axis=0, keepdims=True)
    return hc, ct, n, m_last


def _mlstm_seq(q_ref, k_ref, v_ref, og_ref, g_ref, gb_ref, on, npairs, states, o_ref, *, t, c):
    lane = lax.broadcasted_iota(jnp.int32, (c, LANES), 1)

    def step(i, carry):
        r0 = pl.multiple_of(i * c, c)
        new = []
        for p in range(npairs):
            psl = slice(LANES * p, LANES * (p + 1))
            g = g_ref[pl.ds(r0, c), psl] + gb_ref[p]
            cb = _cumsum_rows(_log_sigmoid(g))
            gt = g.T
            cbt = cb.T
            q = q_ref[pl.ds(r0, c), psl]
            k = k_ref[pl.ds(r0, c), psl] * (C_DK ** -0.5)
            for half in range(2):
                ct, n, m = carry[2 * p + half]
                keep = (lane < C_DK) if half == 0 else (lane >= C_DK)
                vsl = slice(C_DV * (2 * p + half), C_DV * (2 * p + half + 1))
                hc, ct, n, m = _mlstm_chunk(
                    jnp.where(keep, q, 0.0), jnp.where(keep, k, 0.0), v_ref[pl.ds(r0, c), vsl],
                    gt[half:half + 1, :], g[:, half:half + 1], cb[:, 2 + half:3 + half], cbt[2 + half:3 + half, :],
                    ct, n, m, c)
                og = og_ref[pl.ds(r0, c), vsl]
                hc = hc * lax.rsqrt(jnp.mean(hc * hc, axis=-1, keepdims=True) + EPS) * on * _sigmoid(og)
                o_ref[pl.ds(r0, c), vsl] = hc
                new.append((ct, n, m))
        return tuple(new)

    return lax.fori_loop(0, t // c, step, tuple(states))


def _mlstm_body(*refs, t, c, npairs, has_state):
    q_ref, k_ref, v_ref, og_ref, g_ref, gb_ref, on_ref = refs[:7]
    refs = refs[7:]
    if has_state:
        c0_ref, n0_ref, m0_ref = refs[:3]
        refs = refs[3:]
    o_ref, c_ref, n_ref, m_ref = refs
    states = []
    for h in range(2 * npairs):
        half = h % 2
        if has_state:
            ct_h = c0_ref[h]
            z = jnp.zeros((C_DK, C_DV), F32)
            ct = jnp.concatenate([ct_h, z] if half == 0 else [z, ct_h], axis=0)
            n_h = n0_ref[h:h + 1, :]
            zn = jnp.zeros((1, C_DK), F32)
            n = jnp.concatenate([n_h, zn] if half == 0 else [zn, n_h], axis=1)
            m = m0_ref[:, h:h + 1]
        else:
            ct, n, m = jnp.zeros((2 * C_DK, C_DV), F32), jnp.zeros((1, 2 * C_DK), F32), jnp.zeros((1, 1), F32)
        states.append((ct, n, m))
    out = _mlstm_seq(q_ref, k_ref, v_ref, og_ref, g_ref, gb_ref, on_ref[...], npairs, states, o_ref, t=t, c=c)
    for h, (ct, n, m) in enumerate(out):
        rows = slice(C_DK * (h % 2), C_DK * (h % 2 + 1))
        c_ref[h] = ct[rows, :]
        n_ref[h] = n[:, rows]
        m_ref[h] = jnp.broadcast_to(m, (1, LANES))


def mlstm(proj, gb, on, state, *, row0, nb, t, npairs):
    c = math.gcd(CHUNK, t)
    rb0 = row0 // t
    npg = (C_HEADS // 2) // npairs
    hp = 2 * npairs
    qw, vw = npairs * LANES, npairs * 2 * C_DV
    in_specs = [
        pl.BlockSpec((t, qw), lambda b, j: (rb0 + b, OD_QC // qw + j)),
        pl.BlockSpec((t, qw), lambda b, j: (rb0 + b, OD_KC // qw + j)),
        pl.BlockSpec((t, vw), lambda b, j: (rb0 + b, OD_VC // vw + j)),
        pl.BlockSpec((t, vw), lambda b, j: (rb0 + b, OD_OC // vw + j)),
        pl.BlockSpec((t, qw), lambda b, j: (rb0 + b, OD_G // qw + j)),
        pl.BlockSpec((npairs, 1, LANES), lambda b, j: (j, 0, 0)),
        pl.BlockSpec((1, C_DV), lambda b, j: (0, 0)),
    ]
    args = [proj, proj, proj, proj, proj, gb, on.reshape(1, C_DV)]
    if state is not None:
        in_specs += [pl.BlockSpec((None, hp, C_DK, C_DV), lambda b, j: (b, j, 0, 0)),
                     pl.BlockSpec((None, hp, C_DK), lambda b, j: (b, j, 0)),
                     pl.BlockSpec((None, 1, hp), lambda b, j: (b, 0, j))]
        args += list(state)
    return pl.pallas_call(
        functools.partial(_mlstm_body, t=t, c=c, npairs=npairs, has_state=state is not None),
        grid=(nb, npg),
        in_specs=in_specs,
        out_specs=[pl.BlockSpec((t, vw), lambda b, j: (b, j)),
                   pl.BlockSpec((None, hp, C_DK, C_DV), lambda b, j: (b, j, 0, 0)),
                   pl.BlockSpec((None, hp, 1, C_DK), lambda b, j: (b, j, 0, 0)),
                   pl.BlockSpec((None, hp, 1, LANES), lambda b, j: (b, j, 0, 0))],
        out_shape=[jax.ShapeDtypeStruct((nb * t, C_HEADS * C_DV), F32),
                   jax.ShapeDtypeStruct((nb, C_HEADS, C_DK, C_DV), F32),
                   jax.ShapeDtypeStruct((nb, C_HEADS, 1, C_DK), F32),
                   jax.ShapeDtypeStruct((nb, C_HEADS, 1, LANES), F32)],
        compiler_params=pltpu.CompilerParams(
            dimension_semantics=("parallel", "parallel"), vmem_limit_bytes=VMEM_LIMIT),
        name="mlstm",
    )(*args)


def rms_norm(x, g):
    xf = x.astype(F32)
    y = xf * lax.rsqrt(jnp.mean(xf * xf, axis=-1, keepdims=True) + EPS)
    return (y * g.astype(F32)).astype(x.dtype)


def rel_bucket(rel):
    n = jnp.maximum(rel, 0)
    max_exact = REL_BUCKETS // 2
    nf = jnp.maximum(n, 1).astype(F32)
    large = max_exact + (jnp.log(nf / max_exact) / math.log(REL_MAX_DIST / max_exact)
                         * (REL_BUCKETS - max_exact)).astype(jnp.int32)
    return jnp.where(n < max_exact, n, jnp.minimum(large, REL_BUCKETS - 1))


def dsa_bias_tables(rel_bias, ts, past):
    table = rel_bias.astype(F32)[rel_bucket(jnp.arange(2 * LANES))]
    r = np.arange(LANES)[:, None]
    c = np.arange(LANES)[None, :]
    diag = table[np.clip(r - c, 0, 2 * LANES - 1)]
    prev = table[LANES + r - c]
    far = jnp.broadcast_to(table[2 * LANES - 1], (LANES, LANES, A_HEADS))
    tiles = jnp.moveaxis(jnp.stack([diag, prev, far]), -1, 0)
    n_far = max(past - 2 * LANES, 0)
    rel = past + np.arange(ts)[:, None] - np.arange(n_far, past + LANES)[None, :]
    sample = jnp.concatenate([jnp.broadcast_to(table[2 * LANES - 1], (ts, n_far, A_HEADS)),
                              table[np.clip(rel, 0, 2 * LANES - 1)]], axis=1)
    sample = jnp.moveaxis(sample, -1, 0)
    return tiles, sample


def rope_tables(pos):
    half = ROPE_DIM // 2
    inv = ROPE_THETA ** (-jnp.arange(half, dtype=F32) / half)
    ang = pos.astype(F32)[:, None] * inv[None, :]
    return jnp.cos(ang), jnp.sin(ang)


def rope_rows(x, tables):
    half = ROPE_DIM // 2
    shp = (x.shape[0],) + (1,) * (x.ndim - 2) + (half,)
    cos, sin = tables[0].reshape(shp), tables[1].reshape(shp)
    x1, x2 = x[..., :half], x[..., half:]
    return jnp.concatenate([x1 * cos - x2 * sin, x2 * cos + x1 * sin], axis=-1)


def mla_sample_pallas(qd, ckv, kpe, st, o, kn, wb):
    b, t = qd.shape[:2]
    qn = (qd[..., :NOPE_DIM] * kn[:NOPE_DIM]).reshape(b * t, D_HEADS * NOPE_DIM)
    qlat = matmul(qn, wb['d_kt_blockdiag'][o])
    qlat = qlat.reshape(b, t, D_HEADS, KV_LORA).transpose(0, 2, 1, 3).reshape(b, D_HEADS * t, KV_LORA)
    qpe = (qd[..., NOPE_DIM:] * kn[NOPE_DIM:]).transpose(0, 2, 1, 3).reshape(b, D_HEADS * t, ROPE_DIM)
    cnew = jnp.pad(ckv, ((0, 0), (0, LANES - t), (0, 0)))
    rnew_t = jnp.swapaxes(jnp.pad(kpe, ((0, 0), (0, LANES - t), (0, 0))), 1, 2)
    od = mla_sample_attend(qlat.astype(BF16), qpe.astype(BF16), cnew, rnew_t, wb['d_kt'][o], wb['d_v'][o],
                           st['cache_d_ckv'], st['pool_kpe_t'], st['pt_flat'], o,
                           nb=b, ts=t, n_pages=st['page_table'].shape[1])
    return od.reshape(b, t, D_HEADS, D_VDIM)


def kernel(x_prompt, x_sample, cache_a_k, cache_a_v, cache_a_kidx, state_b, state_c_C, state_c_n, state_c_m, cache_d_ckv, cache_d_kpe, cache_mem_k, cache_mem_v, page_table, mem_prompt, g_ffn1, w_ffn1_gu, w_ffn1_dn, g_mix, w_in_even, w_in_odd, w_mix_out, rel_bias, a_qn, a_kn, b_lb, b_on, c_gate_b, c_on, d_qa_g, d_kv_g, w_d_qb, w_d_kvb, d_qn, d_kn, g_mem, w_mem_q, w_mem_kv, w_mem_o, mem_qn, mem_kn, g_ffn2, w_ffn2_gu, w_ffn2_dn):
    bp, tp, d = x_prompt.shape
    bs, ts, _ = x_sample.shape
    n_p, n_s = bp * tp, bs * ts
    n_mem = mem_prompt.shape[1]
    past = page_table.shape[1] * PAGE_SIZE

    wo = w_in_odd.astype(BF16)
    n_odd = wo.shape[0]
    oc = np.cumsum([0] + ODD_SPLITS)
    gate_cols = []
    for pr in range(C_HEADS // 2):
        gate_cols += [wo[:, :, oc[3] + 2 * pr:oc[3] + 2 * pr + 2], wo[:, :, oc[4] + 2 * pr:oc[4] + 2 * pr + 2],
                      jnp.zeros((n_odd, d, LANES - 4), BF16)]
    in_odd = jnp.concatenate(
        [wo[:, :, oc[0]:oc[3]], wo[:, :, oc[5]:oc[6]]] + gate_cols
        + [wo[:, :, oc[6]:oc[9]], jnp.zeros((n_odd, d, LANES - ROPE_DIM), BF16)], axis=-1)
    gb = c_gate_b.astype(F32)
    gate_bias = jnp.concatenate(
        [gb[:, 0].reshape(n_odd, C_HEADS // 2, 2), gb[:, 1].reshape(n_odd, C_HEADS // 2, 2),
         jnp.zeros((n_odd, C_HEADS // 2, LANES - 4), F32)], axis=-1)[:, :, None, :]
    we = w_in_even.astype(BF16)
    c_wi = sum(EVEN_SPLITS[:4])
    c_ki = c_wi + IDX_HEADS
    c_qb = c_ki + IDX_DIM
    n_even = we.shape[0]
    c_ka = EVEN_SPLITS[0]
    in_even = jnp.concatenate(
        [we[:, :, :c_ka], we[:, :, c_qb:], we[:, :, c_ka:c_wi], we[:, :, c_ki:c_qb], we[:, :, c_ki:c_qb],
         we[:, :, c_wi:c_ki], jnp.zeros((n_even, d, LANES - IDX_HEADS), BF16)], axis=-1)
    kvb = w_d_kvb.astype(BF16).reshape(-1, KV_LORA, D_HEADS, NOPE_DIM + D_VDIM)
    d_kt3 = kvb[..., :NOPE_DIM].transpose(0, 2, 3, 1)
    eye = jnp.eye(D_HEADS, dtype=BF16)
    d_kt_bd = (d_kt3[:, :, :, None, :] * eye[None, :, None, :, None]).reshape(
        -1, D_HEADS * NOPE_DIM, D_HEADS * KV_LORA)
    wb = {
        'ffn1_gu': w_ffn1_gu.astype(BF16), 'ffn1_dn': w_ffn1_dn.astype(BF16),
        'ffn2_gu': w_ffn2_gu.astype(BF16), 'ffn2_dn': w_ffn2_dn.astype(BF16),
        'in_even': in_even,
        'in_odd': in_odd,
        'mix_out': w_mix_out.astype(BF16), 'w_d_qb': w_d_qb.astype(BF16), 'w_d_kvb': w_d_kvb.astype(BF16),
        'mem_q': w_mem_q.astype(BF16), 'mem_kv': w_mem_kv.astype(BF16), 'mem_o': w_mem_o.astype(BF16),
        'd_kt': d_kt3.reshape(-1, D_HEADS * NOPE_DIM, KV_LORA), 'd_kt_blockdiag': d_kt_bd,
        'd_v': kvb[..., NOPE_DIM:].transpose(0, 2, 1, 3),
    }
    bias_tiles, bias_sample = dsa_bias_tables(rel_bias, ts, past)
    n_pages = page_table.shape[1]
    n_phys = cache_a_k.shape[1]
    pool_k = cache_a_k.reshape(-1, n_phys, PAGE_SIZE * A_KV_HEADS, HEAD_DIM)
    pool_v = cache_a_v.reshape(-1, n_phys, PAGE_SIZE * A_KV_HEADS, HEAD_DIM)
    pool_kidx_t = jnp.swapaxes(cache_a_kidx, 2, 3)
    pool_kpe_t = jnp.swapaxes(cache_d_kpe, 2, 3)
    mem_k_rows = cache_mem_k.reshape(DEPTH, bs, n_mem * MEM_HEADS, HEAD_DIM)
    mem_v_rows = cache_mem_v.reshape(DEPTH, bs, n_mem * MEM_HEADS, HEAD_DIM)
    pt_flat = page_table.reshape(-1)

    lb_soft = jax.nn.softmax(b_lb.astype(F32), axis=0)
    lower_bound = jnp.cumsum(lb_soft, axis=0) - lb_soft[0]

    pos_all = rope_tables(jnp.concatenate([jnp.tile(jnp.arange(tp), bp), jnp.tile(past + jnp.arange(ts), bs)]))
    st = {'cache_d_ckv': cache_d_ckv, 'pool_kpe_t': pool_kpe_t, 'page_table': page_table, 'pt_flat': pt_flat}

    x = jnp.concatenate([x_prompt.reshape(n_p, d), x_sample.reshape(n_s, d)], axis=0)
    mem2d = mem_prompt.reshape(bp * n_mem, d)
    new_p = {n: [] for n in ('a_k', 'a_v', 'a_kidx', 'b_S', 'c_C', 'c_n', 'c_m', 'd_ckv', 'd_kpe', 'mem_k', 'mem_v')}
    new_s = {n: [] for n in ('a_k', 'a_v', 'a_kidx', 'b_S', 'c_C', 'c_n', 'c_m', 'd_ckv', 'd_kpe')}

    for l in range(DEPTH):
        x = ffn_half_step(x, g_ffn1[l], wb['ffn1_gu'], wb['ffn1_dn'], l)
        if l % 2 == 0:
            e = l // 2
            proj = matmul(x, wb['in_even'], layer=e, gain=g_mix[l], tm=1024, tn=1280)
            ka32, ka16 = head_norm(proj, EV_KA // (A_KV_HEADS * HEAD_DIM), A_KV_HEADS, a_kn[e])
            oa_p = dsa_prompt_attend(proj, ka16, bias_tiles, a_qn[e], nb=bp, t=tp)
            sel_past, sel_new = dsa_sample_select(proj, pool_kidx_t, pt_flat, e, row0=n_p, nb=bs, ts=ts,
                                                  n_pages=n_pages)
            oa_s = dsa_sample_attend(proj, ka32, sel_past, sel_new, bias_sample, pool_k, pool_v, pt_flat, e, a_qn[e],
                                     row0=n_p, nb=bs, ts=ts, n_pages=n_pages)
            lb = lower_bound[e].reshape(B_HEADS, B_DK)
            ob_p, s_p = hgrn2_prompt(proj, lb, b_on[e], nb=bp, t=tp)
            ob_s, s_s = hgrn2_sample(proj, lb, b_on[e], state_b[e], row0=n_p, nb=bs, t=ts)
            mixed_p = jnp.concatenate([oa_p, ob_p], axis=-1)
            mixed_s = jnp.concatenate([oa_s, ob_s], axis=-1)
            va = proj[:, EV_VA:EV_VA + A_KV_HEADS * HEAD_DIM]
            ki = proj[:, EV_KIKI:EV_KIKI + IDX_DIM]
            for new, rows, nb_, t_, s_new in ((new_p, slice(0, n_p), bp, tp, s_p), (new_s, slice(n_p, None), bs, ts, s_s)):
                new['a_k'].append(ka32[rows].reshape(nb_, t_, A_KV_HEADS, HEAD_DIM))
                new['a_v'].append(va[rows].reshape(nb_, t_, A_KV_HEADS, HEAD_DIM))
                new['a_kidx'].append(ki[rows].reshape(nb_, t_, IDX_DIM))
                new['b_S'].append(s_new)
        else:
            o = l // 2
            proj = matmul(x, wb['in_odd'], layer=o, gain=g_mix[l], tm=1024, tn=896)
            hc_p, cc_p, cn_p, cm_p = mlstm(proj, gate_bias[o], c_on[o], None, row0=0, nb=bp, t=tp, npairs=2)
            hc_s, cc_s, cn_s, cm_s = mlstm(
                proj, gate_bias[o], c_on[o],
                (jnp.swapaxes(state_c_C[o], -1, -2), state_c_n[o], state_c_m[o].reshape(bs, 1, C_HEADS)),
                row0=n_p, nb=bs, t=ts, npairs=C_HEADS // 2)
            qf = matmul(proj, wb['w_d_qb'], layer=o, gain=d_qa_g[o], a_col=OD_QA // Q_LORA)
            qf = qf.reshape(n_p + n_s, D_HEADS, NOPE_DIM + ROPE_DIM)
            qd = rms_norm(jnp.concatenate([qf[..., :NOPE_DIM], rope_rows(qf[..., NOPE_DIM:], pos_all)], axis=-1),
                          d_qn[o])
            ckv = rms_norm(proj[:, OD_CKV:OD_CKV + KV_LORA], d_kv_g[o])
            kpe = rope_rows(proj[:, OD_KPE:OD_KPE + ROPE_DIM], pos_all)
            kv = matmul(ckv[:n_p], wb['w_d_kvb'], layer=o).reshape(n_p, D_HEADS, NOPE_DIM + D_VDIM)
            k_pe = jnp.broadcast_to(kpe[:n_p, None, :], (n_p, D_HEADS, ROPE_DIM))
            kd = rms_norm(jnp.concatenate([kv[..., :NOPE_DIM], k_pe], axis=-1), d_kn[o])
            pad = ((0, 0), (0, 0), (0, 2 * LANES - NOPE_DIM - ROPE_DIM))
            q16 = jnp.pad(qd[:n_p], pad).astype(BF16).reshape(n_p, D_HEADS * 2 * LANES)
            k16 = jnp.pad(kd, pad).astype(BF16).reshape(n_p, D_HEADS * 2 * LANES)
            v16 = kv[..., NOPE_DIM:].astype(BF16).reshape(n_p, D_HEADS * D_VDIM)
            od_p = causal_attend(q16, k16, v16, nb=bp, t=tp, heads=D_HEADS, dv=D_VDIM,
                                 scale=(NOPE_DIM + ROPE_DIM) ** -0.5)
            od_s = mla_sample_pallas(qd[n_p:].reshape(bs, ts, D_HEADS, NOPE_DIM + ROPE_DIM),
                                     ckv[n_p:].reshape(bs, ts, KV_LORA), kpe[n_p:].reshape(bs, ts, ROPE_DIM),
                                     st, o, d_kn[o], wb).reshape(n_s, D_HEADS * D_VDIM)
            mixed_p = jnp.concatenate([hc_p, od_p], axis=-1)
            mixed_s = jnp.concatenate([hc_s, od_s], axis=-1)
            for new, rows, nb_, t_, cc, cn, cm in ((new_p, slice(0, n_p), bp, tp, cc_p, cn_p, cm_p),
                                                   (new_s, slice(n_p, None), bs, ts, cc_s, cn_s, cm_s)):
                new['d_ckv'].append(ckv[rows].reshape(nb_, t_, KV_LORA))
                new['d_kpe'].append(kpe[rows].reshape(nb_, t_, ROPE_DIM))
                new['c_C'].append(jnp.swapaxes(cc, -1, -2))
                new['c_n'].append(cn[:, :, 0, :])
                new['c_m'].append(cm[:, :, 0, 0])
        mixed = jnp.concatenate([mixed_p, mixed_s], axis=0)
        x = matmul(mixed, wb['mix_out'], layer=l, res=x)

        kvm = matmul(mem2d, wb['mem_kv'], layer=l)
        mk32, mk16 = head_norm(kvm, 0, MEM_HEADS, mem_kn[l], tm=bp * n_mem)
        wm = MEM_HEADS * HEAD_DIM
        new_p['mem_k'].append(mk32.reshape(bp, n_mem, MEM_HEADS, HEAD_DIM))
        new_p['mem_v'].append(kvm[:, wm:].reshape(bp, n_mem, MEM_HEADS, HEAD_DIM))
        q = matmul(x, wb['mem_q'], layer=l, gain=g_mem[l])
        nqb = 4
        o_p = mem_attend(q, mk16.reshape(bp, n_mem, wm), kvm.reshape(bp, n_mem, 2 * wm), mem_qn[l],
                         row0=0, n_batch=bp * nqb, tq=tp // nqb, nb=1, n_mem=n_mem, head_rows=False,
                         k_index=lambda i: (i // nqb, 0, 0), v_index=lambda i: (i // nqb, 0, 1))
        o_s = mem_attend(q, mem_k_rows, mem_v_rows, mem_qn[l], row0=n_p, n_batch=bs, tq=ts, nb=8,
                         n_mem=n_mem, head_rows=True,
                         k_index=lambda i, l=l: (l, i, 0, 0), v_index=lambda i, l=l: (l, i, 0, 0))
        o_all = jnp.concatenate([o_p, o_s], axis=0)
        x = matmul(o_all, wb['mem_o'], layer=l, res=x)
        x = ffn_half_step(x, g_ffn2[l], wb['ffn2_gu'], wb['ffn2_dn'], l)

    sp = {n: jnp.stack(v) for n, v in new_p.items()}
    ss = {n: jnp.stack(v) for n, v in new_s.items()}
    return (x[:n_p].reshape(bp, tp, d), x[n_p:].reshape(bs, ts, d),
            sp['a_k'], sp['a_v'], sp['a_kidx'], sp['b_S'], sp['c_C'], sp['c_n'], sp['c_m'], sp['d_ckv'], sp['d_kpe'],
            sp['mem_k'], sp['mem_v'],
            ss['a_k'], ss['a_v'], ss['a_kidx'], ss['b_S'], ss['c_C'], ss['c_n'], ss['c_m'], ss['d_ckv'], ss['d_kpe'])
```

```python
import functools
import math

import jax
import jax.numpy as jnp
import numpy as np
from jax import lax
from jax.experimental import pallas as pl
from jax.experimental.pallas import tpu as pltpu

D_MODEL = 2048
DEPTH = 4
PAGE_SIZE = 128
HEAD_DIM = 128
A_HEADS = 8
A_KV_HEADS = 2
IDX_HEADS = 8
IDX_DIM = 64
TOPK_MAX = 256
REL_BUCKETS = 32
REL_MAX_DIST = 128
B_HEADS = 8
B_DK = 128
B_DV = 128
C_HEADS = 8
C_DK = 64
C_DV = 128
D_HEADS = 8
Q_LORA = 512
KV_LORA = 256
NOPE_DIM = 128
ROPE_DIM = 64
D_VDIM = 128
ROPE_THETA = 10000.0
MEM_HEADS = 4
D_FF = 5632
CHUNK = 64
Q_BLOCK = 128
EPS = 1e-6
MASK_NEG = -1e30
LB_FLOOR = 1e-20
F32 = jnp.float32
BF16 = jnp.bfloat16

EVEN_SPLITS = [A_HEADS * HEAD_DIM, A_KV_HEADS * HEAD_DIM, A_KV_HEADS * HEAD_DIM, IDX_HEADS * IDX_DIM, IDX_HEADS,
               IDX_DIM, B_HEADS * B_DK, B_HEADS * B_DK, B_HEADS * B_DV, B_HEADS * B_DV]
ODD_SPLITS = [C_HEADS * C_DK, C_HEADS * C_DK, C_HEADS * C_DV, C_HEADS, C_HEADS, C_HEADS * C_DV, Q_LORA, KV_LORA,
              ROPE_DIM]
EVEN_IN = sum(EVEN_SPLITS)
ODD_IN = sum(ODD_SPLITS)

EV_QA, EV_QB, EV_FB, EV_IB, EV_GB, EV_KA, EV_VA, EV_QI, EV_KIKI, EV_WI, EVEN_PAD = (
    0, 1024, 2048, 3072, 4096, 5120, 5376, 5632, 6144, 6272, 6400)
OD_QC, OD_KC, OD_VC, OD_OC, OD_G, OD_QA, OD_CKV, OD_KPE, ODD_PAD = (
    0, 512, 1024, 2048, 3072, 3584, 4096, 4352, 4480)

LANES = 128
VMEM_LIMIT = 56 * 1024 * 1024


def _round_up(n, m):
    return (n + m - 1) // m * m


def _ffn_body(x_ref, g_ref, wg_ref, wu_ref, wd_ref, o_ref, n_ref):
    f = pl.program_id(1)

    @pl.when(f == 0)
    def _():
        x = x_ref[...]
        ms = jnp.mean(x * x, axis=-1, keepdims=True)
        n_ref[...] = (x * lax.rsqrt(ms + EPS) * g_ref[...]).astype(BF16)
        o_ref[...] = x

    n = n_ref[...]
    a = jnp.dot(n, wg_ref[...], preferred_element_type=F32)
    b = jnp.dot(n, wu_ref[...], preferred_element_type=F32)
    h = (a * jax.nn.sigmoid(a) * b).astype(BF16)
    o_ref[...] += 0.5 * jnp.dot(h, wd_ref[...], preferred_element_type=F32)


def ffn_half_step(x, g, w_gu, w_dn, layer, *, tm=768, tf=512):
    m, d = x.shape
    ff = w_dn.shape[1]
    nf = ff // tf
    assert m % tm == 0 and ff % tf == 0, (m, tm, ff, tf)
    return pl.pallas_call(
        _ffn_body,
        grid=(m // tm, nf),
        in_specs=[
            pl.BlockSpec((tm, d), lambda i, f: (i, 0)),
            pl.BlockSpec((1, d), lambda i, f: (0, 0)),
            pl.BlockSpec((None, d, tf), lambda i, f: (layer, 0, f)),
            pl.BlockSpec((None, d, tf), lambda i, f: (layer, 0, f + nf)),
            pl.BlockSpec((None, tf, d), lambda i, f: (layer, f, 0)),
        ],
        out_specs=pl.BlockSpec((tm, d), lambda i, f: (i, 0)),
        out_shape=jax.ShapeDtypeStruct((m, d), F32),
        scratch_shapes=[pltpu.VMEM((tm, d), BF16)],
        compiler_params=pltpu.CompilerParams(
            dimension_semantics=("parallel", "arbitrary"), vmem_limit_bytes=VMEM_LIMIT),
        name="ffn_half_step",
    )(x, g.reshape(1, d), w_gu, w_gu, w_dn)


def _mm_body(*refs, has_gain, has_res, alpha):
    a_ref, w_ref = refs[0], refs[1]
    k = 2
    g_ref = res_ref = None
    if has_gain:
        g_ref = refs[k]
        k += 1
    if has_res:
        res_ref = refs[k]
        k += 1
    o_ref, n_ref = refs[k], refs[k + 1]

    @pl.when(pl.program_id(1) == 0)
    def _():
        a = a_ref[...]
        if has_gain:
            ms = jnp.mean(a * a, axis=-1, keepdims=True)
            a = a * lax.rsqrt(ms + EPS) * g_ref[...]
        n_ref[...] = a.astype(BF16)

    acc = jnp.dot(n_ref[...], w_ref[...], preferred_element_type=F32)
    if has_res:
        acc = res_ref[...] + alpha * acc
    o_ref[...] = acc


def matmul(a, w, *, gain=None, res=None, alpha=1.0, tm=512, tn=512, a_col=0, layer=None):
    m = a.shape[0]
    kdim, n = w.shape[-2:]
    tm = min(tm, m)
    tn = min(tn, n)
    assert m % tm == 0 and n % tn == 0, (m, tm, n, tn)
    w_spec = (pl.BlockSpec((kdim, tn), lambda i, j: (0, j)) if layer is None
              else pl.BlockSpec((None, kdim, tn), lambda i, j: (layer, 0, j)))
    in_specs = [pl.BlockSpec((tm, kdim), lambda i, j: (i, a_col)), w_spec]
    args = [a, w]
    if gain is not None:
        in_specs.append(pl.BlockSpec((1, kdim), lambda i, j: (0, 0)))
        args.append(gain.reshape(1, kdim))
    if res is not None:
        in_specs.append(pl.BlockSpec((tm, tn), lambda i, j: (i, j)))
        args.append(res)
    return pl.pallas_call(
        functools.partial(_mm_body, has_gain=gain is not None, has_res=res is not None, alpha=alpha),
        grid=(m // tm, n // tn),
        in_specs=in_specs,
        out_specs=pl.BlockSpec((tm, tn), lambda i, j: (i, j)),
        out_shape=jax.ShapeDtypeStruct((m, n), F32),
        scratch_shapes=[pltpu.VMEM((tm, kdim), BF16)],
        compiler_params=pltpu.CompilerParams(
            dimension_semantics=("parallel", "arbitrary"), vmem_limit_bytes=VMEM_LIMIT),
        name="matmul",
    )(*args)


def _head_norm_body(x_ref, g_ref, o32_ref, o16_ref, *, heads):
    outs = []
    for h in range(heads):
        x = x_ref[:, LANES * h:LANES * (h + 1)]
        outs.append(x * lax.rsqrt(jnp.mean(x * x, axis=-1, keepdims=True) + EPS) * g_ref[...])
    y = jnp.concatenate(outs, axis=1)
    o32_ref[...] = y
    o16_ref[...] = y.astype(BF16)


def head_norm(x, col_block, heads, gain, *, tm=512):
    m = x.shape[0]
    w = heads * LANES
    return pl.pallas_call(
        functools.partial(_head_norm_body, heads=heads),
        grid=(m // tm,),
        in_specs=[pl.BlockSpec((tm, w), lambda i: (i, col_block)),
                  pl.BlockSpec((1, LANES), lambda i: (0, 0))],
        out_specs=[pl.BlockSpec((tm, w), lambda i: (i, 0)), pl.BlockSpec((tm, w), lambda i: (i, 0))],
        out_shape=[jax.ShapeDtypeStruct((m, w), F32), jax.ShapeDtypeStruct((m, w), BF16)],
        compiler_params=pltpu.CompilerParams(dimension_semantics=("parallel",)),
        name="head_norm",
    )(x, gain.reshape(1, LANES))


INT_MIN = -2 ** 31
INT_MAX = 2 ** 31 - 1
_NT = (((1,), (1,)), ((), ()))


def _dot_nt(a, b):
    return lax.dot_general(a, b, _NT, preferred_element_type=F32)


def _sort_key(x):
    b = lax.bitcast_convert_type(x + 0.0, jnp.int32)
    return jnp.where(b >= 0, b, b ^ jnp.int32(INT_MAX))


def _row_count(cond):
    c = jnp.where(cond, 1.0, 0.0)
    if c.ndim == 3:
        c = jnp.sum(c, axis=0)
    return jnp.sum(c, axis=-1, keepdims=True)


def _topk_masks(pieces, k, idx_bits):
    kf = float(k)

    def count(fn):
        tot = None
        for key, idx in pieces:
            c = _row_count(fn(key, idx))
            tot = c if tot is None else tot + c
        return tot

    t0 = jnp.where(count(lambda key, idx: key >= 0) >= kf, jnp.int32(0), jnp.int32(INT_MIN))

    def body(i, t):
        cand = t | jnp.left_shift(jnp.int32(1), 30 - i)
        return jnp.where(count(lambda key, idx: key >= cand) >= kf, cand, t)

    thr = lax.fori_loop(0, 31, body, t0)
    need = kf - count(lambda key, idx: key > thr)
    ties = [(jnp.where(key == thr, idx, INT_MAX), idx) for key, idx in pieces]

    def count_ties(cand):
        tot = None
        for tie, _ in ties:
            c = _row_count(tie < cand)
            tot = c if tot is None else tot + c
        return tot

    def body2(i, cur):
        cand = cur + jnp.left_shift(jnp.int32(1), idx_bits - 1 - i)
        return jnp.where(count_ties(cand) < need, cand, cur)

    cut = lax.fori_loop(0, idx_bits, body2, jnp.zeros_like(thr))
    return [jnp.where(key > thr, 1.0, jnp.where(tie <= cut, 1.0, 0.0))
            for (key, _), (tie, _) in zip(pieces, ties)]


def _softmax_rows(lg):
    mx = jnp.max(lg, axis=-1, keepdims=True)
    p = jnp.exp(lg - mx)
    return p / jnp.sum(p, axis=-1, keepdims=True)


def _dsa_prompt_body(qa_ref, qi_ref, wi_ref, kiki_ref, k_ref, v_ref, bias_ref, qn_ref, o_ref, *, qb, t, n_sel, i0, kw):
    i = i0 + pl.program_id(1)
    nsub = qb // LANES

    def run(kw):
        nk = kw // LANES
        kiki = kiki_ref[0:kw, :].astype(BF16)
        qi = qi_ref[...] * (IDX_DIM ** -0.5)
        w = wi_ref[...] * (IDX_HEADS ** -0.5)
        lane = lax.broadcasted_iota(jnp.int32, (qb, LANES), 1)
        score = jnp.zeros((qb, kw), F32)
        for h in range(IDX_HEADS):
            blk = qi[:, LANES * (h // 2):LANES * (h // 2 + 1)]
            keep = (lane < IDX_DIM) if h % 2 == 0 else (lane >= IDX_DIM)
            s = _dot_nt(jnp.where(keep, blk, 0.0).astype(BF16), kiki)
            score = score + jnp.maximum(s, 0.0) * w[:, h:h + 1]
        q_pos = i * qb + lax.broadcasted_iota(jnp.int32, (qb, kw), 0)
        k_pos = lax.broadcasted_iota(jnp.int32, (qb, kw), 1)
        allowed = k_pos <= q_pos
        score = jnp.where(allowed, score, MASK_NEG)
        pieces = [(_sort_key(score), k_pos)]
        nv = _round_up(n_sel, LANES) if kw < t else 0
        if nv:
            pieces.append((_sort_key(jnp.full((qb, nv), MASK_NEG, F32)),
                           kw + lax.broadcasted_iota(jnp.int32, (qb, nv), 1)))
        selm = _topk_masks(pieces, n_sel, max(1, (kw + nv - 1).bit_length()))[0]
        sel = jnp.where(allowed, selm, 0.0) > 0.5

        k = k_ref[0:kw, :]
        v = v_ref[0:kw, :].astype(BF16)
        outs = []
        for h in range(A_HEADS):
            g = h // (A_HEADS // A_KV_HEADS)
            rows = []
            for a in range(nsub):
                d0 = i * nsub + a
                tiles = []
                for j in range(nk):
                    d = d0 - j
                    tiles.append(jnp.where(d == 0, bias_ref[h, 0],
                                           jnp.where(d == 1, bias_ref[h, 1], bias_ref[h, 2])))
                rows.append(jnp.concatenate(tiles, axis=1))
            bias = jnp.concatenate(rows, axis=0) if nsub > 1 else rows[0]
            q = qa_ref[:, LANES * h:LANES * (h + 1)]
            q = q * lax.rsqrt(jnp.mean(q * q, axis=-1, keepdims=True) + EPS) * qn_ref[...]
            lg = _dot_nt(q.astype(BF16), k[:, LANES * g:LANES * (g + 1)]) * (HEAD_DIM ** -0.5) + bias
            p = _softmax_rows(jnp.where(sel, lg, MASK_NEG))
            outs.append(jnp.dot(p.astype(BF16), v[:, LANES * g:LANES * (g + 1)], preferred_element_type=F32))
        o_ref[...] = jnp.concatenate(outs, axis=1)

    run(kw)


def dsa_prompt_attend(proj, ka16, bias_tiles, qn, *, nb, t, qb=256, splits=2):
    nq = t // qb
    wq = A_HEADS * HEAD_DIM
    per = -(-nq // splits)
    outs = []
    for i0 in range(0, nq, per):
        ni = min(per, nq - i0)
        o = _dsa_prompt_call(proj, ka16, bias_tiles, qn, nb=nb, t=t, qb=qb, i0=i0, ni=ni, kw=min(t, (i0 + ni) * qb))
        outs.append(o.reshape(nb, ni * qb, wq))
    return jnp.concatenate(outs, axis=1).reshape(nb * t, wq)


def _dsa_prompt_call(proj, ka16, bias_tiles, qn, *, nb, t, qb, i0, ni, kw):
    nq = t // qb
    n_sel = min(TOPK_MAX, t // 4)
    wq = A_HEADS * HEAD_DIM
    assert t % kw == 0, (t, kw)
    return pl.pallas_call(
        functools.partial(_dsa_prompt_body, qb=qb, t=t, n_sel=n_sel, i0=i0, kw=kw),
        grid=(nb, ni),
        in_specs=[
            pl.BlockSpec((qb, wq), lambda b, i: (b * nq + i0 + i, 0)),
            pl.BlockSpec((qb, IDX_HEADS * IDX_DIM), lambda b, i: (b * nq + i0 + i, EV_QI // (IDX_HEADS * IDX_DIM))),
            pl.BlockSpec((qb, LANES), lambda b, i: (b * nq + i0 + i, EV_WI // LANES)),
            pl.BlockSpec((kw, LANES), lambda b, i: (b * (t // kw), EV_KIKI // LANES)),
            pl.BlockSpec((kw, A_KV_HEADS * HEAD_DIM), lambda b, i: (b * (t // kw), 0)),
            pl.BlockSpec((kw, A_KV_HEADS * HEAD_DIM), lambda b, i: (b * (t // kw), EV_VA // (A_KV_HEADS * HEAD_DIM))),
            pl.BlockSpec((A_HEADS, 3, LANES, LANES), lambda b, i: (0, 0, 0, 0)),
            pl.BlockSpec((1, HEAD_DIM), lambda b, i: (0, 0)),
        ],
        out_specs=pl.BlockSpec((qb, wq), lambda b, i: (b * ni + i, 0)),
        out_shape=jax.ShapeDtypeStruct((nb * ni * qb, wq), F32),
        compiler_params=pltpu.CompilerParams(
            dimension_semantics=("parallel", "arbitrary"), vmem_limit_bytes=VMEM_LIMIT),
        name="dsa_prompt",
    )(proj, proj, proj, proj, ka16, proj, bias_tiles, qn.reshape(1, HEAD_DIM))


def _idx_head_sum(s, w):
    tot = None
    for h in range(IDX_HEADS):
        c = jnp.maximum(s[8 * h:8 * (h + 1)], 0.0) * w[:, h:h + 1]
        tot = c if tot is None else tot + c
    return tot


def _dsa_score_body(pt_ref, qi_ref, wi_ref, knew_ref, *rest, pp, ts):
    pages, sp_ref, sn_ref = rest[:pp], rest[pp], rest[pp + 1]
    qi = qi_ref[...] * (IDX_DIM ** -0.5)
    a = jnp.concatenate([qi[:, IDX_DIM * h:IDX_DIM * (h + 1)] for h in range(IDX_HEADS)], axis=0).astype(BF16)
    w = wi_ref[...] * (IDX_HEADS ** -0.5)
    kt = jnp.concatenate([p[...] for p in pages], axis=1).astype(BF16)
    sp_ref[...] = _idx_head_sum(jnp.dot(a, kt, preferred_element_type=F32), w)

    @pl.when(pl.program_id(1) == 0)
    def _():
        knew = knew_ref[...][:, :IDX_DIM]
        kn = jnp.concatenate([knew, jnp.zeros((LANES - ts, IDX_DIM), F32)], axis=0).astype(BF16)
        sn_ref[...] = _idx_head_sum(_dot_nt(a, kn), w)


def _dsa_mask_body(sp_ref, sn_ref, mp_ref, mn_ref, *, ts, n_sel):
    rows, past = sp_ref.shape
    lane = lax.broadcasted_iota(jnp.int32, (rows // ts, ts, LANES), 2).reshape(rows, LANES)
    t_row = lax.broadcasted_iota(jnp.int32, (rows // ts, ts, LANES), 1).reshape(rows, LANES)
    causal = lane <= t_row
    snew = jnp.where(causal, sn_ref[...], MASK_NEG)
    key_new = jnp.where(lane < ts, _sort_key(snew), INT_MIN)
    idx_past = lax.broadcasted_iota(jnp.int32, (rows, past), 1)
    m_past, m_new = _topk_masks([(_sort_key(sp_ref[...]), idx_past), (key_new, past + lane)], n_sel,
                                (past + LANES - 1).bit_length())
    mp_ref[...] = m_past
    mn_ref[...] = jnp.where(causal, m_new, 0.0)


def dsa_sample_select(proj, pool_kidx_t, pt_flat, layer, *, row0, nb, ts, n_pages, pp=32, rows_per_step=128):
    pp = min(pp, n_pages)
    nj = n_pages // pp
    tk = pp * PAGE_SIZE
    past = n_pages * PAGE_SIZE
    n_sel = min(TOPK_MAX, (past + ts) // 4)
    rb0 = row0 // ts
    rows = nb * ts
    rstep = min(rows_per_step, rows)
    sp, sn = _dsa_sample_scores(proj, pool_kidx_t, pt_flat, layer, rb0=rb0, nb=nb, ts=ts, n_pages=n_pages, pp=pp)
    mp, mn = pl.pallas_call(
        functools.partial(_dsa_mask_body, ts=ts, n_sel=n_sel),
        grid=(rows // rstep,),
        in_specs=[pl.BlockSpec((rstep, past), lambda i: (i, 0)), pl.BlockSpec((rstep, LANES), lambda i: (i, 0))],
        out_specs=[pl.BlockSpec((rstep, past), lambda i: (i, 0)), pl.BlockSpec((rstep, LANES), lambda i: (i, 0))],
        out_shape=[jax.ShapeDtypeStruct((rows, past), F32), jax.ShapeDtypeStruct((rows, LANES), F32)],
        compiler_params=pltpu.CompilerParams(dimension_semantics=("parallel",), vmem_limit_bytes=VMEM_LIMIT),
        name="dsa_sample_mask",
    )(sp.reshape(rows, past), sn.reshape(rows, LANES))
    return mp.reshape(nb, ts, past), mn.reshape(nb, ts, LANES)


def _dsa_sample_scores(proj, pool_kidx_t, pt_flat, layer, *, rb0, nb, ts, n_pages, pp):
    pp = min(pp, n_pages)
    nj = n_pages // pp
    tk = pp * PAGE_SIZE
    past = n_pages * PAGE_SIZE

    def page_spec(r):
        return pl.BlockSpec((None, None, IDX_DIM, PAGE_SIZE),
                            lambda b, j, pt: (layer, pt[b * n_pages + j * pp + r], 0, 0))

    grid_spec = pltpu.PrefetchScalarGridSpec(
        num_scalar_prefetch=1,
        grid=(nb, nj),
        in_specs=[
            pl.BlockSpec((ts, IDX_HEADS * IDX_DIM), lambda b, j, pt: (rb0 + b, EV_QI // (IDX_HEADS * IDX_DIM))),
            pl.BlockSpec((ts, LANES), lambda b, j, pt: (rb0 + b, EV_WI // LANES)),
            pl.BlockSpec((ts, LANES), lambda b, j, pt: (rb0 + b, EV_KIKI // LANES)),
        ] + [page_spec(r) for r in range(pp)],
        out_specs=[pl.BlockSpec((None, ts, tk), lambda b, j, pt: (b, 0, j)),
                   pl.BlockSpec((None, ts, LANES), lambda b, j, pt: (b, 0, 0))],
    )
    return pl.pallas_call(
        functools.partial(_dsa_score_body, pp=pp, ts=ts),
        grid_spec=grid_spec,
        out_shape=[jax.ShapeDtypeStruct((nb, ts, past), F32), jax.ShapeDtypeStruct((nb, ts, LANES), F32)],
        compiler_params=pltpu.CompilerParams(
            dimension_semantics=("parallel", "arbitrary"), vmem_limit_bytes=VMEM_LIMIT),
        name="dsa_sample_scores",
    )(pt_flat, proj, proj, proj, *([pool_kidx_t] * pp))


def _dsa_att_body(pt_ref, q_ref, knew_ref, vnew_ref, qn_ref, mask_ref, mnew_ref, bias_ref, bnew_ref, *rest,
                  pp, sub, nj, ts):
    kpages, vpages = rest[:pp], rest[pp:2 * pp]
    o_ref, qs_ref, m_ref, l_ref, acc_ref = rest[2 * pp:]
    j = pl.program_id(1)
    gsz = A_HEADS // A_KV_HEADS
    rows_g = gsz * ts

    @pl.when(j == 0)
    def _():
        qs = []
        for h in range(A_HEADS):
            q = q_ref[:, LANES * h:LANES * (h + 1)]
            qs.append(q * lax.rsqrt(jnp.mean(q * q, axis=-1, keepdims=True) + EPS) * qn_ref[...])
        qs_ref[...] = jnp.concatenate(qs, axis=0).astype(BF16)
        m_ref[...] = jnp.full(m_ref.shape, MASK_NEG, F32)
        l_ref[...] = jnp.zeros(l_ref.shape, F32)
        acc_ref[...] = jnp.zeros(acc_ref.shape, F32)

    def partial(k_tiles, v_tiles, mask, bias):
        qs = qs_ref[...]
        lg = jnp.concatenate(
            [_dot_nt(qs[rows_g * g:rows_g * (g + 1)], k_tiles[g])
             for g in range(A_KV_HEADS)], axis=0) * (HEAD_DIM ** -0.5) + bias
        sel = jnp.concatenate([mask] * A_HEADS, axis=0) > 0.5
        lg = jnp.where(sel, lg, MASK_NEG)
        m = jnp.max(lg, axis=-1, keepdims=True)
        p = jnp.where(sel, jnp.exp(lg - m), 0.0)
        pb = p.astype(BF16)
        pv = jnp.concatenate(
            [jnp.dot(pb[rows_g * g:rows_g * (g + 1)], v_tiles[g], preferred_element_type=F32)
             for g in range(A_KV_HEADS)], axis=0)
        return m, jnp.sum(p, axis=-1, keepdims=True), pv

    def merge(parts):
        m_old = m_ref[...]
        m_new = m_old
        for m, _, _ in parts:
            m_new = jnp.maximum(m_new, m)
        alpha = jnp.exp(m_old - m_new)
        l_new = alpha * l_ref[...]
        acc = alpha * acc_ref[...]
        for m, l, pv in parts:
            a = jnp.exp(m - m_new)
            l_new = l_new + a * l
            acc = acc + a * pv
        m_ref[...] = m_new
        l_ref[...] = l_new
        acc_ref[...] = acc

    def head_rows(pages, g):
        return jnp.concatenate([p[pl.ds(g, PAGE_SIZE, stride=A_KV_HEADS), :] for p in pages], axis=0).astype(BF16)

    bias_all = bias_ref[...].reshape(A_HEADS * ts, pp * PAGE_SIZE)
    mask_all = mask_ref[...]
    parts = []
    for s0 in range(0, pp, sub):
        kp, vp = kpages[s0:s0 + sub], vpages[s0:s0 + sub]
        cols = slice(s0 * PAGE_SIZE, (s0 + sub) * PAGE_SIZE)
        parts.append(partial([head_rows(kp, g) for g in range(A_KV_HEADS)],
                             [head_rows(vp, g) for g in range(A_KV_HEADS)], mask_all[:, cols], bias_all[:, cols]))
    merge(parts)

    @pl.when(j == nj - 1)
    def _():
        pad = jnp.zeros((LANES - ts, A_KV_HEADS * HEAD_DIM), F32)
        kn = jnp.concatenate([knew_ref[...], pad], axis=0).astype(BF16)
        vn = jnp.concatenate([vnew_ref[...], pad], axis=0).astype(BF16)
        merge([partial([kn[:, LANES * g:LANES * (g + 1)] for g in range(A_KV_HEADS)],
                       [vn[:, LANES * g:LANES * (g + 1)] for g in range(A_KV_HEADS)],
                       mnew_ref[...], bnew_ref[...].reshape(A_HEADS * ts, LANES))])
        out = acc_ref[...] / l_ref[...]
        o_ref[...] = jnp.concatenate([out[ts * h:ts * (h + 1)] for h in range(A_HEADS)], axis=1)


def dsa_sample_attend(proj, ka32, mask_past, mask_new, bias_s, pool_k, pool_v, pt_flat, layer, qn, *, row0, nb, ts,
                      n_pages, pp=32, sub=16):
    pp = min(pp, n_pages)
    nj = n_pages // pp
    tk = pp * PAGE_SIZE
    past = n_pages * PAGE_SIZE
    rb0 = row0 // ts
    wkv = A_KV_HEADS * HEAD_DIM
    wq = A_HEADS * HEAD_DIM

    def page_spec(r):
        return pl.BlockSpec((None, None, PAGE_SIZE * A_KV_HEADS, HEAD_DIM),
                            lambda b, j, pt: (layer, pt[b * n_pages + j * pp + r], 0, 0))

    grid_spec = pltpu.PrefetchScalarGridSpec(
        num_scalar_prefetch=1,
        grid=(nb, nj),
        in_specs=[
            pl.BlockSpec((ts, wq), lambda b, j, pt: (rb0 + b, 0)),
            pl.BlockSpec((ts, wkv), lambda b, j, pt: (rb0 + b, 0)),
            pl.BlockSpec((ts, wkv), lambda b, j, pt: (rb0 + b, EV_VA // wkv)),
            pl.BlockSpec((1, HEAD_DIM), lambda b, j, pt: (0, 0)),
            pl.BlockSpec((None, ts, tk), lambda b, j, pt: (b, 0, j)),
            pl.BlockSpec((None, ts, LANES), lambda b, j, pt: (b, 0, 0)),
            pl.BlockSpec((A_HEADS, ts, tk), lambda b, j, pt: (0, 0, j)),
            pl.BlockSpec((A_HEADS, ts, LANES), lambda b, j, pt: (0, 0, past // LANES)),
        ] + [page_spec(r) for r in range(pp)] * 2,
        out_specs=pl.BlockSpec((ts, wq), lambda b, j, pt: (b, 0)),
        scratch_shapes=[pltpu.VMEM((A_HEADS * ts, HEAD_DIM), BF16), pltpu.VMEM((A_HEADS * ts, 1), F32),
                        pltpu.VMEM((A_HEADS * ts, 1), F32), pltpu.VMEM((A_HEADS * ts, HEAD_DIM), F32)],
    )
    return pl.pallas_call(
        functools.partial(_dsa_att_body, pp=pp, sub=min(sub, pp), nj=nj, ts=ts),
        grid_spec=grid_spec,
        out_shape=jax.ShapeDtypeStruct((nb * ts, wq), F32),
        compiler_params=pltpu.CompilerParams(
            dimension_semantics=("parallel", "arbitrary"), vmem_limit_bytes=VMEM_LIMIT),
        name="dsa_sample_attend",
    )(pt_flat, proj, ka32, proj, qn.reshape(1, HEAD_DIM), mask_past, mask_new, bias_s, bias_s,
      *([pool_k] * pp), *([pool_v] * pp))


def _mla_sample_body(pt_ref, qlat_ref, qpe_ref, cnew_ref, rnew_ref, wkt_ref, wv_ref, *rest, pp, sub, nj, ts):
    cpages, rpages = rest[:pp], rest[pp:2 * pp]
    o_ref, m_ref, l_ref, acc_ref = rest[2 * pp:]
    j = pl.program_id(1)
    dq = NOPE_DIM + ROPE_DIM
    nrow = D_HEADS * ts

    @pl.when(j == 0)
    def _():
        m_ref[...] = jnp.full(m_ref.shape, MASK_NEG, F32)
        l_ref[...] = jnp.zeros(l_ref.shape, F32)
        acc_ref[...] = jnp.zeros(acc_ref.shape, F32)

    def update(c, rt, sel):
        n = c.shape[0]
        cb = c.astype(BF16)
        both = _dot_nt(jnp.concatenate([wkt_ref[...], qlat_ref[...]], axis=0), cb)
        kt = both[:D_HEADS * NOPE_DIM]
        ss = jnp.sum((kt * kt).reshape(D_HEADS, NOPE_DIM, n), axis=1)
        rss = jnp.sum(rt * rt, axis=0, keepdims=True)
        rinv = lax.rsqrt((ss + rss) * (1.0 / dq) + EPS) * (dq ** -0.5)
        lg = both[D_HEADS * NOPE_DIM:] + jnp.dot(qpe_ref[...], rt.astype(BF16),
                                                 preferred_element_type=F32)
        lg = (lg.reshape(D_HEADS, ts, n) * rinv[:, None, :]).reshape(nrow, n)
        if sel is not None:
            lg = jnp.where(sel, lg, MASK_NEG)
        m = jnp.max(lg, axis=-1, keepdims=True)
        p = jnp.exp(lg - m)
        if sel is not None:
            p = jnp.where(sel, p, 0.0)
        return m, jnp.sum(p, axis=-1, keepdims=True), jnp.dot(p.astype(BF16), cb, preferred_element_type=F32)

    def merge(parts):
        m_old = m_ref[...]
        m_new = m_old
        for m, _, _ in parts:
            m_new = jnp.maximum(m_new, m)
        alpha = jnp.exp(m_old - m_new)
        l_new = alpha * l_ref[...]
        acc = alpha * acc_ref[...]
        for m, l, pv in parts:
            a = jnp.exp(m - m_new)
            l_new = l_new + a * l
            acc = acc + a * pv
        m_ref[...] = m_new
        l_ref[...] = l_new
        acc_ref[...] = acc

    parts = []
    for s0 in range(0, pp, sub):
        c = jnp.concatenate([p[...] for p in cpages[s0:s0 + sub]], axis=0)
        rt = jnp.concatenate([p[...] for p in rpages[s0:s0 + sub]], axis=1)
        parts.append(update(c, rt, None))
    merge(parts)

    @pl.when(j == nj - 1)
    def _():
        lane = lax.broadcasted_iota(jnp.int32, (D_HEADS, ts, LANES), 2).reshape(nrow, LANES)
        row_t = lax.broadcasted_iota(jnp.int32, (D_HEADS, ts, LANES), 1).reshape(nrow, LANES)
        merge([update(cnew_ref[...], rnew_ref[...], lane <= row_t)])
        lat = (acc_ref[...] / l_ref[...]).astype(BF16)
        o_ref[...] = jnp.concatenate(
            [jnp.dot(lat[ts * h:ts * (h + 1)], wv_ref[h], preferred_element_type=F32) for h in range(D_HEADS)],
            axis=1)


def mla_sample_attend(qlat, qpe, cnew, rnew_t, wkt, wv, pool_ckv, pool_kpe_t, pt_flat, layer, *, nb, ts, n_pages,
                      pp=32, sub=16):
    pp = min(pp, n_pages)
    nj = n_pages // pp
    nrow = D_HEADS * ts

    def cspec(r):
        return pl.BlockSpec((None, None, PAGE_SIZE, KV_LORA), lambda b, j, pt: (layer, pt[b * n_pages + j * pp + r], 0, 0))

    def rspec(r):
        return pl.BlockSpec((None, None, ROPE_DIM, PAGE_SIZE), lambda b, j, pt: (layer, pt[b * n_pages + j * pp + r], 0, 0))

    grid_spec = pltpu.PrefetchScalarGridSpec(
        num_scalar_prefetch=1,
        grid=(nb, nj),
        in_specs=[
            pl.BlockSpec((None, nrow, KV_LORA), lambda b, j, pt: (b, 0, 0)),
            pl.BlockSpec((None, nrow, ROPE_DIM), lambda b, j, pt: (b, 0, 0)),
            pl.BlockSpec((None, LANES, KV_LORA), lambda b, j, pt: (b, 0, 0)),
            pl.BlockSpec((None, ROPE_DIM, LANES), lambda b, j, pt: (b, 0, 0)),
            pl.BlockSpec((D_HEADS * NOPE_DIM, KV_LORA), lambda b, j, pt: (0, 0)),
            pl.BlockSpec((D_HEADS, KV_LORA, D_VDIM), lambda b, j, pt: (0, 0, 0)),
        ] + [cspec(r) for r in range(pp)] + [rspec(r) for r in range(pp)],
        out_specs=pl.BlockSpec((ts, D_HEADS * D_VDIM), lambda b, j, pt: (b, 0)),
        scratch_shapes=[pltpu.VMEM((nrow, 1), F32), pltpu.VMEM((nrow, 1), F32), pltpu.VMEM((nrow, KV_LORA), F32)],
    )
    return pl.pallas_call(
        functools.partial(_mla_sample_body, pp=pp, sub=min(sub, pp), nj=nj, ts=ts),
        grid_spec=grid_spec,
        out_shape=jax.ShapeDtypeStruct((nb * ts, D_HEADS * D_VDIM), F32),
        compiler_params=pltpu.CompilerParams(
            dimension_semantics=("parallel", "arbitrary"), vmem_limit_bytes=VMEM_LIMIT),
        name="mla_sample",
    )(pt_flat, qlat, qpe, cnew, rnew_t, wkt, wv, *([pool_ckv] * pp), *([pool_kpe_t] * pp))


def _mem_attn_body(q_ref, k_ref, v_ref, qn_ref, o_ref, *, nb, tq, n_mem, head_rows):
    def head(ref, b, h):
        if head_rows:
            return ref[b, pl.ds(h, n_mem, stride=MEM_HEADS), :].astype(BF16)
        return ref[b, :, LANES * h:LANES * (h + 1)].astype(BF16)

    for b in range(nb):
        outs = []
        for h in range(MEM_HEADS):
            q = q_ref[tq * b:tq * (b + 1), LANES * h:LANES * (h + 1)]
            q = q * lax.rsqrt(jnp.mean(q * q, axis=-1, keepdims=True) + EPS) * qn_ref[...]
            lg = _dot_nt(q.astype(BF16), head(k_ref, b, h)) * (HEAD_DIM ** -0.5)
            p = _softmax_rows(lg)
            outs.append(jnp.dot(p.astype(BF16), head(v_ref, b, h), preferred_element_type=F32))
        o_ref[tq * b:tq * (b + 1), :] = jnp.concatenate(outs, axis=1)


def mem_attend(q, mk, mv, qn, *, row0, n_batch, tq, nb, n_mem, head_rows, k_index, v_index):
    w = MEM_HEADS * HEAD_DIM
    rb0 = row0 // (nb * tq)
    slab = (n_mem * MEM_HEADS, HEAD_DIM) if head_rows else (n_mem, w)
    kblock = (None,) * (mk.ndim - 3) + (nb,) + slab
    vblock = (None,) * (mv.ndim - 3) + (nb,) + slab
    return pl.pallas_call(
        functools.partial(_mem_attn_body, nb=nb, tq=tq, n_mem=n_mem, head_rows=head_rows),
        grid=(n_batch // nb,),
        in_specs=[pl.BlockSpec((nb * tq, w), lambda i: (rb0 + i, 0)),
                  pl.BlockSpec(kblock, k_index),
                  pl.BlockSpec(vblock, v_index),
                  pl.BlockSpec((1, HEAD_DIM), lambda i: (0, 0))],
        out_specs=pl.BlockSpec((nb * tq, w), lambda i: (i, 0)),
        out_shape=jax.ShapeDtypeStruct((n_batch * tq, w), F32),
        compiler_params=pltpu.CompilerParams(dimension_semantics=("parallel",), vmem_limit_bytes=VMEM_LIMIT),
        name="mem_attend",
    )(q, mk, mv, qn.reshape(1, HEAD_DIM))


def _causal_attn_body(q_ref, k_ref, v_ref, o_ref, *, qb, kw, i0, scale):
    i = i0 + pl.program_id(2)
    lg = _dot_nt(q_ref[...], k_ref[...]) * scale
    q_pos = i * qb + lax.broadcasted_iota(jnp.int32, (qb, kw), 0)
    k_pos = lax.broadcasted_iota(jnp.int32, (qb, kw), 1)
    p = _softmax_rows(jnp.where(k_pos <= q_pos, lg, MASK_NEG))
    o_ref[...] = jnp.dot(p.astype(BF16), v_ref[...], preferred_element_type=F32)


def causal_attend(q, k, v, *, nb, t, heads, dv, scale, qb=512, splits=2):
    nq = t // qb
    dqk = q.shape[1] // heads
    per = -(-nq // splits)
    outs = []
    for i0 in range(0, nq, per):
        ni = min(per, nq - i0)
        kw = min(t, (i0 + ni) * qb)
        assert t % kw == 0, (t, kw)
        o = pl.pallas_call(
            functools.partial(_causal_attn_body, qb=qb, kw=kw, i0=i0, scale=scale),
            grid=(nb, heads, ni),
            in_specs=[pl.BlockSpec((qb, dqk), lambda b, h, i, i0=i0: (b * nq + i0 + i, h)),
                      pl.BlockSpec((kw, dqk), lambda b, h, i, kw=kw: (b * (t // kw), h)),
                      pl.BlockSpec((kw, dv), lambda b, h, i, kw=kw: (b * (t // kw), h))],
            out_specs=pl.BlockSpec((qb, dv), lambda b, h, i, ni=ni: (b * ni + i, h)),
            out_shape=jax.ShapeDtypeStruct((nb * ni * qb, heads * dv), F32),
            compiler_params=pltpu.CompilerParams(
                dimension_semantics=("parallel", "parallel", "arbitrary"), vmem_limit_bytes=VMEM_LIMIT),
            name="causal_attend",
        )(q, k, v)
        outs.append(o.reshape(nb, ni * qb, heads * dv))
    return jnp.concatenate(outs, axis=1).reshape(nb * t, heads * dv)


def _cumsum_rows(x):
    c = x.shape[0]
    row = lax.broadcasted_iota(jnp.int32, x.shape, 0)
    sh = 1
    while sh < c:
        x = x + jnp.where(row >= sh, pltpu.roll(x, sh, axis=0), 0.0)
        sh *= 2
    return x


def _log_sigmoid(x):
    return jnp.minimum(x, 0.0) - jnp.log1p(jnp.exp(-jnp.abs(x)))


def _sigmoid(x):
    return 1.0 / (1.0 + jnp.exp(-x))


def _hgrn2_run(q_ref, f_ref, v_ref, g_ref, lb_ref, on, st_ref, o_ref, *, nh, t, c):
    row = lax.broadcasted_iota(jnp.int32, (c, 1), 0)

    def step(i, carry):
        r0 = pl.multiple_of(i * c, c)
        for h in range(nh):
            sl = slice(LANES * h, LANES * (h + 1))
            lb = lb_ref[h]
            fpre = f_ref[pl.ds(r0, c), sl]
            a = jnp.log(jnp.maximum(lb, LB_FLOOR))
            b = jnp.log1p(-lb) + _log_sigmoid(fpre)
            lf = jnp.maximum(a, b) + jnp.log1p(jnp.exp(-jnp.abs(a - b)))
            kk = (1.0 - lb) * _sigmoid(-fpre)
            qpre = q_ref[pl.ds(r0, c), sl]
            qv = qpre * _sigmoid(qpre)
            vv = v_ref[pl.ds(r0, c), sl]
            cb = _cumsum_rows(lf)
            st = st_ref[h]
            o = _dot_nt((qv * jnp.exp(cb)).astype(BF16), st.astype(BF16))
            for s in range(c):
                dec = jnp.exp(jnp.minimum(cb - cb[s:s + 1, :], 0.0))
                col = jnp.sum(qv * dec * kk[s:s + 1, :], axis=-1, keepdims=True)
                o = o + jnp.where(row >= s, col, 0.0) * vv[s:s + 1, :]
            c_last = cb[c - 1:c, :]
            kd = (kk * jnp.exp(c_last - cb)).astype(BF16)
            st_ref[h] = st * jnp.exp(c_last) + lax.dot_general(vv.astype(BF16), kd, (((0,), (0,)), ((), ())),
                                                               preferred_element_type=F32)
            gpre = g_ref[pl.ds(r0, c), sl]
            o = o * lax.rsqrt(jnp.mean(o * o, axis=-1, keepdims=True) + EPS) * on * (gpre * _sigmoid(gpre))
            o_ref[pl.ds(r0, c), sl] = o
        return carry

    lax.fori_loop(0, t // c, step, 0)


def _hgrn2_prompt_body(q_ref, f_ref, v_ref, g_ref, lb_ref, on_ref, o_ref, s_ref, st_ref, *, nh, t, c):
    st_ref[...] = jnp.zeros(st_ref.shape, F32)
    _hgrn2_run(q_ref, f_ref, v_ref, g_ref, lb_ref, on_ref[...], st_ref, o_ref, nh=nh, t=t, c=c)
    for h in range(nh):
        s_ref[h] = st_ref[h].T


def hgrn2_prompt(proj, lb, on, *, nb, t, nh=4):
    c = math.gcd(16, t)
    w = nh * LANES
    col = lambda c0: (lambda b, j: (b, c0 // w + j))
    return pl.pallas_call(
        functools.partial(_hgrn2_prompt_body, nh=nh, t=t, c=c),
        grid=(nb, B_HEADS // nh),
        in_specs=[pl.BlockSpec((t, w), col(EV_QB)), pl.BlockSpec((t, w), col(EV_FB)),
                  pl.BlockSpec((t, w), col(EV_IB)), pl.BlockSpec((t, w), col(EV_GB)),
                  pl.BlockSpec((nh, 1, B_DK), lambda b, j: (j, 0, 0)),
                  pl.BlockSpec((1, B_DV), lambda b, j: (0, 0))],
        out_specs=[pl.BlockSpec((t, w), lambda b, j: (b, j)),
                   pl.BlockSpec((None, nh, B_DK, B_DV), lambda b, j: (b, j, 0, 0))],
        out_shape=[jax.ShapeDtypeStruct((nb * t, B_HEADS * B_DV), F32),
                   jax.ShapeDtypeStruct((nb, B_HEADS, B_DK, B_DV), F32)],
        scratch_shapes=[pltpu.VMEM((nh, B_DV, B_DK), F32)],
        compiler_params=pltpu.CompilerParams(
            dimension_semantics=("parallel", "parallel"), vmem_limit_bytes=VMEM_LIMIT),
        name="hgrn2_prompt",
    )(proj, proj, proj, proj, lb.reshape(B_HEADS, 1, B_DK), on.reshape(1, B_DV))


def _hgrn2_sample_body(q_ref, f_ref, v_ref, g_ref, lb_ref, on_ref, s0_ref, o_ref, s_ref, st_ref, *, t):
    for h in range(B_HEADS):
        st_ref[h] = s0_ref[h].T
    _hgrn2_run(q_ref, f_ref, v_ref, g_ref, lb_ref, on_ref[...], st_ref, o_ref, nh=B_HEADS, t=t, c=t)
    for h in range(B_HEADS):
        s_ref[h] = st_ref[h].T


def hgrn2_sample(proj, lb, on, s0, *, row0, nb, t):
    rb0 = row0 // t
    w = B_HEADS * LANES
    col = lambda c0: (lambda b: (rb0 + b, c0 // w))
    return pl.pallas_call(
        functools.partial(_hgrn2_sample_body, t=t),
        grid=(nb,),
        in_specs=[pl.BlockSpec((t, w), col(EV_QB)), pl.BlockSpec((t, w), col(EV_FB)),
                  pl.BlockSpec((t, w), col(EV_IB)), pl.BlockSpec((t, w), col(EV_GB)),
                  pl.BlockSpec((B_HEADS, 1, B_DK), lambda b: (0, 0, 0)),
                  pl.BlockSpec((1, B_DV), lambda b: (0, 0)),
                  pl.BlockSpec((None, B_HEADS, B_DK, B_DV), lambda b: (b, 0, 0, 0))],
        out_specs=[pl.BlockSpec((t, w), lambda b: (b, 0)),
                   pl.BlockSpec((None, B_HEADS, B_DK, B_DV), lambda b: (b, 0, 0, 0))],
        out_shape=[jax.ShapeDtypeStruct((nb * t, w), F32),
                   jax.ShapeDtypeStruct((nb, B_HEADS, B_DK, B_DV), F32)],
        scratch_shapes=[pltpu.VMEM((B_HEADS, B_DV, B_DK), F32)],
        compiler_params=pltpu.CompilerParams(dimension_semantics=("parallel",), vmem_limit_bytes=VMEM_LIMIT),
        name="hgrn2_sample",
    )(proj, proj, proj, proj, lb.reshape(B_HEADS, 1, B_DK), on.reshape(1, B_DV), s0)


def _mlstm_chunk(qm, km_h, v, ic_row, ic_col, cb_col, cb_row, ct, n, m, c):
    r = lax.broadcasted_iota(jnp.int32, (c, c), 0)
    s = lax.broadcasted_iota(jnp.int32, (c, c), 1)
    dmat = jnp.where(r >= s, cb_col - cb_row + ic_row, MASK_NEG)
    m_state = cb_col + m
    m_t = jnp.maximum(m_state, jnp.max(dmat, axis=-1, keepdims=True))
    w = jnp.exp(dmat - m_t)
    w0 = jnp.exp(m_state - m_t)
    qb16 = qm.astype(BF16)
    qk = _dot_nt(qb16, km_h.astype(BF16)) * w
    num = (jnp.dot(qk.astype(BF16), v.astype(BF16), preferred_element_type=F32)
           + w0 * jnp.dot(qb16, ct.astype(BF16), preferred_element_type=F32))
    den = jnp.sum(qk, axis=-1, keepdims=True) + w0 * jnp.sum(qm * n, axis=-1, keepdims=True)
    hc = num / jnp.maximum(jnp.abs(den), jnp.exp(-m_t))
    m_last = m_t[c - 1:c, :]
    cb_last = cb_col[c - 1:c, :]
    ws = jnp.exp(cb_last - cb_col + ic_col - m_last)
    fs = jnp.exp(cb_last + m - m_last)
    ct = fs * ct + lax.dot_general(km_h.astype(BF16), (v * ws).astype(BF16), (((0,), (0,)), ((), ())),
                                   preferred_element_type=F32)
    n = fs * n + jnp.sum(ws * km_h, axis=0, keepdims=True)
    return hc, ct, n, m_last


def _mlstm_seq(q_ref, k_ref, v_ref, og_ref, g_ref, gb_ref, on, npairs, states, o_ref, *, t, c):
    lane = lax.broadcasted_iota(jnp.int32, (c, LANES), 1)

    def step(i, carry):
        r0 = pl.multiple_of(i * c, c)
        new = []
        for p in range(npairs):
            psl = slice(LANES * p, LANES * (p + 1))
            g = g_ref[pl.ds(r0, c), psl] + gb_ref[p]
            cb = _cumsum_rows(_log_sigmoid(g))
            gt = g.T
            cbt = cb.T
            q = q_ref[pl.ds(r0, c), psl]
            k = k_ref[pl.ds(r0, c), psl] * (C_DK ** -0.5)
            for half in range(2):
                ct, n, m = carry[2 * p + half]
                keep = (lane < C_DK) if half == 0 else (lane >= C_DK)
                vsl = slice(C_DV * (2 * p + half), C_DV * (2 * p + half + 1))
                hc, ct, n, m = _mlstm_chunk(
                    jnp.where(keep, q, 0.0), jnp.where(keep, k, 0.0), v_ref[pl.ds(r0, c), vsl],
                    gt[half:half + 1, :], g[:, half:half + 1], cb[:, 2 + half:3 + half], cbt[2 + half:3 + half, :],
                    ct, n, m, c)
                og = og_ref[pl.ds(r0, c), vsl]
                hc = hc * lax.rsqrt(jnp.mean(hc * hc, axis=-1, keepdims=True) + EPS) * on * _sigmoid(og)
                o_ref[pl.ds(r0, c), vsl] = hc
                new.append((ct, n, m))
        return tuple(new)

    return lax.fori_loop(0, t // c, step, tuple(states))


def _mlstm_body(*refs, t, c, npairs, has_state):
    q_ref, k_ref, v_ref, og_ref, g_ref, gb_ref, on_ref = refs[:7]
    refs = refs[7:]
    if has_state:
        c0_ref, n0_ref, m0_ref = refs[:3]
        refs = refs[3:]
    o_ref, c_ref, n_ref, m_ref = refs
    states = []
    for h in range(2 * npairs):
        half = h % 2
        if has_state:
            ct_h = c0_ref[h]
            z = jnp.zeros((C_DK, C_DV), F32)
            ct = jnp.concatenate([ct_h, z] if half == 0 else [z, ct_h], axis=0)
            n_h = n0_ref[h:h + 1, :]
            zn = jnp.zeros((1, C_DK), F32)
            n = jnp.concatenate([n_h, zn] if half == 0 else [zn, n_h], axis=1)
            m = m0_ref[:, h:h + 1]
        else:
            ct, n, m = jnp.zeros((2 * C_DK, C_DV), F32), jnp.zeros((1, 2 * C_DK), F32), jnp.zeros((1, 1), F32)
        states.append((ct, n, m))
    out = _mlstm_seq(q_ref, k_ref, v_ref, og_ref, g_ref, gb_ref, on_ref[...], npairs, states, o_ref, t=t, c=c)
    for h, (ct, n, m) in enumerate(out):
        rows = slice(C_DK * (h % 2), C_DK * (h % 2 + 1))
        c_ref[h] = ct[rows, :]
        n_ref[h] = n[:, rows]
        m_ref[h] = jnp.broadcast_to(m, (1, LANES))


def mlstm(proj, gb, on, state, *, row0, nb, t, npairs):
    c = math.gcd(CHUNK, t)
    rb0 = row0 // t
    npg = (C_HEADS // 2) // npairs
    hp = 2 * npairs
    qw, vw = npairs * LANES, npairs * 2 * C_DV
    in_specs = [
        pl.BlockSpec((t, qw), lambda b, j: (rb0 + b, OD_QC // qw + j)),
        pl.BlockSpec((t, qw), lambda b, j: (rb0 + b, OD_KC // qw + j)),
        pl.BlockSpec((t, vw), lambda b, j: (rb0 + b, OD_VC // vw + j)),
        pl.BlockSpec((t, vw), lambda b, j: (rb0 + b, OD_OC // vw + j)),
        pl.BlockSpec((t, qw), lambda b, j: (rb0 + b, OD_G // qw + j)),
        pl.BlockSpec((npairs, 1, LANES), lambda b, j: (j, 0, 0)),
        pl.BlockSpec((1, C_DV), lambda b, j: (0, 0)),
    ]
    args = [proj, proj, proj, proj, proj, gb, on.reshape(1, C_DV)]
    if state is not None:
        in_specs += [pl.BlockSpec((None, hp, C_DK, C_DV), lambda b, j: (b, j, 0, 0)),
                     pl.BlockSpec((None, hp, C_DK), lambda b, j: (b, j, 0)),
                     pl.BlockSpec((None, 1, hp), lambda b, j: (b, 0, j))]
        args += list(state)
    return pl.pallas_call(
        functools.partial(_mlstm_body, t=t, c=c, npairs=npairs, has_state=state is not None),
        grid=(nb, npg),
        in_specs=in_specs,
        out_specs=[pl.BlockSpec((t, vw), lambda b, j: (b, j)),
                   pl.BlockSpec((None, hp, C_DK, C_DV), lambda b, j: (b, j, 0, 0)),
                   pl.BlockSpec((None, hp, 1, C_DK), lambda b, j: (b, j, 0, 0)),
                   pl.BlockSpec((None, hp, 1, LANES), lambda b, j: (b, j, 0, 0))],
        out_shape=[jax.ShapeDtypeStruct((nb * t, C_HEADS * C_DV), F32),
                   jax.ShapeDtypeStruct((nb, C_HEADS, C_DK, C_DV), F32),
                   jax.ShapeDtypeStruct((nb, C_HEADS, 1, C_DK), F32),
                   jax.ShapeDtypeStruct((nb, C_HEADS, 1, LANES), F32)],
        compiler_params=pltpu.CompilerParams(
            dimension_semantics=("parallel", "parallel"), vmem_limit_bytes=VMEM_LIMIT),
        name="mlstm",
    )(*args)


def rms_norm(x, g):
    xf = x.astype(F32)
    y = xf * lax.rsqrt(jnp.mean(xf * xf, axis=-1, keepdims=True) + EPS)
    return (y * g.astype(F32)).astype(x.dtype)


def rel_bucket(rel):
    n = jnp.maximum(rel, 0)
    max_exact = REL_BUCKETS // 2
    nf = jnp.maximum(n, 1).astype(F32)
    large = max_exact + (jnp.log(nf / max_exact) / math.log(REL_MAX_DIST / max_exact)
                         * (REL_BUCKETS - max_exact)).astype(jnp.int32)
    return jnp.where(n < max_exact, n, jnp.minimum(large, REL_BUCKETS - 1))


def dsa_bias_tables(rel_bias, ts, past):
    table = rel_bias.astype(F32)[rel_bucket(jnp.arange(2 * LANES))]
    r = np.arange(LANES)[:, None]
    c = np.arange(LANES)[None, :]
    diag = table[np.clip(r - c, 0, 2 * LANES - 1)]
    prev = table[LANES + r - c]
    far = jnp.broadcast_to(table[2 * LANES - 1], (LANES, LANES, A_HEADS))
    tiles = jnp.moveaxis(jnp.stack([diag, prev, far]), -1, 0)
    n_far = max(past - 2 * LANES, 0)
    rel = past + np.arange(ts)[:, None] - np.arange(n_far, past + LANES)[None, :]
    sample = jnp.concatenate([jnp.broadcast_to(table[2 * LANES - 1], (ts, n_far, A_HEADS)),
                              table[np.clip(rel, 0, 2 * LANES - 1)]], axis=1)
    sample = jnp.moveaxis(sample, -1, 0)
    return tiles, sample


def rope_tables(pos):
    half = ROPE_DIM // 2
    inv = ROPE_THETA ** (-jnp.arange(half, dtype=F32) / half)
    ang = pos.astype(F32)[:, None] * inv[None, :]
    return jnp.cos(ang), jnp.sin(ang)


def rope_rows(x, tables):
    half = ROPE_DIM // 2
    shp = (x.shape[0],) + (1,) * (x.ndim - 2) + (half,)
    cos, sin = tables[0].reshape(shp), tables[1].reshape(shp)
    x1, x2 = x[..., :half], x[..., half:]
    return jnp.concatenate([x1 * cos - x2 * sin, x2 * cos + x1 * sin], axis=-1)


def mla_sample_pallas(qd, ckv, kpe, st, o, kn, wb):
    b, t = qd.shape[:2]
    qn = (qd[..., :NOPE_DIM] * kn[:NOPE_DIM]).reshape(b * t, D_HEADS * NOPE_DIM)
    qlat = matmul(qn, wb['d_kt_blockdiag'][o])
    qlat = qlat.reshape(b, t, D_HEADS, KV_LORA).transpose(0, 2, 1, 3).reshape(b, D_HEADS * t, KV_LORA)
    qpe = (qd[..., NOPE_DIM:] * kn[NOPE_DIM:]).transpose(0, 2, 1, 3).reshape(b, D_HEADS * t, ROPE_DIM)
    cnew = jnp.pad(ckv, ((0, 0), (0, LANES - t), (0, 0)))
    rnew_t = jnp.swapaxes(jnp.pad(kpe, ((0, 0), (0, LANES - t), (0, 0))), 1, 2)
    od = mla_sample_attend(qlat.astype(BF16), qpe.astype(BF16), cnew, rnew_t, wb['d_kt'][o], wb['d_v'][o],
                           st['cache_d_ckv'], st['pool_kpe_t'], st['pt_flat'], o,
                           nb=b, ts=t, n_pages=st['page_table'].shape[1])
    return od.reshape(b, t, D_HEADS, D_VDIM)


def kernel(x_prompt, x_sample, cache_a_k, cache_a_v, cache_a_kidx, state_b, state_c_C, state_c_n, state_c_m, cache_d_ckv, cache_d_kpe, cache_mem_k, cache_mem_v, page_table, mem_prompt, g_ffn1, w_ffn1_gu, w_ffn1_dn, g_mix, w_in_even, w_in_odd, w_mix_out, rel_bias, a_qn, a_kn, b_lb, b_on, c_gate_b, c_on, d_qa_g, d_kv_g, w_d_qb, w_d_kvb, d_qn, d_kn, g_mem, w_mem_q, w_mem_kv, w_mem_o, mem_qn, mem_kn, g_ffn2, w_ffn2_gu, w_ffn2_dn):
    bp, tp, d = x_prompt.shape
    bs, ts, _ = x_sample.shape
    n_p, n_s = bp * tp, bs * ts
    n_mem = mem_prompt.shape[1]
    past = page_table.shape[1] * PAGE_SIZE

    wo = w_in_odd.astype(BF16)
    n_odd = wo.shape[0]
    oc = np.cumsum([0] + ODD_SPLITS)
    gate_cols = []
    for pr in range(C_HEADS // 2):
        gate_cols += [wo[:, :, oc[3] + 2 * pr:oc[3] + 2 * pr + 2], wo[:, :, oc[4] + 2 * pr:oc[4] + 2 * pr + 2],
                      jnp.zeros((n_odd, d, LANES - 4), BF16)]
    in_odd = jnp.concatenate(
        [wo[:, :, oc[0]:oc[3]], wo[:, :, oc[5]:oc[6]]] + gate_cols
        + [wo[:, :, oc[6]:oc[9]], jnp.zeros((n_odd, d, LANES - ROPE_DIM), BF16)], axis=-1)
    gb = c_gate_b.astype(F32)
    gate_bias = jnp.concatenate(
        [gb[:, 0].reshape(n_odd, C_HEADS // 2, 2), gb[:, 1].reshape(n_odd, C_HEADS // 2, 2),
         jnp.zeros((n_odd, C_HEADS // 2, LANES - 4), F32)], axis=-1)[:, :, None, :]
    we = w_in_even.astype(BF16)
    c_wi = sum(EVEN_SPLITS[:4])
    c_ki = c_wi + IDX_HEADS
    c_qb = c_ki + IDX_DIM
    n_even = we.shape[0]
    c_ka = EVEN_SPLITS[0]
    in_even = jnp.concatenate(
        [we[:, :, :c_ka], we[:, :, c_qb:], we[:, :, c_ka:c_wi], we[:, :, c_ki:c_qb], we[:, :, c_ki:c_qb],
         we[:, :, c_wi:c_ki], jnp.zeros((n_even, d, LANES - IDX_HEADS), BF16)], axis=-1)
    kvb = w_d_kvb.astype(BF16).reshape(-1, KV_LORA, D_HEADS, NOPE_DIM + D_VDIM)
    d_kt3 = kvb[..., :NOPE_DIM].transpose(0, 2, 3, 1)
    eye = jnp.eye(D_HEADS, dtype=BF16)
    d_kt_bd = (d_kt3[:, :, :, None, :] * eye[None, :, None, :, None]).reshape(
        -1, D_HEADS * NOPE_DIM, D_HEADS * KV_LORA)
    wb = {
        'ffn1_gu': w_ffn1_gu.astype(BF16), 'ffn1_dn': w_ffn1_dn.astype(BF16),
        'ffn2_gu': w_ffn2_gu.astype(BF16), 'ffn2_dn': w_ffn2_dn.astype(BF16),
        'in_even': in_even,
        'in_odd': in_odd,
        'mix_out': w_mix_out.astype(BF16), 'w_d_qb': w_d_qb.astype(BF16), 'w_d_kvb': w_d_kvb.astype(BF16),
        'mem_q': w_mem_q.astype(BF16), 'mem_kv': w_mem_kv.astype(BF16), 'mem_o': w_mem_o.astype(BF16),
        'd_kt': d_kt3.reshape(-1, D_HEADS * NOPE_DIM, KV_LORA), 'd_kt_blockdiag': d_kt_bd,
        'd_v': kvb[..., NOPE_DIM:].transpose(0, 2, 1, 3),
    }
    bias_tiles, bias_sample = dsa_bias_tables(rel_bias, ts, past)
    n_pages = page_table.shape[1]
    n_phys = cache_a_k.shape[1]
    pool_k = cache_a_k.reshape(-1, n_phys, PAGE_SIZE * A_KV_HEADS, HEAD_DIM)
    pool_v = cache_a_v.reshape(-1, n_phys, PAGE_SIZE * A_KV_HEADS, HEAD_DIM)
    pool_kidx_t = jnp.swapaxes(cache_a_kidx, 2, 3)
    pool_kpe_t = jnp.swapaxes(cache_d_kpe, 2, 3)
    mem_k_rows = cache_mem_k.reshape(DEPTH, bs, n_mem * MEM_HEADS, HEAD_DIM)
    mem_v_rows = cache_mem_v.reshape(DEPTH, bs, n_mem * MEM_HEADS, HEAD_DIM)
    pt_flat = page_table.reshape(-1)

    lb_soft = jax.nn.softmax(b_lb.astype(F32), axis=0)
    lower_bound = jnp.cumsum(lb_soft, axis=0) - lb_soft[0]

    pos_all = rope_tables(jnp.concatenate([jnp.tile(jnp.arange(tp), bp), jnp.tile(past + jnp.arange(ts), bs)]))
    st = {'cache_d_ckv': cache_d_ckv, 'pool_kpe_t': pool_kpe_t, 'page_table': page_table, 'pt_flat': pt_flat}

    x = jnp.concatenate([x_prompt.reshape(n_p, d), x_sample.reshape(n_s, d)], axis=0)
    mem2d = mem_prompt.reshape(bp * n_mem, d)
    new_p = {n: [] for n in ('a_k', 'a_v', 'a_kidx', 'b_S', 'c_C', 'c_n', 'c_m', 'd_ckv', 'd_kpe', 'mem_k', 'mem_v')}
    new_s = {n: [] for n in ('a_k', 'a_v', 'a_kidx', 'b_S', 'c_C', 'c_n', 'c_m', 'd_ckv', 'd_kpe')}

    for l in range(DEPTH):
        x = ffn_half_step(x, g_ffn1[l], wb['ffn1_gu'], wb['ffn1_dn'], l)
        if l % 2 == 0:
            e = l // 2
            proj = matmul(x, wb['in_even'], layer=e, gain=g_mix[l], tm=1024, tn=1280)
            ka32, ka16 = head_norm(proj, EV_KA // (A_KV_HEADS * HEAD_DIM), A_KV_HEADS, a_kn[e])
            oa_p = dsa_prompt_attend(proj, ka16, bias_tiles, a_qn[e], nb=bp, t=tp)
            sel_past, sel_new = dsa_sample_select(proj, pool_kidx_t, pt_flat, e, row0=n_p, nb=bs, ts=ts,
                                                  n_pages=n_pages)
            oa_s = dsa_sample_attend(proj, ka32, sel_past, sel_new, bias_sample, pool_k, pool_v, pt_flat, e, a_qn[e],
                                     row0=n_p, nb=bs, ts=ts, n_pages=n_pages)
            lb = lower_bound[e].reshape(B_HEADS, B_DK)
            ob_p, s_p = hgrn2_prompt(proj, lb, b_on[e], nb=bp, t=tp)
            ob_s, s_s = hgrn2_sample(proj, lb, b_on[e], state_b[e], row0=n_p, nb=bs, t=ts)
            mixed_p = jnp.concatenate([oa_p, ob_p], axis=-1)
            mixed_s = jnp.concatenate([oa_s, ob_s], axis=-1)
            va = proj[:, EV_VA:EV_VA + A_KV_HEADS * HEAD_DIM]
            ki = proj[:, EV_KIKI:EV_KIKI + IDX_DIM]
            for new, rows, nb_, t_, s_new in ((new_p, slice(0, n_p), bp, tp, s_p), (new_s, slice(n_p, None), bs, ts, s_s)):
                new['a_k'].append(ka32[rows].reshape(nb_, t_, A_KV_HEADS, HEAD_DIM))
                new['a_v'].append(va[rows].reshape(nb_, t_, A_KV_HEADS, HEAD_DIM))
                new['a_kidx'].append(ki[rows].reshape(nb_, t_, IDX_DIM))
                new['b_S'].append(s_new)
        else:
            o = l // 2
            proj = matmul(x, wb['in_odd'], layer=o, gain=g_mix[l], tm=1024, tn=896)
            hc_p, cc_p, cn_p, cm_p = mlstm(proj, gate_bias[o], c_on[o], None, row0=0, nb=bp, t=tp, npairs=2)
            hc_s, cc_s, cn_s, cm_s = mlstm(
                proj, gate_bias[o], c_on[o],
                (jnp.swapaxes(state_c_C[o], -1, -2), state_c_n[o], state_c_m[o].reshape(bs, 1, C_HEADS)),
                row0=n_p, nb=bs, t=ts, npairs=C_HEADS // 2)
            qf = matmul(proj, wb['w_d_qb'], layer=o, gain=d_qa_g[o], a_col=OD_QA // Q_LORA)
            qf = qf.reshape(n_p + n_s, D_HEADS, NOPE_DIM + ROPE_DIM)
            qd = rms_norm(jnp.concatenate([qf[..., :NOPE_DIM], rope_rows(qf[..., NOPE_DIM:], pos_all)], axis=-1),
                          d_qn[o])
            ckv = rms_norm(proj[:, OD_CKV:OD_CKV + KV_LORA], d_kv_g[o])
            kpe = rope_rows(proj[:, OD_KPE:OD_KPE + ROPE_DIM], pos_all)
            kv = matmul(ckv[:n_p], wb['w_d_kvb'], layer=o).reshape(n_p, D_HEADS, NOPE_DIM + D_VDIM)
            k_pe = jnp.broadcast_to(kpe[:n_p, None, :], (n_p, D_HEADS, ROPE_DIM))
            kd = rms_norm(jnp.concatenate([kv[..., :NOPE_DIM], k_pe], axis=-1), d_kn[o])
            pad = ((0, 0), (0, 0), (0, 2 * LANES - NOPE_DIM - ROPE_DIM))
            q16 = jnp.pad(qd[:n_p], pad).astype(BF16).reshape(n_p, D_HEADS * 2 * LANES)
            k16 = jnp.pad(kd, pad).astype(BF16).reshape(n_p, D_HEADS * 2 * LANES)
            v16 = kv[..., NOPE_DIM:].astype(BF16).reshape(n_p, D_HEADS * D_VDIM)
            od_p = causal_attend(q16, k16, v16, nb=bp, t=tp, heads=D_HEADS, dv=D_VDIM,
                                 scale=(NOPE_DIM + ROPE_DIM) ** -0.5)
            od_s = mla_sample_pallas(qd[n_p:].reshape(bs, ts, D_HEADS, NOPE_DIM + ROPE_DIM),
                                     ckv[n_p:].reshape(bs, ts, KV_LORA), kpe[n_p:].reshape(bs, ts, ROPE_DIM),
                                     st, o, d_kn[o], wb).reshape(n_s, D_HEADS * D_VDIM)
            mixed_p = jnp.concatenate([hc_p, od_p], axis=-1)
            mixed_s = jnp.concatenate([hc_s, od_s], axis=-1)
            for new, rows, nb_, t_, cc, cn, cm in ((new_p, slice(0, n_p), bp, tp, cc_p, cn_p, cm_p),
                                                   (new_s, slice(n_p, None), bs, ts, cc_s, cn_s, cm_s)):
                new['d_ckv'].append(ckv[rows].reshape(nb_, t_, KV_LORA))
                new['d_kpe'].append(kpe[rows].reshape(nb_, t_, ROPE_DIM))
                new['c_C'].append(jnp.swapaxes(cc, -1, -2))
                new['c_n'].append(cn[:, :, 0, :])
                new['c_m'].append(cm[:, :, 0, 0])
        mixed = jnp.concatenate([mixed_p, mixed_s], axis=0)
        x = matmul(mixed, wb['mix_out'], layer=l, res=x)

        kvm = matmul(mem2d, wb['mem_kv'], layer=l)
        mk32, mk16 = head_norm(kvm, 0, MEM_HEADS, mem_kn[l], tm=bp * n_mem)
        wm = MEM_HEADS * HEAD_DIM
        new_p['mem_k'].append(mk32.reshape(bp, n_mem, MEM_HEADS, HEAD_DIM))
        new_p['mem_v'].append(kvm[:, wm:].reshape(bp, n_mem, MEM_HEADS, HEAD_DIM))
        q = matmul(x, wb['mem_q'], layer=l, gain=g_mem[l])
        nqb = 1
        o_p = mem_attend(q, mk16.reshape(bp, n_mem, wm), kvm.reshape(bp, n_mem, 2 * wm), mem_qn[l],
                         row0=0, n_batch=bp * nqb, tq=tp // nqb, nb=1, n_mem=n_mem, head_rows=False,
                         k_index=lambda i: (i // nqb, 0, 0), v_index=lambda i: (i // nqb, 0, 1))
        o_s = mem_attend(q, mem_k_rows, mem_v_rows, mem_qn[l], row0=n_p, n_batch=bs, tq=ts, nb=8,
                         n_mem=n_mem, head_rows=True,
                         k_index=lambda i, l=l: (l, i, 0, 0), v_index=lambda i, l=l: (l, i, 0, 0))
        o_all = jnp.concatenate([o_p, o_s], axis=0)
        x = matmul(o_all, wb['mem_o'], layer=l, res=x)
        x = ffn_half_step(x, g_ffn2[l], wb['ffn2_gu'], wb['ffn2_dn'], l)

    sp = {n: jnp.stack(v) for n, v in new_p.items()}
    ss = {n: jnp.stack(v) for n, v in new_s.items()}
    return (x[:n_p].reshape(bp, tp, d), x[n_p:].reshape(bs, ts, d),
            sp['a_k'], sp['a_v'], sp['a_kidx'], sp['b_S'], sp['c_C'], sp['c_n'], sp['c_m'], sp['d_ckv'], sp['d_kpe'],
            sp['mem_k'], sp['mem_v'],
            ss['a_k'], ss['a_v'], ss['a_kidx'], ss['b_S'], ss['c_C'], ss['c_n'], ss['c_m'], ss['d_ckv'], ss['d_kpe'])
```
